```python
import math
import jax, jax.numpy as jnp
from jax import lax
import numpy as np

D_MODEL = 1024
BATCH = 8
SEQ = 2048
DEPTH = 2
DEC_BATCH = 128
DEC_SEQ = 4
PAST_LEN = 16384
PAGE_SIZE = 128

D_MIX = D_MODEL
D_MLSTM = D_MIX // 2
MH = 4
DH = D_MLSTM // MH
D_S5 = D_MIX // 4
S5_CH = 16
S5_G = D_S5 // S5_CH
S5_P = 64
D_CONV = D_MIX - D_MLSTM - D_S5
CONV_W = 3
D_FF = ((8 * D_MODEL // 3 + 127) // 128) * 128
CHUNK = 64
EPS = 1e-6
SPLIT_SIZES = (D_MLSTM, D_MLSTM, D_MLSTM, D_MLSTM, MH, MH, D_S5, D_CONV, D_CONV, D_CONV)
PROJ_W = sum(SPLIT_SIZES)

kernel_name = "hymba_style_mlstm_s5_shortconv_macaron"


def _rms(x):
    xf = x.astype(jnp.float32)
    return xf * lax.rsqrt(jnp.mean(xf * xf, axis=-1, keepdims=True) + EPS)


def rmsnorm(x, g):
    return (_rms(x) * g.astype(jnp.float32)).astype(x.dtype)


def swiglu(x, w1, w3, w2):
    return (jax.nn.silu(x @ w1) * (x @ w3)) @ w2


def mlstm_chunkwise(q, k, v, ig, fg, C0, n0, m0):
    f32 = jnp.float32
    Bb, S, H, Dh = q.shape
    L = CHUNK if S % CHUNK == 0 else S
    NC = S // L
    q = q.astype(f32)
    k = k.astype(f32) * (Dh ** -0.5)
    v = v.astype(f32)
    it_all = ig.astype(f32)
    lf_all = jax.nn.log_sigmoid(fg.astype(f32))

    def to_chunks(a):
        return a.reshape((Bb, NC, L) + a.shape[2:]).swapaxes(0, 1)

    mask = jnp.tril(jnp.ones((L, L), dtype=bool))

    def step(carry, inp):
        C, n, m = carry
        qc, kc, vc, ic, lfc = inp
        bt = jnp.cumsum(lfc, axis=1).transpose(0, 2, 1)
        it = ic.transpose(0, 2, 1)
        dmat = bt[..., :, None] - bt[..., None, :] + it[..., None, :]
        dmat = jnp.where(mask, dmat, -jnp.inf)
        inter = bt + m[..., None]
        mt = jnp.maximum(jnp.max(dmat, axis=-1), inter)
        w = jnp.exp(dmat - mt[..., None])
        s_inter = jnp.exp(inter - mt)
        sqk = jnp.einsum('bthd,bshd->bhts', qc, kc) * w
        num = (jnp.einsum('bhts,bshd->bthd', sqk, vc)
               + s_inter.transpose(0, 2, 1)[..., None] * jnp.einsum('bhvk,bthk->bthv', C, qc))
        den = jnp.sum(sqk, axis=-1) + s_inter * jnp.einsum('bhk,bthk->bht', n, qc)
        h = num / jnp.maximum(jnp.abs(den), jnp.exp(-mt)).transpose(0, 2, 1)[..., None]
        m_new = mt[..., -1]
        wk = jnp.exp(bt[..., -1:] - bt + it - m_new[..., None])
        decay = jnp.exp(bt[..., -1] + m - m_new)
        C_new = decay[..., None, None] * C + jnp.einsum('bhs,bshv,bshk->bhvk', wk, vc, kc)
        n_new = decay[..., None] * n + jnp.einsum('bhs,bshk->bhk', wk, kc)
        return (C_new, n_new, m_new), h

    carry, hs = lax.scan(step, (C0.astype(f32), n0.astype(f32), m0.astype(f32)),
                         (to_chunks(q), to_chunks(k), to_chunks(v), to_chunks(it_all), to_chunks(lf_all)))
    h = hs.swapaxes(0, 1).reshape(Bb, S, H, Dh)
    return h, carry


def s5_ssm(u, a_re, a_im, log_dt, b_re, b_im, c_re, c_im, d_skip, h0_re, h0_im):
    f32 = jnp.float32
    Bb, S, _ = u.shape
    uf = u.astype(f32).reshape(Bb, S, S5_G, S5_CH)
    dt = jnp.exp(log_dt.astype(f32))[:, None]
    lre = jnp.minimum(a_re.astype(f32), -1e-4)
    lim = a_im.astype(f32)
    mag = jnp.exp(lre * dt)
    ab_re = mag * jnp.cos(lim * dt)
    ab_im = mag * jnp.sin(lim * dt)
    den = lre * lre + lim * lim
    nr = ab_re - 1.0
    fre = (nr * lre + ab_im * lim) / den
    fim = (ab_im * lre - nr * lim) / den
    bu_re = jnp.einsum('bsgc,gpc->bsgp', uf, b_re.astype(f32))
    bu_im = jnp.einsum('bsgc,gpc->bsgp', uf, b_im.astype(f32))
    x_re = fre * bu_re - fim * bu_im
    x_im = fre * bu_im + fim * bu_re
    A_re = jnp.broadcast_to(ab_re, x_re.shape)
    A_im = jnp.broadcast_to(ab_im, x_re.shape)

    def combine(e1, e2):
        a1r, a1i, b1r, b1i = e1
        a2r, a2i, b2r, b2i = e2
        return (a2r * a1r - a2i * a1i, a2r * a1i + a2i * a1r,
                a2r * b1r - a2i * b1i + b2r, a2r * b1i + a2i * b1r + b2i)

    Pr, Pi, Hr, Hi = lax.associative_scan(combine, (A_re, A_im, x_re, x_im), axis=1)
    h0r = h0_re.astype(f32)[:, None]
    h0i = h0_im.astype(f32)[:, None]
    hr = Hr + Pr * h0r - Pi * h0i
    hi = Hi + Pr * h0i + Pi * h0r
    y = jnp.einsum('bsgp,gcp->bsgc', hr, c_re.astype(f32)) - jnp.einsum('bsgp,gcp->bsgc', hi, c_im.astype(f32))
    y = y.reshape(Bb, S, D_S5) + d_skip.astype(f32) * u.astype(f32)
    return y.astype(u.dtype), hr[:, -1], hi[:, -1]


def short_conv(z, w, buf):
    S = z.shape[1]
    zp = jnp.concatenate([buf.astype(z.dtype), z], axis=1)
    y = w[0] * zp[:, 0:S]
    for j in range(1, CONV_W):
        y = y + w[j] * zp[:, j:j + S]
    return y, zp[:, -(CONV_W - 1):]


def mixer(h, w_in, ig_bias, fg_bias, s5_a_re, s5_a_im, s5_log_dt, s5_b_re, s5_b_im, s5_c_re, s5_c_im,
          s5_d, w_glu, conv_w, mix_g, w_out, C0, n0, m0, sr0, si0, cb0):
    Bb, S, _ = h.shape
    proj = h @ w_in
    idx = [int(i) for i in np.cumsum(SPLIT_SIZES)[:-1]]
    q, k, v, o, ig, fg, u, bg, cg, hc = jnp.split(proj, idx, axis=-1)
    hm, (C1, n1, m1) = mlstm_chunkwise(q.reshape(Bb, S, MH, DH), k.reshape(Bb, S, MH, DH),
                                        v.reshape(Bb, S, MH, DH), ig + ig_bias, fg + fg_bias, C0, n0, m0)
    hm = jax.nn.sigmoid(o.astype(jnp.float32)).reshape(Bb, S, MH, DH) * hm
    out_a = _rms(hm).reshape(Bb, S, D_MLSTM)
    ys, sr1, si1 = s5_ssm(u, s5_a_re, s5_a_im, s5_log_dt, s5_b_re, s5_b_im, s5_c_re, s5_c_im, s5_d, sr0, si0)
    g = jax.nn.gelu(ys)
    out_b = _rms(g * jax.nn.sigmoid(g @ w_glu))
    yc, cb1 = short_conv(cg * hc, conv_w, cb0)
    out_c = _rms(bg * yc)
    cat = (jnp.concatenate([out_a, out_b, out_c], axis=-1) * mix_g.astype(jnp.float32)).astype(h.dtype)
    return cat @ w_out, (C1, n1, m1, sr1, si1, cb1)


def block(x, ffn1_w1, ffn1_w3, ffn1_w2, ffn2_w1, ffn2_w3, ffn2_w2, norm_g, w_in, ig_bias, fg_bias,
          s5_a_re, s5_a_im, s5_log_dt, s5_b_re, s5_b_im, s5_c_re, s5_c_im, s5_d, w_glu, conv_w, mix_g, w_out,
          C0, n0, m0, sr0, si0, cb0):
    x = x + 0.5 * rmsnorm(swiglu(rmsnorm(x, norm_g[0]), ffn1_w1, ffn1_w3, ffn1_w2), norm_g[1])
    mo, st = mixer(rmsnorm(x, norm_g[2]), w_in, ig_bias, fg_bias, s5_a_re, s5_a_im, s5_log_dt, s5_b_re, s5_b_im,
                   s5_c_re, s5_c_im, s5_d, w_glu, conv_w, mix_g, w_out, C0, n0, m0, sr0, si0, cb0)
    x = x + rmsnorm(mo, norm_g[3])
    x = x + 0.5 * rmsnorm(swiglu(rmsnorm(x, norm_g[4]), ffn2_w1, ffn2_w3, ffn2_w2), norm_g[5])
    return x, st


def run_trunk(x, C0, n0, m0, sr0, si0, cb0, weights):
    new = ([], [], [], [], [], [])
    for l in range(DEPTH):
        lw = [w[l] for w in weights]
        x, st = block(x, *lw, C0[l], n0[l], m0[l], sr0[l], si0[l], cb0[l])
        for lst, s in zip(new, st):
            lst.append(s)
    return x, [jnp.stack(s) for s in new]


def setup_inputs(seed: int = 0) -> dict:
    key = jax.random.key(seed)
    ks = jax.random.split(key, 32)
    f32 = jnp.float32
    nrm = lambda k, shp, s: jax.random.normal(k, shp, f32) * s
    L = DEPTH
    a_im = jnp.broadcast_to(jnp.pi * jnp.arange(S5_P, dtype=f32), (L, S5_G, S5_P))
    fg_base = jnp.broadcast_to(jnp.linspace(3.0, 6.0, MH, dtype=f32), (L, MH))
    return {
        "x_prompt": nrm(ks[0], (BATCH, SEQ, D_MODEL), 1.0),
        "x_sample": nrm(ks[1], (DEC_BATCH, DEC_SEQ, D_MODEL), 1.0),
        "state_mlstm_C": nrm(ks[2], (L, DEC_BATCH, MH, DH, DH), 0.1),
        "state_mlstm_n": nrm(ks[3], (L, DEC_BATCH, MH, DH), 0.1),
        "state_mlstm_m": nrm(ks[4], (L, DEC_BATCH, MH), 0.5),
        "state_s5_re": nrm(ks[5], (L, DEC_BATCH, S5_G, S5_P), 0.5),
        "state_s5_im": nrm(ks[6], (L, DEC_BATCH, S5_G, S5_P), 0.5),
        "state_conv": nrm(ks[7], (L, DEC_BATCH, CONV_W - 1, D_CONV), 1.0),
        "ffn1_w1": nrm(ks[8], (L, D_MODEL, D_FF), D_MODEL ** -0.5),
        "ffn1_w3": nrm(ks[9], (L, D_MODEL, D_FF), D_MODEL ** -0.5),
        "ffn1_w2": nrm(ks[10], (L, D_FF, D_MODEL), D_FF ** -0.5),
        "ffn2_w1": nrm(ks[11], (L, D_MODEL, D_FF), D_MODEL ** -0.5),
        "ffn2_w3": nrm(ks[12], (L, D_MODEL, D_FF), D_MODEL ** -0.5),
        "ffn2_w2": nrm(ks[13], (L, D_FF, D_MODEL), D_FF ** -0.5),
        "norm_g": 1.0 + nrm(ks[14], (L, 6, D_MODEL), 0.02),
        "w_in": nrm(ks[15], (L, D_MODEL, PROJ_W), D_MODEL ** -0.5),
        "ig_bias": nrm(ks[16], (L, MH), 0.1),
        "fg_bias": fg_base + nrm(ks[17], (L, MH), 0.1),
        "s5_a_re": -0.5 + nrm(ks[18], (L, S5_G, S5_P), 0.01),
        "s5_a_im": a_im + nrm(ks[19], (L, S5_G, S5_P), 0.01),
        "s5_log_dt": jax.random.uniform(ks[20], (L, S5_G), f32, math.log(0.001), math.log(0.1)),
        "s5_b_re": nrm(ks[21], (L, S5_G, S5_P, S5_CH), (2 * S5_CH) ** -0.5),
        "s5_b_im": nrm(ks[22], (L, S5_G, S5_P, S5_CH), (2 * S5_CH) ** -0.5),
        "s5_c_re": nrm(ks[23], (L, S5_G, S5_CH, S5_P), (2 * S5_P) ** -0.5),
        "s5_c_im": nrm(ks[24], (L, S5_G, S5_CH, S5_P), (2 * S5_P) ** -0.5),
        "s5_d": nrm(ks[25], (L, D_S5), 1.0),
        "w_glu": nrm(ks[26], (L, D_S5, D_S5), D_S5 ** -0.5),
        "conv_w": nrm(ks[27], (L, CONV_W, D_CONV), CONV_W ** -0.5),
        "mix_g": 1.0 + nrm(ks[28], (L, D_MIX), 0.02),
        "w_out": nrm(ks[29], (L, D_MIX, D_MODEL), D_MIX ** -0.5),
    }


def reference(x_prompt, x_sample, state_mlstm_C, state_mlstm_n, state_mlstm_m, state_s5_re, state_s5_im,
              state_conv, ffn1_w1, ffn1_w3, ffn1_w2, ffn2_w1, ffn2_w3, ffn2_w2, norm_g, w_in, ig_bias, fg_bias,
              s5_a_re, s5_a_im, s5_log_dt, s5_b_re, s5_b_im, s5_c_re, s5_c_im, s5_d, w_glu, conv_w, mix_g, w_out):
    weights = (ffn1_w1, ffn1_w3, ffn1_w2, ffn2_w1, ffn2_w3, ffn2_w2, norm_g, w_in, ig_bias, fg_bias,
               s5_a_re, s5_a_im, s5_log_dt, s5_b_re, s5_b_im, s5_c_re, s5_c_im, s5_d, w_glu, conv_w, mix_g, w_out)
    f32 = jnp.float32
    bp = x_prompt.shape[0]
    zC = jnp.zeros((DEPTH, bp, MH, DH, DH), f32)
    zn = jnp.zeros((DEPTH, bp, MH, DH), f32)
    zm = jnp.zeros((DEPTH, bp, MH), f32)
    zs = jnp.zeros((DEPTH, bp, S5_G, S5_P), f32)
    zc = jnp.zeros((DEPTH, bp, CONV_W - 1, D_CONV), x_prompt.dtype)
    y_prompt, p_st = run_trunk(x_prompt, zC, zn, zm, zs, zs, zc, weights)
    y_sample, s_st = run_trunk(x_sample, state_mlstm_C, state_mlstm_n, state_mlstm_m,
                               state_s5_re, state_s5_im, state_conv, weights)
    return (y_prompt, y_sample, p_st[0], p_st[1], p_st[2], p_st[3], p_st[4], p_st[5],
            s_st[0], s_st[1], s_st[2], s_st[3], s_st[4], s_st[5])
```

```python
import functools

import jax
import jax.numpy as jnp
from jax import lax
from jax.experimental import pallas as pl
from jax.experimental.pallas import tpu as pltpu

F32 = jnp.float32
BF16 = jnp.bfloat16

D_MODEL = 1024
MH = 4
DH = 128
D_MLSTM = MH * DH
S5_CH = 16
S5_G = 16
S5_P = 64
D_S5 = S5_G * S5_CH
S5_N = S5_G * S5_P
D_CONV = 256
CONV_W = 3
D_FF = 2816
EPS = 1e-6
GATE_W = 128
PROJ_PAD = 4 * D_MLSTM + GATE_W + D_S5 + 3 * D_CONV
NEG = -1e30
VMEM_LIMIT = 56 * 1024 * 1024


def _dot(a, b):
    return jnp.dot(a, b, preferred_element_type=F32)


def _dot_nt(a, b):
    return lax.dot_general(a, b, (((1,), (1,)), ((), ())), preferred_element_type=F32)


def _dot_tn(a, b):
    return lax.dot_general(a, b, (((0,), (0,)), ((), ())), preferred_element_type=F32)


def _dot_exact(a, b):
    return jnp.dot(a, b, preferred_element_type=F32, precision=lax.Precision.HIGHEST)


def _rms(x):
    return x * lax.rsqrt(jnp.mean(x * x, axis=-1, keepdims=True) + EPS)


def _log_sigmoid(x):
    return jnp.minimum(x, 0.0) - jnp.log1p(jnp.exp(-jnp.abs(x)))


def _ffn_residual(x, g_pre, g_post, w1, w3, w2):
    xn = (_rms(x) * g_pre).astype(BF16)
    h1 = _dot(xn, w1[...])
    h3 = _dot(xn, w3[...])
    a = (jax.nn.silu(h1) * h3).astype(BF16)
    y = _dot(a, w2[...])
    return x + 0.5 * (_rms(y) * g_post)


def _ffn_inproj_kernel(x_ref, g_ref, w1_ref, w3_ref, w2_ref, win_ref,
                       x1_ref, qkvo_ref, gate_ref, u_ref, bch_ref):
    x = x_ref[...]
    x1 = _ffn_residual(x, g_ref[0:1, :], g_ref[1:2, :], w1_ref, w3_ref, w2_ref)
    x1_ref[...] = x1
    hn = (_rms(x1) * g_ref[2:3, :]).astype(BF16)
    proj = _dot(hn, win_ref[...])
    o = 4 * D_MLSTM
    qkvo_ref[...] = proj[:, :o]
    gate_ref[...] = proj[:, o:o + GATE_W]
    u_ref[...] = proj[:, o + GATE_W:o + GATE_W + D_S5]
    bch_ref[...] = proj[:, o + GATE_W + D_S5:]


def _const_spec(shape):
    nd = len(shape)
    return pl.BlockSpec(shape, lambda *_: (0,) * nd, pipeline_mode=pl.Buffered(1))


def _ffn_inproj(x, g, w1, w3, w2, win, tm):
    t = x.shape[0]
    row = lambda w: pl.BlockSpec((tm, w), lambda i: (i, 0))
    return pl.pallas_call(
        _ffn_inproj_kernel,
        grid=(t // tm,),
        in_specs=[row(D_MODEL), _const_spec(g.shape), _const_spec(w1.shape), _const_spec(w3.shape),
                  _const_spec(w2.shape), _const_spec(win.shape)],
        out_specs=[row(D_MODEL), row(4 * D_MLSTM), row(GATE_W), row(D_S5), row(3 * D_CONV)],
        out_shape=[jax.ShapeDtypeStruct((t, D_MODEL), F32), jax.ShapeDtypeStruct((t, 4 * D_MLSTM), F32),
                   jax.ShapeDtypeStruct((t, GATE_W), F32), jax.ShapeDtypeStruct((t, D_S5), F32),
                   jax.ShapeDtypeStruct((t, 3 * D_CONV), F32)],
        compiler_params=pltpu.CompilerParams(dimension_semantics=("arbitrary",), vmem_limit_bytes=VMEM_LIMIT),
        name="ffn_inproj",
    )(x, g, w1, w3, w2, win)


def _outproj_ffn_kernel(x_ref, a_ref, b_ref, c_ref, g_ref, wo_ref, w1_ref, w3_ref, w2_ref, y_ref):
    mo = (_dot(a_ref[...].astype(BF16), wo_ref[0:D_MLSTM, :])
          + _dot(b_ref[...].astype(BF16), wo_ref[D_MLSTM:D_MLSTM + D_S5, :])
          + _dot(c_ref[...].astype(BF16), wo_ref[D_MLSTM + D_S5:, :]))
    x2 = x_ref[...] + _rms(mo) * g_ref[3:4, :]
    y_ref[...] = _ffn_residual(x2, g_ref[4:5, :], g_ref[5:6, :], w1_ref, w3_ref, w2_ref)


def _outproj_ffn(x, oa, ob, oc, g, wo, w1, w3, w2, tm):
    t = x.shape[0]
    row = lambda w: pl.BlockSpec((tm, w), lambda i: (i, 0))
    return pl.pallas_call(
        _outproj_ffn_kernel,
        grid=(t // tm,),
        in_specs=[row(D_MODEL), row(D_MLSTM), row(D_S5), row(D_CONV), _const_spec(g.shape),
                  _const_spec(wo.shape), _const_spec(w1.shape), _const_spec(w3.shape), _const_spec(w2.shape)],
        out_specs=row(D_MODEL),
        out_shape=jax.ShapeDtypeStruct((t, D_MODEL), F32),
        compiler_params=pltpu.CompilerParams(dimension_semantics=("arbitrary",), vmem_limit_bytes=VMEM_LIMIT),
        name="outproj_ffn",
    )(x, oa, ob, oc, g, wo, w1, w3, w2)


def _mlstm_conv_kernel(q_ref, k_ref, v_ref, o_ref, gate_ref, bg_ref, cg_ref, hc_ref,
                       gb_ref, gbt_ref, cw_ref, mg_ref, c0_ref, n0_ref, m0_ref, cb0_ref,
                       oa_ref, oc_ref, c_ref, n_ref, m_ref, cb_ref, *, bb, chunk, s_valid, nc):
    ci = pl.program_id(1)

    @pl.when(ci == 0)
    def _():
        c_ref[...] = c0_ref[...]
        n_ref[...] = n0_ref[...]
        m_ref[...] = m0_ref[...]
        cb_ref[...] = cb0_ref[...]

    padded = s_valid < nc * chunk
    r_last = s_valid - (nc - 1) * chunk - 1
    assert r_last >= 1 and (nc == 1 or not padded)
    row = lax.broadcasted_iota(jnp.int32, (chunk, chunk), 0)
    col = lax.broadcasted_iota(jnp.int32, (chunk, chunk), 1)
    causal = col <= row
    tril = causal.astype(F32)
    triu = (row <= col).astype(F32)
    sel = (lax.broadcasted_iota(jnp.int32, (8, GATE_W), 0)
           == lax.broadcasted_iota(jnp.int32, (8, GATE_W), 1)).astype(F32)
    rowi = lax.broadcasted_iota(jnp.int32, (chunk, D_CONV), 0)
    scale = DH ** -0.5

    def per_batch(bi, carry):
        graw = gate_ref[bi]
        g_c = graw + gb_ref[...]
        lf_c = _log_sigmoid(g_c)
        g_r = lax.dot_general(sel, graw, (((1,), (1,)), ((), ())), preferred_element_type=F32,
                              precision=lax.Precision.HIGHEST) + gbt_ref[...]
        lf_r = _log_sigmoid(g_r)
        if padded:
            vc = ci * chunk + lax.broadcasted_iota(jnp.int32, (chunk, GATE_W), 0) < s_valid
            vr = ci * chunk + lax.broadcasted_iota(jnp.int32, (8, chunk), 1) < s_valid
            g_c = jnp.where(vc, g_c, NEG)
            lf_c = jnp.where(vc, lf_c, 0.0)
            g_r = jnp.where(vr, g_r, NEG)
            lf_r = jnp.where(vr, lf_r, 0.0)
        bt_c = _dot_exact(tril, lf_c)
        bt_r = _dot_exact(lf_r, triu)
        for h in range(MH):
            hs = slice(h * DH, (h + 1) * DH)
            it_row = g_r[h:h + 1, :]
            bt_row = bt_r[MH + h:MH + h + 1, :]
            it_col = g_c[:, h:h + 1]
            bt_col = bt_c[:, MH + h:MH + h + 1]
            m_prev = m_ref[bi, h:h + 1, 0:1]
            dmat = jnp.where(causal, bt_col - bt_row + it_row, NEG)
            inter = bt_col + m_prev
            mt = jnp.maximum(jnp.max(dmat, axis=-1, keepdims=True), inter)
            w = jnp.exp(dmat - mt)
            s_inter = jnp.exp(inter - mt)
            qh = q_ref[bi, :, hs]
            kh = k_ref[bi, :, hs] * scale
            vh = v_ref[bi, :, hs]
            qb = qh.astype(BF16)
            kb = kh.astype(BF16)
            sqk = _dot_nt(qb, kb) * w
            c_old = c_ref[bi, h]
            n_old = n_ref[bi, h:h + 1, :]
            num = _dot(sqk.astype(BF16), vh.astype(BF16)) + s_inter * _dot_nt(qb, c_old.astype(BF16))
            den = (jnp.sum(sqk, axis=-1, keepdims=True)
                   + s_inter * jnp.sum(qh * n_old, axis=-1, keepdims=True))
            hval = num / jnp.maximum(jnp.abs(den), jnp.exp(-mt))
            hm = jax.nn.sigmoid(o_ref[bi, :, hs]) * hval
            oa_ref[bi, :, hs] = _rms(hm) * mg_ref[:, hs]
            m_new = mt[chunk - 1:chunk, :]
            bt_last = bt_col[chunk - 1:chunk, :]
            wk_col = jnp.exp(bt_last - bt_col + it_col - m_new)
            decay = jnp.exp(bt_last + m_prev - m_new)
            vw = (vh * wk_col).astype(BF16)
            c_ref[bi, h] = decay * c_old + _dot_tn(vw, kb)
            n_ref[bi, h:h + 1, :] = decay * n_old + jnp.sum(kh * wk_col, axis=0, keepdims=True)
            m_ref[bi, h:h + 1, :] = jnp.broadcast_to(m_new, (1, DH))
        z = cg_ref[bi] * hc_ref[bi]
        cb = cb_ref[bi]
        z1 = jnp.where(rowi == 0, cb[1:2, :], pltpu.roll(z, 1, 0))
        z2 = jnp.where(rowi == 0, cb[0:1, :], jnp.where(rowi == 1, cb[1:2, :], pltpu.roll(z, 2, 0)))
        yc = cw_ref[0:1, :] * z2 + cw_ref[1:2, :] * z1 + cw_ref[2:3, :] * z
        oc_ref[bi] = _rms(bg_ref[bi] * yc) * mg_ref[:, D_MLSTM + D_S5:]
        cb_ref[bi] = z[r_last - 1:r_last + 1, :]
        return carry

    lax.fori_loop(0, bb, per_batch, 0)


def _mlstm_conv(qkvo, gate, bch, gb, gbt, cw, mg, c0, n0, m0, cb0, *, bb, chunk, s_valid):
    b, s, _ = qkvo.shape
    nc = s // chunk
    kern = functools.partial(_mlstm_conv_kernel, bb=bb, chunk=chunk, s_valid=s_valid, nc=nc)
    col = lambda w, j: pl.BlockSpec((bb, chunk, w), lambda i, c: (i, c, j))
    st = lambda shape: pl.BlockSpec((bb,) + shape, lambda i, c: (i,) + (0,) * len(shape))
    cst = lambda a: pl.BlockSpec(a.shape, lambda i, c: (0,) * a.ndim)
    return pl.pallas_call(
        kern,
        grid=(b // bb, nc),
        in_specs=[col(D_MLSTM, 0), col(D_MLSTM, 1), col(D_MLSTM, 2), col(D_MLSTM, 3), col(GATE_W, 0),
                  col(D_CONV, 0), col(D_CONV, 1), col(D_CONV, 2),
                  cst(gb), cst(gbt), cst(cw), cst(mg),
                  st((MH, DH, DH)), st((MH, DH)), st((MH, DH)), st((CONV_W - 1, D_CONV))],
        out_specs=[col(D_MLSTM, 0), col(D_CONV, 0),
                   st((MH, DH, DH)), st((MH, DH)), st((MH, DH)), st((CONV_W - 1, D_CONV))],
        out_shape=[jax.ShapeDtypeStruct((b, s, D_MLSTM), F32), jax.ShapeDtypeStruct((b, s, D_CONV), F32),
                   jax.ShapeDtypeStruct((b, MH, DH, DH), F32), jax.ShapeDtypeStruct((b, MH, DH), F32),
                   jax.ShapeDtypeStruct((b, MH, DH), F32), jax.ShapeDtypeStruct((b, CONV_W - 1, D_CONV), F32)],
        compiler_params=pltpu.CompilerParams(dimension_semantics=("arbitrary", "arbitrary"),
                                             vmem_limit_bytes=VMEM_LIMIT),
        name="mlstm_conv",
    )(qkvo, qkvo, qkvo, qkvo, gate, bch, bch, bch, gb, gbt, cw, mg, c0, n0, m0, cb0)


def _s5_kernel(u_ref, lam_ref, bblk_ref, cblk_ref, d_ref, wglu_ref, mg_ref, h0_ref,
               ob_ref, h_ref, xs_ref, a_ref, *, nb, tt):
    ti = pl.program_id(0)
    n = S5_N

    @pl.when(ti == 0)
    def _():
        h_ref[...] = h0_ref[...]

    lre = jnp.minimum(lam_ref[0:1, :], -1e-4)
    lim = lam_ref[1:2, :]
    dt = jnp.exp(lam_ref[2:3, :])
    mag = jnp.exp(lre * dt)
    ab_re = mag * jnp.cos(lim * dt)
    ab_im = mag * jnp.sin(lim * dt)
    den = lre * lre + lim * lim
    nr = ab_re - 1.0
    fre = (nr * lre + ab_im * lim) / den
    fim = (ab_im * lre - nr * lim) / den
    a_ref[:, :n] = jnp.broadcast_to(ab_re, (8, n))
    a_ref[:, n:] = jnp.broadcast_to(ab_im, (8, n))

    u = u_ref[...]
    bu = _dot(u.astype(BF16), bblk_ref[...])
    bu_re = bu[:, :n]
    bu_im = bu[:, n:]
    xs_ref[:, :n] = fre * bu_re - fim * bu_im
    xs_ref[:, n:] = fre * bu_im + fim * bu_re

    lc = 512
    for rg in range(nb // 8):
        rs = slice(rg * 8, (rg + 1) * 8)
        for c0 in range(0, n, lc):
            ar = a_ref[:, c0:c0 + lc]
            ai = a_ref[:, n + c0:n + c0 + lc]

            def step(t, hc, rg=rg, c0=c0, ar=ar, ai=ai):
                hr, hi = hc
                r0 = pl.multiple_of(t * nb + rg * 8, 8)
                xr = xs_ref[pl.ds(r0, 8), c0:c0 + lc]
                xi = xs_ref[pl.ds(r0, 8), n + c0:n + c0 + lc]
                nr_ = ar * hr - ai * hi + xr
                ni_ = ar * hi + ai * hr + xi
                xs_ref[pl.ds(r0, 8), c0:c0 + lc] = nr_
                xs_ref[pl.ds(r0, 8), n + c0:n + c0 + lc] = ni_
                return nr_, ni_

            hr, hi = lax.fori_loop(0, tt, step, (h_ref[rs, c0:c0 + lc], h_ref[rs, n + c0:n + c0 + lc]))
            h_ref[rs, c0:c0 + lc] = hr
            h_ref[rs, n + c0:n + c0 + lc] = hi

    y = _dot(xs_ref[...].astype(BF16), cblk_ref[...]) + d_ref[...] * u
    g = jax.nn.gelu(y)
    ob = g * jax.nn.sigmoid(_dot(g.astype(BF16), wglu_ref[...]))
    ob_ref[...] = _rms(ob) * mg_ref[:, D_MLSTM:D_MLSTM + D_S5]


def _s5(u_tm, lam, bblk, cblk, d, wglu, mg, h0, *, nb, tt):
    rows = u_tm.shape[0]
    rt = nb * tt
    kern = functools.partial(_s5_kernel, nb=nb, tt=tt)
    cst = lambda a: pl.BlockSpec(a.shape, lambda i: (0,) * a.ndim)
    return pl.pallas_call(
        kern,
        grid=(rows // rt,),
        in_specs=[pl.BlockSpec((rt, D_S5), lambda i: (i, 0)), cst(lam), cst(bblk), cst(cblk), cst(d),
                  cst(wglu), cst(mg), cst(h0)],
        out_specs=[pl.BlockSpec((rt, D_S5), lambda i: (i, 0)), cst(h0)],
        out_shape=[jax.ShapeDtypeStruct((rows, D_S5), F32), jax.ShapeDtypeStruct(h0.shape, F32)],
        scratch_shapes=[pltpu.VMEM((rt, 2 * S5_N), F32), pltpu.VMEM((8, 2 * S5_N), F32)],
        compiler_params=pltpu.CompilerParams(dimension_semantics=("arbitrary",), vmem_limit_bytes=VMEM_LIMIT),
        name="s5",
    )(u_tm, lam, bblk, cblk, d, wglu, mg, h0)


def _block_diag(w):
    g, r, c = w.shape
    eye = jnp.eye(g, dtype=w.dtype)
    return jnp.einsum("grc,gh->grhc", w, eye).reshape(g * r, g * c)


def _layer_weights(l, ffn1_w1, ffn1_w3, ffn1_w2, ffn2_w1, ffn2_w3, ffn2_w2, norm_g, w_in, ig_bias, fg_bias,
                   s5_a_re, s5_a_im, s5_log_dt, s5_b_re, s5_b_im, s5_c_re, s5_c_im, s5_d, w_glu, conv_w,
                   mix_g, w_out):
    o = 4 * D_MLSTM
    win = w_in[l]
    win_pad = jnp.concatenate(
        [win[:, :o], win[:, o:o + 2 * MH], jnp.zeros((D_MODEL, GATE_W - 2 * MH), F32), win[:, o + 2 * MH:]],
        axis=1).astype(BF16)
    gb = jnp.concatenate([ig_bias[l], fg_bias[l], jnp.zeros((GATE_W - 2 * MH,), F32)])
    lam = jnp.stack([s5_a_re[l].reshape(-1), s5_a_im[l].reshape(-1),
                     jnp.repeat(s5_log_dt[l], S5_P)])
    bblk = jnp.concatenate([_block_diag(jnp.swapaxes(s5_b_re[l], 1, 2)),
                            _block_diag(jnp.swapaxes(s5_b_im[l], 1, 2))], axis=1).astype(BF16)
    cblk = jnp.concatenate([_block_diag(jnp.swapaxes(s5_c_re[l], 1, 2)),
                            -_block_diag(jnp.swapaxes(s5_c_im[l], 1, 2))], axis=0).astype(BF16)
    return dict(
        f1=(ffn1_w1[l].astype(BF16), ffn1_w3[l].astype(BF16), ffn1_w2[l].astype(BF16)),
        f2=(ffn2_w1[l].astype(BF16), ffn2_w3[l].astype(BF16), ffn2_w2[l].astype(BF16)),
        g=norm_g[l], win=win_pad, gb=gb.reshape(1, GATE_W), gbt=gb[:8].reshape(8, 1),
        lam=lam, bblk=bblk, cblk=cblk, d=s5_d[l].reshape(1, D_S5), wglu=w_glu[l].astype(BF16),
        cw=conv_w[l], mg=mix_g[l].reshape(1, D_MODEL), wo=w_out[l].astype(BF16))


def _run_group(x, states, weights, *, tm, bb, chunk, tt):
    b, s, _ = x.shape
    s_pad = -(-s // chunk) * chunk
    c_all, n_all, m_all, sr_all, si_all, cb_all = states
    xf = x.reshape(b * s, D_MODEL)
    new = [[] for _ in range(6)]
    for l, w in enumerate(weights):
        x1, qkvo, gate, u, bch = _ffn_inproj(xf, w["g"], *w["f1"], w["win"], tm)
        r3 = lambda a: a.reshape(b, s, a.shape[-1])
        pad = lambda a: jnp.pad(r3(a), ((0, 0), (0, s_pad - s), (0, 0))) if s_pad != s else r3(a)
        m0 = jnp.broadcast_to(m_all[l][:, :, None], (b, MH, DH))
        oa, oc, c1, n1, m1, cb1 = _mlstm_conv(
            pad(qkvo), pad(gate), pad(bch), w["gb"], w["gbt"], w["cw"], w["mg"],
            c_all[l], n_all[l], m0, cb_all[l], bb=bb, chunk=chunk, s_valid=s)
        u_tm = jnp.swapaxes(r3(u), 0, 1).reshape(s * b, D_S5)
        h0 = jnp.concatenate([sr_all[l].reshape(b, S5_N), si_all[l].reshape(b, S5_N)], axis=1)
        ob_tm, h1 = _s5(u_tm, w["lam"], w["bblk"], w["cblk"], w["d"], w["wglu"], w["mg"], h0, nb=b, tt=tt)
        ob = jnp.swapaxes(ob_tm.reshape(s, b, D_S5), 0, 1).reshape(b * s, D_S5)
        xf = _outproj_ffn(x1, oa[:, :s].reshape(b * s, D_MLSTM), ob, oc[:, :s].reshape(b * s, D_CONV),
                          w["g"], w["wo"], *w["f2"], tm)
        for lst, v in zip(new, (c1, n1, m1[:, :, 0], h1[:, :S5_N].reshape(b, S5_G, S5_P),
                                h1[:, S5_N:].reshape(b, S5_G, S5_P), cb1)):
            lst.append(v)
    return xf.reshape(b, s, D_MODEL), [jnp.stack(v) for v in new]


def kernel(x_prompt, x_sample, state_mlstm_C, state_mlstm_n, state_mlstm_m, state_s5_re, state_s5_im,
           state_conv, ffn1_w1, ffn1_w3, ffn1_w2, ffn2_w1, ffn2_w3, ffn2_w2, norm_g, w_in, ig_bias, fg_bias,
           s5_a_re, s5_a_im, s5_log_dt, s5_b_re, s5_b_im, s5_c_re, s5_c_im, s5_d, w_glu, conv_w, mix_g, w_out):
    depth = norm_g.shape[0]
    params = (ffn1_w1, ffn1_w3, ffn1_w2, ffn2_w1, ffn2_w3, ffn2_w2, norm_g, w_in, ig_bias, fg_bias,
              s5_a_re, s5_a_im, s5_log_dt, s5_b_re, s5_b_im, s5_c_re, s5_c_im, s5_d, w_glu, conv_w, mix_g, w_out)
    weights = [_layer_weights(l, *params) for l in range(depth)]
    bp = x_prompt.shape[0]
    zeros = (jnp.zeros((depth, bp, MH, DH, DH), F32), jnp.zeros((depth, bp, MH, DH), F32),
             jnp.zeros((depth, bp, MH), F32), jnp.zeros((depth, bp, S5_G, S5_P), F32),
             jnp.zeros((depth, bp, S5_G, S5_P), F32), jnp.zeros((depth, bp, CONV_W - 1, D_CONV), F32))
    y_p, st_p = _run_group(x_prompt, zeros, weights, tm=512, bb=1, chunk=128, tt=64)
    y_s, st_s = _run_group(
        x_sample, (state_mlstm_C, state_mlstm_n, state_mlstm_m, state_s5_re, state_s5_im, state_conv),
        weights, tm=512, bb=8, chunk=8, tt=x_sample.shape[1])
    return (y_p, y_s, *st_p, *st_s)
```

```python
import functools

import jax
import jax.numpy as jnp
from jax import lax
from jax.experimental import pallas as pl
from jax.experimental.pallas import tpu as pltpu

F32 = jnp.float32
BF16 = jnp.bfloat16

D_MODEL = 1024
MH = 4
DH = 128
D_MLSTM = MH * DH
S5_CH = 16
S5_G = 16
S5_P = 64
D_S5 = S5_G * S5_CH
S5_N = S5_G * S5_P
D_CONV = 256
CONV_W = 3
EPS = 1e-6
GATE_W = 128
SUBLANES = 8
NEG = -1e30
VMEM_LIMIT = 56 * 1024 * 1024


def _dot(a, b):
    return jnp.dot(a, b, preferred_element_type=F32)


def _dot_nt(a, b):
    return lax.dot_general(a, b, (((1,), (1,)), ((), ())), preferred_element_type=F32)


def _dot_tn(a, b):
    return lax.dot_general(a, b, (((0,), (0,)), ((), ())), preferred_element_type=F32)


def _dot_exact(a, b):
    return jnp.dot(a, b, preferred_element_type=F32, precision=lax.Precision.HIGHEST)


def _dot_nt_exact(a, b):
    return lax.dot_general(a, b, (((1,), (1,)), ((), ())), preferred_element_type=F32,
                           precision=lax.Precision.HIGHEST)


def _rms(x):
    return x * lax.rsqrt(jnp.mean(x * x, axis=-1, keepdims=True) + EPS)


def _log_sigmoid(x):
    return jnp.minimum(x, 0.0) - jnp.log1p(jnp.exp(-jnp.abs(x)))


def _ffn_residual(x, g_pre, g_post, w1, w3, w2):
    xn = (_rms(x) * g_pre).astype(BF16)
    h1 = _dot(xn, w1[...])
    h3 = _dot(xn, w3[...])
    a = (jax.nn.silu(h1) * h3).astype(BF16)
    y = _dot(a, w2[...])
    return x + 0.5 * (_rms(y) * g_post)


def _const_spec(shape):
    nd = len(shape)
    return pl.BlockSpec(shape, lambda *_: (0,) * nd, pipeline_mode=pl.Buffered(1))


_ANY = pl.BlockSpec(memory_space=pl.ANY)


def _ffn_inproj_kernel(x_ref, g_ref, w1_ref, w3_ref, w2_ref, win_ref,
                       x1_ref, qkvo_ref, gate_ref, u_ref, bch_ref):
    x = x_ref[...]
    x1 = _ffn_residual(x, g_ref[0:1, :], g_ref[1:2, :], w1_ref, w3_ref, w2_ref)
    x1_ref[...] = x1
    hn = (_rms(x1) * g_ref[2:3, :]).astype(BF16)
    proj = _dot(hn, win_ref[...])
    o = 4 * D_MLSTM
    qkvo_ref[...] = proj[:, :o]
    gate_ref[...] = proj[:, o:o + GATE_W]
    u_ref[...] = proj[:, o + GATE_W:o + GATE_W + D_S5]
    bch_ref[...] = proj[:, o + GATE_W + D_S5:]


def _ffn_inproj(x, g, w1, w3, w2, win, tm):
    t = x.shape[0]
    row = lambda w: pl.BlockSpec((tm, w), lambda i: (i, 0))
    return pl.pallas_call(
        _ffn_inproj_kernel,
        grid=(t // tm,),
        in_specs=[row(D_MODEL), _const_spec(g.shape), _const_spec(w1.shape), _const_spec(w3.shape),
                  _const_spec(w2.shape), _const_spec(win.shape)],
        out_specs=[row(D_MODEL), row(4 * D_MLSTM), row(GATE_W), row(D_S5), row(3 * D_CONV)],
        out_shape=[jax.ShapeDtypeStruct((t, D_MODEL), F32), jax.ShapeDtypeStruct((t, 4 * D_MLSTM), F32),
                   jax.ShapeDtypeStruct((t, GATE_W), F32), jax.ShapeDtypeStruct((t, D_S5), F32),
                   jax.ShapeDtypeStruct((t, 3 * D_CONV), F32)],
        compiler_params=pltpu.CompilerParams(dimension_semantics=("arbitrary",), vmem_limit_bytes=VMEM_LIMIT),
        name="ffn_inproj",
    )(x, g, w1, w3, w2, win)


def _outproj_ffn_kernel(x_ref, a_ref, b_ref, c_ref, g_ref, wo_ref, w1_ref, w3_ref, w2_ref, y_ref):
    mo = (_dot(a_ref[...].astype(BF16), wo_ref[0:D_MLSTM, :])
          + _dot(b_ref[...].astype(BF16), wo_ref[D_MLSTM:D_MLSTM + D_S5, :])
          + _dot(c_ref[...].astype(BF16), wo_ref[D_MLSTM + D_S5:, :]))
    x2 = x_ref[...] + _rms(mo) * g_ref[3:4, :]
    y_ref[...] = _ffn_residual(x2, g_ref[4:5, :], g_ref[5:6, :], w1_ref, w3_ref, w2_ref)


def _outproj_ffn(x, oa, ob, oc, g, wo, w1, w3, w2, tm):
    t = x.shape[0]
    row = lambda w: pl.BlockSpec((tm, w), lambda i: (i, 0))
    return pl.pallas_call(
        _outproj_ffn_kernel,
        grid=(t // tm,),
        in_specs=[row(D_MODEL), row(D_MLSTM), row(D_S5), row(D_CONV), _const_spec(g.shape),
                  _const_spec(wo.shape), _const_spec(w1.shape), _const_spec(w3.shape), _const_spec(w2.shape)],
        out_specs=row(D_MODEL),
        out_shape=jax.ShapeDtypeStruct((t, D_MODEL), F32),
        compiler_params=pltpu.CompilerParams(dimension_semantics=("arbitrary",), vmem_limit_bytes=VMEM_LIMIT),
        name="outproj_ffn",
    )(x, oa, ob, oc, g, wo, w1, w3, w2)


def _state_shapes(depth, b):
    return [jax.ShapeDtypeStruct((depth, b, MH, DH, DH), F32), jax.ShapeDtypeStruct((depth, b, MH, DH), F32),
            jax.ShapeDtypeStruct((depth, b, MH), F32), jax.ShapeDtypeStruct((depth, b, CONV_W - 1, D_CONV), F32)]


def _gate_terms(graw, gb_ref, gbt_ref, tril, triu):
    sel = (lax.broadcasted_iota(jnp.int32, (SUBLANES, GATE_W), 0)
           == lax.broadcasted_iota(jnp.int32, (SUBLANES, GATE_W), 1)).astype(F32)
    g_c = graw + gb_ref[...]
    lf_c = _log_sigmoid(g_c)
    g_r = _dot_nt_exact(sel, graw) + gbt_ref[...]
    lf_r = _log_sigmoid(g_r)
    return g_c, lf_c, _dot_exact(tril, lf_c), g_r, _dot_exact(lf_r, triu)


def _head_scores(h, g_c, bt_c, g_r, bt_r, m_prev, mask):
    it_row = g_r[h:h + 1, :]
    bt_row = bt_r[MH + h:MH + h + 1, :]
    it_col = g_c[:, h:h + 1]
    bt_col = bt_c[:, MH + h:MH + h + 1]
    dmat = jnp.where(mask, bt_col - bt_row + it_row, NEG)
    inter = bt_col + m_prev
    mt = jnp.maximum(jnp.max(dmat, axis=-1, keepdims=True), inter)
    return jnp.exp(dmat - mt), jnp.exp(inter - mt), mt, it_col, bt_col


def _head_output(sqk, vb, s_inter, qc, qn, mt, o, mg):
    num = _dot(sqk.astype(BF16), vb) + s_inter * qc
    den = jnp.sum(sqk, axis=-1, keepdims=True) + s_inter * qn
    hm = jax.nn.sigmoid(o) * (num / jnp.maximum(jnp.abs(den), jnp.exp(-mt)))
    return _rms(hm) * mg


def _mlstm_conv_kernel(*refs, layer, bb, chunk, zero_state):
    (q_ref, k_ref, v_ref, o_ref, gate_ref, bg_ref, cg_ref, hc_ref, gb_ref, gbt_ref, cw_ref, mg_ref) = refs[:12]
    refs = refs[12:]
    if not zero_state:
        c0_ref, n0_ref, m0_ref, cb0_ref = refs[:4]
        refs = refs[4:]
    if layer > 0:
        refs = refs[4:]
    oa_ref, oc_ref, c_ref, n_ref, m_ref, cb_ref = refs
    bi0 = pl.program_id(0) * bb
    ci = pl.program_id(1)

    @pl.when(ci == 0)
    def _():
        if zero_state:
            c_ref[...] = jnp.zeros_like(c_ref)
            n_ref[...] = jnp.zeros_like(n_ref)
            cb_ref[...] = jnp.zeros_like(cb_ref)
        else:
            c_ref[...] = c0_ref[...]
            n_ref[...] = n0_ref[...]
            cb_ref[...] = cb0_ref[...]

    @pl.when((ci == 0) & (bi0 == 0))
    def _():
        m_ref[...] = jnp.zeros_like(m_ref) if zero_state else m0_ref[...]

    row = lax.broadcasted_iota(jnp.int32, (chunk, chunk), 0)
    col = lax.broadcasted_iota(jnp.int32, (chunk, chunk), 1)
    causal = col <= row
    tril = causal.astype(F32)
    triu = (row <= col).astype(F32)
    rowi = lax.broadcasted_iota(jnp.int32, (chunk, D_CONV), 0)
    scale = DH ** -0.5

    for bi in range(bb):
        g_c, _, bt_c, g_r, bt_r = _gate_terms(gate_ref[bi], gb_ref, gbt_ref, tril, triu)
        for h in range(MH):
            hs = slice(h * DH, (h + 1) * DH)
            m_prev = m_ref[pl.ds(bi0 + bi, 1), h:h + 1]
            w, s_inter, mt, it_col, bt_col = _head_scores(h, g_c, bt_c, g_r, bt_r, m_prev, causal)
            qh = q_ref[bi, :, hs]
            kh = k_ref[bi, :, hs] * scale
            vh = v_ref[bi, :, hs]
            qb = qh.astype(BF16)
            kb = kh.astype(BF16)
            sqk = _dot_nt(qb, kb) * w
            c_old = c_ref[bi, h]
            n_old = n_ref[bi, h:h + 1, :]
            qc = _dot_nt(qb, c_old.astype(BF16))
            qn = jnp.sum(qh * n_old, axis=-1, keepdims=True)
            oa_ref[bi, :, hs] = _head_output(sqk, vh.astype(BF16), s_inter, qc, qn, mt,
                                             o_ref[bi, :, hs], mg_ref[:, hs])
            m_new = mt[chunk - 1:chunk, :]
            bt_last = bt_col[chunk - 1:chunk, :]
            wk_col = jnp.exp(bt_last - bt_col + it_col - m_new)
            decay = jnp.exp(bt_last + m_prev - m_new)
            c_ref[bi, h] = decay * c_old + _dot_tn((vh * wk_col).astype(BF16), kb)
            n_ref[bi, h:h + 1, :] = decay * n_old + jnp.sum(kh * wk_col, axis=0, keepdims=True)
            m_ref[pl.ds(bi0 + bi, 1), h:h + 1] = m_new
        z = cg_ref[bi] * hc_ref[bi]
        cb = cb_ref[bi]
        z1 = jnp.where(rowi == 0, cb[1:2, :], pltpu.roll(z, 1, 0))
        z2 = jnp.where(rowi == 0, cb[0:1, :], jnp.where(rowi == 1, cb[1:2, :], pltpu.roll(z, 2, 0)))
        yc = cw_ref[0:1, :] * z2 + cw_ref[1:2, :] * z1 + cw_ref[2:3, :] * z
        oc_ref[bi] = _rms(bg_ref[bi] * yc) * mg_ref[:, D_MLSTM + D_S5:]
        cb_ref[bi] = z[chunk - 2:chunk, :]


def _mlstm_conv(qkvo, gate, bch, w, states_in, states_prev, *, layer, depth, bb, chunk):
    b, s, _ = qkvo.shape
    assert s % chunk == 0 and b % bb == 0
    zero_state = states_in is None
    kern = functools.partial(_mlstm_conv_kernel, layer=layer, bb=bb, chunk=chunk, zero_state=zero_state)
    col = lambda wd, j: pl.BlockSpec((bb, chunk, wd), lambda i, c: (i, c, j))
    st = lambda shape: pl.BlockSpec((None, bb) + shape, lambda i, c: (layer, i) + (0,) * len(shape))
    m_spec = pl.BlockSpec((None, b, MH), lambda i, c: (layer, 0, 0))
    cst = lambda a: pl.BlockSpec(a.shape, lambda i, c: (0,) * a.ndim)
    state_specs = [st((MH, DH, DH)), st((MH, DH)), m_spec, st((CONV_W - 1, D_CONV))]
    consts = (w["gb"], w["gbt"], w["cw"], w["mg"])
    args = [qkvo, qkvo, qkvo, qkvo, gate, bch, bch, bch, *consts]
    in_specs = [col(D_MLSTM, 0), col(D_MLSTM, 1), col(D_MLSTM, 2), col(D_MLSTM, 3), col(GATE_W, 0),
                col(D_CONV, 0), col(D_CONV, 1), col(D_CONV, 2)] + [cst(a) for a in consts]
    if not zero_state:
        args += list(states_in)
        in_specs += state_specs
    aliases = {}
    if layer > 0:
        aliases = {len(args) + j: 2 + j for j in range(4)}
        args += list(states_prev)
        in_specs += [_ANY] * 4
    return pl.pallas_call(
        kern,
        grid=(b // bb, s // chunk),
        in_specs=in_specs,
        out_specs=[col(D_MLSTM, 0), col(D_CONV, 0)] + state_specs,
        out_shape=[jax.ShapeDtypeStruct((b, s, D_MLSTM), F32), jax.ShapeDtypeStruct((b, s, D_CONV), F32)]
        + _state_shapes(depth, b),
        input_output_aliases=aliases,
        compiler_params=pltpu.CompilerParams(dimension_semantics=("arbitrary", "arbitrary"),
                                             vmem_limit_bytes=VMEM_LIMIT),
        name="mlstm_conv",
    )(*args)


def _bcast_block_last(x, t, s):
    out = x
    for d in range(1, s):
        out = jnp.where(t == s - 1 - d, pltpu.roll(x, x.shape[0] - d, 0), out)
    return out


def _mlstm_conv_short_kernel(*refs, layer, nb, s):
    (q_ref, k_ref, v_ref, o_ref, gate_ref, bg_ref, cg_ref, hc_ref, gb_ref, gbt_ref, cw_ref, mg_ref,
     c0_ref, n0_ref, m0_ref, cb0_ref) = refs[:16]
    refs = refs[16:]
    if layer > 0:
        refs = refs[4:]
    oa_ref, oc_ref, c_ref, n_ref, m_ref, cb_ref, qc_scr, st_scr = refs
    r = nb * s
    per_tile = SUBLANES // s
    shift = s.bit_length() - 1
    row = lax.broadcasted_iota(jnp.int32, (r, r), 0)
    col = lax.broadcasted_iota(jnp.int32, (r, r), 1)
    same = (row >> shift) == (col >> shift)
    mask = same & (col <= row)
    tril = mask.astype(F32)
    triu = (same & (row <= col)).astype(F32)
    expand = ((lax.broadcasted_iota(jnp.int32, (r, nb), 0) >> shift)
              == lax.broadcasted_iota(jnp.int32, (r, nb), 1)).astype(F32)
    gather = (lax.broadcasted_iota(jnp.int32, (nb, r), 0)
              == (lax.broadcasted_iota(jnp.int32, (nb, r), 1) >> shift)).astype(F32)
    t128 = lax.broadcasted_iota(jnp.int32, (r, DH), 0) & (s - 1)
    sub8 = lax.broadcasted_iota(jnp.int32, (SUBLANES, DH), 0) >> shift
    scale = DH ** -0.5

    g_c, lf_c, bt_c, g_r, bt_r = _gate_terms(gate_ref[...], gb_ref, gbt_ref, tril, triu)
    bt_last_c = _dot_exact(same.astype(F32), lf_c)
    m_rows = _dot_exact(expand, m0_ref[...])
    for h in range(MH):
        hs = slice(h * DH, (h + 1) * DH)
        m_prev = m_rows[:, h:h + 1]
        w, s_inter, mt, it_col, bt_col = _head_scores(h, g_c, bt_c, g_r, bt_r, m_prev, mask)
        qh = q_ref[:, hs]
        kh = k_ref[:, hs] * scale
        vh = v_ref[:, hs]
        qb = qh.astype(BF16)
        kb = kh.astype(BF16)
        sqk = _dot_nt(qb, kb) * w
        for j in range(r // SUBLANES):
            rows = slice(j * SUBLANES, (j + 1) * SUBLANES)
            acc = None
            q8 = q_ref[rows, hs].astype(BF16)
            for p in range(per_tile):
                part = _dot_nt(q8, c0_ref[j * per_tile + p, h].astype(BF16))
                acc = part if acc is None else jnp.where(sub8 == p, part, acc)
            qc_scr[rows, :] = acc
        n_rows = _dot_exact(expand, n0_ref[:, h, :])
        qn = jnp.sum(qh * n_rows, axis=-1, keepdims=True)
        oa_ref[:, hs] = _head_output(sqk, vh.astype(BF16), s_inter, qc_scr[...], qn, mt,
                                     o_ref[:, hs], mg_ref[:, hs])
        m_new = _bcast_block_last(jnp.broadcast_to(mt, (r, DH)), t128, s)
        bt_last = bt_last_c[:, MH + h:MH + h + 1]
        wk = jnp.exp(bt_last - bt_col + it_col - m_new)
        decay = jnp.exp(bt_last + m_prev - m_new)
        vw = vh * wk
        for j in range(r // SUBLANES):
            rows = slice(j * SUBLANES, (j + 1) * SUBLANES)
            k8 = (k_ref[rows, hs] * scale).astype(BF16)
            for p in range(per_tile):
                bidx = j * per_tile + p
                last = bidx * s + s - 1
                upd = _dot_tn(jnp.where(sub8 == p, vw[rows], 0.0).astype(BF16), k8)
                c_ref[bidx, h] = decay[last:last + 1, 0:1] * c0_ref[bidx, h] + upd
        st_scr[0] = decay
        st_scr[1] = m_new
        last_rows = pl.ds(s - 1, nb, stride=s)
        n_ref[:, h, :] = st_scr[0, last_rows, :] * n0_ref[:, h, :] + _dot_exact(gather, kh * wk)
        m_ref[:, h:h + 1] = st_scr[1, last_rows, :][:, 0:1]
    t256 = lax.broadcasted_iota(jnp.int32, (r, D_CONV), 0) & (s - 1)
    z = cg_ref[...] * hc_ref[...]
    cb_a = _dot_exact(expand, cb0_ref[:, 0, :])
    cb_b = _dot_exact(expand, cb0_ref[:, 1, :])
    z1 = jnp.where(t256 == 0, cb_b, pltpu.roll(z, 1, 0))
    z2 = jnp.where(t256 == 0, cb_a, jnp.where(t256 == 1, cb_b, pltpu.roll(z, 2, 0)))
    yc = cw_ref[0:1, :] * z2 + cw_ref[1:2, :] * z1 + cw_ref[2:3, :] * z
    oc_ref[...] = _rms(bg_ref[...] * yc) * mg_ref[:, D_MLSTM + D_S5:]
    for half in range(D_CONV // DH):
        lanes = slice(half * DH, (half + 1) * DH)
        st_scr[half] = z[:, lanes]
        cb_ref[:, 0, lanes] = st_scr[half, pl.ds(s - 2, nb, stride=s), :]
        cb_ref[:, 1, lanes] = st_scr[half, pl.ds(s - 1, nb, stride=s), :]


def _mlstm_conv_short(qkvo, gate, bch, w, states_in, states_prev, *, layer, depth, b, s, nb):
    assert SUBLANES % s == 0 and s >= CONV_W - 1 and b % nb == 0 and (nb * s) % SUBLANES == 0
    r = nb * s
    kern = functools.partial(_mlstm_conv_short_kernel, layer=layer, nb=nb, s=s)
    col = lambda wd, j: pl.BlockSpec((r, wd), lambda i: (i, j))
    st = lambda shape: pl.BlockSpec((None, nb) + shape, lambda i: (layer, i) + (0,) * len(shape))
    cst = lambda a: pl.BlockSpec(a.shape, lambda i: (0,) * a.ndim)
    state_specs = [st((MH, DH, DH)), st((MH, DH)), st((MH,)), st((CONV_W - 1, D_CONV))]
    consts = (w["gb"], w["gbt"], w["cw"], w["mg"])
    args = [qkvo, qkvo, qkvo, qkvo, gate, bch, bch, bch, *consts, *states_in]
    in_specs = [col(D_MLSTM, 0), col(D_MLSTM, 1), col(D_MLSTM, 2), col(D_MLSTM, 3), col(GATE_W, 0),
                col(D_CONV, 0), col(D_CONV, 1), col(D_CONV, 2)] + [cst(a) for a in consts] + state_specs
    aliases = {}
    if layer > 0:
        aliases = {len(args) + j: 2 + j for j in range(4)}
        args += list(states_prev)
        in_specs += [_ANY] * 4
    return pl.pallas_call(
        kern,
        grid=(b // nb,),
        in_specs=in_specs,
        out_specs=[col(D_MLSTM, 0), col(D_CONV, 0)] + state_specs,
        out_shape=[jax.ShapeDtypeStruct((b * s, D_MLSTM), F32), jax.ShapeDtypeStruct((b * s, D_CONV), F32)]
        + _state_shapes(depth, b),
        scratch_shapes=[pltpu.VMEM((r, DH), F32), pltpu.VMEM((2, r, DH), F32)],
        input_output_aliases=aliases,
        compiler_params=pltpu.CompilerParams(dimension_semantics=("arbitrary",), vmem_limit_bytes=VMEM_LIMIT),
        name="mlstm_conv_short",
    )(*args)


def _s5_kernel(*refs, layer, nb, tt, zero_state):
    u_ref, lam_ref, bblk_ref, cblk_ref, d_ref, wglu_ref, mg_ref = refs[:7]
    refs = refs[7:]
    if not zero_state:
        hr0_ref, hi0_ref = refs[:2]
        refs = refs[2:]
    if layer > 0:
        refs = refs[2:]
    ob_ref, hr_ref, hi_ref, xs_ref, a_ref = refs
    ti = pl.program_id(0)
    n = S5_N

    @pl.when(ti == 0)
    def _():
        hr_ref[...] = jnp.zeros_like(hr_ref) if zero_state else hr0_ref[...]
        hi_ref[...] = jnp.zeros_like(hi_ref) if zero_state else hi0_ref[...]

    lre = jnp.minimum(lam_ref[0:1, :], -1e-4)
    lim = lam_ref[1:2, :]
    dt = jnp.exp(lam_ref[2:3, :])
    mag = jnp.exp(lre * dt)
    ab_re = mag * jnp.cos(lim * dt)
    ab_im = mag * jnp.sin(lim * dt)
    den = lre * lre + lim * lim
    nr = ab_re - 1.0
    fre = (nr * lre + ab_im * lim) / den
    fim = (ab_im * lre - nr * lim) / den
    a_ref[:, :n] = jnp.broadcast_to(ab_re, (SUBLANES, n))
    a_ref[:, n:] = jnp.broadcast_to(ab_im, (SUBLANES, n))

    u = u_ref[...]
    bu = _dot(u.astype(BF16), bblk_ref[...])
    bu_re = bu[:, :n]
    bu_im = bu[:, n:]
    xs_ref[:, :n] = fre * bu_re - fim * bu_im
    xs_ref[:, n:] = fre * bu_im + fim * bu_re

    lc = 512
    for rg in range(nb // SUBLANES):
        rs = slice(rg * SUBLANES, (rg + 1) * SUBLANES)
        for c0 in range(0, n, lc):
            ar = a_ref[:, c0:c0 + lc]
            ai = a_ref[:, n + c0:n + c0 + lc]

            def step(t, hc, rg=rg, c0=c0, ar=ar, ai=ai):
                hr, hi = hc
                r0 = pl.multiple_of(t * nb + rg * SUBLANES, SUBLANES)
                xr = xs_ref[pl.ds(r0, SUBLANES), c0:c0 + lc]
                xi = xs_ref[pl.ds(r0, SUBLANES), n + c0:n + c0 + lc]
                nr_ = ar * hr - ai * hi + xr
                ni_ = ar * hi + ai * hr + xi
                xs_ref[pl.ds(r0, SUBLANES), c0:c0 + lc] = nr_
                xs_ref[pl.ds(r0, SUBLANES), n + c0:n + c0 + lc] = ni_
                return nr_, ni_

            hr, hi = lax.fori_loop(0, tt, step, (hr_ref[rs, c0:c0 + lc], hi_ref[rs, c0:c0 + lc]))
            hr_ref[rs, c0:c0 + lc] = hr
            hi_ref[rs, c0:c0 + lc] = hi

    y = _dot(xs_ref[...].astype(BF16), cblk_ref[...]) + d_ref[...] * u
    g = jax.nn.gelu(y)
    ob = g * jax.nn.sigmoid(_dot(g.astype(BF16), wglu_ref[...]))
    ob_ref[...] = _rms(ob) * mg_ref[:, D_MLSTM:D_MLSTM + D_S5]


def _s5(u_tm, w, states_in, states_prev, *, layer, depth, nb, tt):
    rows = u_tm.shape[0]
    rt = nb * tt
    zero_state = states_in is None
    kern = functools.partial(_s5_kernel, layer=layer, nb=nb, tt=tt, zero_state=zero_state)
    cst = lambda a: pl.BlockSpec(a.shape, lambda i: (0,) * a.ndim)
    st = pl.BlockSpec((None, nb, S5_N), lambda i: (layer, 0, 0))
    consts = (w["lam"], w["bblk"], w["cblk"], w["d"], w["wglu"], w["mg"])
    args = [u_tm, *consts]
    in_specs = [pl.BlockSpec((rt, D_S5), lambda i: (i, 0))] + [cst(a) for a in consts]
    if not zero_state:
        args += list(states_in)
        in_specs += [st, st]
    aliases = {}
    if layer > 0:
        aliases = {len(args) + j: 1 + j for j in range(2)}
        args += list(states_prev)
        in_specs += [_ANY] * 2
    return pl.pallas_call(
        kern,
        grid=(rows // rt,),
        in_specs=in_specs,
        out_specs=[pl.BlockSpec((rt, D_S5), lambda i: (i, 0)), st, st],
        out_shape=[jax.ShapeDtypeStruct((rows, D_S5), F32), jax.ShapeDtypeStruct((depth, nb, S5_N), F32),
                   jax.ShapeDtypeStruct((depth, nb, S5_N), F32)],
        scratch_shapes=[pltpu.VMEM((rt, 2 * S5_N), F32), pltpu.VMEM((SUBLANES, 2 * S5_N), F32)],
        input_output_aliases=aliases,
        compiler_params=pltpu.CompilerParams(dimension_semantics=("arbitrary",), vmem_limit_bytes=VMEM_LIMIT),
        name="s5",
    )(*args)


def _block_diag(w):
    g, r, c = w.shape
    eye = jnp.eye(g, dtype=w.dtype)
    return jnp.einsum("grc,gh->grhc", w, eye).reshape(g * r, g * c)


def _layer_weights(l, ffn1_w1, ffn1_w3, ffn1_w2, ffn2_w1, ffn2_w3, ffn2_w2, norm_g, w_in, ig_bias, fg_bias,
                   s5_a_re, s5_a_im, s5_log_dt, s5_b_re, s5_b_im, s5_c_re, s5_c_im, s5_d, w_glu, conv_w,
                   mix_g, w_out):
    o = 4 * D_MLSTM
    win = w_in[l]
    win_pad = jnp.concatenate(
        [win[:, :o], win[:, o:o + 2 * MH], jnp.zeros((D_MODEL, GATE_W - 2 * MH), F32), win[:, o + 2 * MH:]],
        axis=1).astype(BF16)
    gb = jnp.concatenate([ig_bias[l], fg_bias[l], jnp.zeros((GATE_W - 2 * MH,), F32)])
    lam = jnp.stack([s5_a_re[l].reshape(-1), s5_a_im[l].reshape(-1),
                     jnp.repeat(s5_log_dt[l], S5_P)])
    bblk = jnp.concatenate([_block_diag(jnp.swapaxes(s5_b_re[l], 1, 2)),
                            _block_diag(jnp.swapaxes(s5_b_im[l], 1, 2))], axis=1).astype(BF16)
    cblk = jnp.concatenate([_block_diag(jnp.swapaxes(s5_c_re[l], 1, 2)),
                            -_block_diag(jnp.swapaxes(s5_c_im[l], 1, 2))], axis=0).astype(BF16)
    return dict(
        f1=(ffn1_w1[l].astype(BF16), ffn1_w3[l].astype(BF16), ffn1_w2[l].astype(BF16)),
        f2=(ffn2_w1[l].astype(BF16), ffn2_w3[l].astype(BF16), ffn2_w2[l].astype(BF16)),
        g=norm_g[l], win=win_pad, gb=gb.reshape(1, GATE_W), gbt=gb[:SUBLANES].reshape(SUBLANES, 1),
        lam=lam, bblk=bblk, cblk=cblk, d=s5_d[l].reshape(1, D_S5), wglu=w_glu[l].astype(BF16),
        cw=conv_w[l], mg=mix_g[l].reshape(1, D_MODEL), wo=w_out[l].astype(BF16))


def _run_group(x, states, weights, *, tm, tt, chunk=None, bb=None, nb=None):
    b, s, _ = x.shape
    depth = len(weights)
    xf = x.reshape(b * s, D_MODEL)
    mstates = sstates = None
    if states is not None:
        c_all, n_all, m_all, sr_all, si_all, cb_all = states
        ms_in = (c_all, n_all, m_all, cb_all)
        ss_in = (sr_all.reshape(depth, b, S5_N), si_all.reshape(depth, b, S5_N))
    else:
        ms_in = ss_in = None
    for l, w in enumerate(weights):
        x1, qkvo, gate, u, bch = _ffn_inproj(xf, w["g"], *w["f1"], w["win"], tm)
        if chunk is not None:
            r3 = lambda a: a.reshape(b, s, a.shape[-1])
            oa, oc, *mstates = _mlstm_conv(r3(qkvo), r3(gate), r3(bch), w, ms_in, mstates,
                                           layer=l, depth=depth, bb=bb, chunk=chunk)
        else:
            oa, oc, *mstates = _mlstm_conv_short(qkvo, gate, bch, w, ms_in, mstates,
                                                 layer=l, depth=depth, b=b, s=s, nb=nb)
        u_tm = jnp.swapaxes(u.reshape(b, s, D_S5), 0, 1).reshape(s * b, D_S5)
        ob_tm, *sstates = _s5(u_tm, w, ss_in, sstates, layer=l, depth=depth, nb=b, tt=tt)
        ob = jnp.swapaxes(ob_tm.reshape(s, b, D_S5), 0, 1).reshape(b * s, D_S5)
        xf = _outproj_ffn(x1, oa.reshape(b * s, D_MLSTM), ob, oc.reshape(b * s, D_CONV),
                          w["g"], w["wo"], *w["f2"], tm)
    c1, n1, m1, cb1 = mstates
    sr1, si1 = sstates
    return xf.reshape(b, s, D_MODEL), (c1, n1, m1, sr1.reshape(depth, b, S5_G, S5_P),
                                       si1.reshape(depth, b, S5_G, S5_P), cb1)


def kernel(x_prompt, x_sample, state_mlstm_C, state_mlstm_n, state_mlstm_m, state_s5_re, state_s5_im,
           state_conv, ffn1_w1, ffn1_w3, ffn1_w2, ffn2_w1, ffn2_w3, ffn2_w2, norm_g, w_in, ig_bias, fg_bias,
           s5_a_re, s5_a_im, s5_log_dt, s5_b_re, s5_b_im, s5_c_re, s5_c_im, s5_d, w_glu, conv_w, mix_g, w_out):
    depth = norm_g.shape[0]
    params = (ffn1_w1, ffn1_w3, ffn1_w2, ffn2_w1, ffn2_w3, ffn2_w2, norm_g, w_in, ig_bias, fg_bias,
              s5_a_re, s5_a_im, s5_log_dt, s5_b_re, s5_b_im, s5_c_re, s5_c_im, s5_d, w_glu, conv_w, mix_g, w_out)
    weights = [_layer_weights(l, *params) for l in range(depth)]
    y_p, st_p = _run_group(x_prompt, None, weights, tm=512, tt=64, chunk=128, bb=2)
    y_s, st_s = _run_group(
        x_sample, (state_mlstm_C, state_mlstm_n, state_mlstm_m, state_s5_re, state_s5_im, state_conv),
        weights, tm=512, tt=x_sample.shape[1], nb=32)
    return (y_p, y_s, *st_p, *st_s)
```

```python
import functools

import jax
import jax.numpy as jnp
from jax import lax
from jax.experimental import pallas as pl
from jax.experimental.pallas import tpu as pltpu

F32 = jnp.float32
BF16 = jnp.bfloat16

D_MODEL = 1024
MH = 4
DH = 128
D_MLSTM = MH * DH
S5_CH = 16
S5_G = 16
S5_P = 64
D_S5 = S5_G * S5_CH
S5_N = S5_G * S5_P
D_CONV = 256
CONV_W = 3
EPS = 1e-6
GATE_W = 128
SUBLANES = 8
NEG = -1e30
VMEM_LIMIT = 56 * 1024 * 1024


def _dot(a, b):
    return jnp.dot(a, b, preferred_element_type=F32)


def _dot_nt(a, b):
    return lax.dot_general(a, b, (((1,), (1,)), ((), ())), preferred_element_type=F32)


def _dot_tn(a, b):
    return lax.dot_general(a, b, (((0,), (0,)), ((), ())), preferred_element_type=F32)


def _dot_exact(a, b):
    return jnp.dot(a, b, preferred_element_type=F32, precision=lax.Precision.HIGHEST)


def _dot_nt_exact(a, b):
    return lax.dot_general(a, b, (((1,), (1,)), ((), ())), preferred_element_type=F32,
                           precision=lax.Precision.HIGHEST)


def _rms(x):
    return x * lax.rsqrt(jnp.mean(x * x, axis=-1, keepdims=True) + EPS)


def _log_sigmoid(x):
    return jnp.minimum(x, 0.0) - jnp.log1p(jnp.exp(-jnp.abs(x)))


def _split_bf16(x, parts):
    out = []
    for _ in range(parts):
        p = x.astype(BF16)
        out.append(p)
        x = x - p.astype(F32)
    return jnp.concatenate(out, axis=1)


def _ffn_residual(x, g_pre, g_post, w1, w3, w2):
    xn = (_rms(x) * g_pre).astype(BF16)
    h1 = _dot(xn, w1[...])
    h3 = _dot(xn, w3[...])
    a = (jax.nn.silu(h1) * h3).astype(BF16)
    y = _dot(a, w2[...])
    return x + 0.5 * (_rms(y) * g_post)


def _const_spec(shape):
    nd = len(shape)
    return pl.BlockSpec(shape, lambda *_: (0,) * nd, pipeline_mode=pl.Buffered(1))


_ANY = pl.BlockSpec(memory_space=pl.ANY)


def _ffn_inproj_kernel(x_ref, g_ref, w1_ref, w3_ref, w2_ref, win_ref,
                       x1_ref, qkvo_ref, gate_ref, u_ref, bch_ref):
    x = x_ref[...]
    x1 = _ffn_residual(x, g_ref[0:1, :], g_ref[1:2, :], w1_ref, w3_ref, w2_ref)
    x1_ref[...] = x1
    hn = (_rms(x1) * g_ref[2:3, :]).astype(BF16)
    proj = _dot(hn, win_ref[...])
    o = 4 * D_MLSTM
    qkvo_ref[...] = proj[:, :o]
    gate_ref[...] = proj[:, o:o + GATE_W]
    u_ref[...] = proj[:, o + GATE_W:o + GATE_W + D_S5]
    bch_ref[...] = proj[:, o + GATE_W + D_S5:]


def _ffn_inproj(x, g, w1, w3, w2, win, tm):
    t = x.shape[0]
    row = lambda w: pl.BlockSpec((tm, w), lambda i: (i, 0))
    return pl.pallas_call(
        _ffn_inproj_kernel,
        grid=(t // tm,),
        in_specs=[row(D_MODEL), _const_spec(g.shape), _const_spec(w1.shape), _const_spec(w3.shape),
                  _const_spec(w2.shape), _const_spec(win.shape)],
        out_specs=[row(D_MODEL), row(4 * D_MLSTM), row(GATE_W), row(D_S5), row(3 * D_CONV)],
        out_shape=[jax.ShapeDtypeStruct((t, D_MODEL), F32), jax.ShapeDtypeStruct((t, 4 * D_MLSTM), F32),
                   jax.ShapeDtypeStruct((t, GATE_W), F32), jax.ShapeDtypeStruct((t, D_S5), F32),
                   jax.ShapeDtypeStruct((t, 3 * D_CONV), F32)],
        compiler_params=pltpu.CompilerParams(dimension_semantics=("arbitrary",), vmem_limit_bytes=VMEM_LIMIT),
        name="ffn_inproj",
    )(x, g, w1, w3, w2, win)


def _outproj_ffn_kernel(x_ref, a_ref, b_ref, c_ref, g_ref, wo_ref, w1_ref, w3_ref, w2_ref, y_ref):
    mo = (_dot(a_ref[...].astype(BF16), wo_ref[0:D_MLSTM, :])
          + _dot(b_ref[...].astype(BF16), wo_ref[D_MLSTM:D_MLSTM + D_S5, :])
          + _dot(c_ref[...].astype(BF16), wo_ref[D_MLSTM + D_S5:, :]))
    x2 = x_ref[...] + _rms(mo) * g_ref[3:4, :]
    y_ref[...] = _ffn_residual(x2, g_ref[4:5, :], g_ref[5:6, :], w1_ref, w3_ref, w2_ref)


def _outproj_ffn(x, oa, ob, oc, g, wo, w1, w3, w2, tm):
    t = x.shape[0]
    row = lambda w: pl.BlockSpec((tm, w), lambda i: (i, 0))
    return pl.pallas_call(
        _outproj_ffn_kernel,
        grid=(t // tm,),
        in_specs=[row(D_MODEL), row(D_MLSTM), row(D_S5), row(D_CONV), _const_spec(g.shape),
                  _const_spec(wo.shape), _const_spec(w1.shape), _const_spec(w3.shape), _const_spec(w2.shape)],
        out_specs=row(D_MODEL),
        out_shape=jax.ShapeDtypeStruct((t, D_MODEL), F32),
        compiler_params=pltpu.CompilerParams(dimension_semantics=("arbitrary",), vmem_limit_bytes=VMEM_LIMIT),
        name="outproj_ffn",
    )(x, oa, ob, oc, g, wo, w1, w3, w2)


def _state_shapes(depth, b):
    return [jax.ShapeDtypeStruct((depth, b, MH, DH, DH), F32), jax.ShapeDtypeStruct((depth, b, MH, DH), F32),
            jax.ShapeDtypeStruct((depth, b, MH), F32), jax.ShapeDtypeStruct((depth, b, CONV_W - 1, D_CONV), F32)]


def _cummax_rows(x, rowi):
    s = 1
    while s < x.shape[0]:
        x = jnp.maximum(x, jnp.where(rowi >= s, pltpu.roll(x, s, 0), NEG))
        s *= 2
    return x


def _mlstm_conv_kernel(*refs, layer, bb, chunk, zero_state):
    (q_ref, k_ref, v_ref, o_ref, gate_ref, bg_ref, cg_ref, hc_ref, gb_ref, gbt_ref, cw_ref, mg_ref) = refs[:12]
    refs = refs[12:]
    if not zero_state:
        c0_ref, n0_ref, m0_ref, cb0_ref = refs[:4]
        refs = refs[4:]
    if layer > 0:
        refs = refs[4:]
    oa_ref, oc_ref, c_ref, n_ref, m_ref, cb_ref, m_scr = refs
    assert chunk == DH
    bi0 = pl.program_id(0) * bb
    ci = pl.program_id(1)

    @pl.when(ci == 0)
    def _():
        m_scr[...] = jnp.zeros_like(m_scr)
        if zero_state:
            c_ref[...] = jnp.zeros_like(c_ref)
            n_ref[...] = jnp.zeros_like(n_ref)
            cb_ref[...] = jnp.zeros_like(cb_ref)
        else:
            c_ref[...] = c0_ref[...]
            n_ref[...] = n0_ref[...]
            cb_ref[...] = cb0_ref[...]
            for bi in range(bb):
                m_scr[bi:bi + 1, 0:MH] = m0_ref[pl.ds(bi0 + bi, 1), :]

    row = lax.broadcasted_iota(jnp.int32, (chunk, chunk), 0)
    col = lax.broadcasted_iota(jnp.int32, (chunk, chunk), 1)
    causal = col <= row
    tril = jnp.where(causal, 1.0, 0.0).astype(BF16)
    rowg = lax.broadcasted_iota(jnp.int32, (chunk, GATE_W), 0)
    rowi = lax.broadcasted_iota(jnp.int32, (chunk, D_CONV), 0)
    sel8 = (lax.broadcasted_iota(jnp.int32, (SUBLANES, GATE_W), 0)
            == lax.broadcasted_iota(jnp.int32, (SUBLANES, GATE_W), 1)).astype(F32)
    spread = jnp.where((lax.broadcasted_iota(jnp.int32, (2 * GATE_W, D_MLSTM), 0) & (GATE_W - 1))
                       == (lax.broadcasted_iota(jnp.int32, (2 * GATE_W, D_MLSTM), 1) >> 7), 1.0, 0.0).astype(BF16)
    ones_b = jnp.ones((chunk, DH), BF16)
    zeros_b = jnp.zeros((chunk, DH), BF16)
    ones_tall = jnp.ones((2 * DH, DH), BF16)
    scale = DH ** -0.5

    units = [(bi, h) for bi in range(bb) for h in range(MH)]
    hsl = lambda h: slice(h * DH, (h + 1) * DH)
    g_c = [gate_ref[bi] + gb_ref[...] for bi in range(bb)]
    csum = [_dot(tril, _split_bf16(pltpu.roll(_log_sigmoid(g), GATE_W - MH, 1), 3)) for g in g_c]
    qb = {u: q_ref[u[0], :, hsl(u[1])].astype(BF16) for u in units}
    kb = {u: (k_ref[u[0], :, hsl(u[1])] * scale).astype(BF16) for u in units}
    s_qk = {u: _dot_nt(qb[u], kb[u]) for u in units}
    inter = {u: _dot_nt(qb[u], jnp.concatenate(
        [c_ref[u[0], u[1]].astype(BF16),
         jnp.broadcast_to(n_ref[u[0], u[1]:u[1] + 1, :], (DH, DH)).astype(BF16)], axis=0)) for u in units}
    rep, a_r, decay = [], [], []
    for bi in range(bb):
        f_c = csum[bi][:, :GATE_W] + csum[bi][:, GATE_W:2 * GATE_W] + csum[bi][:, 2 * GATE_W:]
        a_c = g_c[bi] - f_c
        m_prev = m_scr[bi:bi + 1, :]
        big_m = jnp.maximum(_cummax_rows(a_c, rowg), m_prev)
        mt = f_c + big_m
        m_new = mt[chunk - 1:chunk, :]
        f_last = f_c[chunk - 1:chunk, :]
        decay.append(jnp.exp(f_last + m_prev - m_new))
        stacked = jnp.concatenate([_split_bf16(-big_m, 2), _split_bf16(jnp.exp(m_prev - big_m), 2),
                                   _split_bf16(-mt, 2), _split_bf16(jnp.exp(a_c + (f_last - m_new)), 2)],
                                  axis=0)
        rep.append(_dot(stacked, spread))
        a_r.append(_dot_nt_exact(sel8, a_c))
        m_scr[bi:bi + 1, :] = m_new
        m_ref[pl.ds(bi0 + bi, 1), :] = m_new[:, 0:MH]
    intra = {}
    for u in units:
        bi, h = u
        w = jnp.exp(jnp.where(causal, rep[bi][0:chunk, hsl(h)] + a_r[bi][h:h + 1, :], NEG))
        rhs = jnp.concatenate([jnp.concatenate([v_ref[bi, :, hsl(h)].astype(BF16), ones_b], axis=1),
                               jnp.concatenate([zeros_b, ones_b], axis=1)], axis=0)
        intra[u] = _dot(_split_bf16(s_qk[u] * w, 2), rhs)
    hm, ssq = {}, {}
    for u in units:
        bi, h = u
        s_inter = rep[bi][chunk:2 * chunk, hsl(h)]
        e_floor = jnp.exp(rep[bi][2 * chunk:3 * chunk, hsl(h)])
        num = intra[u][:, :DH] + s_inter * inter[u][:, :DH]
        den = intra[u][:, DH:] + s_inter * inter[u][:, DH:]
        hm[u] = jax.nn.sigmoid(o_ref[bi, :, hsl(h)]) * (num / jnp.maximum(jnp.abs(den), e_floor))
        ssq[u] = _dot(_split_bf16(hm[u] * hm[u], 2), ones_tall)
    for u in units:
        bi, h = u
        oa_ref[bi, :, hsl(h)] = hm[u] * lax.rsqrt(ssq[u] * (1.0 / DH) + EPS) * mg_ref[:, hsl(h)]
        wk = rep[bi][3 * chunk:, hsl(h)]
        dec = decay[bi][:, h:h + 1]
        c_ref[bi, h] = dec * c_ref[bi, h] + _dot_tn((v_ref[bi, :, hsl(h)] * wk).astype(BF16), kb[u])
        n_ref[bi, h:h + 1, :] = (dec * n_ref[bi, h:h + 1, :]
                                 + jnp.sum(k_ref[bi, :, hsl(h)] * scale * wk, axis=0, keepdims=True))
    for bi in range(bb):
        z = cg_ref[bi] * hc_ref[bi]
        cb = cb_ref[bi]
        z1 = jnp.where(rowi == 0, cb[1:2, :], pltpu.roll(z, 1, 0))
        z2 = jnp.where(rowi == 0, cb[0:1, :], jnp.where(rowi == 1, cb[1:2, :], pltpu.roll(z, 2, 0)))
        yc = cw_ref[0:1, :] * z2 + cw_ref[1:2, :] * z1 + cw_ref[2:3, :] * z
        oc_ref[bi] = _rms(bg_ref[bi] * yc) * mg_ref[:, D_MLSTM + D_S5:]
        cb_ref[bi] = z[chunk - 2:chunk, :]


def _mlstm_conv(qkvo, gate, bch, w, states_in, states_prev, *, layer, depth, bb, chunk):
    b, s, _ = qkvo.shape
    assert s % chunk == 0 and b % bb == 0
    zero_state = states_in is None
    kern = functools.partial(_mlstm_conv_kernel, layer=layer, bb=bb, chunk=chunk, zero_state=zero_state)
    col = lambda wd, j: pl.BlockSpec((bb, chunk, wd), lambda i, c: (i, c, j))
    st = lambda shape: pl.BlockSpec((None, bb) + shape, lambda i, c: (layer, i) + (0,) * len(shape))
    m_spec = pl.BlockSpec((None, b, MH), lambda i, c: (layer, 0, 0))
    cst = lambda a: pl.BlockSpec(a.shape, lambda i, c: (0,) * a.ndim)
    state_specs = [st((MH, DH, DH)), st((MH, DH)), m_spec, st((CONV_W - 1, D_CONV))]
    consts = (w["gb"], w["gbt"], w["cw"], w["mg"])
    args = [qkvo, qkvo, qkvo, qkvo, gate, bch, bch, bch, *consts]
    in_specs = [col(D_MLSTM, 0), col(D_MLSTM, 1), col(D_MLSTM, 2), col(D_MLSTM, 3), col(GATE_W, 0),
                col(D_CONV, 0), col(D_CONV, 1), col(D_CONV, 2)] + [cst(a) for a in consts]
    if not zero_state:
        args += list(states_in)
        in_specs += state_specs
    aliases = {}
    if layer > 0:
        aliases = {len(args) + j: 2 + j for j in range(4)}
        args += list(states_prev)
        in_specs += [_ANY] * 4
    return pl.pallas_call(
        kern,
        grid=(b // bb, s // chunk),
        in_specs=in_specs,
        out_specs=[col(D_MLSTM, 0), col(D_CONV, 0)] + state_specs,
        out_shape=[jax.ShapeDtypeStruct((b, s, D_MLSTM), F32), jax.ShapeDtypeStruct((b, s, D_CONV), F32)]
        + _state_shapes(depth, b),
        scratch_shapes=[pltpu.VMEM((-(-bb // SUBLANES) * SUBLANES, GATE_W), F32)],
        input_output_aliases=aliases,
        compiler_params=pltpu.CompilerParams(dimension_semantics=("arbitrary", "arbitrary"),
                                             vmem_limit_bytes=VMEM_LIMIT),
        name="mlstm_conv",
    )(*args)


def _gate_terms(graw, gb_ref, gbt_ref, tril, triu):
    sel = (lax.broadcasted_iota(jnp.int32, (SUBLANES, GATE_W), 0)
           == lax.broadcasted_iota(jnp.int32, (SUBLANES, GATE_W), 1)).astype(F32)
    g_c = graw + gb_ref[...]
    lf_c = _log_sigmoid(g_c)
    g_r = _dot_nt_exact(sel, graw) + gbt_ref[...]
    lf_r = _log_sigmoid(g_r)
    return g_c, lf_c, _dot_exact(tril, lf_c), g_r, _dot_exact(lf_r, triu)


def _head_scores(h, g_c, bt_c, g_r, bt_r, m_prev, mask):
    it_row = g_r[h:h + 1, :]
    bt_row = bt_r[MH + h:MH + h + 1, :]
    it_col = g_c[:, h:h + 1]
    bt_col = bt_c[:, MH + h:MH + h + 1]
    dmat = jnp.where(mask, bt_col - bt_row + it_row, NEG)
    inter = bt_col + m_prev
    mt = jnp.maximum(jnp.max(dmat, axis=-1, keepdims=True), inter)
    return jnp.exp(dmat - mt), jnp.exp(inter - mt), mt, it_col, bt_col


def _head_output(sqk, vb, s_inter, qc, qn, mt, o, mg):
    num = _dot(sqk.astype(BF16), vb) + s_inter * qc
    den = jnp.sum(sqk, axis=-1, keepdims=True) + s_inter * qn
    hm = jax.nn.sigmoid(o) * (num / jnp.maximum(jnp.abs(den), jnp.exp(-mt)))
    return _rms(hm) * mg


def _bcast_block_last(x, t, s):
    out = x
    for d in range(1, s):
        out = jnp.where(t == s - 1 - d, pltpu.roll(x, x.shape[0] - d, 0), out)
    return out


def _mlstm_conv_short_kernel(*refs, layer, nb, s):
    (q_ref, k_ref, v_ref, o_ref, gate_ref, bg_ref, cg_ref, hc_ref, gb_ref, gbt_ref, cw_ref, mg_ref,
     c0_ref, n0_ref, m0_ref, cb0_ref) = refs[:16]
    refs = refs[16:]
    if layer > 0:
        refs = refs[4:]
    oa_ref, oc_ref, c_ref, n_ref, m_ref, cb_ref, qc_scr, st_scr = refs
    r = nb * s
    per_tile = SUBLANES // s
    shift = s.bit_length() - 1
    row = lax.broadcasted_iota(jnp.int32, (r, r), 0)
    col = lax.broadcasted_iota(jnp.int32, (r, r), 1)
    same = (row >> shift) == (col >> shift)
    mask = same & (col <= row)
    tril = mask.astype(F32)
    triu = (same & (row <= col)).astype(F32)
    expand = ((lax.broadcasted_iota(jnp.int32, (r, nb), 0) >> shift)
              == lax.broadcasted_iota(jnp.int32, (r, nb), 1)).astype(F32)
    gather = (lax.broadcasted_iota(jnp.int32, (nb, r), 0)
              == (lax.broadcasted_iota(jnp.int32, (nb, r), 1) >> shift)).astype(F32)
    t128 = lax.broadcasted_iota(jnp.int32, (r, DH), 0) & (s - 1)
    sub8 = lax.broadcasted_iota(jnp.int32, (SUBLANES, DH), 0) >> shift
    scale = DH ** -0.5

    g_c, lf_c, bt_c, g_r, bt_r = _gate_terms(gate_ref[...], gb_ref, gbt_ref, tril, triu)
    bt_last_c = _dot_exact(same.astype(F32), lf_c)
    m_rows = _dot_exact(expand, m0_ref[...])
    for h in range(MH):
        hs = slice(h * DH, (h + 1) * DH)
        m_prev = m_rows[:, h:h + 1]
        w, s_inter, mt, it_col, bt_col = _head_scores(h, g_c, bt_c, g_r, bt_r, m_prev, mask)
        qh = q_ref[:, hs]
        kh = k_ref[:, hs] * scale
        vh = v_ref[:, hs]
        qb = qh.astype(BF16)
        kb = kh.astype(BF16)
        sqk = _dot_nt(qb, kb) * w
        for j in range(r // SUBLANES):
            rows = slice(j * SUBLANES, (j + 1) * SUBLANES)
            acc = None
            q8 = q_ref[rows, hs].astype(BF16)
            for p in range(per_tile):
                part = _dot_nt(q8, c0_ref[j * per_tile + p, h].astype(BF16))
                acc = part if acc is None else jnp.where(sub8 == p, part, acc)
            qc_scr[rows, :] = acc
        n_rows = _dot_exact(expand, n0_ref[:, h, :])
        qn = jnp.sum(qh * n_rows, axis=-1, keepdims=True)
        oa_ref[:, hs] = _head_output(sqk, vh.astype(BF16), s_inter, qc_scr[...], qn, mt,
                                     o_ref[:, hs], mg_ref[:, hs])
        m_new = _bcast_block_last(jnp.broadcast_to(mt, (r, DH)), t128, s)
        bt_last = bt_last_c[:, MH + h:MH + h + 1]
        wk = jnp.exp(bt_last - bt_col + it_col - m_new)
        decay = jnp.exp(bt_last + m_prev - m_new)
        vw = vh * wk
        for j in range(r // SUBLANES):
            rows = slice(j * SUBLANES, (j + 1) * SUBLANES)
            k8 = (k_ref[rows, hs] * scale).astype(BF16)
            for p in range(per_tile):
                bidx = j * per_tile + p
                last = bidx * s + s - 1
                upd = _dot_tn(jnp.where(sub8 == p, vw[rows], 0.0).astype(BF16), k8)
                c_ref[bidx, h] = decay[last:last + 1, 0:1] * c0_ref[bidx, h] + upd
        st_scr[0] = decay
        st_scr[1] = m_new
        last_rows = pl.ds(s - 1, nb, stride=s)
        n_ref[:, h, :] = st_scr[0, last_rows, :] * n0_ref[:, h, :] + _dot_exact(gather, kh * wk)
        m_ref[:, h:h + 1] = st_scr[1, last_rows, :][:, 0:1]
    t256 = lax.broadcasted_iota(jnp.int32, (r, D_CONV), 0) & (s - 1)
    z = cg_ref[...] * hc_ref[...]
    cb_a = _dot_exact(expand, cb0_ref[:, 0, :])
    cb_b = _dot_exact(expand, cb0_ref[:, 1, :])
    z1 = jnp.where(t256 == 0, cb_b, pltpu.roll(z, 1, 0))
    z2 = jnp.where(t256 == 0, cb_a, jnp.where(t256 == 1, cb_b, pltpu.roll(z, 2, 0)))
    yc = cw_ref[0:1, :] * z2 + cw_ref[1:2, :] * z1 + cw_ref[2:3, :] * z
    oc_ref[...] = _rms(bg_ref[...] * yc) * mg_ref[:, D_MLSTM + D_S5:]
    for half in range(D_CONV // DH):
        lanes = slice(half * DH, (half + 1) * DH)
        st_scr[half] = z[:, lanes]
        cb_ref[:, 0, lanes] = st_scr[half, pl.ds(s - 2, nb, stride=s), :]
        cb_ref[:, 1, lanes] = st_scr[half, pl.ds(s - 1, nb, stride=s), :]


def _mlstm_conv_short(qkvo, gate, bch, w, states_in, states_prev, *, layer, depth, b, s, nb):
    assert SUBLANES % s == 0 and s >= CONV_W - 1 and b % nb == 0 and (nb * s) % SUBLANES == 0
    r = nb * s
    kern = functools.partial(_mlstm_conv_short_kernel, layer=layer, nb=nb, s=s)
    col = lambda wd, j: pl.BlockSpec((r, wd), lambda i: (i, j))
    st = lambda shape: pl.BlockSpec((None, nb) + shape, lambda i: (layer, i) + (0,) * len(shape))
    cst = lambda a: pl.BlockSpec(a.shape, lambda i: (0,) * a.ndim)
    state_specs = [st((MH, DH, DH)), st((MH, DH)), st((MH,)), st((CONV_W - 1, D_CONV))]
    consts = (w["gb"], w["gbt"], w["cw"], w["mg"])
    args = [qkvo, qkvo, qkvo, qkvo, gate, bch, bch, bch, *consts, *states_in]
    in_specs = [col(D_MLSTM, 0), col(D_MLSTM, 1), col(D_MLSTM, 2), col(D_MLSTM, 3), col(GATE_W, 0),
                col(D_CONV, 0), col(D_CONV, 1), col(D_CONV, 2)] + [cst(a) for a in consts] + state_specs
    aliases = {}
    if layer > 0:
        aliases = {len(args) + j: 2 + j for j in range(4)}
        args += list(states_prev)
        in_specs += [_ANY] * 4
    return pl.pallas_call(
        kern,
        grid=(b // nb,),
        in_specs=in_specs,
        out_specs=[col(D_MLSTM, 0), col(D_CONV, 0)] + state_specs,
        out_shape=[jax.ShapeDtypeStruct((b * s, D_MLSTM), F32), jax.ShapeDtypeStruct((b * s, D_CONV), F32)]
        + _state_shapes(depth, b),
        scratch_shapes=[pltpu.VMEM((r, DH), F32), pltpu.VMEM((2, r, DH), F32)],
        input_output_aliases=aliases,
        compiler_params=pltpu.CompilerParams(dimension_semantics=("arbitrary",), vmem_limit_bytes=VMEM_LIMIT),
        name="mlstm_conv_short",
    )(*args)


def _s5_kernel(*refs, layer, nb, tt, zero_state):
    u_ref, lam_ref, bblk_ref, cblk_ref, d_ref, wglu_ref, mg_ref = refs[:7]
    refs = refs[7:]
    if not zero_state:
        hr0_ref, hi0_ref = refs[:2]
        refs = refs[2:]
    if layer > 0:
        refs = refs[2:]
    ob_ref, hr_ref, hi_ref, xs_ref, a_ref = refs
    ti = pl.program_id(0)
    n = S5_N

    @pl.when(ti == 0)
    def _():
        hr_ref[...] = jnp.zeros_like(hr_ref) if zero_state else hr0_ref[...]
        hi_ref[...] = jnp.zeros_like(hi_ref) if zero_state else hi0_ref[...]

    lre = jnp.minimum(lam_ref[0:1, :], -1e-4)
    lim = lam_ref[1:2, :]
    dt = jnp.exp(lam_ref[2:3, :])
    mag = jnp.exp(lre * dt)
    ab_re = mag * jnp.cos(lim * dt)
    ab_im = mag * jnp.sin(lim * dt)
    den = lre * lre + lim * lim
    nr = ab_re - 1.0
    fre = (nr * lre + ab_im * lim) / den
    fim = (ab_im * lre - nr * lim) / den
    a_ref[:, :n] = jnp.broadcast_to(ab_re, (SUBLANES, n))
    a_ref[:, n:] = jnp.broadcast_to(ab_im, (SUBLANES, n))

    u = u_ref[...]
    bu = _dot(u.astype(BF16), bblk_ref[...])
    bu_re = bu[:, :n]
    bu_im = bu[:, n:]
    xs_ref[:, :n] = fre * bu_re - fim * bu_im
    xs_ref[:, n:] = fre * bu_im + fim * bu_re

    lc = 512
    for rg in range(nb // SUBLANES):
        rs = slice(rg * SUBLANES, (rg + 1) * SUBLANES)
        for c0 in range(0, n, lc):
            ar = a_ref[:, c0:c0 + lc]
            ai = a_ref[:, n + c0:n + c0 + lc]

            def step(t, hc, rg=rg, c0=c0, ar=ar, ai=ai):
                hr, hi = hc
                r0 = pl.multiple_of(t * nb + rg * SUBLANES, SUBLANES)
                xr = xs_ref[pl.ds(r0, SUBLANES), c0:c0 + lc]
                xi = xs_ref[pl.ds(r0, SUBLANES), n + c0:n + c0 + lc]
                nr_ = ar * hr - ai * hi + xr
                ni_ = ar * hi + ai * hr + xi
                xs_ref[pl.ds(r0, SUBLANES), c0:c0 + lc] = nr_
                xs_ref[pl.ds(r0, SUBLANES), n + c0:n + c0 + lc] = ni_
                return nr_, ni_

            hr, hi = lax.fori_loop(0, tt, step, (hr_ref[rs, c0:c0 + lc], hi_ref[rs, c0:c0 + lc]))
            hr_ref[rs, c0:c0 + lc] = hr
            hi_ref[rs, c0:c0 + lc] = hi

    y = _dot(xs_ref[...].astype(BF16), cblk_ref[...]) + d_ref[...] * u
    g = jax.nn.gelu(y)
    ob = g * jax.nn.sigmoid(_dot(g.astype(BF16), wglu_ref[...]))
    ob_ref[...] = _rms(ob) * mg_ref[:, D_MLSTM:D_MLSTM + D_S5]


def _s5(u_tm, w, states_in, states_prev, *, layer, depth, nb, tt):
    rows = u_tm.shape[0]
    rt = nb * tt
    zero_state = states_in is None
    kern = functools.partial(_s5_kernel, layer=layer, nb=nb, tt=tt, zero_state=zero_state)
    cst = lambda a: pl.BlockSpec(a.shape, lambda i: (0,) * a.ndim)
    st = pl.BlockSpec((None, nb, S5_N), lambda i: (layer, 0, 0))
    consts = (w["lam"], w["bblk"], w["cblk"], w["d"], w["wglu"], w["mg"])
    args = [u_tm, *consts]
    in_specs = [pl.BlockSpec((rt, D_S5), lambda i: (i, 0))] + [cst(a) for a in consts]
    if not zero_state:
        args += list(states_in)
        in_specs += [st, st]
    aliases = {}
    if layer > 0:
        aliases = {len(args) + j: 1 + j for j in range(2)}
        args += list(states_prev)
        in_specs += [_ANY] * 2
    return pl.pallas_call(
        kern,
        grid=(rows // rt,),
        in_specs=in_specs,
        out_specs=[pl.BlockSpec((rt, D_S5), lambda i: (i, 0)), st, st],
        out_shape=[jax.ShapeDtypeStruct((rows, D_S5), F32), jax.ShapeDtypeStruct((depth, nb, S5_N), F32),
                   jax.ShapeDtypeStruct((depth, nb, S5_N), F32)],
        scratch_shapes=[pltpu.VMEM((rt, 2 * S5_N), F32), pltpu.VMEM((SUBLANES, 2 * S5_N), F32)],
        input_output_aliases=aliases,
        compiler_params=pltpu.CompilerParams(dimension_semantics=("arbitrary",), vmem_limit_bytes=VMEM_LIMIT),
        name="s5",
    )(*args)


def _block_diag(w):
    g, r, c = w.shape
    eye = jnp.eye(g, dtype=w.dtype)
    return jnp.einsum("grc,gh->grhc", w, eye).reshape(g * r, g * c)


def _layer_weights(l, ffn1_w1, ffn1_w3, ffn1_w2, ffn2_w1, ffn2_w3, ffn2_w2, norm_g, w_in, ig_bias, fg_bias,
                   s5_a_re, s5_a_im, s5_log_dt, s5_b_re, s5_b_im, s5_c_re, s5_c_im, s5_d, w_glu, conv_w,
                   mix_g, w_out):
    o = 4 * D_MLSTM
    win = w_in[l]
    win_pad = jnp.concatenate(
        [win[:, :o], win[:, o:o + 2 * MH], jnp.zeros((D_MODEL, GATE_W - 2 * MH), F32), win[:, o + 2 * MH:]],
        axis=1).astype(BF16)
    gb = jnp.concatenate([ig_bias[l], fg_bias[l], jnp.zeros((GATE_W - 2 * MH,), F32)])
    lam = jnp.stack([s5_a_re[l].reshape(-1), s5_a_im[l].reshape(-1),
                     jnp.repeat(s5_log_dt[l], S5_P)])
    bblk = jnp.concatenate([_block_diag(jnp.swapaxes(s5_b_re[l], 1, 2)),
                            _block_diag(jnp.swapaxes(s5_b_im[l], 1, 2))], axis=1).astype(BF16)
    cblk = jnp.concatenate([_block_diag(jnp.swapaxes(s5_c_re[l], 1, 2)),
                            -_block_diag(jnp.swapaxes(s5_c_im[l], 1, 2))], axis=0).astype(BF16)
    return dict(
        f1=(ffn1_w1[l].astype(BF16), ffn1_w3[l].astype(BF16), ffn1_w2[l].astype(BF16)),
        f2=(ffn2_w1[l].astype(BF16), ffn2_w3[l].astype(BF16), ffn2_w2[l].astype(BF16)),
        g=norm_g[l], win=win_pad, gb=gb.reshape(1, GATE_W), gbt=gb[:SUBLANES].reshape(SUBLANES, 1),
        lam=lam, bblk=bblk, cblk=cblk, d=s5_d[l].reshape(1, D_S5), wglu=w_glu[l].astype(BF16),
        cw=conv_w[l], mg=mix_g[l].reshape(1, D_MODEL), wo=w_out[l].astype(BF16))


def _run_group(x, states, weights, *, tm, tt, chunk=None, bb=None, nb=None):
    b, s, _ = x.shape
    depth = len(weights)
    xf = x.reshape(b * s, D_MODEL)
    mstates = sstates = None
    if states is not None:
        c_all, n_all, m_all, sr_all, si_all, cb_all = states
        ms_in = (c_all, n_all, m_all, cb_all)
        ss_in = (sr_all.reshape(depth, b, S5_N), si_all.reshape(depth, b, S5_N))
    else:
        ms_in = ss_in = None
    for l, w in enumerate(weights):
        x1, qkvo, gate, u, bch = _ffn_inproj(xf, w["g"], *w["f1"], w["win"], tm)
        u_tm = jnp.swapaxes(u.reshape(b, s, D_S5), 0, 1).reshape(s * b, D_S5)
        ob_tm, *sstates = _s5(u_tm, w, ss_in, sstates, layer=l, depth=depth, nb=b, tt=tt)
        ob = jnp.swapaxes(ob_tm.reshape(s, b, D_S5), 0, 1).reshape(b * s, D_S5)
        if chunk is not None:
            r3 = lambda a: a.reshape(b, s, a.shape[-1])
            oa, oc, *mstates = _mlstm_conv(r3(qkvo), r3(gate), r3(bch), w, ms_in, mstates,
                                           layer=l, depth=depth, bb=bb, chunk=chunk)
            xf = _outproj_ffn(x1, oa.reshape(b * s, D_MLSTM), ob, oc.reshape(b * s, D_CONV),
                              w["g"], w["wo"], *w["f2"], tm)
        else:
            oa, oc, *mstates = _mlstm_conv_short(qkvo, gate, bch, w, ms_in, mstates,
                                                 layer=l, depth=depth, b=b, s=s, nb=nb)
            xf = _outproj_ffn(x1, oa, ob, oc, w["g"], w["wo"], *w["f2"], tm)
    c1, n1, m1, cb1 = mstates
    sr1, si1 = sstates
    return xf.reshape(b, s, D_MODEL), (c1, n1, m1, sr1.reshape(depth, b, S5_G, S5_P),
                                       si1.reshape(depth, b, S5_G, S5_P), cb1)


def kernel(x_prompt, x_sample, state_mlstm_C, state_mlstm_n, state_mlstm_m, state_s5_re, state_s5_im,
           state_conv, ffn1_w1, ffn1_w3, ffn1_w2, ffn2_w1, ffn2_w3, ffn2_w2, norm_g, w_in, ig_bias, fg_bias,
           s5_a_re, s5_a_im, s5_log_dt, s5_b_re, s5_b_im, s5_c_re, s5_c_im, s5_d, w_glu, conv_w, mix_g, w_out):
    depth = norm_g.shape[0]
    params = (ffn1_w1, ffn1_w3, ffn1_w2, ffn2_w1, ffn2_w3, ffn2_w2, norm_g, w_in, ig_bias, fg_bias,
              s5_a_re, s5_a_im, s5_log_dt, s5_b_re, s5_b_im, s5_c_re, s5_c_im, s5_d, w_glu, conv_w, mix_g, w_out)
    weights = [_layer_weights(l, *params) for l in range(depth)]
    y_p, st_p = _run_group(x_prompt, None, weights, tm=512, tt=64, chunk=128, bb=4)
    y_s, st_s = _run_group(
        x_sample, (state_mlstm_C, state_mlstm_n, state_mlstm_m, state_s5_re, state_s5_im, state_conv),
        weights, tm=512, tt=x_sample.shape[1], nb=32)
    return (y_p, y_s, *st_p, *st_s)
```

```python
import functools

import jax
import jax.numpy as jnp
from jax import lax
from jax.experimental import pallas as pl
from jax.experimental.pallas import tpu as pltpu

F32 = jnp.float32
BF16 = jnp.bfloat16

D_MODEL = 1024
MH = 4
DH = 128
D_MLSTM = MH * DH
S5_CH = 16
S5_G = 16
S5_P = 64
D_S5 = S5_G * S5_CH
S5_N = S5_G * S5_P
D_CONV = 256
CONV_W = 3
EPS = 1e-6
GATE_W = 128
SUBLANES = 8
NEG = -1e30
VMEM_LIMIT = 56 * 1024 * 1024
ROW_PARTS = 2


def _dot(a, b):
    return jnp.dot(a, b, preferred_element_type=F32)


def _dot_nt(a, b):
    return lax.dot_general(a, b, (((1,), (1,)), ((), ())), preferred_element_type=F32)


def _dot_tn(a, b):
    return lax.dot_general(a, b, (((0,), (0,)), ((), ())), preferred_element_type=F32)


def _dot_exact(a, b):
    return jnp.dot(a, b, preferred_element_type=F32, precision=lax.Precision.HIGHEST)


def _dot_nt_exact(a, b):
    return lax.dot_general(a, b, (((1,), (1,)), ((), ())), preferred_element_type=F32,
                           precision=lax.Precision.HIGHEST)


def _rms(x):
    return x * lax.rsqrt(jnp.mean(x * x, axis=-1, keepdims=True) + EPS)


def _log_sigmoid(x):
    return jnp.minimum(x, 0.0) - jnp.log1p(jnp.exp(-jnp.abs(x)))


def _split_bf16(x, parts):
    out = []
    for _ in range(parts):
        p = x.astype(BF16)
        out.append(p)
        x = x - p.astype(F32)
    return jnp.concatenate(out, axis=1)


def _ffn_residual(xs, g_pre, g_post, w1, w3, w2):
    xn = [(_rms(x) * g_pre).astype(BF16) for x in xs]
    h1 = [_dot(v, w1[...]) for v in xn]
    h3 = [_dot(v, w3[...]) for v in xn]
    a = [(jax.nn.silu(p) * q).astype(BF16) for p, q in zip(h1, h3)]
    y = [_dot(v, w2[...]) for v in a]
    return [x + 0.5 * (_rms(t) * g_post) for x, t in zip(xs, y)]


def _row_parts(tm, parts):
    step = tm // parts
    return [slice(i * step, (i + 1) * step) for i in range(parts)]


def _const_spec(shape):
    nd = len(shape)
    return pl.BlockSpec(shape, lambda *_: (0,) * nd, pipeline_mode=pl.Buffered(1))


_ANY = pl.BlockSpec(memory_space=pl.ANY)


def _ffn_inproj_kernel(x_ref, g_ref, w1_ref, w3_ref, w2_ref, win_ref,
                       x1_ref, qkvo_ref, gate_ref, u_ref, bch_ref):
    parts = _row_parts(x_ref.shape[0], ROW_PARTS)
    x1 = _ffn_residual([x_ref[r, :] for r in parts], g_ref[0:1, :], g_ref[1:2, :], w1_ref, w3_ref, w2_ref)
    hn = [(_rms(v) * g_ref[2:3, :]).astype(BF16) for v in x1]
    proj = [_dot(v, win_ref[...]) for v in hn]
    o = 4 * D_MLSTM
    for r, v, p in zip(parts, x1, proj):
        x1_ref[r, :] = v
        qkvo_ref[r, :] = p[:, :o]
        gate_ref[r, :] = p[:, o:o + GATE_W]
        u_ref[r, :] = p[:, o + GATE_W:o + GATE_W + D_S5]
        bch_ref[r, :] = p[:, o + GATE_W + D_S5:]


def _ffn_inproj(x, g, w1, w3, w2, win, tm):
    t = x.shape[0]
    row = lambda w: pl.BlockSpec((tm, w), lambda i: (i, 0))
    return pl.pallas_call(
        _ffn_inproj_kernel,
        grid=(t // tm,),
        in_specs=[row(D_MODEL), _const_spec(g.shape), _const_spec(w1.shape), _const_spec(w3.shape),
                  _const_spec(w2.shape), _const_spec(win.shape)],
        out_specs=[row(D_MODEL), row(4 * D_MLSTM), row(GATE_W), row(D_S5), row(3 * D_CONV)],
        out_shape=[jax.ShapeDtypeStruct((t, D_MODEL), F32), jax.ShapeDtypeStruct((t, 4 * D_MLSTM), F32),
                   jax.ShapeDtypeStruct((t, GATE_W), F32), jax.ShapeDtypeStruct((t, D_S5), F32),
                   jax.ShapeDtypeStruct((t, 3 * D_CONV), F32)],
        compiler_params=pltpu.CompilerParams(dimension_semantics=("arbitrary",), vmem_limit_bytes=VMEM_LIMIT),
        name="ffn_inproj",
    )(x, g, w1, w3, w2, win)


def _outproj_ffn_kernel(x_ref, a_ref, b_ref, c_ref, g_ref, wo_ref, w1_ref, w3_ref, w2_ref, y_ref):
    parts = _row_parts(x_ref.shape[0], ROW_PARTS)
    mo = [(_dot(a_ref[r, :].astype(BF16), wo_ref[0:D_MLSTM, :])
           + _dot(b_ref[r, :].astype(BF16), wo_ref[D_MLSTM:D_MLSTM + D_S5, :])
           + _dot(c_ref[r, :].astype(BF16), wo_ref[D_MLSTM + D_S5:, :])) for r in parts]
    x2 = [x_ref[r, :] + _rms(v) * g_ref[3:4, :] for r, v in zip(parts, mo)]
    y = _ffn_residual(x2, g_ref[4:5, :], g_ref[5:6, :], w1_ref, w3_ref, w2_ref)
    for r, v in zip(parts, y):
        y_ref[r, :] = v


def _outproj_ffn(x, oa, ob, oc, g, wo, w1, w3, w2, tm):
    t = x.shape[0]
    row = lambda w: pl.BlockSpec((tm, w), lambda i: (i, 0))
    return pl.pallas_call(
        _outproj_ffn_kernel,
        grid=(t // tm,),
        in_specs=[row(D_MODEL), row(D_MLSTM), row(D_S5), row(D_CONV), _const_spec(g.shape),
                  _const_spec(wo.shape), _const_spec(w1.shape), _const_spec(w3.shape), _const_spec(w2.shape)],
        out_specs=row(D_MODEL),
        out_shape=jax.ShapeDtypeStruct((t, D_MODEL), F32),
        compiler_params=pltpu.CompilerParams(dimension_semantics=("arbitrary",), vmem_limit_bytes=VMEM_LIMIT),
        name="outproj_ffn",
    )(x, oa, ob, oc, g, wo, w1, w3, w2)


def _state_shapes(depth, b):
    return [jax.ShapeDtypeStruct((depth, b, MH, DH, DH), F32), jax.ShapeDtypeStruct((depth, b, MH, DH), F32),
            jax.ShapeDtypeStruct((depth, b, MH), F32), jax.ShapeDtypeStruct((depth, b, CONV_W - 1, D_CONV), F32)]


def _cummax_rows(x, rowi):
    s = 1
    while s < x.shape[0]:
        x = jnp.maximum(x, jnp.where(rowi >= s, pltpu.roll(x, s, 0), NEG))
        s *= 2
    return x


def _mlstm_conv_kernel(*refs, layer, bb, chunk, zero_state):
    (q_ref, k_ref, v_ref, o_ref, gate_ref, bg_ref, cg_ref, hc_ref, gb_ref, gbt_ref, cw_ref, mg_ref) = refs[:12]
    refs = refs[12:]
    if not zero_state:
        c0_ref, n0_ref, m0_ref, cb0_ref = refs[:4]
        refs = refs[4:]
    if layer > 0:
        refs = refs[4:]
    oa_ref, oc_ref, c_ref, n_ref, m_ref, cb_ref, m_scr = refs
    assert chunk == DH
    bi0 = pl.program_id(0) * bb
    ci = pl.program_id(1)

    @pl.when(ci == 0)
    def _():
        m_scr[...] = jnp.zeros_like(m_scr)
        if zero_state:
            c_ref[...] = jnp.zeros_like(c_ref)
            n_ref[...] = jnp.zeros_like(n_ref)
            cb_ref[...] = jnp.zeros_like(cb_ref)
        else:
            c_ref[...] = c0_ref[...]
            n_ref[...] = n0_ref[...]
            cb_ref[...] = cb0_ref[...]
            for bi in range(bb):
                m_scr[bi:bi + 1, 0:MH] = m0_ref[pl.ds(bi0 + bi, 1), :]

    row = lax.broadcasted_iota(jnp.int32, (chunk, chunk), 0)
    col = lax.broadcasted_iota(jnp.int32, (chunk, chunk), 1)
    causal = col <= row
    tril = jnp.where(causal, 1.0, 0.0).astype(BF16)
    rowg = lax.broadcasted_iota(jnp.int32, (chunk, GATE_W), 0)
    rowi = lax.broadcasted_iota(jnp.int32, (chunk, D_CONV), 0)
    sel8 = (lax.broadcasted_iota(jnp.int32, (SUBLANES, GATE_W), 0)
            == lax.broadcasted_iota(jnp.int32, (SUBLANES, GATE_W), 1)).astype(F32)
    spread = jnp.where((lax.broadcasted_iota(jnp.int32, (2 * GATE_W, D_MLSTM), 0) & (GATE_W - 1))
                       == (lax.broadcasted_iota(jnp.int32, (2 * GATE_W, D_MLSTM), 1) >> 7), 1.0, 0.0).astype(BF16)
    ones_b = jnp.ones((chunk, DH), BF16)
    zeros_b = jnp.zeros((chunk, DH), BF16)
    ones_tall = jnp.ones((2 * DH, DH), BF16)
    scale = DH ** -0.5

    units = [(bi, h) for bi in range(bb) for h in range(MH)]
    hsl = lambda h: slice(h * DH, (h + 1) * DH)
    g_c = [gate_ref[bi] + gb_ref[...] for bi in range(bb)]
    csum = [_dot(tril, _split_bf16(pltpu.roll(_log_sigmoid(g), GATE_W - MH, 1), 3)) for g in g_c]
    qb = {u: q_ref[u[0], :, hsl(u[1])].astype(BF16) for u in units}
    kb = {u: (k_ref[u[0], :, hsl(u[1])] * scale).astype(BF16) for u in units}
    s_qk = {u: _dot_nt(qb[u], kb[u]) for u in units}
    inter = {u: _dot_nt(qb[u], jnp.concatenate(
        [c_ref[u[0], u[1]].astype(BF16),
         jnp.broadcast_to(n_ref[u[0], u[1]:u[1] + 1, :], (DH, DH)).astype(BF16)], axis=0)) for u in units}
    rep, a_r, decay = [], [], []
    for bi in range(bb):
        f_c = csum[bi][:, :GATE_W] + csum[bi][:, GATE_W:2 * GATE_W] + csum[bi][:, 2 * GATE_W:]
        a_c = g_c[bi] - f_c
        m_prev = m_scr[bi:bi + 1, :]
        big_m = jnp.maximum(_cummax_rows(a_c, rowg), m_prev)
        mt = f_c + big_m
        m_new = mt[chunk - 1:chunk, :]
        f_last = f_c[chunk - 1:chunk, :]
        decay.append(jnp.exp(f_last + m_prev - m_new))
        stacked = jnp.concatenate([_split_bf16(-big_m, 2), _split_bf16(jnp.exp(m_prev - big_m), 2),
                                   _split_bf16(-mt, 2), _split_bf16(jnp.exp(a_c + (f_last - m_new)), 2)],
                                  axis=0)
        rep.append(_dot(stacked, spread))
        a_r.append(_dot_nt_exact(sel8, a_c))
        m_scr[bi:bi + 1, :] = m_new
        m_ref[pl.ds(bi0 + bi, 1), :] = m_new[:, 0:MH]
    intra = {}
    for u in units:
        bi, h = u
        w = jnp.exp(jnp.where(causal, rep[bi][0:chunk, hsl(h)] + a_r[bi][h:h + 1, :], NEG))
        rhs = jnp.concatenate([jnp.concatenate([v_ref[bi, :, hsl(h)].astype(BF16), ones_b], axis=1),
                               jnp.concatenate([zeros_b, ones_b], axis=1)], axis=0)
        intra[u] = _dot(_split_bf16(s_qk[u] * w, 2), rhs)
    hm, ssq = {}, {}
    for u in units:
        bi, h = u
        s_inter = rep[bi][chunk:2 * chunk, hsl(h)]
        e_floor = jnp.exp(rep[bi][2 * chunk:3 * chunk, hsl(h)])
        num = intra[u][:, :DH] + s_inter * inter[u][:, :DH]
        den = intra[u][:, DH:] + s_inter * inter[u][:, DH:]
        hm[u] = jax.nn.sigmoid(o_ref[bi, :, hsl(h)]) * (num / jnp.maximum(jnp.abs(den), e_floor))
        ssq[u] = _dot(_split_bf16(hm[u] * hm[u], 2), ones_tall)
    for u in units:
        bi, h = u
        oa_ref[bi, :, hsl(h)] = hm[u] * lax.rsqrt(ssq[u] * (1.0 / DH) + EPS) * mg_ref[:, hsl(h)]
        wk = rep[bi][3 * chunk:, hsl(h)]
        dec = decay[bi][:, h:h + 1]
        c_ref[bi, h] = dec * c_ref[bi, h] + _dot_tn((v_ref[bi, :, hsl(h)] * wk).astype(BF16), kb[u])
        n_ref[bi, h:h + 1, :] = (dec * n_ref[bi, h:h + 1, :]
                                 + jnp.sum(k_ref[bi, :, hsl(h)] * scale * wk, axis=0, keepdims=True))
    for bi in range(bb):
        z = cg_ref[bi] * hc_ref[bi]
        cb = cb_ref[bi]
        z1 = jnp.where(rowi == 0, cb[1:2, :], pltpu.roll(z, 1, 0))
        z2 = jnp.where(rowi == 0, cb[0:1, :], jnp.where(rowi == 1, cb[1:2, :], pltpu.roll(z, 2, 0)))
        yc = cw_ref[0:1, :] * z2 + cw_ref[1:2, :] * z1 + cw_ref[2:3, :] * z
        oc_ref[bi] = _rms(bg_ref[bi] * yc) * mg_ref[:, D_MLSTM + D_S5:]
        cb_ref[bi] = z[chunk - 2:chunk, :]


def _mlstm_conv(qkvo, gate, bch, w, states_in, states_prev, *, layer, depth, bb, chunk):
    b, s, _ = qkvo.shape
    assert s % chunk == 0 and b % bb == 0
    zero_state = states_in is None
    kern = functools.partial(_mlstm_conv_kernel, layer=layer, bb=bb, chunk=chunk, zero_state=zero_state)
    col = lambda wd, j: pl.BlockSpec((bb, chunk, wd), lambda i, c: (i, c, j))
    st = lambda shape: pl.BlockSpec((None, bb) + shape, lambda i, c: (layer, i) + (0,) * len(shape))
    m_spec = pl.BlockSpec((None, b, MH), lambda i, c: (layer, 0, 0))
    cst = lambda a: pl.BlockSpec(a.shape, lambda i, c: (0,) * a.ndim)
    state_specs = [st((MH, DH, DH)), st((MH, DH)), m_spec, st((CONV_W - 1, D_CONV))]
    consts = (w["gb"], w["gbt"], w["cw"], w["mg"])
    args = [qkvo, qkvo, qkvo, qkvo, gate, bch, bch, bch, *consts]
    in_specs = [col(D_MLSTM, 0), col(D_MLSTM, 1), col(D_MLSTM, 2), col(D_MLSTM, 3), col(GATE_W, 0),
                col(D_CONV, 0), col(D_CONV, 1), col(D_CONV, 2)] + [cst(a) for a in consts]
    if not zero_state:
        args += list(states_in)
        in_specs += state_specs
    aliases = {}
    if layer > 0:
        aliases = {len(args) + j: 2 + j for j in range(4)}
        args += list(states_prev)
        in_specs += [_ANY] * 4
    return pl.pallas_call(
        kern,
        grid=(b // bb, s // chunk),
        in_specs=in_specs,
        out_specs=[col(D_MLSTM, 0), col(D_CONV, 0)] + state_specs,
        out_shape=[jax.ShapeDtypeStruct((b, s, D_MLSTM), F32), jax.ShapeDtypeStruct((b, s, D_CONV), F32)]
        + _state_shapes(depth, b),
        scratch_shapes=[pltpu.VMEM((-(-bb // SUBLANES) * SUBLANES, GATE_W), F32)],
        input_output_aliases=aliases,
        compiler_params=pltpu.CompilerParams(dimension_semantics=("arbitrary", "arbitrary"),
                                             vmem_limit_bytes=VMEM_LIMIT),
        name="mlstm_conv",
    )(*args)


def _gate_terms(graw, gb_ref, gbt_ref, tril, triu):
    sel = (lax.broadcasted_iota(jnp.int32, (SUBLANES, GATE_W), 0)
           == lax.broadcasted_iota(jnp.int32, (SUBLANES, GATE_W), 1)).astype(F32)
    g_c = graw + gb_ref[...]
    lf_c = _log_sigmoid(g_c)
    g_r = _dot_nt_exact(sel, graw) + gbt_ref[...]
    lf_r = _log_sigmoid(g_r)
    return g_c, lf_c, _dot_exact(tril, lf_c), g_r, _dot_exact(lf_r, triu)


def _head_scores(h, g_c, bt_c, g_r, bt_r, m_prev, mask):
    it_row = g_r[h:h + 1, :]
    bt_row = bt_r[MH + h:MH + h + 1, :]
    it_col = g_c[:, h:h + 1]
    bt_col = bt_c[:, MH + h:MH + h + 1]
    dmat = jnp.where(mask, bt_col - bt_row + it_row, NEG)
    inter = bt_col + m_prev
    mt = jnp.maximum(jnp.max(dmat, axis=-1, keepdims=True), inter)
    return jnp.exp(dmat - mt), jnp.exp(inter - mt), mt, it_col, bt_col


def _head_output(sqk, vb, s_inter, qc, qn, mt, o, mg):
    num = _dot(sqk.astype(BF16), vb) + s_inter * qc
    den = jnp.sum(sqk, axis=-1, keepdims=True) + s_inter * qn
    hm = jax.nn.sigmoid(o) * (num / jnp.maximum(jnp.abs(den), jnp.exp(-mt)))
    return _rms(hm) * mg


def _bcast_block_last(x, t, s):
    out = x
    for d in range(1, s):
        out = jnp.where(t == s - 1 - d, pltpu.roll(x, x.shape[0] - d, 0), out)
    return out


def _mlstm_conv_short_kernel(*refs, layer, nb, s):
    (q_ref, k_ref, v_ref, o_ref, gate_ref, bg_ref, cg_ref, hc_ref, gb_ref, gbt_ref, cw_ref, mg_ref,
     c0_ref, n0_ref, m0_ref, cb0_ref) = refs[:16]
    refs = refs[16:]
    if layer > 0:
        refs = refs[4:]
    oa_ref, oc_ref, c_ref, n_ref, m_ref, cb_ref, qc_scr, st_scr = refs
    r = nb * s
    per_tile = SUBLANES // s
    shift = s.bit_length() - 1
    row = lax.broadcasted_iota(jnp.int32, (r, r), 0)
    col = lax.broadcasted_iota(jnp.int32, (r, r), 1)
    same = (row >> shift) == (col >> shift)
    mask = same & (col <= row)
    tril = mask.astype(F32)
    triu = (same & (row <= col)).astype(F32)
    expand = ((lax.broadcasted_iota(jnp.int32, (r, nb), 0) >> shift)
              == lax.broadcasted_iota(jnp.int32, (r, nb), 1)).astype(F32)
    gather = (lax.broadcasted_iota(jnp.int32, (nb, r), 0)
              == (lax.broadcasted_iota(jnp.int32, (nb, r), 1) >> shift)).astype(F32)
    t128 = lax.broadcasted_iota(jnp.int32, (r, DH), 0) & (s - 1)
    sub8 = lax.broadcasted_iota(jnp.int32, (SUBLANES, DH), 0) >> shift
    scale = DH ** -0.5

    g_c, lf_c, bt_c, g_r, bt_r = _gate_terms(gate_ref[...], gb_ref, gbt_ref, tril, triu)
    bt_last_c = _dot_exact(same.astype(F32), lf_c)
    m_rows = _dot_exact(expand, m0_ref[...])
    for h in range(MH):
        hs = slice(h * DH, (h + 1) * DH)
        m_prev = m_rows[:, h:h + 1]
        w, s_inter, mt, it_col, bt_col = _head_scores(h, g_c, bt_c, g_r, bt_r, m_prev, mask)
        qh = q_ref[:, hs]
        kh = k_ref[:, hs] * scale
        vh = v_ref[:, hs]
        qb = qh.astype(BF16)
        kb = kh.astype(BF16)
        sqk = _dot_nt(qb, kb) * w
        for j in range(r // SUBLANES):
            rows = slice(j * SUBLANES, (j + 1) * SUBLANES)
            acc = None
            q8 = q_ref[rows, hs].astype(BF16)
            for p in range(per_tile):
                part = _dot_nt(q8, c0_ref[j * per_tile + p, h].astype(BF16))
                acc = part if acc is None else jnp.where(sub8 == p, part, acc)
            qc_scr[rows, :] = acc
        n_rows = _dot_exact(expand, n0_ref[:, h, :])
        qn = jnp.sum(qh * n_rows, axis=-1, keepdims=True)
        oa_ref[:, hs] = _head_output(sqk, vh.astype(BF16), s_inter, qc_scr[...], qn, mt,
                                     o_ref[:, hs], mg_ref[:, hs])
        m_new = _bcast_block_last(jnp.broadcast_to(mt, (r, DH)), t128, s)
        bt_last = bt_last_c[:, MH + h:MH + h + 1]
        wk = jnp.exp(bt_last - bt_col + it_col - m_new)
        decay = jnp.exp(bt_last + m_prev - m_new)
        vw = vh * wk
        for j in range(r // SUBLANES):
            rows = slice(j * SUBLANES, (j + 1) * SUBLANES)
            k8 = (k_ref[rows, hs] * scale).astype(BF16)
            for p in range(per_tile):
                bidx = j * per_tile + p
                last = bidx * s + s - 1
                upd = _dot_tn(jnp.where(sub8 == p, vw[rows], 0.0).astype(BF16), k8)
                c_ref[bidx, h] = decay[last:last + 1, 0:1] * c0_ref[bidx, h] + upd
        st_scr[0] = decay
        st_scr[1] = m_new
        last_rows = pl.ds(s - 1, nb, stride=s)
        n_ref[:, h, :] = st_scr[0, last_rows, :] * n0_ref[:, h, :] + _dot_exact(gather, kh * wk)
        m_ref[:, h:h + 1] = st_scr[1, last_rows, :][:, 0:1]
    t256 = lax.broadcasted_iota(jnp.int32, (r, D_CONV), 0) & (s - 1)
    z = cg_ref[...] * hc_ref[...]
    cb_a = _dot_exact(expand, cb0_ref[:, 0, :])
    cb_b = _dot_exact(expand, cb0_ref[:, 1, :])
    z1 = jnp.where(t256 == 0, cb_b, pltpu.roll(z, 1, 0))
    z2 = jnp.where(t256 == 0, cb_a, jnp.where(t256 == 1, cb_b, pltpu.roll(z, 2, 0)))
    yc = cw_ref[0:1, :] * z2 + cw_ref[1:2, :] * z1 + cw_ref[2:3, :] * z
    oc_ref[...] = _rms(bg_ref[...] * yc) * mg_ref[:, D_MLSTM + D_S5:]
    for half in range(D_CONV // DH):
        lanes = slice(half * DH, (half + 1) * DH)
        st_scr[half] = z[:, lanes]
        cb_ref[:, 0, lanes] = st_scr[half, pl.ds(s - 2, nb, stride=s), :]
        cb_ref[:, 1, lanes] = st_scr[half, pl.ds(s - 1, nb, stride=s), :]


def _mlstm_conv_short(qkvo, gate, bch, w, states_in, states_prev, *, layer, depth, b, s, nb):
    assert SUBLANES % s == 0 and s >= CONV_W - 1 and b % nb == 0 and (nb * s) % SUBLANES == 0
    r = nb * s
    kern = functools.partial(_mlstm_conv_short_kernel, layer=layer, nb=nb, s=s)
    col = lambda wd, j: pl.BlockSpec((r, wd), lambda i: (i, j))
    st = lambda shape: pl.BlockSpec((None, nb) + shape, lambda i: (layer, i) + (0,) * len(shape))
    cst = lambda a: pl.BlockSpec(a.shape, lambda i: (0,) * a.ndim)
    state_specs = [st((MH, DH, DH)), st((MH, DH)), st((MH,)), st((CONV_W - 1, D_CONV))]
    consts = (w["gb"], w["gbt"], w["cw"], w["mg"])
    args = [qkvo, qkvo, qkvo, qkvo, gate, bch, bch, bch, *consts, *states_in]
    in_specs = [col(D_MLSTM, 0), col(D_MLSTM, 1), col(D_MLSTM, 2), col(D_MLSTM, 3), col(GATE_W, 0),
                col(D_CONV, 0), col(D_CONV, 1), col(D_CONV, 2)] + [cst(a) for a in consts] + state_specs
    aliases = {}
    if layer > 0:
        aliases = {len(args) + j: 2 + j for j in range(4)}
        args += list(states_prev)
        in_specs += [_ANY] * 4
    return pl.pallas_call(
        kern,
        grid=(b // nb,),
        in_specs=in_specs,
        out_specs=[col(D_MLSTM, 0), col(D_CONV, 0)] + state_specs,
        out_shape=[jax.ShapeDtypeStruct((b * s, D_MLSTM), F32), jax.ShapeDtypeStruct((b * s, D_CONV), F32)]
        + _state_shapes(depth, b),
        scratch_shapes=[pltpu.VMEM((r, DH), F32), pltpu.VMEM((2, r, DH), F32)],
        input_output_aliases=aliases,
        compiler_params=pltpu.CompilerParams(dimension_semantics=("arbitrary",), vmem_limit_bytes=VMEM_LIMIT),
        name="mlstm_conv_short",
    )(*args)


def _s5_kernel(*refs, layer, nb, tt, zero_state):
    u_ref, lam_ref, bblk_ref, cblk_ref, d_ref, wglu_ref, mg_ref = refs[:7]
    refs = refs[7:]
    if not zero_state:
        hr0_ref, hi0_ref = refs[:2]
        refs = refs[2:]
    if layer > 0:
        refs = refs[2:]
    ob_ref, hr_ref, hi_ref, xs_ref, a_ref = refs
    ti = pl.program_id(0)
    n = S5_N

    @pl.when(ti == 0)
    def _():
        hr_ref[...] = jnp.zeros_like(hr_ref) if zero_state else hr0_ref[...]
        hi_ref[...] = jnp.zeros_like(hi_ref) if zero_state else hi0_ref[...]

    lre = jnp.minimum(lam_ref[0:1, :], -1e-4)
    lim = lam_ref[1:2, :]
    dt = jnp.exp(lam_ref[2:3, :])
    mag = jnp.exp(lre * dt)
    ab_re = mag * jnp.cos(lim * dt)
    ab_im = mag * jnp.sin(lim * dt)
    den = lre * lre + lim * lim
    nr = ab_re - 1.0
    fre = (nr * lre + ab_im * lim) / den
    fim = (ab_im * lre - nr * lim) / den
    a_ref[:, :n] = jnp.broadcast_to(ab_re, (SUBLANES, n))
    a_ref[:, n:] = jnp.broadcast_to(ab_im, (SUBLANES, n))

    u = u_ref[...]
    bu = _dot(u.astype(BF16), bblk_ref[...])
    bu_re = bu[:, :n]
    bu_im = bu[:, n:]
    xs_ref[:, :n] = fre * bu_re - fim * bu_im
    xs_ref[:, n:] = fre * bu_im + fim * bu_re

    lc = 512
    for rg in range(nb // SUBLANES):
        rs = slice(rg * SUBLANES, (rg + 1) * SUBLANES)
        for c0 in range(0, n, lc):
            ar = a_ref[:, c0:c0 + lc]
            ai = a_ref[:, n + c0:n + c0 + lc]

            def step(t, hc, rg=rg, c0=c0, ar=ar, ai=ai):
                hr, hi = hc
                r0 = pl.multiple_of(t * nb + rg * SUBLANES, SUBLANES)
                xr = xs_ref[pl.ds(r0, SUBLANES), c0:c0 + lc]
                xi = xs_ref[pl.ds(r0, SUBLANES), n + c0:n + c0 + lc]
                nr_ = ar * hr - ai * hi + xr
                ni_ = ar * hi + ai * hr + xi
                xs_ref[pl.ds(r0, SUBLANES), c0:c0 + lc] = nr_
                xs_ref[pl.ds(r0, SUBLANES), n + c0:n + c0 + lc] = ni_
                return nr_, ni_

            hr, hi = lax.fori_loop(0, tt, step, (hr_ref[rs, c0:c0 + lc], hi_ref[rs, c0:c0 + lc]))
            hr_ref[rs, c0:c0 + lc] = hr
            hi_ref[rs, c0:c0 + lc] = hi

    y = _dot(xs_ref[...].astype(BF16), cblk_ref[...]) + d_ref[...] * u
    g = jax.nn.gelu(y)
    ob = g * jax.nn.sigmoid(_dot(g.astype(BF16), wglu_ref[...]))
    ob_ref[...] = _rms(ob) * mg_ref[:, D_MLSTM:D_MLSTM + D_S5]


def _s5(u_tm, w, states_in, states_prev, *, layer, depth, nb, tt):
    rows = u_tm.shape[0]
    rt = nb * tt
    zero_state = states_in is None
    kern = functools.partial(_s5_kernel, layer=layer, nb=nb, tt=tt, zero_state=zero_state)
    cst = lambda a: pl.BlockSpec(a.shape, lambda i: (0,) * a.ndim)
    st = pl.BlockSpec((None, nb, S5_N), lambda i: (layer, 0, 0))
    consts = (w["lam"], w["bblk"], w["cblk"], w["d"], w["wglu"], w["mg"])
    args = [u_tm, *consts]
    in_specs = [pl.BlockSpec((rt, D_S5), lambda i: (i, 0))] + [cst(a) for a in consts]
    if not zero_state:
        args += list(states_in)
        in_specs += [st, st]
    aliases = {}
    if layer > 0:
        aliases = {len(args) + j: 1 + j for j in range(2)}
        args += list(states_prev)
        in_specs += [_ANY] * 2
    return pl.pallas_call(
        kern,
        grid=(rows // rt,),
        in_specs=in_specs,
        out_specs=[pl.BlockSpec((rt, D_S5), lambda i: (i, 0)), st, st],
        out_shape=[jax.ShapeDtypeStruct((rows, D_S5), F32), jax.ShapeDtypeStruct((depth, nb, S5_N), F32),
                   jax.ShapeDtypeStruct((depth, nb, S5_N), F32)],
        scratch_shapes=[pltpu.VMEM((rt, 2 * S5_N), F32), pltpu.VMEM((SUBLANES, 2 * S5_N), F32)],
        input_output_aliases=aliases,
        compiler_params=pltpu.CompilerParams(dimension_semantics=("arbitrary",), vmem_limit_bytes=VMEM_LIMIT),
        name="s5",
    )(*args)


def _block_diag(w):
    g, r, c = w.shape
    eye = jnp.eye(g, dtype=w.dtype)
    return jnp.einsum("grc,gh->grhc", w, eye).reshape(g * r, g * c)


def _layer_weights(l, ffn1_w1, ffn1_w3, ffn1_w2, ffn2_w1, ffn2_w3, ffn2_w2, norm_g, w_in, ig_bias, fg_bias,
                   s5_a_re, s5_a_im, s5_log_dt, s5_b_re, s5_b_im, s5_c_re, s5_c_im, s5_d, w_glu, conv_w,
                   mix_g, w_out):
    o = 4 * D_MLSTM
    win = w_in[l]
    win_pad = jnp.concatenate(
        [win[:, :o], win[:, o:o + 2 * MH], jnp.zeros((D_MODEL, GATE_W - 2 * MH), F32), win[:, o + 2 * MH:]],
        axis=1).astype(BF16)
    gb = jnp.concatenate([ig_bias[l], fg_bias[l], jnp.zeros((GATE_W - 2 * MH,), F32)])
    lam = jnp.stack([s5_a_re[l].reshape(-1), s5_a_im[l].reshape(-1),
                     jnp.repeat(s5_log_dt[l], S5_P)])
    bblk = jnp.concatenate([_block_diag(jnp.swapaxes(s5_b_re[l], 1, 2)),
                            _block_diag(jnp.swapaxes(s5_b_im[l], 1, 2))], axis=1).astype(BF16)
    cblk = jnp.concatenate([_block_diag(jnp.swapaxes(s5_c_re[l], 1, 2)),
                            -_block_diag(jnp.swapaxes(s5_c_im[l], 1, 2))], axis=0).astype(BF16)
    return dict(
        f1=(ffn1_w1[l].astype(BF16), ffn1_w3[l].astype(BF16), ffn1_w2[l].astype(BF16)),
        f2=(ffn2_w1[l].astype(BF16), ffn2_w3[l].astype(BF16), ffn2_w2[l].astype(BF16)),
        g=norm_g[l], win=win_pad, gb=gb.reshape(1, GATE_W), gbt=gb[:SUBLANES].reshape(SUBLANES, 1),
        lam=lam, bblk=bblk, cblk=cblk, d=s5_d[l].reshape(1, D_S5), wglu=w_glu[l].astype(BF16),
        cw=conv_w[l], mg=mix_g[l].reshape(1, D_MODEL), wo=w_out[l].astype(BF16))


def _run_group(x, states, weights, *, tm, tt, chunk=None, bb=None, nb=None):
    b, s, _ = x.shape
    depth = len(weights)
    xf = x.reshape(b * s, D_MODEL)
    mstates = sstates = None
    if states is not None:
        c_all, n_all, m_all, sr_all, si_all, cb_all = states
        ms_in = (c_all, n_all, m_all, cb_all)
        ss_in = (sr_all.reshape(depth, b, S5_N), si_all.reshape(depth, b, S5_N))
    else:
        ms_in = ss_in = None
    for l, w in enumerate(weights):
        x1, qkvo, gate, u, bch = _ffn_inproj(xf, w["g"], *w["f1"], w["win"], tm)
        u_tm = jnp.swapaxes(u.reshape(b, s, D_S5), 0, 1).reshape(s * b, D_S5)
        ob_tm, *sstates = _s5(u_tm, w, ss_in, sstates, layer=l, depth=depth, nb=b, tt=tt)
        ob = jnp.swapaxes(ob_tm.reshape(s, b, D_S5), 0, 1).reshape(b * s, D_S5)
        if chunk is not None:
            r3 = lambda a: a.reshape(b, s, a.shape[-1])
            oa, oc, *mstates = _mlstm_conv(r3(qkvo), r3(gate), r3(bch), w, ms_in, mstates,
                                           layer=l, depth=depth, bb=bb, chunk=chunk)
            xf = _outproj_ffn(x1, oa.reshape(b * s, D_MLSTM), ob, oc.reshape(b * s, D_CONV),
                              w["g"], w["wo"], *w["f2"], tm)
        else:
            oa, oc, *mstates = _mlstm_conv_short(qkvo, gate, bch, w, ms_in, mstates,
                                                 layer=l, depth=depth, b=b, s=s, nb=nb)
            xf = _outproj_ffn(x1, oa, ob, oc, w["g"], w["wo"], *w["f2"], tm)
    c1, n1, m1, cb1 = mstates
    sr1, si1 = sstates
    return xf.reshape(b, s, D_MODEL), (c1, n1, m1, sr1.reshape(depth, b, S5_G, S5_P),
                                       si1.reshape(depth, b, S5_G, S5_P), cb1)


def kernel(x_prompt, x_sample, state_mlstm_C, state_mlstm_n, state_mlstm_m, state_s5_re, state_s5_im,
           state_conv, ffn1_w1, ffn1_w3, ffn1_w2, ffn2_w1, ffn2_w3, ffn2_w2, norm_g, w_in, ig_bias, fg_bias,
           s5_a_re, s5_a_im, s5_log_dt, s5_b_re, s5_b_im, s5_c_re, s5_c_im, s5_d, w_glu, conv_w, mix_g, w_out):
    depth = norm_g.shape[0]
    params = (ffn1_w1, ffn1_w3, ffn1_w2, ffn2_w1, ffn2_w3, ffn2_w2, norm_g, w_in, ig_bias, fg_bias,
              s5_a_re, s5_a_im, s5_log_dt, s5_b_re, s5_b_im, s5_c_re, s5_c_im, s5_d, w_glu, conv_w, mix_g, w_out)
    weights = [_layer_weights(l, *params) for l in range(depth)]
    y_p, st_p = _run_group(x_prompt, None, weights, tm=512, tt=64, chunk=128, bb=4)
    y_s, st_s = _run_group(
        x_sample, (state_mlstm_C, state_mlstm_n, state_mlstm_m, state_s5_re, state_s5_im, state_conv),
        weights, tm=512, tt=x_sample.shape[1], nb=32)
    return (y_p, y_s, *st_p, *st_s)
```

```python
import functools

import jax
import jax.numpy as jnp
from jax import lax
from jax.experimental import pallas as pl
from jax.experimental.pallas import tpu as pltpu

F32 = jnp.float32
BF16 = jnp.bfloat16

D_MODEL = 1024
MH = 4
DH = 128
D_MLSTM = MH * DH
S5_CH = 16
S5_G = 16
S5_P = 64
D_S5 = S5_G * S5_CH
S5_N = S5_G * S5_P
D_CONV = 256
CONV_W = 3
EPS = 1e-6
GATE_W = 128
SUBLANES = 8
NEG = -1e30
VMEM_LIMIT = 56 * 1024 * 1024
ROW_PARTS = 2


def _dot(a, b):
    return jnp.dot(a, b, preferred_element_type=F32)


def _dot_nt(a, b):
    return lax.dot_general(a, b, (((1,), (1,)), ((), ())), preferred_element_type=F32)


def _dot_tn(a, b):
    return lax.dot_general(a, b, (((0,), (0,)), ((), ())), preferred_element_type=F32)


def _dot_exact(a, b):
    return jnp.dot(a, b, preferred_element_type=F32, precision=lax.Precision.HIGHEST)


def _dot_nt_exact(a, b):
    return lax.dot_general(a, b, (((1,), (1,)), ((), ())), preferred_element_type=F32,
                           precision=lax.Precision.HIGHEST)


def _rms(x):
    return x * lax.rsqrt(jnp.mean(x * x, axis=-1, keepdims=True) + EPS)


def _log_sigmoid(x):
    return jnp.minimum(x, 0.0) - jnp.log1p(jnp.exp(-jnp.abs(x)))


def _split_bf16(x, parts):
    out = []
    for _ in range(parts):
        p = x.astype(BF16)
        out.append(p)
        x = x - p.astype(F32)
    return jnp.concatenate(out, axis=1)


def _ffn_residual(xs, g_pre, g_post, w1, w3, w2):
    xn = [(_rms(x) * g_pre).astype(BF16) for x in xs]
    h1 = [_dot(v, w1[...]) for v in xn]
    h3 = [_dot(v, w3[...]) for v in xn]
    a = [(jax.nn.silu(p) * q).astype(BF16) for p, q in zip(h1, h3)]
    y = [_dot(v, w2[...]) for v in a]
    return [x + 0.5 * (_rms(t) * g_post) for x, t in zip(xs, y)]


def _row_parts(tm, parts):
    step = tm // parts
    return [slice(i * step, (i + 1) * step) for i in range(parts)]


def _const_spec(shape):
    nd = len(shape)
    return pl.BlockSpec(shape, lambda *_: (0,) * nd, pipeline_mode=pl.Buffered(1))


_ANY = pl.BlockSpec(memory_space=pl.ANY)


def _ffn_inproj_kernel(x_ref, g_ref, w1_ref, w3_ref, w2_ref, win_ref,
                       x1_ref, qkvo_ref, gate_ref, u_ref, bch_ref):
    parts = _row_parts(x_ref.shape[0], ROW_PARTS)
    x1 = _ffn_residual([x_ref[r, :] for r in parts], g_ref[0:1, :], g_ref[1:2, :], w1_ref, w3_ref, w2_ref)
    hn = [(_rms(v) * g_ref[2:3, :]).astype(BF16) for v in x1]
    proj = [_dot(v, win_ref[...]) for v in hn]
    o = 4 * D_MLSTM
    for r, v, p in zip(parts, x1, proj):
        x1_ref[r, :] = v
        qkvo_ref[r, :] = p[:, :o]
        gate_ref[r, :] = p[:, o:o + GATE_W]
        u_ref[r, :] = p[:, o + GATE_W:o + GATE_W + D_S5]
        bch_ref[r, :] = p[:, o + GATE_W + D_S5:]


def _ffn_inproj(x, g, w1, w3, w2, win, tm):
    t = x.shape[0]
    row = lambda w: pl.BlockSpec((tm, w), lambda i: (i, 0))
    return pl.pallas_call(
        _ffn_inproj_kernel,
        grid=(t // tm,),
        in_specs=[row(D_MODEL), _const_spec(g.shape), _const_spec(w1.shape), _const_spec(w3.shape),
                  _const_spec(w2.shape), _const_spec(win.shape)],
        out_specs=[row(D_MODEL), row(4 * D_MLSTM), row(GATE_W), row(D_S5), row(3 * D_CONV)],
        out_shape=[jax.ShapeDtypeStruct((t, D_MODEL), F32), jax.ShapeDtypeStruct((t, 4 * D_MLSTM), F32),
                   jax.ShapeDtypeStruct((t, GATE_W), F32), jax.ShapeDtypeStruct((t, D_S5), F32),
                   jax.ShapeDtypeStruct((t, 3 * D_CONV), F32)],
        compiler_params=pltpu.CompilerParams(dimension_semantics=("arbitrary",), vmem_limit_bytes=VMEM_LIMIT),
        name="ffn_inproj",
    )(x, g, w1, w3, w2, win)


def _outproj_ffn_kernel(x_ref, a_ref, b_ref, c_ref, g_ref, wo_ref, w1_ref, w3_ref, w2_ref, y_ref):
    parts = _row_parts(x_ref.shape[0], ROW_PARTS)
    mo = [(_dot(a_ref[r, :].astype(BF16), wo_ref[0:D_MLSTM, :])
           + _dot(b_ref[r, :].astype(BF16), wo_ref[D_MLSTM:D_MLSTM + D_S5, :])
           + _dot(c_ref[r, :].astype(BF16), wo_ref[D_MLSTM + D_S5:, :])) for r in parts]
    x2 = [x_ref[r, :] + _rms(v) * g_ref[3:4, :] for r, v in zip(parts, mo)]
    y = _ffn_residual(x2, g_ref[4:5, :], g_ref[5:6, :], w1_ref, w3_ref, w2_ref)
    for r, v in zip(parts, y):
        y_ref[r, :] = v


def _outproj_ffn(x, oa, ob, oc, g, wo, w1, w3, w2, tm):
    t = x.shape[0]
    row = lambda w: pl.BlockSpec((tm, w), lambda i: (i, 0))
    return pl.pallas_call(
        _outproj_ffn_kernel,
        grid=(t // tm,),
        in_specs=[row(D_MODEL), row(D_MLSTM), row(D_S5), row(D_CONV), _const_spec(g.shape),
                  _const_spec(wo.shape), _const_spec(w1.shape), _const_spec(w3.shape), _const_spec(w2.shape)],
        out_specs=row(D_MODEL),
        out_shape=jax.ShapeDtypeStruct((t, D_MODEL), F32),
        compiler_params=pltpu.CompilerParams(dimension_semantics=("arbitrary",), vmem_limit_bytes=VMEM_LIMIT),
        name="outproj_ffn",
    )(x, oa, ob, oc, g, wo, w1, w3, w2)


def _state_shapes(depth, b):
    return [jax.ShapeDtypeStruct((depth, b, MH, DH, DH), F32), jax.ShapeDtypeStruct((depth, b, MH, DH), F32),
            jax.ShapeDtypeStruct((depth, b, MH), F32), jax.ShapeDtypeStruct((depth, b, CONV_W - 1, D_CONV), F32)]


def _cummax_rows(x, rowi):
    s = 1
    while s < x.shape[0]:
        x = jnp.maximum(x, jnp.where(rowi >= s, pltpu.roll(x, s, 0), NEG))
        s *= 2
    return x


def _mlstm_conv_kernel(*refs, layer, bb, chunk, zero_state):
    (q_ref, k_ref, v_ref, o_ref, gate_ref, bg_ref, cg_ref, hc_ref, gb_ref, gbt_ref, cw_ref, mg_ref) = refs[:12]
    refs = refs[12:]
    if not zero_state:
        c0_ref, n0_ref, m0_ref, cb0_ref = refs[:4]
        refs = refs[4:]
    if layer > 0:
        refs = refs[4:]
    oa_ref, oc_ref, c_ref, n_ref, m_ref, cb_ref, m_scr = refs
    other_layers = ()
    if layer == 0:
        other_layers = tuple(r.at[1:] for r in (c_ref, n_ref, m_ref, cb_ref) if r.shape[0] > 1)
        c_ref, n_ref, m_ref, cb_ref = c_ref.at[0], n_ref.at[0], m_ref.at[0], cb_ref.at[0]
    assert chunk == DH
    bi0 = pl.program_id(0) * bb
    ci = pl.program_id(1)

    @pl.when(ci == 0)
    def _():
        for r in other_layers:
            r[...] = jnp.zeros_like(r)
        m_scr[...] = jnp.zeros_like(m_scr)
        if zero_state:
            c_ref[...] = jnp.zeros_like(c_ref)
            n_ref[...] = jnp.zeros_like(n_ref)
            cb_ref[...] = jnp.zeros_like(cb_ref)
        else:
            c_ref[...] = c0_ref[...]
            n_ref[...] = n0_ref[...]
            cb_ref[...] = cb0_ref[...]
            for bi in range(bb):
                m_scr[bi:bi + 1, 0:MH] = m0_ref[pl.ds(bi0 + bi, 1), :]

    row = lax.broadcasted_iota(jnp.int32, (chunk, chunk), 0)
    col = lax.broadcasted_iota(jnp.int32, (chunk, chunk), 1)
    causal = col <= row
    tril = jnp.where(causal, 1.0, 0.0).astype(BF16)
    rowg = lax.broadcasted_iota(jnp.int32, (chunk, GATE_W), 0)
    rowi = lax.broadcasted_iota(jnp.int32, (chunk, D_CONV), 0)
    sel8 = (lax.broadcasted_iota(jnp.int32, (SUBLANES, GATE_W), 0)
            == lax.broadcasted_iota(jnp.int32, (SUBLANES, GATE_W), 1)).astype(F32)
    spread = jnp.where((lax.broadcasted_iota(jnp.int32, (2 * GATE_W, D_MLSTM), 0) & (GATE_W - 1))
                       == (lax.broadcasted_iota(jnp.int32, (2 * GATE_W, D_MLSTM), 1) >> 7), 1.0, 0.0).astype(BF16)
    ones_b = jnp.ones((chunk, DH), BF16)
    zeros_b = jnp.zeros((chunk, DH), BF16)
    ones_tall = jnp.ones((2 * DH, DH), BF16)
    scale = DH ** -0.5

    units = [(bi, h) for bi in range(bb) for h in range(MH)]
    hsl = lambda h: slice(h * DH, (h + 1) * DH)
    g_c = [gate_ref[bi] + gb_ref[...] for bi in range(bb)]
    csum = [_dot(tril, _split_bf16(pltpu.roll(_log_sigmoid(g), GATE_W - MH, 1), 3)) for g in g_c]
    qb = {u: q_ref[u[0], :, hsl(u[1])].astype(BF16) for u in units}
    kb = {u: (k_ref[u[0], :, hsl(u[1])] * scale).astype(BF16) for u in units}
    s_qk = {u: _dot_nt(qb[u], kb[u]) for u in units}
    inter = {u: _dot_nt(qb[u], jnp.concatenate(
        [c_ref[u[0], u[1]].astype(BF16),
         jnp.broadcast_to(n_ref[u[0], u[1]:u[1] + 1, :], (DH, DH)).astype(BF16)], axis=0)) for u in units}
    rep, a_r, decay = [], [], []
    for bi in range(bb):
        f_c = csum[bi][:, :GATE_W] + csum[bi][:, GATE_W:2 * GATE_W] + csum[bi][:, 2 * GATE_W:]
        a_c = g_c[bi] - f_c
        m_prev = m_scr[bi:bi + 1, :]
        big_m = jnp.maximum(_cummax_rows(a_c, rowg), m_prev)
        mt = f_c + big_m
        m_new = mt[chunk - 1:chunk, :]
        f_last = f_c[chunk - 1:chunk, :]
        decay.append(jnp.exp(f_last + m_prev - m_new))
        stacked = jnp.concatenate([_split_bf16(-big_m, 2), _split_bf16(jnp.exp(m_prev - big_m), 2),
                                   _split_bf16(-mt, 2), _split_bf16(jnp.exp(a_c + (f_last - m_new)), 2)],
                                  axis=0)
        rep.append(_dot(stacked, spread))
        a_r.append(_dot_nt_exact(sel8, a_c))
        m_scr[bi:bi + 1, :] = m_new
        m_ref[pl.ds(bi0 + bi, 1), :] = m_new[:, 0:MH]
    intra = {}
    for u in units:
        bi, h = u
        w = jnp.exp(jnp.where(causal, rep[bi][0:chunk, hsl(h)] + a_r[bi][h:h + 1, :], NEG))
        rhs = jnp.concatenate([jnp.concatenate([v_ref[bi, :, hsl(h)].astype(BF16), ones_b], axis=1),
                               jnp.concatenate([zeros_b, ones_b], axis=1)], axis=0)
        intra[u] = _dot(_split_bf16(s_qk[u] * w, 2), rhs)
    hm, ssq = {}, {}
    for u in units:
        bi, h = u
        s_inter = rep[bi][chunk:2 * chunk, hsl(h)]
        e_floor = jnp.exp(rep[bi][2 * chunk:3 * chunk, hsl(h)])
        num = intra[u][:, :DH] + s_inter * inter[u][:, :DH]
        den = intra[u][:, DH:] + s_inter * inter[u][:, DH:]
        hm[u] = jax.nn.sigmoid(o_ref[bi, :, hsl(h)]) * (num / jnp.maximum(jnp.abs(den), e_floor))
        ssq[u] = _dot(_split_bf16(hm[u] * hm[u], 2), ones_tall)
    for u in units:
        bi, h = u
        oa_ref[bi, :, hsl(h)] = hm[u] * lax.rsqrt(ssq[u] * (1.0 / DH) + EPS) * mg_ref[:, hsl(h)]
        wk = rep[bi][3 * chunk:, hsl(h)]
        dec = decay[bi][:, h:h + 1]
        c_ref[bi, h] = dec * c_ref[bi, h] + _dot_tn((v_ref[bi, :, hsl(h)] * wk).astype(BF16), kb[u])
        n_ref[bi, h:h + 1, :] = (dec * n_ref[bi, h:h + 1, :]
                                 + jnp.sum(k_ref[bi, :, hsl(h)] * scale * wk, axis=0, keepdims=True))
    for bi in range(bb):
        z = cg_ref[bi] * hc_ref[bi]
        cb = cb_ref[bi]
        z1 = jnp.where(rowi == 0, cb[1:2, :], pltpu.roll(z, 1, 0))
        z2 = jnp.where(rowi == 0, cb[0:1, :], jnp.where(rowi == 1, cb[1:2, :], pltpu.roll(z, 2, 0)))
        yc = cw_ref[0:1, :] * z2 + cw_ref[1:2, :] * z1 + cw_ref[2:3, :] * z
        oc_ref[bi] = _rms(bg_ref[bi] * yc) * mg_ref[:, D_MLSTM + D_S5:]
        cb_ref[bi] = z[chunk - 2:chunk, :]


def _mlstm_conv(qkvo, gate, bch, w, states_in, states_prev, *, layer, depth, bb, chunk):
    b, s, _ = qkvo.shape
    assert s % chunk == 0 and b % bb == 0
    zero_state = states_in is None
    kern = functools.partial(_mlstm_conv_kernel, layer=layer, bb=bb, chunk=chunk, zero_state=zero_state)
    col = lambda wd, j: pl.BlockSpec((bb, chunk, wd), lambda i, c: (i, c, j))
    st = lambda shape: pl.BlockSpec((None, bb) + shape, lambda i, c: (layer, i) + (0,) * len(shape))
    m_spec = pl.BlockSpec((None, b, MH), lambda i, c: (layer, 0, 0))
    cst = lambda a: pl.BlockSpec(a.shape, lambda i, c: (0,) * a.ndim)
    state_specs = [st((MH, DH, DH)), st((MH, DH)), m_spec, st((CONV_W - 1, D_CONV))]
    out_state_specs = state_specs
    if layer == 0:
        st0 = lambda shape: pl.BlockSpec((depth, bb) + shape, lambda i, c: (0, i) + (0,) * len(shape))
        out_state_specs = [st0((MH, DH, DH)), st0((MH, DH)), pl.BlockSpec((depth, b, MH), lambda i, c: (0, 0, 0)),
                           st0((CONV_W - 1, D_CONV))]
    consts = (w["gb"], w["gbt"], w["cw"], w["mg"])
    args = [qkvo, qkvo, qkvo, qkvo, gate, bch, bch, bch, *consts]
    in_specs = [col(D_MLSTM, 0), col(D_MLSTM, 1), col(D_MLSTM, 2), col(D_MLSTM, 3), col(GATE_W, 0),
                col(D_CONV, 0), col(D_CONV, 1), col(D_CONV, 2)] + [cst(a) for a in consts]
    if not zero_state:
        args += list(states_in)
        in_specs += state_specs
    aliases = {}
    if layer > 0:
        aliases = {len(args) + j: 2 + j for j in range(4)}
        args += list(states_prev)
        in_specs += [_ANY] * 4
    return pl.pallas_call(
        kern,
        grid=(b // bb, s // chunk),
        in_specs=in_specs,
        out_specs=[col(D_MLSTM, 0), col(D_CONV, 0)] + out_state_specs,
        out_shape=[jax.ShapeDtypeStruct((b, s, D_MLSTM), F32), jax.ShapeDtypeStruct((b, s, D_CONV), F32)]
        + _state_shapes(depth, b),
        scratch_shapes=[pltpu.VMEM((-(-bb // SUBLANES) * SUBLANES, GATE_W), F32)],
        input_output_aliases=aliases,
        compiler_params=pltpu.CompilerParams(dimension_semantics=("arbitrary", "arbitrary"),
                                             vmem_limit_bytes=VMEM_LIMIT),
        name="mlstm_conv",
    )(*args)


def _gate_terms(graw, gb_ref, gbt_ref, tril, triu):
    sel = (lax.broadcasted_iota(jnp.int32, (SUBLANES, GATE_W), 0)
           == lax.broadcasted_iota(jnp.int32, (SUBLANES, GATE_W), 1)).astype(F32)
    g_c = graw + gb_ref[...]
    lf_c = _log_sigmoid(g_c)
    g_r = _dot_nt_exact(sel, graw) + gbt_ref[...]
    lf_r = _log_sigmoid(g_r)
    return g_c, lf_c, _dot_exact(tril, lf_c), g_r, _dot_exact(lf_r, triu)


def _head_scores(h, g_c, bt_c, g_r, bt_r, m_prev, mask):
    it_row = g_r[h:h + 1, :]
    bt_row = bt_r[MH + h:MH + h + 1, :]
    it_col = g_c[:, h:h + 1]
    bt_col = bt_c[:, MH + h:MH + h + 1]
    dmat = jnp.where(mask, bt_col - bt_row + it_row, NEG)
    inter = bt_col + m_prev
    mt = jnp.maximum(jnp.max(dmat, axis=-1, keepdims=True), inter)
    return jnp.exp(dmat - mt), jnp.exp(inter - mt), mt, it_col, bt_col


def _head_output(sqk, vb, s_inter, qc, qn, mt, o, mg):
    num = _dot(sqk.astype(BF16), vb) + s_inter * qc
    den = jnp.sum(sqk, axis=-1, keepdims=True) + s_inter * qn
    hm = jax.nn.sigmoid(o) * (num / jnp.maximum(jnp.abs(den), jnp.exp(-mt)))
    return _rms(hm) * mg


def _bcast_block_last(x, t, s):
    out = x
    for d in range(1, s):
        out = jnp.where(t == s - 1 - d, pltpu.roll(x, x.shape[0] - d, 0), out)
    return out


def _mlstm_conv_short_kernel(*refs, layer, nb, s):
    (q_ref, k_ref, v_ref, o_ref, gate_ref, bg_ref, cg_ref, hc_ref, gb_ref, gbt_ref, cw_ref, mg_ref,
     c0_ref, n0_ref, m0_ref, cb0_ref) = refs[:16]
    refs = refs[16:]
    if layer > 0:
        refs = refs[4:]
    oa_ref, oc_ref, c_ref, n_ref, m_ref, cb_ref, qc_scr, st_scr = refs
    if layer == 0:
        for ref in (c_ref, n_ref, m_ref, cb_ref):
            if ref.shape[0] > 1:
                ref[1:] = jnp.zeros((ref.shape[0] - 1,) + ref.shape[1:], F32)
        c_ref, n_ref, m_ref, cb_ref = c_ref.at[0], n_ref.at[0], m_ref.at[0], cb_ref.at[0]
    r = nb * s
    per_tile = SUBLANES // s
    shift = s.bit_length() - 1
    row = lax.broadcasted_iota(jnp.int32, (r, r), 0)
    col = lax.broadcasted_iota(jnp.int32, (r, r), 1)
    same = (row >> shift) == (col >> shift)
    mask = same & (col <= row)
    tril = mask.astype(F32)
    triu = (same & (row <= col)).astype(F32)
    expand = ((lax.broadcasted_iota(jnp.int32, (r, nb), 0) >> shift)
              == lax.broadcasted_iota(jnp.int32, (r, nb), 1)).astype(F32)
    gather = (lax.broadcasted_iota(jnp.int32, (nb, r), 0)
              == (lax.broadcasted_iota(jnp.int32, (nb, r), 1) >> shift)).astype(F32)
    t128 = lax.broadcasted_iota(jnp.int32, (r, DH), 0) & (s - 1)
    sub8 = lax.broadcasted_iota(jnp.int32, (SUBLANES, DH), 0) >> shift
    scale = DH ** -0.5

    g_c, lf_c, bt_c, g_r, bt_r = _gate_terms(gate_ref[...], gb_ref, gbt_ref, tril, triu)
    bt_last_c = _dot_exact(same.astype(F32), lf_c)
    m_rows = _dot_exact(expand, m0_ref[...])
    for h in range(MH):
        hs = slice(h * DH, (h + 1) * DH)
        m_prev = m_rows[:, h:h + 1]
        w, s_inter, mt, it_col, bt_col = _head_scores(h, g_c, bt_c, g_r, bt_r, m_prev, mask)
        qh = q_ref[:, hs]
        kh = k_ref[:, hs] * scale
        vh = v_ref[:, hs]
        qb = qh.astype(BF16)
        kb = kh.astype(BF16)
        sqk = _dot_nt(qb, kb) * w
        for j in range(r // SUBLANES):
            rows = slice(j * SUBLANES, (j + 1) * SUBLANES)
            acc = None
            q8 = q_ref[rows, hs].astype(BF16)
            for p in range(per_tile):
                part = _dot_nt(q8, c0_ref[j * per_tile + p, h].astype(BF16))
                acc = part if acc is None else jnp.where(sub8 == p, part, acc)
            qc_scr[rows, :] = acc
        n_rows = _dot_exact(expand, n0_ref[:, h, :])
        qn = jnp.sum(qh * n_rows, axis=-1, keepdims=True)
        oa_ref[:, hs] = _head_output(sqk, vh.astype(BF16), s_inter, qc_scr[...], qn, mt,
                                     o_ref[:, hs], mg_ref[:, hs])
        m_new = _bcast_block_last(jnp.broadcast_to(mt, (r, DH)), t128, s)
        bt_last = bt_last_c[:, MH + h:MH + h + 1]
        wk = jnp.exp(bt_last - bt_col + it_col - m_new)
        decay = jnp.exp(bt_last + m_prev - m_new)
        vw = vh * wk
        for j in range(r // SUBLANES):
            rows = slice(j * SUBLANES, (j + 1) * SUBLANES)
            k8 = (k_ref[rows, hs] * scale).astype(BF16)
            for p in range(per_tile):
                bidx = j * per_tile + p
                last = bidx * s + s - 1
                upd = _dot_tn(jnp.where(sub8 == p, vw[rows], 0.0).astype(BF16), k8)
                c_ref[bidx, h] = decay[last:last + 1, 0:1] * c0_ref[bidx, h] + upd
        st_scr[0] = decay
        st_scr[1] = m_new
        last_rows = pl.ds(s - 1, nb, stride=s)
        n_ref[:, h, :] = st_scr[0, last_rows, :] * n0_ref[:, h, :] + _dot_exact(gather, kh * wk)
        m_ref[:, h:h + 1] = st_scr[1, last_rows, :][:, 0:1]
    t256 = lax.broadcasted_iota(jnp.int32, (r, D_CONV), 0) & (s - 1)
    z = cg_ref[...] * hc_ref[...]
    cb_a = _dot_exact(expand, cb0_ref[:, 0, :])
    cb_b = _dot_exact(expand, cb0_ref[:, 1, :])
    z1 = jnp.where(t256 == 0, cb_b, pltpu.roll(z, 1, 0))
    z2 = jnp.where(t256 == 0, cb_a, jnp.where(t256 == 1, cb_b, pltpu.roll(z, 2, 0)))
    yc = cw_ref[0:1, :] * z2 + cw_ref[1:2, :] * z1 + cw_ref[2:3, :] * z
    oc_ref[...] = _rms(bg_ref[...] * yc) * mg_ref[:, D_MLSTM + D_S5:]
    for half in range(D_CONV // DH):
        lanes = slice(half * DH, (half + 1) * DH)
        st_scr[half] = z[:, lanes]
        cb_ref[:, 0, lanes] = st_scr[half, pl.ds(s - 2, nb, stride=s), :]
        cb_ref[:, 1, lanes] = st_scr[half, pl.ds(s - 1, nb, stride=s), :]


def _mlstm_conv_short(qkvo, gate, bch, w, states_in, states_prev, *, layer, depth, b, s, nb):
    assert SUBLANES % s == 0 and s >= CONV_W - 1 and b % nb == 0 and (nb * s) % SUBLANES == 0
    r = nb * s
    kern = functools.partial(_mlstm_conv_short_kernel, layer=layer, nb=nb, s=s)
    col = lambda wd, j: pl.BlockSpec((r, wd), lambda i: (i, j))
    st = lambda shape: pl.BlockSpec((None, nb) + shape, lambda i: (layer, i) + (0,) * len(shape))
    cst = lambda a: pl.BlockSpec(a.shape, lambda i: (0,) * a.ndim)
    state_specs = [st((MH, DH, DH)), st((MH, DH)), st((MH,)), st((CONV_W - 1, D_CONV))]
    out_state_specs = state_specs
    if layer == 0:
        st0 = lambda shape: pl.BlockSpec((depth, nb) + shape, lambda i: (0, i) + (0,) * len(shape))
        out_state_specs = [st0((MH, DH, DH)), st0((MH, DH)), st0((MH,)), st0((CONV_W - 1, D_CONV))]
    consts = (w["gb"], w["gbt"], w["cw"], w["mg"])
    args = [qkvo, qkvo, qkvo, qkvo, gate, bch, bch, bch, *consts, *states_in]
    in_specs = [col(D_MLSTM, 0), col(D_MLSTM, 1), col(D_MLSTM, 2), col(D_MLSTM, 3), col(GATE_W, 0),
                col(D_CONV, 0), col(D_CONV, 1), col(D_CONV, 2)] + [cst(a) for a in consts] + state_specs
    aliases = {}
    if layer > 0:
        aliases = {len(args) + j: 2 + j for j in range(4)}
        args += list(states_prev)
        in_specs += [_ANY] * 4
    return pl.pallas_call(
        kern,
        grid=(b // nb,),
        in_specs=in_specs,
        out_specs=[col(D_MLSTM, 0), col(D_CONV, 0)] + out_state_specs,
        out_shape=[jax.ShapeDtypeStruct((b * s, D_MLSTM), F32), jax.ShapeDtypeStruct((b * s, D_CONV), F32)]
        + _state_shapes(depth, b),
        scratch_shapes=[pltpu.VMEM((r, DH), F32), pltpu.VMEM((2, r, DH), F32)],
        input_output_aliases=aliases,
        compiler_params=pltpu.CompilerParams(dimension_semantics=("arbitrary",), vmem_limit_bytes=VMEM_LIMIT),
        name="mlstm_conv_short",
    )(*args)


def _s5_kernel(*refs, layer, nb, tt, zero_state, parts):
    u_ref, lam_ref, bblk_ref, cblk_ref, d_ref, wglu_ref, mg_ref = refs[:7]
    refs = refs[7:]
    if not zero_state:
        hr0_ref, hi0_ref = refs[:2]
        refs = refs[2:]
    if layer > 0:
        refs = refs[2:]
    ob_ref, hr_ref, hi_ref, xs_ref, a_ref, bf_ref = refs
    if layer == 0:
        hr_all, hi_all = hr_ref, hi_ref
        hr_ref, hi_ref = hr_all.at[0], hi_all.at[0]
    ti = pl.program_id(0)
    n = S5_N
    lc = 512

    @pl.when(ti == 0)
    def _():
        if layer == 0:
            hr_all[...] = jnp.zeros_like(hr_all)
            hi_all[...] = jnp.zeros_like(hi_all)
        hr_ref[...] = jnp.zeros_like(hr_ref) if zero_state else hr0_ref[...]
        hi_ref[...] = jnp.zeros_like(hi_ref) if zero_state else hi0_ref[...]
        lre = jnp.minimum(lam_ref[0:1, :], -1e-4)
        lim = lam_ref[1:2, :]
        dt = jnp.exp(lam_ref[2:3, :])
        mag = jnp.exp(lre * dt)
        ab_re = mag * jnp.cos(lim * dt)
        ab_im = mag * jnp.sin(lim * dt)
        den = lre * lre + lim * lim
        nr = ab_re - 1.0
        fre = (nr * lre + ab_im * lim) / den
        fim = (ab_im * lre - nr * lim) / den
        a_ref[:, :n] = jnp.broadcast_to(ab_re, (SUBLANES, n))
        a_ref[:, n:] = jnp.broadcast_to(ab_im, (SUBLANES, n))
        b_re = bblk_ref[:, :n]
        b_im = bblk_ref[:, n:]
        bf_ref[:, :n] = (fre * b_re - fim * b_im).astype(BF16)
        bf_ref[:, n:] = (fre * b_im + fim * b_re).astype(BF16)

    rows_p = nb * tt // parts
    part_rows = [slice(p * rows_p, (p + 1) * rows_p) for p in range(parts)]
    for rs in part_rows:
        xs_ref[rs, :] = _dot(u_ref[rs, :].astype(BF16), bf_ref[...])
    ar = [a_ref[:, c0:c0 + lc] for c0 in range(0, n, lc)]
    ai = [a_ref[:, n + c0:n + c0 + lc] for c0 in range(0, n, lc)]
    h = {}
    for p, rs in enumerate(part_rows):
        for rg in range(nb // SUBLANES):
            g8 = slice(rg * SUBLANES, (rg + 1) * SUBLANES)
            for ci, c0 in enumerate(range(0, n, lc)):
                if p == 0:
                    h[rg, ci] = (hr_ref[g8, c0:c0 + lc], hi_ref[g8, c0:c0 + lc])
            for t in range(p * tt // parts, (p + 1) * tt // parts):
                r8 = slice(t * nb + rg * SUBLANES, t * nb + (rg + 1) * SUBLANES)
                for ci, c0 in enumerate(range(0, n, lc)):
                    hr, hi = h[rg, ci]
                    nr_ = ar[ci] * hr - ai[ci] * hi + xs_ref[r8, c0:c0 + lc]
                    ni_ = ar[ci] * hi + ai[ci] * hr + xs_ref[r8, n + c0:n + c0 + lc]
                    xs_ref[r8, c0:c0 + lc] = nr_
                    xs_ref[r8, n + c0:n + c0 + lc] = ni_
                    h[rg, ci] = (nr_, ni_)
            for ci, c0 in enumerate(range(0, n, lc)):
                if p == parts - 1:
                    hr_ref[g8, c0:c0 + lc] = h[rg, ci][0]
                    hi_ref[g8, c0:c0 + lc] = h[rg, ci][1]
    y = [_dot(xs_ref[rs, :].astype(BF16), cblk_ref[...]) + d_ref[...] * u_ref[rs, :] for rs in part_rows]
    g = [jax.nn.gelu(v) for v in y]
    gate = [_dot(v.astype(BF16), wglu_ref[...]) for v in g]
    for rs, v, z in zip(part_rows, g, gate):
        ob_ref[rs, :] = _rms(v * jax.nn.sigmoid(z)) * mg_ref[:, D_MLSTM:D_MLSTM + D_S5]


def _s5(u_tm, w, states_in, states_prev, *, layer, depth, nb, tt):
    rows = u_tm.shape[0]
    rt = nb * tt
    zero_state = states_in is None
    parts = 4 if tt % 4 == 0 and tt >= 16 else 1
    kern = functools.partial(_s5_kernel, layer=layer, nb=nb, tt=tt, zero_state=zero_state, parts=parts)
    cst = lambda a: pl.BlockSpec(a.shape, lambda i: (0,) * a.ndim)
    st = pl.BlockSpec((None, nb, S5_N), lambda i: (layer, 0, 0))
    st_out = pl.BlockSpec((depth, nb, S5_N), lambda i: (0, 0, 0)) if layer == 0 else st
    consts = (w["lam"], w["bblk"], w["cblk"], w["d"], w["wglu"], w["mg"])
    args = [u_tm, *consts]
    in_specs = [pl.BlockSpec((rt, D_S5), lambda i: (i, 0))] + [cst(a) for a in consts]
    if not zero_state:
        args += list(states_in)
        in_specs += [st, st]
    aliases = {}
    if layer > 0:
        aliases = {len(args) + j: 1 + j for j in range(2)}
        args += list(states_prev)
        in_specs += [_ANY] * 2
    return pl.pallas_call(
        kern,
        grid=(rows // rt,),
        in_specs=in_specs,
        out_specs=[pl.BlockSpec((rt, D_S5), lambda i: (i, 0)), st_out, st_out],
        out_shape=[jax.ShapeDtypeStruct((rows, D_S5), F32), jax.ShapeDtypeStruct((depth, nb, S5_N), F32),
                   jax.ShapeDtypeStruct((depth, nb, S5_N), F32)],
        scratch_shapes=[pltpu.VMEM((rt, 2 * S5_N), F32), pltpu.VMEM((SUBLANES, 2 * S5_N), F32),
                        pltpu.VMEM((D_S5, 2 * S5_N), BF16)],
        input_output_aliases=aliases,
        compiler_params=pltpu.CompilerParams(dimension_semantics=("arbitrary",), vmem_limit_bytes=VMEM_LIMIT),
        name="s5",
    )(*args)


def _block_diag(w):
    g, r, c = w.shape
    eye = jnp.eye(g, dtype=w.dtype)
    return jnp.einsum("grc,gh->grhc", w, eye).reshape(g * r, g * c)


def _layer_weights(l, ffn1_w1, ffn1_w3, ffn1_w2, ffn2_w1, ffn2_w3, ffn2_w2, norm_g, w_in, ig_bias, fg_bias,
                   s5_a_re, s5_a_im, s5_log_dt, s5_b_re, s5_b_im, s5_c_re, s5_c_im, s5_d, w_glu, conv_w,
                   mix_g, w_out):
    o = 4 * D_MLSTM
    win = w_in[l]
    win_pad = jnp.concatenate(
        [win[:, :o], win[:, o:o + 2 * MH], jnp.zeros((D_MODEL, GATE_W - 2 * MH), F32), win[:, o + 2 * MH:]],
        axis=1).astype(BF16)
    gb = jnp.concatenate([ig_bias[l], fg_bias[l], jnp.zeros((GATE_W - 2 * MH,), F32)])
    lam = jnp.stack([s5_a_re[l].reshape(-1), s5_a_im[l].reshape(-1),
                     jnp.repeat(s5_log_dt[l], S5_P)])
    bblk = jnp.concatenate([_block_diag(jnp.swapaxes(s5_b_re[l], 1, 2)),
                            _block_diag(jnp.swapaxes(s5_b_im[l], 1, 2))], axis=1)
    cblk = jnp.concatenate([_block_diag(jnp.swapaxes(s5_c_re[l], 1, 2)),
                            -_block_diag(jnp.swapaxes(s5_c_im[l], 1, 2))], axis=0).astype(BF16)
    return dict(
        f1=(ffn1_w1[l].astype(BF16), ffn1_w3[l].astype(BF16), ffn1_w2[l].astype(BF16)),
        f2=(ffn2_w1[l].astype(BF16), ffn2_w3[l].astype(BF16), ffn2_w2[l].astype(BF16)),
        g=norm_g[l], win=win_pad, gb=gb.reshape(1, GATE_W), gbt=gb[:SUBLANES].reshape(SUBLANES, 1),
        lam=lam, bblk=bblk, cblk=cblk, d=s5_d[l].reshape(1, D_S5), wglu=w_glu[l].astype(BF16),
        cw=conv_w[l], mg=mix_g[l].reshape(1, D_MODEL), wo=w_out[l].astype(BF16))


def _run_group(x, states, weights, *, tm, tt, chunk=None, bb=None, nb=None):
    b, s, _ = x.shape
    depth = len(weights)
    xf = x.reshape(b * s, D_MODEL)
    mstates = sstates = None
    if states is not None:
        c_all, n_all, m_all, sr_all, si_all, cb_all = states
        ms_in = (c_all, n_all, m_all, cb_all)
        ss_in = (sr_all.reshape(depth, b, S5_N), si_all.reshape(depth, b, S5_N))
    else:
        ms_in = ss_in = None
    for l, w in enumerate(weights):
        x1, qkvo, gate, u, bch = _ffn_inproj(xf, w["g"], *w["f1"], w["win"], tm)
        u_tm = jnp.swapaxes(u.reshape(b, s, D_S5), 0, 1).reshape(s * b, D_S5)
        ob_tm, *sstates = _s5(u_tm, w, ss_in, sstates, layer=l, depth=depth, nb=b, tt=tt)
        ob = jnp.swapaxes(ob_tm.reshape(s, b, D_S5), 0, 1).reshape(b * s, D_S5)
        if chunk is not None:
            r3 = lambda a: a.reshape(b, s, a.shape[-1])
            oa, oc, *mstates = _mlstm_conv(r3(qkvo), r3(gate), r3(bch), w, ms_in, mstates,
                                           layer=l, depth=depth, bb=bb, chunk=chunk)
            xf = _outproj_ffn(x1, oa.reshape(b * s, D_MLSTM), ob, oc.reshape(b * s, D_CONV),
                              w["g"], w["wo"], *w["f2"], tm)
        else:
            oa, oc, *mstates = _mlstm_conv_short(qkvo, gate, bch, w, ms_in, mstates,
                                                 layer=l, depth=depth, b=b, s=s, nb=nb)
            xf = _outproj_ffn(x1, oa, ob, oc, w["g"], w["wo"], *w["f2"], tm)
    c1, n1, m1, cb1 = mstates
    sr1, si1 = sstates
    return xf.reshape(b, s, D_MODEL), (c1, n1, m1, sr1.reshape(depth, b, S5_G, S5_P),
                                       si1.reshape(depth, b, S5_G, S5_P), cb1)


def kernel(x_prompt, x_sample, state_mlstm_C, state_mlstm_n, state_mlstm_m, state_s5_re, state_s5_im,
           state_conv, ffn1_w1, ffn1_w3, ffn1_w2, ffn2_w1, ffn2_w3, ffn2_w2, norm_g, w_in, ig_bias, fg_bias,
           s5_a_re, s5_a_im, s5_log_dt, s5_b_re, s5_b_im, s5_c_re, s5_c_im, s5_d, w_glu, conv_w, mix_g, w_out):
    depth = norm_g.shape[0]
    params = (ffn1_w1, ffn1_w3, ffn1_w2, ffn2_w1, ffn2_w3, ffn2_w2, norm_g, w_in, ig_bias, fg_bias,
              s5_a_re, s5_a_im, s5_log_dt, s5_b_re, s5_b_im, s5_c_re, s5_c_im, s5_d, w_glu, conv_w, mix_g, w_out)
    weights = [_layer_weights(l, *params) for l in range(depth)]
    y_p, st_p = _run_group(x_prompt, None, weights, tm=512, tt=64, chunk=128, bb=4)
    y_s, st_s = _run_group(
        x_sample, (state_mlstm_C, state_mlstm_n, state_mlstm_m, state_s5_re, state_s5_im, state_conv),
        weights, tm=512, tt=x_sample.shape[1], nb=16)
    return (y_p, y_s, *st_p, *st_s)
```

```python
import functools

import jax
import jax.numpy as jnp
from jax import lax
from jax.experimental import pallas as pl
from jax.experimental.pallas import tpu as pltpu

F32 = jnp.float32
BF16 = jnp.bfloat16

D_MODEL = 1024
MH = 4
DH = 128
D_MLSTM = MH * DH
S5_CH = 16
S5_G = 16
S5_P = 64
D_S5 = S5_G * S5_CH
S5_N = S5_G * S5_P
D_CONV = 256
CONV_W = 3
EPS = 1e-6
GATE_W = 128
SUBLANES = 8
NEG = -1e30
VMEM_LIMIT = 56 * 1024 * 1024
ROW_PARTS = 2


def _dot(a, b):
    return jnp.dot(a, b, preferred_element_type=F32)


def _dot_nt(a, b):
    return lax.dot_general(a, b, (((1,), (1,)), ((), ())), preferred_element_type=F32)


def _dot_tn(a, b):
    return lax.dot_general(a, b, (((0,), (0,)), ((), ())), preferred_element_type=F32)


def _dot_exact(a, b):
    return jnp.dot(a, b, preferred_element_type=F32, precision=lax.Precision.HIGHEST)


def _dot_nt_exact(a, b):
    return lax.dot_general(a, b, (((1,), (1,)), ((), ())), preferred_element_type=F32,
                           precision=lax.Precision.HIGHEST)


def _rms(x):
    return x * lax.rsqrt(jnp.mean(x * x, axis=-1, keepdims=True) + EPS)


def _log_sigmoid(x):
    return jnp.minimum(x, 0.0) - jnp.log1p(jnp.exp(-jnp.abs(x)))


def _split_bf16(x, parts):
    out = []
    for _ in range(parts):
        p = x.astype(BF16)
        out.append(p)
        x = x - p.astype(F32)
    return jnp.concatenate(out, axis=1)


def _ffn_residual(xs, g_pre, g_post, w1, w3, w2):
    xn = [(_rms(x) * g_pre).astype(BF16) for x in xs]
    h1 = [_dot(v, w1[...]) for v in xn]
    h3 = [_dot(v, w3[...]) for v in xn]
    a = [(jax.nn.silu(p) * q).astype(BF16) for p, q in zip(h1, h3)]
    y = [_dot(v, w2[...]) for v in a]
    return [x + 0.5 * (_rms(t) * g_post) for x, t in zip(xs, y)]


def _row_parts(tm, parts):
    step = tm // parts
    return [slice(i * step, (i + 1) * step) for i in range(parts)]


def _const_spec(shape):
    nd = len(shape)
    return pl.BlockSpec(shape, lambda *_: (0,) * nd, pipeline_mode=pl.Buffered(1))


_ANY = pl.BlockSpec(memory_space=pl.ANY)


def _ffn_inproj_kernel(x_ref, g_ref, w1_ref, w3_ref, w2_ref, wq_ref, wg_ref, wr_ref,
                       x1_ref, qkvo_ref, gate_ref, u_ref, bch_ref):
    parts = _row_parts(x_ref.shape[0], ROW_PARTS)
    x1 = _ffn_residual([x_ref[r, :] for r in parts], g_ref[0:1, :], g_ref[1:2, :], w1_ref, w3_ref, w2_ref)
    hn = [(_rms(v) * g_ref[2:3, :]).astype(BF16) for v in x1]
    qkvo = [_dot(v, wq_ref[...]) for v in hn]
    gate = [_dot(v, wg_ref[...]) for v in hn]
    rest = [_dot(v, wr_ref[...]) for v in hn]
    for i, r in enumerate(parts):
        x1_ref[r, :] = x1[i]
        qkvo_ref[r, :] = qkvo[i]
        gate_ref[r, :] = gate[i]
        u_ref[r, :] = rest[i][:, :D_S5]
        bch_ref[r, :] = rest[i][:, D_S5:]


def _ffn_inproj(x, g, w1, w3, w2, win, tm):
    t = x.shape[0]
    row = lambda w: pl.BlockSpec((tm, w), lambda i: (i, 0))
    return pl.pallas_call(
        _ffn_inproj_kernel,
        grid=(t // tm,),
        in_specs=[row(D_MODEL), _const_spec(g.shape), _const_spec(w1.shape), _const_spec(w3.shape),
                  _const_spec(w2.shape)] + [_const_spec(a.shape) for a in win],
        out_specs=[row(D_MODEL), row(4 * D_MLSTM), row(GATE_W), row(D_S5), row(3 * D_CONV)],
        out_shape=[jax.ShapeDtypeStruct((t, D_MODEL), F32), jax.ShapeDtypeStruct((t, 4 * D_MLSTM), F32),
                   jax.ShapeDtypeStruct((t, GATE_W), F32), jax.ShapeDtypeStruct((t, D_S5), F32),
                   jax.ShapeDtypeStruct((t, 3 * D_CONV), F32)],
        compiler_params=pltpu.CompilerParams(dimension_semantics=("arbitrary",), vmem_limit_bytes=VMEM_LIMIT),
        name="ffn_inproj",
    )(x, g, w1, w3, w2, *win)


def _outproj_ffn_kernel(x_ref, a_ref, b_ref, c_ref, g_ref, wo_ref, w1_ref, w3_ref, w2_ref, y_ref):
    parts = _row_parts(x_ref.shape[0], ROW_PARTS)
    mo = [(_dot(a_ref[r, :].astype(BF16), wo_ref[0:D_MLSTM, :])
           + _dot(b_ref[r, :].astype(BF16), wo_ref[D_MLSTM:D_MLSTM + D_S5, :])
           + _dot(c_ref[r, :].astype(BF16), wo_ref[D_MLSTM + D_S5:, :])) for r in parts]
    x2 = [x_ref[r, :] + _rms(v) * g_ref[3:4, :] for r, v in zip(parts, mo)]
    y = _ffn_residual(x2, g_ref[4:5, :], g_ref[5:6, :], w1_ref, w3_ref, w2_ref)
    for r, v in zip(parts, y):
        y_ref[r, :] = v


def _outproj_ffn(x, oa, ob, oc, g, wo, w1, w3, w2, tm):
    t = x.shape[0]
    row = lambda w: pl.BlockSpec((tm, w), lambda i: (i, 0))
    return pl.pallas_call(
        _outproj_ffn_kernel,
        grid=(t // tm,),
        in_specs=[row(D_MODEL), row(D_MLSTM), row(D_S5), row(D_CONV), _const_spec(g.shape),
                  _const_spec(wo.shape), _const_spec(w1.shape), _const_spec(w3.shape), _const_spec(w2.shape)],
        out_specs=row(D_MODEL),
        out_shape=jax.ShapeDtypeStruct((t, D_MODEL), F32),
        compiler_params=pltpu.CompilerParams(dimension_semantics=("arbitrary",), vmem_limit_bytes=VMEM_LIMIT),
        name="outproj_ffn",
    )(x, oa, ob, oc, g, wo, w1, w3, w2)


def _state_shapes(depth, b):
    return [jax.ShapeDtypeStruct((depth, b, MH, DH, DH), F32), jax.ShapeDtypeStruct((depth, b, MH, DH), F32),
            jax.ShapeDtypeStruct((depth, b, MH), F32), jax.ShapeDtypeStruct((depth, b, CONV_W - 1, D_CONV), F32)]


def _cummax_rows(x, rowi):
    s = 1
    while s < x.shape[0]:
        x = jnp.maximum(x, jnp.where(rowi >= s, pltpu.roll(x, s, 0), NEG))
        s *= 2
    return x


def _mlstm_conv_kernel(*refs, layer, bb, chunk, zero_state):
    (q_ref, k_ref, v_ref, o_ref, gate_ref, bg_ref, cg_ref, hc_ref, gb_ref, gbt_ref, cw_ref, mg_ref) = refs[:12]
    refs = refs[12:]
    if not zero_state:
        c0_ref, n0_ref, m0_ref, cb0_ref = refs[:4]
        refs = refs[4:]
    if layer > 0:
        refs = refs[4:]
    oa_ref, oc_ref, c_ref, n_ref, m_ref, cb_ref, m_scr = refs
    other_layers = ()
    if layer == 0:
        other_layers = tuple(r.at[1:] for r in (c_ref, n_ref, m_ref, cb_ref) if r.shape[0] > 1)
        c_ref, n_ref, m_ref, cb_ref = c_ref.at[0], n_ref.at[0], m_ref.at[0], cb_ref.at[0]
    assert chunk == DH
    bi0 = pl.program_id(0) * bb
    ci = pl.program_id(1)

    @pl.when(ci == 0)
    def _():
        for r in other_layers:
            r[...] = jnp.zeros_like(r)
        m_scr[...] = jnp.zeros_like(m_scr)
        if zero_state:
            c_ref[...] = jnp.zeros_like(c_ref)
            n_ref[...] = jnp.zeros_like(n_ref)
            cb_ref[...] = jnp.zeros_like(cb_ref)
        else:
            c_ref[...] = c0_ref[...]
            n_ref[...] = n0_ref[...]
            cb_ref[...] = cb0_ref[...]
            for bi in range(bb):
                m_scr[bi:bi + 1, 0:MH] = m0_ref[pl.ds(bi0 + bi, 1), :]

    row = lax.broadcasted_iota(jnp.int32, (chunk, chunk), 0)
    col = lax.broadcasted_iota(jnp.int32, (chunk, chunk), 1)
    causal = col <= row
    tril = jnp.where(causal, 1.0, 0.0).astype(BF16)
    rowg = lax.broadcasted_iota(jnp.int32, (chunk, GATE_W), 0)
    rowi = lax.broadcasted_iota(jnp.int32, (chunk, D_CONV), 0)
    sel8 = (lax.broadcasted_iota(jnp.int32, (SUBLANES, GATE_W), 0)
            == lax.broadcasted_iota(jnp.int32, (SUBLANES, GATE_W), 1)).astype(F32)
    spread = jnp.where((lax.broadcasted_iota(jnp.int32, (2 * GATE_W, D_MLSTM), 0) & (GATE_W - 1))
                       == (lax.broadcasted_iota(jnp.int32, (2 * GATE_W, D_MLSTM), 1) >> 7), 1.0, 0.0).astype(BF16)
    ones_b = jnp.ones((chunk, DH), BF16)
    zeros_b = jnp.zeros((chunk, DH), BF16)
    ones_tall = jnp.ones((2 * DH, DH), BF16)
    scale = DH ** -0.5

    units = [(bi, h) for bi in range(bb) for h in range(MH)]
    hsl = lambda h: slice(h * DH, (h + 1) * DH)
    g_c = [gate_ref[bi] + gb_ref[...] for bi in range(bb)]
    csum = [_dot(tril, _split_bf16(pltpu.roll(_log_sigmoid(g), GATE_W - MH, 1), 3)) for g in g_c]
    qb = {u: q_ref[u[0], :, hsl(u[1])].astype(BF16) for u in units}
    kb = {u: (k_ref[u[0], :, hsl(u[1])] * scale).astype(BF16) for u in units}
    s_qk = {u: _dot_nt(qb[u], kb[u]) for u in units}
    inter = {u: _dot_nt(qb[u], jnp.concatenate(
        [c_ref[u[0], u[1]].astype(BF16),
         jnp.broadcast_to(n_ref[u[0], u[1]:u[1] + 1, :], (DH, DH)).astype(BF16)], axis=0)) for u in units}
    rep, a_r, decay = [], [], []
    for bi in range(bb):
        f_c = csum[bi][:, :GATE_W] + csum[bi][:, GATE_W:2 * GATE_W] + csum[bi][:, 2 * GATE_W:]
        a_c = g_c[bi] - f_c
        m_prev = m_scr[bi:bi + 1, :]
        big_m = jnp.maximum(_cummax_rows(a_c, rowg), m_prev)
        mt = f_c + big_m
        m_new = mt[chunk - 1:chunk, :]
        f_last = f_c[chunk - 1:chunk, :]
        decay.append(jnp.exp(f_last + m_prev - m_new))
        stacked = jnp.concatenate([_split_bf16(-big_m, 2), _split_bf16(jnp.exp(m_prev - big_m), 2),
                                   _split_bf16(-mt, 2), _split_bf16(jnp.exp(a_c + (f_last - m_new)), 2)],
                                  axis=0)
        rep.append(_dot(stacked, spread))
        a_r.append(_dot_nt_exact(sel8, a_c))
        m_scr[bi:bi + 1, :] = m_new
        m_ref[pl.ds(bi0 + bi, 1), :] = m_new[:, 0:MH]
    intra = {}
    for u in units:
        bi, h = u
        w = jnp.exp(jnp.where(causal, rep[bi][0:chunk, hsl(h)] + a_r[bi][h:h + 1, :], NEG))
        rhs = jnp.concatenate([jnp.concatenate([v_ref[bi, :, hsl(h)].astype(BF16), ones_b], axis=1),
                               jnp.concatenate([zeros_b, ones_b], axis=1)], axis=0)
        intra[u] = _dot(_split_bf16(s_qk[u] * w, 2), rhs)
    hm, ssq = {}, {}
    for u in units:
        bi, h = u
        s_inter = rep[bi][chunk:2 * chunk, hsl(h)]
        e_floor = jnp.exp(rep[bi][2 * chunk:3 * chunk, hsl(h)])
        num = intra[u][:, :DH] + s_inter * inter[u][:, :DH]
        den = intra[u][:, DH:] + s_inter * inter[u][:, DH:]
        hm[u] = jax.nn.sigmoid(o_ref[bi, :, hsl(h)]) * (num / jnp.maximum(jnp.abs(den), e_floor))
        ssq[u] = _dot(_split_bf16(hm[u] * hm[u], 2), ones_tall)
    for u in units:
        bi, h = u
        oa_ref[bi, :, hsl(h)] = hm[u] * lax.rsqrt(ssq[u] * (1.0 / DH) + EPS) * mg_ref[:, hsl(h)]
        wk = rep[bi][3 * chunk:, hsl(h)]
        dec = decay[bi][:, h:h + 1]
        c_ref[bi, h] = dec * c_ref[bi, h] + _dot_tn((v_ref[bi, :, hsl(h)] * wk).astype(BF16), kb[u])
        n_ref[bi, h:h + 1, :] = (dec * n_ref[bi, h:h + 1, :]
                                 + jnp.sum(k_ref[bi, :, hsl(h)] * scale * wk, axis=0, keepdims=True))
    for bi in range(bb):
        z = cg_ref[bi] * hc_ref[bi]
        cb = cb_ref[bi]
        z1 = jnp.where(rowi == 0, cb[1:2, :], pltpu.roll(z, 1, 0))
        z2 = jnp.where(rowi == 0, cb[0:1, :], jnp.where(rowi == 1, cb[1:2, :], pltpu.roll(z, 2, 0)))
        yc = cw_ref[0:1, :] * z2 + cw_ref[1:2, :] * z1 + cw_ref[2:3, :] * z
        oc_ref[bi] = _rms(bg_ref[bi] * yc) * mg_ref[:, D_MLSTM + D_S5:]
        cb_ref[bi] = z[chunk - 2:chunk, :]


def _mlstm_conv(qkvo, gate, bch, w, states_in, states_prev, *, layer, depth, bb, chunk):
    b, s, _ = qkvo.shape
    assert s % chunk == 0 and b % bb == 0
    zero_state = states_in is None
    kern = functools.partial(_mlstm_conv_kernel, layer=layer, bb=bb, chunk=chunk, zero_state=zero_state)
    col = lambda wd, j: pl.BlockSpec((bb, chunk, wd), lambda i, c: (i, c, j))
    st = lambda shape: pl.BlockSpec((None, bb) + shape, lambda i, c: (layer, i) + (0,) * len(shape))
    m_spec = pl.BlockSpec((None, b, MH), lambda i, c: (layer, 0, 0))
    cst = lambda a: pl.BlockSpec(a.shape, lambda i, c: (0,) * a.ndim)
    state_specs = [st((MH, DH, DH)), st((MH, DH)), m_spec, st((CONV_W - 1, D_CONV))]
    out_state_specs = state_specs
    if layer == 0:
        st0 = lambda shape: pl.BlockSpec((depth, bb) + shape, lambda i, c: (0, i) + (0,) * len(shape))
        out_state_specs = [st0((MH, DH, DH)), st0((MH, DH)), pl.BlockSpec((depth, b, MH), lambda i, c: (0, 0, 0)),
                           st0((CONV_W - 1, D_CONV))]
    consts = (w["gb"], w["gbt"], w["cw"], w["mg"])
    args = [qkvo, qkvo, qkvo, qkvo, gate, bch, bch, bch, *consts]
    in_specs = [col(D_MLSTM, 0), col(D_MLSTM, 1), col(D_MLSTM, 2), col(D_MLSTM, 3), col(GATE_W, 0),
                col(D_CONV, 0), col(D_CONV, 1), col(D_CONV, 2)] + [cst(a) for a in consts]
    if not zero_state:
        args += list(states_in)
        in_specs += state_specs
    aliases = {}
    if layer > 0:
        aliases = {len(args) + j: 2 + j for j in range(4)}
        args += list(states_prev)
        in_specs += [_ANY] * 4
    return pl.pallas_call(
        kern,
        grid=(b // bb, s // chunk),
        in_specs=in_specs,
        out_specs=[col(D_MLSTM, 0), col(D_CONV, 0)] + out_state_specs,
        out_shape=[jax.ShapeDtypeStruct((b, s, D_MLSTM), F32), jax.ShapeDtypeStruct((b, s, D_CONV), F32)]
        + _state_shapes(depth, b),
        scratch_shapes=[pltpu.VMEM((-(-bb // SUBLANES) * SUBLANES, GATE_W), F32)],
        input_output_aliases=aliases,
        compiler_params=pltpu.CompilerParams(dimension_semantics=("arbitrary", "arbitrary"),
                                             vmem_limit_bytes=VMEM_LIMIT),
        name="mlstm_conv",
    )(*args)


def _gate_terms(graw, gb_ref, gbt_ref, tril, triu):
    sel = (lax.broadcasted_iota(jnp.int32, (SUBLANES, GATE_W), 0)
           == lax.broadcasted_iota(jnp.int32, (SUBLANES, GATE_W), 1)).astype(F32)
    g_c = graw + gb_ref[...]
    lf_c = _log_sigmoid(g_c)
    g_r = _dot_nt_exact(sel, graw) + gbt_ref[...]
    lf_r = _log_sigmoid(g_r)
    return g_c, lf_c, _dot_exact(tril, lf_c), g_r, _dot_exact(lf_r, triu)


def _head_scores(h, g_c, bt_c, g_r, bt_r, m_prev, mask):
    it_row = g_r[h:h + 1, :]
    bt_row = bt_r[MH + h:MH + h + 1, :]
    it_col = g_c[:, h:h + 1]
    bt_col = bt_c[:, MH + h:MH + h + 1]
    dmat = jnp.where(mask, bt_col - bt_row + it_row, NEG)
    inter = bt_col + m_prev
    mt = jnp.maximum(jnp.max(dmat, axis=-1, keepdims=True), inter)
    return jnp.exp(dmat - mt), jnp.exp(inter - mt), mt, it_col, bt_col


def _head_output(sqk, vb, s_inter, qc, qn, mt, o, mg):
    num = _dot(sqk.astype(BF16), vb) + s_inter * qc
    den = jnp.sum(sqk, axis=-1, keepdims=True) + s_inter * qn
    hm = jax.nn.sigmoid(o) * (num / jnp.maximum(jnp.abs(den), jnp.exp(-mt)))
    return _rms(hm) * mg


def _bcast_block_last(x, t, s):
    out = x
    for d in range(1, s):
        out = jnp.where(t == s - 1 - d, pltpu.roll(x, x.shape[0] - d, 0), out)
    return out


def _mlstm_conv_short_kernel(*refs, layer, nb, s):
    (q_ref, k_ref, v_ref, o_ref, gate_ref, bg_ref, cg_ref, hc_ref, gb_ref, gbt_ref, cw_ref, mg_ref,
     c0_ref, n0_ref, m0_ref, cb0_ref) = refs[:16]
    refs = refs[16:]
    if layer > 0:
        refs = refs[4:]
    oa_ref, oc_ref, c_ref, n_ref, m_ref, cb_ref, qc_scr, st_scr = refs
    if layer == 0:
        for ref in (c_ref, n_ref, m_ref, cb_ref):
            if ref.shape[0] > 1:
                ref[1:] = jnp.zeros((ref.shape[0] - 1,) + ref.shape[1:], F32)
        c_ref, n_ref, m_ref, cb_ref = c_ref.at[0], n_ref.at[0], m_ref.at[0], cb_ref.at[0]
    r = nb * s
    per_tile = SUBLANES // s
    shift = s.bit_length() - 1
    row = lax.broadcasted_iota(jnp.int32, (r, r), 0)
    col = lax.broadcasted_iota(jnp.int32, (r, r), 1)
    same = (row >> shift) == (col >> shift)
    mask = same & (col <= row)
    tril = mask.astype(F32)
    triu = (same & (row <= col)).astype(F32)
    expand = ((lax.broadcasted_iota(jnp.int32, (r, nb), 0) >> shift)
              == lax.broadcasted_iota(jnp.int32, (r, nb), 1)).astype(F32)
    gather = (lax.broadcasted_iota(jnp.int32, (nb, r), 0)
              == (lax.broadcasted_iota(jnp.int32, (nb, r), 1) >> shift)).astype(F32)
    t128 = lax.broadcasted_iota(jnp.int32, (r, DH), 0) & (s - 1)
    sub8 = lax.broadcasted_iota(jnp.int32, (SUBLANES, DH), 0) >> shift
    scale = DH ** -0.5

    g_c, lf_c, bt_c, g_r, bt_r = _gate_terms(gate_ref[...], gb_ref, gbt_ref, tril, triu)
    bt_last_c = _dot_exact(same.astype(F32), lf_c)
    m_rows = _dot_exact(expand, m0_ref[...])
    for h in range(MH):
        hs = slice(h * DH, (h + 1) * DH)
        m_prev = m_rows[:, h:h + 1]
        w, s_inter, mt, it_col, bt_col = _head_scores(h, g_c, bt_c, g_r, bt_r, m_prev, mask)
        qh = q_ref[:, hs]
        kh = k_ref[:, hs] * scale
        vh = v_ref[:, hs]
        qb = qh.astype(BF16)
        kb = kh.astype(BF16)
        sqk = _dot_nt(qb, kb) * w
        for j in range(r // SUBLANES):
            rows = slice(j * SUBLANES, (j + 1) * SUBLANES)
            acc = None
            q8 = q_ref[rows, hs].astype(BF16)
            for p in range(per_tile):
                part = _dot_nt(q8, c0_ref[j * per_tile + p, h].astype(BF16))
                acc = part if acc is None else jnp.where(sub8 == p, part, acc)
            qc_scr[rows, :] = acc
        n_rows = _dot_exact(expand, n0_ref[:, h, :])
        qn = jnp.sum(qh * n_rows, axis=-1, keepdims=True)
        oa_ref[:, hs] = _head_output(sqk, vh.astype(BF16), s_inter, qc_scr[...], qn, mt,
                                     o_ref[:, hs], mg_ref[:, hs])
        m_new = _bcast_block_last(jnp.broadcast_to(mt, (r, DH)), t128, s)
        bt_last = bt_last_c[:, MH + h:MH + h + 1]
        wk = jnp.exp(bt_last - bt_col + it_col - m_new)
        decay = jnp.exp(bt_last + m_prev - m_new)
        vw = vh * wk
        for j in range(r // SUBLANES):
            rows = slice(j * SUBLANES, (j + 1) * SUBLANES)
            k8 = (k_ref[rows, hs] * scale).astype(BF16)
            for p in range(per_tile):
                bidx = j * per_tile + p
                last = bidx * s + s - 1
                upd = _dot_tn(jnp.where(sub8 == p, vw[rows], 0.0).astype(BF16), k8)
                c_ref[bidx, h] = decay[last:last + 1, 0:1] * c0_ref[bidx, h] + upd
        st_scr[0] = decay
        st_scr[1] = m_new
        last_rows = pl.ds(s - 1, nb, stride=s)
        n_ref[:, h, :] = st_scr[0, last_rows, :] * n0_ref[:, h, :] + _dot_exact(gather, kh * wk)
        m_ref[:, h:h + 1] = st_scr[1, last_rows, :][:, 0:1]
    t256 = lax.broadcasted_iota(jnp.int32, (r, D_CONV), 0) & (s - 1)
    z = cg_ref[...] * hc_ref[...]
    cb_a = _dot_exact(expand, cb0_ref[:, 0, :])
    cb_b = _dot_exact(expand, cb0_ref[:, 1, :])
    z1 = jnp.where(t256 == 0, cb_b, pltpu.roll(z, 1, 0))
    z2 = jnp.where(t256 == 0, cb_a, jnp.where(t256 == 1, cb_b, pltpu.roll(z, 2, 0)))
    yc = cw_ref[0:1, :] * z2 + cw_ref[1:2, :] * z1 + cw_ref[2:3, :] * z
    oc_ref[...] = _rms(bg_ref[...] * yc) * mg_ref[:, D_MLSTM + D_S5:]
    for half in range(D_CONV // DH):
        lanes = slice(half * DH, (half + 1) * DH)
        st_scr[half] = z[:, lanes]
        cb_ref[:, 0, lanes] = st_scr[half, pl.ds(s - 2, nb, stride=s), :]
        cb_ref[:, 1, lanes] = st_scr[half, pl.ds(s - 1, nb, stride=s), :]


def _mlstm_conv_short(qkvo, gate, bch, w, states_in, states_prev, *, layer, depth, b, s, nb):
    assert SUBLANES % s == 0 and s >= CONV_W - 1 and b % nb == 0 and (nb * s) % SUBLANES == 0
    r = nb * s
    kern = functools.partial(_mlstm_conv_short_kernel, layer=layer, nb=nb, s=s)
    col = lambda wd, j: pl.BlockSpec((r, wd), lambda i: (i, j))
    st = lambda shape: pl.BlockSpec((None, nb) + shape, lambda i: (layer, i) + (0,) * len(shape))
    cst = lambda a: pl.BlockSpec(a.shape, lambda i: (0,) * a.ndim)
    state_specs = [st((MH, DH, DH)), st((MH, DH)), st((MH,)), st((CONV_W - 1, D_CONV))]
    out_state_specs = state_specs
    if layer == 0:
        st0 = lambda shape: pl.BlockSpec((depth, nb) + shape, lambda i: (0, i) + (0,) * len(shape))
        out_state_specs = [st0((MH, DH, DH)), st0((MH, DH)), st0((MH,)), st0((CONV_W - 1, D_CONV))]
    consts = (w["gb"], w["gbt"], w["cw"], w["mg"])
    args = [qkvo, qkvo, qkvo, qkvo, gate, bch, bch, bch, *consts, *states_in]
    in_specs = [col(D_MLSTM, 0), col(D_MLSTM, 1), col(D_MLSTM, 2), col(D_MLSTM, 3), col(GATE_W, 0),
                col(D_CONV, 0), col(D_CONV, 1), col(D_CONV, 2)] + [cst(a) for a in consts] + state_specs
    aliases = {}
    if layer > 0:
        aliases = {len(args) + j: 2 + j for j in range(4)}
        args += list(states_prev)
        in_specs += [_ANY] * 4
    return pl.pallas_call(
        kern,
        grid=(b // nb,),
        in_specs=in_specs,
        out_specs=[col(D_MLSTM, 0), col(D_CONV, 0)] + out_state_specs,
        out_shape=[jax.ShapeDtypeStruct((b * s, D_MLSTM), F32), jax.ShapeDtypeStruct((b * s, D_CONV), F32)]
        + _state_shapes(depth, b),
        scratch_shapes=[pltpu.VMEM((r, DH), F32), pltpu.VMEM((2, r, DH), F32)],
        input_output_aliases=aliases,
        compiler_params=pltpu.CompilerParams(dimension_semantics=("arbitrary",), vmem_limit_bytes=VMEM_LIMIT),
        name="mlstm_conv_short",
    )(*args)


def _s5_kernel(*refs, layer, nb, tt, zero_state, parts, batch_major):
    u_ref, lam_ref, bblk_ref, cblk_ref, d_ref, wglu_ref, mg_ref = refs[:7]
    refs = refs[7:]
    if not zero_state:
        hr0_ref, hi0_ref = refs[:2]
        refs = refs[2:]
    if layer > 0:
        refs = refs[2:]
    ob_ref, hr_ref, hi_ref, xs_ref, a_ref, bf_ref = refs[:6]
    if batch_major:
        ut_scr, ot_scr = refs[6:]
        ut_scr[...] = jnp.swapaxes(u_ref[...], 0, 1).reshape(tt * nb, D_S5)
        u_bm_ref, ob_bm_ref, u_ref, ob_ref = u_ref, ob_ref, ut_scr, ot_scr
    if layer == 0:
        hr_all, hi_all = hr_ref, hi_ref
        hr_ref, hi_ref = hr_all.at[0], hi_all.at[0]
    ti = pl.program_id(0)
    n = S5_N
    lc = 512

    @pl.when(ti == 0)
    def _():
        if layer == 0:
            hr_all[...] = jnp.zeros_like(hr_all)
            hi_all[...] = jnp.zeros_like(hi_all)
        hr_ref[...] = jnp.zeros_like(hr_ref) if zero_state else hr0_ref[...]
        hi_ref[...] = jnp.zeros_like(hi_ref) if zero_state else hi0_ref[...]
        lre = jnp.minimum(lam_ref[0:1, :], -1e-4)
        lim = lam_ref[1:2, :]
        dt = jnp.exp(lam_ref[2:3, :])
        mag = jnp.exp(lre * dt)
        ab_re = mag * jnp.cos(lim * dt)
        ab_im = mag * jnp.sin(lim * dt)
        den = lre * lre + lim * lim
        nr = ab_re - 1.0
        fre = (nr * lre + ab_im * lim) / den
        fim = (ab_im * lre - nr * lim) / den
        a_ref[:, :n] = jnp.broadcast_to(ab_re, (SUBLANES, n))
        a_ref[:, n:] = jnp.broadcast_to(ab_im, (SUBLANES, n))
        b_re = bblk_ref[:, :n]
        b_im = bblk_ref[:, n:]
        bf_ref[:, :n] = (fre * b_re - fim * b_im).astype(BF16)
        bf_ref[:, n:] = (fre * b_im + fim * b_re).astype(BF16)

    rows_p = nb * tt // parts
    part_rows = [slice(p * rows_p, (p + 1) * rows_p) for p in range(parts)]
    for rs in part_rows:
        xs_ref[rs, :] = _dot(u_ref[rs, :].astype(BF16), bf_ref[...])
    ar = [a_ref[:, c0:c0 + lc] for c0 in range(0, n, lc)]
    ai = [a_ref[:, n + c0:n + c0 + lc] for c0 in range(0, n, lc)]
    h = {}
    for p, rs in enumerate(part_rows):
        for rg in range(nb // SUBLANES):
            g8 = slice(rg * SUBLANES, (rg + 1) * SUBLANES)
            for ci, c0 in enumerate(range(0, n, lc)):
                if p == 0:
                    h[rg, ci] = (hr_ref[g8, c0:c0 + lc], hi_ref[g8, c0:c0 + lc])
            for t in range(p * tt // parts, (p + 1) * tt // parts):
                r8 = slice(t * nb + rg * SUBLANES, t * nb + (rg + 1) * SUBLANES)
                for ci, c0 in enumerate(range(0, n, lc)):
                    hr, hi = h[rg, ci]
                    nr_ = ar[ci] * hr - ai[ci] * hi + xs_ref[r8, c0:c0 + lc]
                    ni_ = ar[ci] * hi + ai[ci] * hr + xs_ref[r8, n + c0:n + c0 + lc]
                    xs_ref[r8, c0:c0 + lc] = nr_
                    xs_ref[r8, n + c0:n + c0 + lc] = ni_
                    h[rg, ci] = (nr_, ni_)
            for ci, c0 in enumerate(range(0, n, lc)):
                if p == parts - 1:
                    hr_ref[g8, c0:c0 + lc] = h[rg, ci][0]
                    hi_ref[g8, c0:c0 + lc] = h[rg, ci][1]
    y = [_dot(xs_ref[rs, :].astype(BF16), cblk_ref[...]) + d_ref[...] * u_ref[rs, :] for rs in part_rows]
    g = [jax.nn.gelu(v) for v in y]
    gate = [_dot(v.astype(BF16), wglu_ref[...]) for v in g]
    for rs, v, z in zip(part_rows, g, gate):
        ob_ref[rs, :] = _rms(v * jax.nn.sigmoid(z)) * mg_ref[:, D_MLSTM:D_MLSTM + D_S5]
    if batch_major:
        ob_bm_ref[...] = jnp.swapaxes(ot_scr[...].reshape(tt, nb, D_S5), 0, 1)


def _s5(u, w, states_in, states_prev, *, layer, depth, nb, tt):
    batch_major = u.ndim == 3
    rows = u.shape[0] * u.shape[1] if batch_major else u.shape[0]
    rt = nb * tt
    zero_state = states_in is None
    parts = 4 if tt % 4 == 0 and tt >= 16 else 1
    kern = functools.partial(_s5_kernel, layer=layer, nb=nb, tt=tt, zero_state=zero_state, parts=parts,
                             batch_major=batch_major)
    if batch_major:
        assert u.shape[0] == nb and tt % SUBLANES == 0
        io_spec = pl.BlockSpec((nb, tt, D_S5), lambda i: (0, i, 0))
        io_shape = jax.ShapeDtypeStruct(u.shape, F32)
        io_scratch = [pltpu.VMEM((rt, D_S5), F32), pltpu.VMEM((rt, D_S5), F32)]
    else:
        io_spec = pl.BlockSpec((rt, D_S5), lambda i: (i, 0))
        io_shape = jax.ShapeDtypeStruct((rows, D_S5), F32)
        io_scratch = []
    cst = lambda a: pl.BlockSpec(a.shape, lambda i: (0,) * a.ndim)
    st = pl.BlockSpec((None, nb, S5_N), lambda i: (layer, 0, 0))
    st_out = pl.BlockSpec((depth, nb, S5_N), lambda i: (0, 0, 0)) if layer == 0 else st
    consts = (w["lam"], w["bblk"], w["cblk"], w["d"], w["wglu"], w["mg"])
    args = [u, *consts]
    in_specs = [io_spec] + [cst(a) for a in consts]
    if not zero_state:
        args += list(states_in)
        in_specs += [st, st]
    aliases = {}
    if layer > 0:
        aliases = {len(args) + j: 1 + j for j in range(2)}
        args += list(states_prev)
        in_specs += [_ANY] * 2
    return pl.pallas_call(
        kern,
        grid=(rows // rt,),
        in_specs=in_specs,
        out_specs=[io_spec, st_out, st_out],
        out_shape=[io_shape, jax.ShapeDtypeStruct((depth, nb, S5_N), F32),
                   jax.ShapeDtypeStruct((depth, nb, S5_N), F32)],
        scratch_shapes=[pltpu.VMEM((rt, 2 * S5_N), F32), pltpu.VMEM((SUBLANES, 2 * S5_N), F32),
                        pltpu.VMEM((D_S5, 2 * S5_N), BF16)] + io_scratch,
        input_output_aliases=aliases,
        compiler_params=pltpu.CompilerParams(dimension_semantics=("arbitrary",), vmem_limit_bytes=VMEM_LIMIT),
        name="s5",
    )(*args)


def _block_diag(w):
    g, r, c = w.shape
    eye = jnp.eye(g, dtype=w.dtype)
    return jnp.einsum("grc,gh->grhc", w, eye).reshape(g * r, g * c)


def _layer_weights(l, ffn1_w1, ffn1_w3, ffn1_w2, ffn2_w1, ffn2_w3, ffn2_w2, norm_g, w_in, ig_bias, fg_bias,
                   s5_a_re, s5_a_im, s5_log_dt, s5_b_re, s5_b_im, s5_c_re, s5_c_im, s5_d, w_glu, conv_w,
                   mix_g, w_out):
    o = 4 * D_MLSTM
    win = (w_in[l, :, :o].astype(BF16),
           jnp.pad(w_in[l, :, o:o + 2 * MH].astype(BF16), ((0, 0), (0, GATE_W - 2 * MH))),
           w_in[l, :, o + 2 * MH:].astype(BF16))
    gb = jnp.concatenate([ig_bias[l], fg_bias[l], jnp.zeros((GATE_W - 2 * MH,), F32)])
    lam = jnp.stack([s5_a_re[l].reshape(-1), s5_a_im[l].reshape(-1),
                     jnp.repeat(s5_log_dt[l], S5_P)])
    bblk = jnp.concatenate([_block_diag(jnp.swapaxes(s5_b_re[l], 1, 2)),
                            _block_diag(jnp.swapaxes(s5_b_im[l], 1, 2))], axis=1)
    cblk = jnp.concatenate([_block_diag(jnp.swapaxes(s5_c_re[l], 1, 2)),
                            -_block_diag(jnp.swapaxes(s5_c_im[l], 1, 2))], axis=0).astype(BF16)
    return dict(
        f1=(ffn1_w1[l].astype(BF16), ffn1_w3[l].astype(BF16), ffn1_w2[l].astype(BF16)),
        f2=(ffn2_w1[l].astype(BF16), ffn2_w3[l].astype(BF16), ffn2_w2[l].astype(BF16)),
        g=norm_g[l], win=win, gb=gb.reshape(1, GATE_W), gbt=gb[:SUBLANES].reshape(SUBLANES, 1),
        lam=lam, bblk=bblk, cblk=cblk, d=s5_d[l].reshape(1, D_S5), wglu=w_glu[l].astype(BF16),
        cw=conv_w[l], mg=mix_g[l].reshape(1, D_MODEL), wo=w_out[l].astype(BF16))


def _run_group(x, states, weights, *, tm, tt, chunk=None, bb=None, nb=None):
    b, s, _ = x.shape
    depth = len(weights)
    xf = x.reshape(b * s, D_MODEL)
    mstates = sstates = None
    if states is not None:
        c_all, n_all, m_all, sr_all, si_all, cb_all = states
        ms_in = (c_all, n_all, m_all, cb_all)
        ss_in = (sr_all.reshape(depth, b, S5_N), si_all.reshape(depth, b, S5_N))
    else:
        ms_in = ss_in = None
    for l, w in enumerate(weights):
        x1, qkvo, gate, u, bch = _ffn_inproj(xf, w["g"], *w["f1"], w["win"], tm)
        if tt % SUBLANES == 0:
            ob, *sstates = _s5(u.reshape(b, s, D_S5), w, ss_in, sstates, layer=l, depth=depth, nb=b, tt=tt)
            ob = ob.reshape(b * s, D_S5)
        else:
            u_tm = jnp.swapaxes(u.reshape(b, s, D_S5), 0, 1).reshape(s * b, D_S5)
            ob_tm, *sstates = _s5(u_tm, w, ss_in, sstates, layer=l, depth=depth, nb=b, tt=tt)
            ob = jnp.swapaxes(ob_tm.reshape(s, b, D_S5), 0, 1).reshape(b * s, D_S5)
        if chunk is not None:
            r3 = lambda a: a.reshape(b, s, a.shape[-1])
            oa, oc, *mstates = _mlstm_conv(r3(qkvo), r3(gate), r3(bch), w, ms_in, mstates,
                                           layer=l, depth=depth, bb=bb, chunk=chunk)
            xf = _outproj_ffn(x1, oa.reshape(b * s, D_MLSTM), ob, oc.reshape(b * s, D_CONV),
                              w["g"], w["wo"], *w["f2"], tm)
        else:
            oa, oc, *mstates = _mlstm_conv_short(qkvo, gate, bch, w, ms_in, mstates,
                                                 layer=l, depth=depth, b=b, s=s, nb=nb)
            xf = _outproj_ffn(x1, oa, ob, oc, w["g"], w["wo"], *w["f2"], tm)
    c1, n1, m1, cb1 = mstates
    sr1, si1 = sstates
    return xf.reshape(b, s, D_MODEL), (c1, n1, m1, sr1.reshape(depth, b, S5_G, S5_P),
                                       si1.reshape(depth, b, S5_G, S5_P), cb1)


def kernel(x_prompt, x_sample, state_mlstm_C, state_mlstm_n, state_mlstm_m, state_s5_re, state_s5_im,
           state_conv, ffn1_w1, ffn1_w3, ffn1_w2, ffn2_w1, ffn2_w3, ffn2_w2, norm_g, w_in, ig_bias, fg_bias,
           s5_a_re, s5_a_im, s5_log_dt, s5_b_re, s5_b_im, s5_c_re, s5_c_im, s5_d, w_glu, conv_w, mix_g, w_out):
    depth = norm_g.shape[0]
    params = (ffn1_w1, ffn1_w3, ffn1_w2, ffn2_w1, ffn2_w3, ffn2_w2, norm_g, w_in, ig_bias, fg_bias,
              s5_a_re, s5_a_im, s5_log_dt, s5_b_re, s5_b_im, s5_c_re, s5_c_im, s5_d, w_glu, conv_w, mix_g, w_out)
    weights = [_layer_weights(l, *params) for l in range(depth)]
    y_p, st_p = _run_group(x_prompt, None, weights, tm=512, tt=64, chunk=128, bb=4)
    y_s, st_s = _run_group(
        x_sample, (state_mlstm_C, state_mlstm_n, state_mlstm_m, state_s5_re, state_s5_im, state_conv),
        weights, tm=512, tt=x_sample.shape[1], nb=16)
    return (y_p, y_s, *st_p, *st_s)
```

```python
import functools

import jax
import jax.numpy as jnp
from jax import lax
from jax.experimental import pallas as pl
from jax.experimental.pallas import tpu as pltpu

F32 = jnp.float32
BF16 = jnp.bfloat16

D_MODEL = 1024
MH = 4
DH = 128
D_MLSTM = MH * DH
S5_CH = 16
S5_G = 16
S5_P = 64
D_S5 = S5_G * S5_CH
S5_N = S5_G * S5_P
D_CONV = 256
CONV_W = 3
EPS = 1e-6
GATE_W = 128
SUBLANES = 8
NEG = -1e30
VMEM_LIMIT = 56 * 1024 * 1024
ROW_PARTS = 2


def _dot(a, b):
    return jnp.dot(a, b, preferred_element_type=F32)


def _dot_nt(a, b):
    return lax.dot_general(a, b, (((1,), (1,)), ((), ())), preferred_element_type=F32)


def _dot_tn(a, b):
    return lax.dot_general(a, b, (((0,), (0,)), ((), ())), preferred_element_type=F32)


def _dot_exact(a, b):
    return jnp.dot(a, b, preferred_element_type=F32, precision=lax.Precision.HIGHEST)


def _dot_nt_exact(a, b):
    return lax.dot_general(a, b, (((1,), (1,)), ((), ())), preferred_element_type=F32,
                           precision=lax.Precision.HIGHEST)


def _rms(x):
    return x * lax.rsqrt(jnp.mean(x * x, axis=-1, keepdims=True) + EPS)


def _log_sigmoid(x):
    return jnp.minimum(x, 0.0) - jnp.log(1.0 + jnp.exp(-jnp.abs(x)))


def _split_bf16(x, parts):
    out = []
    for _ in range(parts):
        p = x.astype(BF16)
        out.append(p)
        x = x - p.astype(F32)
    return jnp.concatenate(out, axis=1)


def _ffn_residual(xs, g_pre, g_post, w1, w3, w2):
    xn = [(_rms(x) * g_pre).astype(BF16) for x in xs]
    h1 = [_dot(v, w1[...]) for v in xn]
    h3 = [_dot(v, w3[...]) for v in xn]
    a = [(jax.nn.silu(p) * q).astype(BF16) for p, q in zip(h1, h3)]
    y = [_dot(v, w2[...]) for v in a]
    return [x + 0.5 * (_rms(t) * g_post) for x, t in zip(xs, y)]


def _row_parts(tm, parts):
    step = tm // parts
    return [slice(i * step, (i + 1) * step) for i in range(parts)]


def _layer_spec(a, layer, cols=None):
    shape = (None, a.shape[1], a.shape[2] if cols is None else cols)
    return pl.BlockSpec(shape, lambda *_: (layer, 0, 0), pipeline_mode=pl.Buffered(1))


_ANY = pl.BlockSpec(memory_space=pl.ANY)


def _ffn_inproj_kernel(x_ref, g_ref, w1_ref, w3_ref, w2_ref, wq_ref, wg_ref, wr_ref,
                       x1_ref, qkvo_ref, gate_ref, u_ref, bch_ref):
    parts = _row_parts(x_ref.shape[0], ROW_PARTS)
    x1 = _ffn_residual([x_ref[r, :] for r in parts], g_ref[0:1, :], g_ref[1:2, :], w1_ref, w3_ref, w2_ref)
    hn = [(_rms(v) * g_ref[2:3, :]).astype(BF16) for v in x1]
    qkvo = [_dot(v, wq_ref[...]) for v in hn]
    gate = [_dot(v, wg_ref[...]) for v in hn]
    rest = [_dot(v, wr_ref[...]) for v in hn]
    for i, r in enumerate(parts):
        x1_ref[r, :] = x1[i]
        qkvo_ref[r, :] = qkvo[i]
        gate_ref[r, :] = gate[i]
        u_ref[r, :] = rest[i][:, :D_S5]
        bch_ref[r, :] = rest[i][:, D_S5:]


def _ffn_inproj(x, g, w1, w3, w2, win, tm, layer):
    t = x.shape[0]
    row = lambda w: pl.BlockSpec((tm, w), lambda i: (i, 0))
    return pl.pallas_call(
        _ffn_inproj_kernel,
        grid=(t // tm,),
        in_specs=[row(D_MODEL), _layer_spec(g, layer), _layer_spec(w1, layer), _layer_spec(w3, layer),
                  _layer_spec(w2, layer), _layer_spec(win[0], layer, cols=4 * D_MLSTM),
                  _layer_spec(win[1], layer), _layer_spec(win[2], layer)],
        out_specs=[row(D_MODEL), row(4 * D_MLSTM), row(GATE_W), row(D_S5), row(3 * D_CONV)],
        out_shape=[jax.ShapeDtypeStruct((t, D_MODEL), F32), jax.ShapeDtypeStruct((t, 4 * D_MLSTM), F32),
                   jax.ShapeDtypeStruct((t, GATE_W), F32), jax.ShapeDtypeStruct((t, D_S5), F32),
                   jax.ShapeDtypeStruct((t, 3 * D_CONV), F32)],
        compiler_params=pltpu.CompilerParams(dimension_semantics=("arbitrary",), vmem_limit_bytes=VMEM_LIMIT),
        name="ffn_inproj",
    )(x, g, w1, w3, w2, *win)


def _outproj_ffn_kernel(x_ref, a_ref, b_ref, c_ref, g_ref, wo_ref, w1_ref, w3_ref, w2_ref, y_ref):
    parts = _row_parts(x_ref.shape[0], ROW_PARTS)
    mo = [(_dot(a_ref[r, :].astype(BF16), wo_ref[0:D_MLSTM, :])
           + _dot(b_ref[r, :].astype(BF16), wo_ref[D_MLSTM:D_MLSTM + D_S5, :])
           + _dot(c_ref[r, :].astype(BF16), wo_ref[D_MLSTM + D_S5:, :])) for r in parts]
    x2 = [x_ref[r, :] + _rms(v) * g_ref[3:4, :] for r, v in zip(parts, mo)]
    y = _ffn_residual(x2, g_ref[4:5, :], g_ref[5:6, :], w1_ref, w3_ref, w2_ref)
    for r, v in zip(parts, y):
        y_ref[r, :] = v


def _outproj_ffn(x, oa, ob, oc, g, wo, w1, w3, w2, tm, layer):
    t = x.shape[0]
    row = lambda w: pl.BlockSpec((tm, w), lambda i: (i, 0))
    return pl.pallas_call(
        _outproj_ffn_kernel,
        grid=(t // tm,),
        in_specs=[row(D_MODEL), row(D_MLSTM), row(D_S5), row(D_CONV), _layer_spec(g, layer),
                  _layer_spec(wo, layer), _layer_spec(w1, layer), _layer_spec(w3, layer),
                  _layer_spec(w2, layer)],
        out_specs=row(D_MODEL),
        out_shape=jax.ShapeDtypeStruct((t, D_MODEL), F32),
        compiler_params=pltpu.CompilerParams(dimension_semantics=("arbitrary",), vmem_limit_bytes=VMEM_LIMIT),
        name="outproj_ffn",
    )(x, oa, ob, oc, g, wo, w1, w3, w2)


def _state_shapes(depth, b):
    return [jax.ShapeDtypeStruct((depth, b, MH, DH, DH), F32), jax.ShapeDtypeStruct((depth, b, MH, DH), F32),
            jax.ShapeDtypeStruct((depth, b, MH), F32), jax.ShapeDtypeStruct((depth, b, CONV_W - 1, D_CONV), F32)]


def _cummax_rows(x, rowi):
    s = 1
    while s < x.shape[0]:
        x = jnp.maximum(x, jnp.where(rowi >= s, pltpu.roll(x, s, 0), NEG))
        s *= 2
    return x


def _mlstm_conv_kernel(*refs, layer, bb, chunk, zero_state):
    (q_ref, k_ref, v_ref, o_ref, gate_ref, bg_ref, cg_ref, hc_ref, gb_ref, gbt_ref, cw_ref, mg_ref) = refs[:12]
    refs = refs[12:]
    if not zero_state:
        c0_ref, n0_ref, m0_ref, cb0_ref = refs[:4]
        refs = refs[4:]
    if layer > 0:
        refs = refs[4:]
    oa_ref, oc_ref, c_ref, n_ref, m_ref, cb_ref, m_scr = refs
    other_layers = ()
    if layer == 0:
        other_layers = tuple(r.at[1:] for r in (c_ref, n_ref, m_ref, cb_ref) if r.shape[0] > 1)
        c_ref, n_ref, m_ref, cb_ref = c_ref.at[0], n_ref.at[0], m_ref.at[0], cb_ref.at[0]
    assert chunk == DH
    bi0 = pl.program_id(0) * bb
    ci = pl.program_id(1)

    @pl.when(ci == 0)
    def _():
        for r in other_layers:
            r[...] = jnp.zeros_like(r)
        m_scr[...] = jnp.zeros_like(m_scr)
        if zero_state:
            c_ref[...] = jnp.zeros_like(c_ref)
            n_ref[...] = jnp.zeros_like(n_ref)
            cb_ref[...] = jnp.zeros_like(cb_ref)
        else:
            c_ref[...] = c0_ref[...]
            n_ref[...] = n0_ref[...]
            cb_ref[...] = cb0_ref[...]
            for bi in range(bb):
                m_scr[bi:bi + 1, 0:MH] = m0_ref[pl.ds(bi0 + bi, 1), :]

    row = lax.broadcasted_iota(jnp.int32, (chunk, chunk), 0)
    col = lax.broadcasted_iota(jnp.int32, (chunk, chunk), 1)
    causal = col <= row
    tril = jnp.where(causal, 1.0, 0.0).astype(BF16)
    rowg = lax.broadcasted_iota(jnp.int32, (chunk, GATE_W), 0)
    rowi = lax.broadcasted_iota(jnp.int32, (chunk, D_CONV), 0)
    sel8 = (lax.broadcasted_iota(jnp.int32, (SUBLANES, GATE_W), 0)
            == lax.broadcasted_iota(jnp.int32, (SUBLANES, GATE_W), 1)).astype(F32)
    spread = jnp.where((lax.broadcasted_iota(jnp.int32, (2 * GATE_W, D_MLSTM), 0) & (GATE_W - 1))
                       == (lax.broadcasted_iota(jnp.int32, (2 * GATE_W, D_MLSTM), 1) >> 7), 1.0, 0.0).astype(BF16)
    ones_b = jnp.ones((chunk, DH), BF16)
    zeros_b = jnp.zeros((chunk, DH), BF16)
    ones_tall = jnp.ones((2 * DH, DH), BF16)
    scale = DH ** -0.5

    units = [(bi, h) for bi in range(bb) for h in range(MH)]
    hsl = lambda h: slice(h * DH, (h + 1) * DH)
    g_c = [gate_ref[bi] + gb_ref[...] for bi in range(bb)]
    csum = [_dot(tril, _split_bf16(pltpu.roll(_log_sigmoid(g), GATE_W - MH, 1), 3)) for g in g_c]
    qb = {u: q_ref[u[0], :, hsl(u[1])].astype(BF16) for u in units}
    kb = {u: (k_ref[u[0], :, hsl(u[1])] * scale).astype(BF16) for u in units}
    s_qk = {u: _dot_nt(qb[u], kb[u]) for u in units}
    inter = {u: _dot_nt(qb[u], jnp.concatenate(
        [c_ref[u[0], u[1]].astype(BF16),
         jnp.broadcast_to(n_ref[u[0], u[1]:u[1] + 1, :], (DH, DH)).astype(BF16)], axis=0)) for u in units}
    rep, a_r, decay = [], [], []
    for bi in range(bb):
        f_c = csum[bi][:, :GATE_W] + csum[bi][:, GATE_W:2 * GATE_W] + csum[bi][:, 2 * GATE_W:]
        a_c = g_c[bi] - f_c
        m_prev = m_scr[bi:bi + 1, :]
        big_m = jnp.maximum(_cummax_rows(a_c, rowg), m_prev)
        mt = f_c + big_m
        m_new = mt[chunk - 1:chunk, :]
        f_last = f_c[chunk - 1:chunk, :]
        decay.append(jnp.exp(f_last + m_prev - m_new))
        stacked = jnp.concatenate([_split_bf16(-big_m, 2), _split_bf16(jnp.exp(m_prev - big_m), 2),
                                   _split_bf16(-mt, 2), _split_bf16(jnp.exp(a_c + (f_last - m_new)), 2)],
                                  axis=0)
        rep.append(_dot(stacked, spread))
        a_r.append(_dot_nt_exact(sel8, a_c))
        m_scr[bi:bi + 1, :] = m_new
        m_ref[pl.ds(bi0 + bi, 1), :] = m_new[:, 0:MH]
    intra = {}
    for u in units:
        bi, h = u
        w = jnp.exp(jnp.where(causal, rep[bi][0:chunk, hsl(h)] + a_r[bi][h:h + 1, :], NEG))
        rhs = jnp.concatenate([jnp.concatenate([v_ref[bi, :, hsl(h)].astype(BF16), ones_b], axis=1),
                               jnp.concatenate([zeros_b, ones_b], axis=1)], axis=0)
        intra[u] = _dot(_split_bf16(s_qk[u] * w, 2), rhs)
    hm, ssq = {}, {}
    for u in units:
        bi, h = u
        s_inter = rep[bi][chunk:2 * chunk, hsl(h)]
        e_floor = jnp.exp(rep[bi][2 * chunk:3 * chunk, hsl(h)])
        num = intra[u][:, :DH] + s_inter * inter[u][:, :DH]
        den = intra[u][:, DH:] + s_inter * inter[u][:, DH:]
        hm[u] = jax.nn.sigmoid(o_ref[bi, :, hsl(h)]) * (num / jnp.maximum(jnp.abs(den), e_floor))
        ssq[u] = _dot(_split_bf16(hm[u] * hm[u], 2), ones_tall)
    for u in units:
        bi, h = u
        oa_ref[bi, :, hsl(h)] = hm[u] * lax.rsqrt(ssq[u] * (1.0 / DH) + EPS) * mg_ref[:, hsl(h)]
        wk = rep[bi][3 * chunk:, hsl(h)]
        dec = decay[bi][:, h:h + 1]
        c_ref[bi, h] = dec * c_ref[bi, h] + _dot_tn((v_ref[bi, :, hsl(h)] * wk).astype(BF16), kb[u])
        n_ref[bi, h:h + 1, :] = (dec * n_ref[bi, h:h + 1, :]
                                 + jnp.sum(k_ref[bi, :, hsl(h)] * scale * wk, axis=0, keepdims=True))
    for bi in range(bb):
        z = cg_ref[bi] * hc_ref[bi]
        cb = cb_ref[bi]
        z1 = jnp.where(rowi == 0, cb[1:2, :], pltpu.roll(z, 1, 0))
        z2 = jnp.where(rowi == 0, cb[0:1, :], jnp.where(rowi == 1, cb[1:2, :], pltpu.roll(z, 2, 0)))
        yc = cw_ref[0:1, :] * z2 + cw_ref[1:2, :] * z1 + cw_ref[2:3, :] * z
        oc_ref[bi] = _rms(bg_ref[bi] * yc) * mg_ref[:, D_MLSTM + D_S5:]
        cb_ref[bi] = z[chunk - 2:chunk, :]


def _mlstm_conv(qkvo, gate, bch, w, states_in, states_prev, *, layer, depth, bb, chunk):
    b, s, _ = qkvo.shape
    assert s % chunk == 0 and b % bb == 0
    zero_state = states_in is None
    kern = functools.partial(_mlstm_conv_kernel, layer=layer, bb=bb, chunk=chunk, zero_state=zero_state)
    col = lambda wd, j: pl.BlockSpec((bb, chunk, wd), lambda i, c: (i, c, j))
    st = lambda shape: pl.BlockSpec((None, bb) + shape, lambda i, c: (layer, i) + (0,) * len(shape))
    m_spec = pl.BlockSpec((None, b, MH), lambda i, c: (layer, 0, 0))
    cst = lambda a: pl.BlockSpec(a.shape, lambda i, c: (0,) * a.ndim)
    state_specs = [st((MH, DH, DH)), st((MH, DH)), m_spec, st((CONV_W - 1, D_CONV))]
    out_state_specs = state_specs
    if layer == 0:
        st0 = lambda shape: pl.BlockSpec((depth, bb) + shape, lambda i, c: (0, i) + (0,) * len(shape))
        out_state_specs = [st0((MH, DH, DH)), st0((MH, DH)), pl.BlockSpec((depth, b, MH), lambda i, c: (0, 0, 0)),
                           st0((CONV_W - 1, D_CONV))]
    consts = (w["gb"], w["gbt"], w["cw"], w["mg"])
    args = [qkvo, qkvo, qkvo, qkvo, gate, bch, bch, bch, *consts]
    in_specs = [col(D_MLSTM, 0), col(D_MLSTM, 1), col(D_MLSTM, 2), col(D_MLSTM, 3), col(GATE_W, 0),
                col(D_CONV, 0), col(D_CONV, 1), col(D_CONV, 2)] + [cst(a) for a in consts]
    if not zero_state:
        args += list(states_in)
        in_specs += state_specs
    aliases = {}
    if layer > 0:
        aliases = {len(args) + j: 2 + j for j in range(4)}
        args += list(states_prev)
        in_specs += [_ANY] * 4
    return pl.pallas_call(
        kern,
        grid=(b // bb, s // chunk),
        in_specs=in_specs,
        out_specs=[col(D_MLSTM, 0), col(D_CONV, 0)] + out_state_specs,
        out_shape=[jax.ShapeDtypeStruct((b, s, D_MLSTM), F32), jax.ShapeDtypeStruct((b, s, D_CONV), F32)]
        + _state_shapes(depth, b),
        scratch_shapes=[pltpu.VMEM((-(-bb // SUBLANES) * SUBLANES, GATE_W), F32)],
        input_output_aliases=aliases,
        compiler_params=pltpu.CompilerParams(dimension_semantics=("arbitrary", "arbitrary"),
                                             vmem_limit_bytes=VMEM_LIMIT),
        name="mlstm_conv",
    )(*args)


def _gate_terms(graw, gb_ref, gbt_ref, tril, triu):
    sel = (lax.broadcasted_iota(jnp.int32, (SUBLANES, GATE_W), 0)
           == lax.broadcasted_iota(jnp.int32, (SUBLANES, GATE_W), 1)).astype(F32)
    g_c = graw + gb_ref[...]
    lf_c = _log_sigmoid(g_c)
    g_r = _dot_nt_exact(sel, graw) + gbt_ref[...]
    lf_r = _log_sigmoid(g_r)
    return g_c, lf_c, _dot_exact(tril, lf_c), g_r, _dot_exact(lf_r, triu)


def _head_scores(h, g_c, bt_c, g_r, bt_r, m_prev, mask):
    it_row = g_r[h:h + 1, :]
    bt_row = bt_r[MH + h:MH + h + 1, :]
    it_col = g_c[:, h:h + 1]
    bt_col = bt_c[:, MH + h:MH + h + 1]
    dmat = jnp.where(mask, bt_col - bt_row + it_row, NEG)
    inter = bt_col + m_prev
    mt = jnp.maximum(jnp.max(dmat, axis=-1, keepdims=True), inter)
    return jnp.exp(dmat - mt), jnp.exp(inter - mt), mt, it_col, bt_col


def _head_output(sqk, vb, s_inter, qc, qn, mt, o, mg):
    num = _dot(sqk.astype(BF16), vb) + s_inter * qc
    den = jnp.sum(sqk, axis=-1, keepdims=True) + s_inter * qn
    hm = jax.nn.sigmoid(o) * (num / jnp.maximum(jnp.abs(den), jnp.exp(-mt)))
    return _rms(hm) * mg


def _bcast_block_last(x, t, s):
    out = x
    for d in range(1, s):
        out = jnp.where(t == s - 1 - d, pltpu.roll(x, x.shape[0] - d, 0), out)
    return out


def _mlstm_conv_short_kernel(*refs, layer, nb, s):
    (q_ref, k_ref, v_ref, o_ref, gate_ref, bg_ref, cg_ref, hc_ref, gb_ref, gbt_ref, cw_ref, mg_ref,
     c0_ref, n0_ref, m0_ref, cb0_ref) = refs[:16]
    refs = refs[16:]
    if layer > 0:
        refs = refs[4:]
    oa_ref, oc_ref, c_ref, n_ref, m_ref, cb_ref, qc_scr, st_scr = refs
    if layer == 0:
        for ref in (c_ref, n_ref, m_ref, cb_ref):
            if ref.shape[0] > 1:
                ref[1:] = jnp.zeros((ref.shape[0] - 1,) + ref.shape[1:], F32)
        c_ref, n_ref, m_ref, cb_ref = c_ref.at[0], n_ref.at[0], m_ref.at[0], cb_ref.at[0]
    r = nb * s
    per_tile = SUBLANES // s
    shift = s.bit_length() - 1
    row = lax.broadcasted_iota(jnp.int32, (r, r), 0)
    col = lax.broadcasted_iota(jnp.int32, (r, r), 1)
    same = (row >> shift) == (col >> shift)
    mask = same & (col <= row)
    tril = mask.astype(F32)
    triu = (same & (row <= col)).astype(F32)
    expand = ((lax.broadcasted_iota(jnp.int32, (r, nb), 0) >> shift)
              == lax.broadcasted_iota(jnp.int32, (r, nb), 1)).astype(F32)
    gather = (lax.broadcasted_iota(jnp.int32, (nb, r), 0)
              == (lax.broadcasted_iota(jnp.int32, (nb, r), 1) >> shift)).astype(F32)
    t128 = lax.broadcasted_iota(jnp.int32, (r, DH), 0) & (s - 1)
    sub8 = lax.broadcasted_iota(jnp.int32, (SUBLANES, DH), 0) >> shift
    scale = DH ** -0.5

    g_c, lf_c, bt_c, g_r, bt_r = _gate_terms(gate_ref[...], gb_ref, gbt_ref, tril, triu)
    bt_last_c = _dot_exact(same.astype(F32), lf_c)
    m_rows = _dot_exact(expand, m0_ref[...])
    for h in range(MH):
        hs = slice(h * DH, (h + 1) * DH)
        m_prev = m_rows[:, h:h + 1]
        w, s_inter, mt, it_col, bt_col = _head_scores(h, g_c, bt_c, g_r, bt_r, m_prev, mask)
        qh = q_ref[:, hs]
        kh = k_ref[:, hs] * scale
        vh = v_ref[:, hs]
        qb = qh.astype(BF16)
        kb = kh.astype(BF16)
        sqk = _dot_nt(qb, kb) * w
        for j in range(r // SUBLANES):
            rows = slice(j * SUBLANES, (j + 1) * SUBLANES)
            acc = None
            q8 = q_ref[rows, hs].astype(BF16)
            for p in range(per_tile):
                part = _dot_nt(q8, c0_ref[j * per_tile + p, h].astype(BF16))
                acc = part if acc is None else jnp.where(sub8 == p, part, acc)
            qc_scr[rows, :] = acc
        n_rows = _dot_exact(expand, n0_ref[:, h, :])
        qn = jnp.sum(qh * n_rows, axis=-1, keepdims=True)
        oa_ref[:, hs] = _head_output(sqk, vh.astype(BF16), s_inter, qc_scr[...], qn, mt,
                                     o_ref[:, hs], mg_ref[:, hs])
        m_new = _bcast_block_last(jnp.broadcast_to(mt, (r, DH)), t128, s)
        bt_last = bt_last_c[:, MH + h:MH + h + 1]
        wk = jnp.exp(bt_last - bt_col + it_col - m_new)
        decay = jnp.exp(bt_last + m_prev - m_new)
        vw = vh * wk
        for j in range(r // SUBLANES):
            rows = slice(j * SUBLANES, (j + 1) * SUBLANES)
            k8 = (k_ref[rows, hs] * scale).astype(BF16)
            for p in range(per_tile):
                bidx = j * per_tile + p
                last = bidx * s + s - 1
                upd = _dot_tn(jnp.where(sub8 == p, vw[rows], 0.0).astype(BF16), k8)
                c_ref[bidx, h] = decay[last:last + 1, 0:1] * c0_ref[bidx, h] + upd
        st_scr[0] = decay
        st_scr[1] = m_new
        last_rows = pl.ds(s - 1, nb, stride=s)
        n_ref[:, h, :] = st_scr[0, last_rows, :] * n0_ref[:, h, :] + _dot_exact(gather, kh * wk)
        m_ref[:, h:h + 1] = st_scr[1, last_rows, :][:, 0:1]
    t256 = lax.broadcasted_iota(jnp.int32, (r, D_CONV), 0) & (s - 1)
    z = cg_ref[...] * hc_ref[...]
    cb_a = _dot_exact(expand, cb0_ref[:, 0, :])
    cb_b = _dot_exact(expand, cb0_ref[:, 1, :])
    z1 = jnp.where(t256 == 0, cb_b, pltpu.roll(z, 1, 0))
    z2 = jnp.where(t256 == 0, cb_a, jnp.where(t256 == 1, cb_b, pltpu.roll(z, 2, 0)))
    yc = cw_ref[0:1, :] * z2 + cw_ref[1:2, :] * z1 + cw_ref[2:3, :] * z
    oc_ref[...] = _rms(bg_ref[...] * yc) * mg_ref[:, D_MLSTM + D_S5:]
    for half in range(D_CONV // DH):
        lanes = slice(half * DH, (half + 1) * DH)
        st_scr[half] = z[:, lanes]
        cb_ref[:, 0, lanes] = st_scr[half, pl.ds(s - 2, nb, stride=s), :]
        cb_ref[:, 1, lanes] = st_scr[half, pl.ds(s - 1, nb, stride=s), :]


def _mlstm_conv_short(qkvo, gate, bch, w, states_in, states_prev, *, layer, depth, b, s, nb):
    assert SUBLANES % s == 0 and s >= CONV_W - 1 and b % nb == 0 and (nb * s) % SUBLANES == 0
    r = nb * s
    kern = functools.partial(_mlstm_conv_short_kernel, layer=layer, nb=nb, s=s)
    col = lambda wd, j: pl.BlockSpec((r, wd), lambda i: (i, j))
    st = lambda shape: pl.BlockSpec((None, nb) + shape, lambda i: (layer, i) + (0,) * len(shape))
    cst = lambda a: pl.BlockSpec(a.shape, lambda i: (0,) * a.ndim)
    state_specs = [st((MH, DH, DH)), st((MH, DH)), st((MH,)), st((CONV_W - 1, D_CONV))]
    out_state_specs = state_specs
    if layer == 0:
        st0 = lambda shape: pl.BlockSpec((depth, nb) + shape, lambda i: (0, i) + (0,) * len(shape))
        out_state_specs = [st0((MH, DH, DH)), st0((MH, DH)), st0((MH,)), st0((CONV_W - 1, D_CONV))]
    consts = (w["gb"], w["gbt"], w["cw"], w["mg"])
    args = [qkvo, qkvo, qkvo, qkvo, gate, bch, bch, bch, *consts, *states_in]
    in_specs = [col(D_MLSTM, 0), col(D_MLSTM, 1), col(D_MLSTM, 2), col(D_MLSTM, 3), col(GATE_W, 0),
                col(D_CONV, 0), col(D_CONV, 1), col(D_CONV, 2)] + [cst(a) for a in consts] + state_specs
    aliases = {}
    if layer > 0:
        aliases = {len(args) + j: 2 + j for j in range(4)}
        args += list(states_prev)
        in_specs += [_ANY] * 4
    return pl.pallas_call(
        kern,
        grid=(b // nb,),
        in_specs=in_specs,
        out_specs=[col(D_MLSTM, 0), col(D_CONV, 0)] + out_state_specs,
        out_shape=[jax.ShapeDtypeStruct((b * s, D_MLSTM), F32), jax.ShapeDtypeStruct((b * s, D_CONV), F32)]
        + _state_shapes(depth, b),
        scratch_shapes=[pltpu.VMEM((r, DH), F32), pltpu.VMEM((2, r, DH), F32)],
        input_output_aliases=aliases,
        compiler_params=pltpu.CompilerParams(dimension_semantics=("arbitrary",), vmem_limit_bytes=VMEM_LIMIT),
        name="mlstm_conv_short",
    )(*args)


def _s5_kernel(*refs, layer, nb, tt, zero_state, parts, batch_major):
    u_ref, lam_ref, bblk_ref, cblk_ref, d_ref, wglu_ref, mg_ref = refs[:7]
    refs = refs[7:]
    if not zero_state:
        hr0_ref, hi0_ref = refs[:2]
        refs = refs[2:]
    if layer > 0:
        refs = refs[2:]
    ob_ref, hr_ref, hi_ref, xs_ref, a_ref, bf_ref = refs[:6]
    if batch_major:
        ut_scr, ot_scr = refs[6:]
        ut_scr[...] = jnp.swapaxes(u_ref[...], 0, 1).reshape(tt * nb, D_S5)
        u_bm_ref, ob_bm_ref, u_ref, ob_ref = u_ref, ob_ref, ut_scr, ot_scr
    if layer == 0:
        hr_all, hi_all = hr_ref, hi_ref
        hr_ref, hi_ref = hr_all.at[0], hi_all.at[0]
    ti = pl.program_id(0)
    n = S5_N
    lc = 512

    @pl.when(ti == 0)
    def _():
        if layer == 0:
            hr_all[...] = jnp.zeros_like(hr_all)
            hi_all[...] = jnp.zeros_like(hi_all)
        hr_ref[...] = jnp.zeros_like(hr_ref) if zero_state else hr0_ref[...]
        hi_ref[...] = jnp.zeros_like(hi_ref) if zero_state else hi0_ref[...]
        lre = jnp.minimum(lam_ref[0:1, :], -1e-4)
        lim = lam_ref[1:2, :]
        dt = jnp.exp(lam_ref[2:3, :])
        mag = jnp.exp(lre * dt)
        ab_re = mag * jnp.cos(lim * dt)
        ab_im = mag * jnp.sin(lim * dt)
        den = lre * lre + lim * lim
        nr = ab_re - 1.0
        fre = (nr * lre + ab_im * lim) / den
        fim = (ab_im * lre - nr * lim) / den
        a_ref[:, :n] = jnp.broadcast_to(ab_re, (SUBLANES, n))
        a_ref[:, n:] = jnp.broadcast_to(ab_im, (SUBLANES, n))
        b_re = bblk_ref[:, :n]
        b_im = bblk_ref[:, n:]
        bf_ref[:, :n] = (fre * b_re - fim * b_im).astype(BF16)
        bf_ref[:, n:] = (fre * b_im + fim * b_re).astype(BF16)

    rows_p = nb * tt // parts
    part_rows = [slice(p * rows_p, (p + 1) * rows_p) for p in range(parts)]
    for rs in part_rows:
        xs_ref[rs, :] = _dot(u_ref[rs, :].astype(BF16), bf_ref[...])
    ar = [a_ref[:, c0:c0 + lc] for c0 in range(0, n, lc)]
    ai = [a_ref[:, n + c0:n + c0 + lc] for c0 in range(0, n, lc)]
    h = {}
    for p, rs in enumerate(part_rows):
        for rg in range(nb // SUBLANES):
            g8 = slice(rg * SUBLANES, (rg + 1) * SUBLANES)
            for ci, c0 in enumerate(range(0, n, lc)):
                if p == 0:
                    h[rg, ci] = (hr_ref[g8, c0:c0 + lc], hi_ref[g8, c0:c0 + lc])
            for t in range(p * tt // parts, (p + 1) * tt // parts):
                r8 = slice(t * nb + rg * SUBLANES, t * nb + (rg + 1) * SUBLANES)
                for ci, c0 in enumerate(range(0, n, lc)):
                    hr, hi = h[rg, ci]
                    nr_ = ar[ci] * hr - ai[ci] * hi + xs_ref[r8, c0:c0 + lc]
                    ni_ = ar[ci] * hi + ai[ci] * hr + xs_ref[r8, n + c0:n + c0 + lc]
                    xs_ref[r8, c0:c0 + lc] = nr_
                    xs_ref[r8, n + c0:n + c0 + lc] = ni_
                    h[rg, ci] = (nr_, ni_)
            for ci, c0 in enumerate(range(0, n, lc)):
                if p == parts - 1:
                    hr_ref[g8, c0:c0 + lc] = h[rg, ci][0]
                    hi_ref[g8, c0:c0 + lc] = h[rg, ci][1]
    y = [_dot(xs_ref[rs, :].astype(BF16), cblk_ref[...]) + d_ref[...] * u_ref[rs, :] for rs in part_rows]
    g = [jax.nn.gelu(v) for v in y]
    gate = [_dot(v.astype(BF16), wglu_ref[...]) for v in g]
    for rs, v, z in zip(part_rows, g, gate):
        ob_ref[rs, :] = _rms(v * jax.nn.sigmoid(z)) * mg_ref[:, D_MLSTM:D_MLSTM + D_S5]
    if batch_major:
        ob_bm_ref[...] = jnp.swapaxes(ot_scr[...].reshape(tt, nb, D_S5), 0, 1)


def _s5(u, w, states_in, states_prev, *, layer, depth, nb, tt):
    batch_major = u.ndim == 3
    rows = u.shape[0] * u.shape[1] if batch_major else u.shape[0]
    rt = nb * tt
    zero_state = states_in is None
    parts = 4 if tt % 4 == 0 and tt >= 16 else 1
    kern = functools.partial(_s5_kernel, layer=layer, nb=nb, tt=tt, zero_state=zero_state, parts=parts,
                             batch_major=batch_major)
    if batch_major:
        assert u.shape[0] == nb and tt % SUBLANES == 0
        io_spec = pl.BlockSpec((nb, tt, D_S5), lambda i: (0, i, 0))
        io_shape = jax.ShapeDtypeStruct(u.shape, F32)
        io_scratch = [pltpu.VMEM((rt, D_S5), F32), pltpu.VMEM((rt, D_S5), F32)]
    else:
        io_spec = pl.BlockSpec((rt, D_S5), lambda i: (i, 0))
        io_shape = jax.ShapeDtypeStruct((rows, D_S5), F32)
        io_scratch = []
    cst = lambda a: pl.BlockSpec(a.shape, lambda i: (0,) * a.ndim)
    st = pl.BlockSpec((None, nb, S5_N), lambda i: (layer, 0, 0))
    st_out = pl.BlockSpec((depth, nb, S5_N), lambda i: (0, 0, 0)) if layer == 0 else st
    consts = (w["lam"], w["bblk"], w["cblk"], w["d"], w["wglu"], w["mg"])
    args = [u, *consts]
    in_specs = [io_spec] + [cst(a) for a in consts]
    if not zero_state:
        args += list(states_in)
        in_specs += [st, st]
    aliases = {}
    if layer > 0:
        aliases = {len(args) + j: 1 + j for j in range(2)}
        args += list(states_prev)
        in_specs += [_ANY] * 2
    return pl.pallas_call(
        kern,
        grid=(rows // rt,),
        in_specs=in_specs,
        out_specs=[io_spec, st_out, st_out],
        out_shape=[io_shape, jax.ShapeDtypeStruct((depth, nb, S5_N), F32),
                   jax.ShapeDtypeStruct((depth, nb, S5_N), F32)],
        scratch_shapes=[pltpu.VMEM((rt, 2 * S5_N), F32), pltpu.VMEM((SUBLANES, 2 * S5_N), F32),
                        pltpu.VMEM((D_S5, 2 * S5_N), BF16)] + io_scratch,
        input_output_aliases=aliases,
        compiler_params=pltpu.CompilerParams(dimension_semantics=("arbitrary",), vmem_limit_bytes=VMEM_LIMIT),
        name="s5",
    )(*args)


def _block_diag(w):
    g, r, c = w.shape
    eye = jnp.eye(g, dtype=w.dtype)
    return jnp.einsum("grc,gh->grhc", w, eye).reshape(g * r, g * c)


def _layer_weights(l, ig_bias, fg_bias, s5_a_re, s5_a_im, s5_log_dt, s5_b_re, s5_b_im, s5_c_re, s5_c_im, s5_d,
                   w_glu, conv_w, mix_g):
    gb = jnp.concatenate([ig_bias[l], fg_bias[l], jnp.zeros((GATE_W - 2 * MH,), F32)])
    lam = jnp.stack([s5_a_re[l].reshape(-1), s5_a_im[l].reshape(-1),
                     jnp.repeat(s5_log_dt[l], S5_P)])
    bblk = jnp.concatenate([_block_diag(jnp.swapaxes(s5_b_re[l], 1, 2)),
                            _block_diag(jnp.swapaxes(s5_b_im[l], 1, 2))], axis=1)
    cblk = jnp.concatenate([_block_diag(jnp.swapaxes(s5_c_re[l], 1, 2)),
                            -_block_diag(jnp.swapaxes(s5_c_im[l], 1, 2))], axis=0).astype(BF16)
    return dict(
        gb=gb.reshape(1, GATE_W), gbt=gb[:SUBLANES].reshape(SUBLANES, 1),
        lam=lam, bblk=bblk, cblk=cblk, d=s5_d[l].reshape(1, D_S5), wglu=w_glu[l].astype(BF16),
        cw=conv_w[l], mg=mix_g[l].reshape(1, D_MODEL))


def _stacked_weights(ffn1_w1, ffn1_w3, ffn1_w2, ffn2_w1, ffn2_w3, ffn2_w2, norm_g, w_in, w_out):
    o = 4 * D_MLSTM
    win = (w_in.astype(BF16),
           jnp.pad(w_in[:, :, o:o + 2 * MH].astype(BF16), ((0, 0), (0, 0), (0, GATE_W - 2 * MH))),
           w_in[:, :, o + 2 * MH:].astype(BF16))
    return dict(f1=(ffn1_w1.astype(BF16), ffn1_w3.astype(BF16), ffn1_w2.astype(BF16)),
                f2=(ffn2_w1.astype(BF16), ffn2_w3.astype(BF16), ffn2_w2.astype(BF16)),
                g=norm_g, win=win, wo=w_out.astype(BF16))


def _run_group(x, states, weights, big, *, tm, tt, chunk=None, bb=None, nb=None):
    b, s, _ = x.shape
    depth = len(weights)
    xf = x.reshape(b * s, D_MODEL)
    mstates = sstates = None
    if states is not None:
        c_all, n_all, m_all, sr_all, si_all, cb_all = states
        ms_in = (c_all, n_all, m_all, cb_all)
        ss_in = (sr_all.reshape(depth, b, S5_N), si_all.reshape(depth, b, S5_N))
    else:
        ms_in = ss_in = None
    for l, w in enumerate(weights):
        x1, qkvo, gate, u, bch = _ffn_inproj(xf, big["g"], *big["f1"], big["win"], tm, l)
        if tt % SUBLANES == 0:
            ob, *sstates = _s5(u.reshape(b, s, D_S5), w, ss_in, sstates, layer=l, depth=depth, nb=b, tt=tt)
            ob = ob.reshape(b * s, D_S5)
        else:
            u_tm = jnp.swapaxes(u.reshape(b, s, D_S5), 0, 1).reshape(s * b, D_S5)
            ob_tm, *sstates = _s5(u_tm, w, ss_in, sstates, layer=l, depth=depth, nb=b, tt=tt)
            ob = jnp.swapaxes(ob_tm.reshape(s, b, D_S5), 0, 1).reshape(b * s, D_S5)
        if chunk is not None:
            r3 = lambda a: a.reshape(b, s, a.shape[-1])
            oa, oc, *mstates = _mlstm_conv(r3(qkvo), r3(gate), r3(bch), w, ms_in, mstates,
                                           layer=l, depth=depth, bb=bb, chunk=chunk)
            xf = _outproj_ffn(x1, oa.reshape(b * s, D_MLSTM), ob, oc.reshape(b * s, D_CONV),
                              big["g"], big["wo"], *big["f2"], tm, l)
        else:
            oa, oc, *mstates = _mlstm_conv_short(qkvo, gate, bch, w, ms_in, mstates,
                                                 layer=l, depth=depth, b=b, s=s, nb=nb)
            xf = _outproj_ffn(x1, oa, ob, oc, big["g"], big["wo"], *big["f2"], tm, l)
    c1, n1, m1, cb1 = mstates
    sr1, si1 = sstates
    return xf.reshape(b, s, D_MODEL), (c1, n1, m1, sr1.reshape(depth, b, S5_G, S5_P),
                                       si1.reshape(depth, b, S5_G, S5_P), cb1)


def kernel(x_prompt, x_sample, state_mlstm_C, state_mlstm_n, state_mlstm_m, state_s5_re, state_s5_im,
           state_conv, ffn1_w1, ffn1_w3, ffn1_w2, ffn2_w1, ffn2_w3, ffn2_w2, norm_g, w_in, ig_bias, fg_bias,
           s5_a_re, s5_a_im, s5_log_dt, s5_b_re, s5_b_im, s5_c_re, s5_c_im, s5_d, w_glu, conv_w, mix_g, w_out):
    depth = norm_g.shape[0]
    params = (ig_bias, fg_bias, s5_a_re, s5_a_im, s5_log_dt, s5_b_re, s5_b_im, s5_c_re, s5_c_im, s5_d, w_glu,
              conv_w, mix_g)
    weights = [_layer_weights(l, *params) for l in range(depth)]
    big = _stacked_weights(ffn1_w1, ffn1_w3, ffn1_w2, ffn2_w1, ffn2_w3, ffn2_w2, norm_g, w_in, w_out)
    y_p, st_p = _run_group(x_prompt, None, weights, big, tm=512, tt=64, chunk=128, bb=4)
    y_s, st_s = _run_group(
        x_sample, (state_mlstm_C, state_mlstm_n, state_mlstm_m, state_s5_re, state_s5_im, state_conv),
        weights, big, tm=512, tt=x_sample.shape[1], nb=16)
    return (y_p, y_s, *st_p, *st_s)
```

```python
import functools

import jax
import jax.numpy as jnp
from jax import lax
from jax.experimental import pallas as pl
from jax.experimental.pallas import tpu as pltpu

F32 = jnp.float32
BF16 = jnp.bfloat16

D_MODEL = 1024
MH = 4
DH = 128
D_MLSTM = MH * DH
S5_CH = 16
S5_G = 16
S5_P = 64
D_S5 = S5_G * S5_CH
S5_N = S5_G * S5_P
D_CONV = 256
CONV_W = 3
EPS = 1e-6
GATE_W = 128
SUBLANES = 8
NEG = -1e30
VMEM_LIMIT = 56 * 1024 * 1024
ROW_PARTS = 2


def _dot(a, b):
    return jnp.dot(a, b, preferred_element_type=F32)


def _dot_nt(a, b):
    return lax.dot_general(a, b, (((1,), (1,)), ((), ())), preferred_element_type=F32)


def _dot_tn(a, b):
    return lax.dot_general(a, b, (((0,), (0,)), ((), ())), preferred_element_type=F32)


def _dot_exact(a, b):
    return jnp.dot(a, b, preferred_element_type=F32, precision=lax.Precision.HIGHEST)


def _dot_nt_exact(a, b):
    return lax.dot_general(a, b, (((1,), (1,)), ((), ())), preferred_element_type=F32,
                           precision=lax.Precision.HIGHEST)


def _rms(x):
    return x * lax.rsqrt(jnp.mean(x * x, axis=-1, keepdims=True) + EPS)


def _log_sigmoid(x):
    return jnp.minimum(x, 0.0) - jnp.log(1.0 + jnp.exp(-jnp.abs(x)))


def _split_bf16(x, parts):
    out = []
    for _ in range(parts):
        p = x.astype(BF16)
        out.append(p)
        x = x - p.astype(F32)
    return jnp.concatenate(out, axis=1)


def _ffn_residual(xs, g_pre, g_post, w1, w3, w2):
    xn = [(_rms(x) * g_pre).astype(BF16) for x in xs]
    h1 = [_dot(v, w1[...]) for v in xn]
    h3 = [_dot(v, w3[...]) for v in xn]
    a = [(jax.nn.silu(p) * q).astype(BF16) for p, q in zip(h1, h3)]
    y = [_dot(v, w2[...]) for v in a]
    return [x + 0.5 * (_rms(t) * g_post) for x, t in zip(xs, y)]


def _row_parts(tm, parts):
    step = tm // parts
    return [slice(i * step, (i + 1) * step) for i in range(parts)]


def _layer_spec(a, layer):
    return pl.BlockSpec((None,) + a.shape[1:], lambda *_: (layer, 0, 0), pipeline_mode=pl.Buffered(1))


_ANY = pl.BlockSpec(memory_space=pl.ANY)


def _ffn_inproj_kernel(x_ref, g_ref, w1_ref, w3_ref, w2_ref, wq_ref, wg_ref, wr_ref,
                       x1_ref, qkvo_ref, gate_ref, u_ref, bch_ref):
    parts = _row_parts(x_ref.shape[0], ROW_PARTS)
    x1 = _ffn_residual([x_ref[r, :] for r in parts], g_ref[0:1, :], g_ref[1:2, :], w1_ref, w3_ref, w2_ref)
    hn = [(_rms(v) * g_ref[2:3, :]).astype(BF16) for v in x1]
    qkvo = [_dot(v, wq_ref[...]) for v in hn]
    gate = [_dot(v, wg_ref[...]) for v in hn]
    rest = [_dot(v, wr_ref[...]) for v in hn]
    for i, r in enumerate(parts):
        x1_ref[r, :] = x1[i]
        qkvo_ref[r, :] = qkvo[i]
        gate_ref[r, :] = gate[i]
        u_ref[r, :] = rest[i][:, :D_S5]
        bch_ref[r, :] = rest[i][:, D_S5:]


def _ffn_inproj(x, g, w1, w3, w2, win, tm, layer):
    t = x.shape[0]
    row = lambda w: pl.BlockSpec((tm, w), lambda i: (i, 0))
    return pl.pallas_call(
        _ffn_inproj_kernel,
        grid=(t // tm,),
        in_specs=[row(D_MODEL), _layer_spec(g, layer), _layer_spec(w1, layer), _layer_spec(w3, layer),
                  _layer_spec(w2, layer)] + [_layer_spec(a, layer) for a in win],
        out_specs=[row(D_MODEL), row(4 * D_MLSTM), row(GATE_W), row(D_S5), row(3 * D_CONV)],
        out_shape=[jax.ShapeDtypeStruct((t, D_MODEL), F32), jax.ShapeDtypeStruct((t, 4 * D_MLSTM), F32),
                   jax.ShapeDtypeStruct((t, GATE_W), F32), jax.ShapeDtypeStruct((t, D_S5), F32),
                   jax.ShapeDtypeStruct((t, 3 * D_CONV), F32)],
        compiler_params=pltpu.CompilerParams(dimension_semantics=("arbitrary",), vmem_limit_bytes=VMEM_LIMIT),
        name="ffn_inproj",
    )(x, g, w1, w3, w2, *win)


def _outproj_ffn_kernel(x_ref, a_ref, b_ref, c_ref, g_ref, wo_ref, w1_ref, w3_ref, w2_ref, y_ref):
    parts = _row_parts(x_ref.shape[0], ROW_PARTS)
    mo = [(_dot(a_ref[r, :].astype(BF16), wo_ref[0:D_MLSTM, :])
           + _dot(b_ref[r, :].astype(BF16), wo_ref[D_MLSTM:D_MLSTM + D_S5, :])
           + _dot(c_ref[r, :].astype(BF16), wo_ref[D_MLSTM + D_S5:, :])) for r in parts]
    x2 = [x_ref[r, :] + _rms(v) * g_ref[3:4, :] for r, v in zip(parts, mo)]
    y = _ffn_residual(x2, g_ref[4:5, :], g_ref[5:6, :], w1_ref, w3_ref, w2_ref)
    for r, v in zip(parts, y):
        y_ref[r, :] = v


def _outproj_ffn(x, oa, ob, oc, g, wo, w1, w3, w2, tm, layer):
    t = x.shape[0]
    row = lambda w: pl.BlockSpec((tm, w), lambda i: (i, 0))
    return pl.pallas_call(
        _outproj_ffn_kernel,
        grid=(t // tm,),
        in_specs=[row(D_MODEL), row(D_MLSTM), row(D_S5), row(D_CONV), _layer_spec(g, layer),
                  _layer_spec(wo, layer), _layer_spec(w1, layer), _layer_spec(w3, layer),
                  _layer_spec(w2, layer)],
        out_specs=row(D_MODEL),
        out_shape=jax.ShapeDtypeStruct((t, D_MODEL), F32),
        compiler_params=pltpu.CompilerParams(dimension_semantics=("arbitrary",), vmem_limit_bytes=VMEM_LIMIT),
        name="outproj_ffn",
    )(x, oa, ob, oc, g, wo, w1, w3, w2)


def _state_shapes(depth, b):
    return [jax.ShapeDtypeStruct((depth, b, MH, DH, DH), F32), jax.ShapeDtypeStruct((depth, b, MH, DH), F32),
            jax.ShapeDtypeStruct((depth, b, MH), F32), jax.ShapeDtypeStruct((depth, b, CONV_W - 1, D_CONV), F32)]


def _cummax_rows(x, rowi):
    s = 1
    while s < x.shape[0]:
        x = jnp.maximum(x, jnp.where(rowi >= s, pltpu.roll(x, s, 0), NEG))
        s *= 2
    return x


def _mlstm_conv_kernel(*refs, layer, bb, chunk, zero_state):
    (q_ref, k_ref, v_ref, o_ref, gate_ref, bg_ref, cg_ref, hc_ref, gb_ref, gbt_ref, cw_ref, mg_ref) = refs[:12]
    refs = refs[12:]
    if not zero_state:
        c0_ref, n0_ref, m0_ref, cb0_ref = refs[:4]
        refs = refs[4:]
    if layer > 0:
        refs = refs[4:]
    oa_ref, oc_ref, c_ref, n_ref, m_ref, cb_ref, m_scr = refs
    other_layers = ()
    if layer == 0:
        other_layers = tuple(r.at[1:] for r in (c_ref, n_ref, m_ref, cb_ref) if r.shape[0] > 1)
        c_ref, n_ref, m_ref, cb_ref = c_ref.at[0], n_ref.at[0], m_ref.at[0], cb_ref.at[0]
    assert chunk == DH
    bi0 = pl.program_id(0) * bb
    ci = pl.program_id(1)

    @pl.when(ci == 0)
    def _():
        for r in other_layers:
            r[...] = jnp.zeros_like(r)
        m_scr[...] = jnp.zeros_like(m_scr)
        if zero_state:
            c_ref[...] = jnp.zeros_like(c_ref)
            n_ref[...] = jnp.zeros_like(n_ref)
            cb_ref[...] = jnp.zeros_like(cb_ref)
        else:
            c_ref[...] = c0_ref[...]
            n_ref[...] = n0_ref[...]
            cb_ref[...] = cb0_ref[...]
            for bi in range(bb):
                m_scr[bi:bi + 1, 0:MH] = m0_ref[pl.ds(bi0 + bi, 1), :]

    row = lax.broadcasted_iota(jnp.int32, (chunk, chunk), 0)
    col = lax.broadcasted_iota(jnp.int32, (chunk, chunk), 1)
    causal = col <= row
    tril = jnp.where(causal, 1.0, 0.0).astype(BF16)
    rowg = lax.broadcasted_iota(jnp.int32, (chunk, GATE_W), 0)
    rowi = lax.broadcasted_iota(jnp.int32, (chunk, D_CONV), 0)
    sel8 = (lax.broadcasted_iota(jnp.int32, (SUBLANES, GATE_W), 0)
            == lax.broadcasted_iota(jnp.int32, (SUBLANES, GATE_W), 1)).astype(F32)
    spread = jnp.where((lax.broadcasted_iota(jnp.int32, (2 * GATE_W, D_MLSTM), 0) & (GATE_W - 1))
                       == (lax.broadcasted_iota(jnp.int32, (2 * GATE_W, D_MLSTM), 1) >> 7), 1.0, 0.0).astype(BF16)
    scale = DH ** -0.5

    units = [(bi, h) for bi in range(bb) for h in range(MH)]
    hsl = lambda h: slice(h * DH, (h + 1) * DH)
    g_c = [gate_ref[bi] + gb_ref[...] for bi in range(bb)]
    csum = [_dot(tril, _split_bf16(pltpu.roll(_log_sigmoid(g), GATE_W - MH, 1), 3)) for g in g_c]
    qb = {u: q_ref[u[0], :, hsl(u[1])].astype(BF16) for u in units}
    kb = {u: (k_ref[u[0], :, hsl(u[1])] * scale).astype(BF16) for u in units}
    s_qk = {u: _dot_nt(qb[u], kb[u]) for u in units}
    inter = {u: _dot_nt(qb[u], jnp.concatenate(
        [c_ref[u[0], u[1]].astype(BF16),
         jnp.broadcast_to(n_ref[u[0], u[1]:u[1] + 1, :], (DH, DH)).astype(BF16)], axis=0)) for u in units}
    rep, a_r, decay = [], [], []
    for bi in range(bb):
        f_c = csum[bi][:, :GATE_W] + csum[bi][:, GATE_W:2 * GATE_W] + csum[bi][:, 2 * GATE_W:]
        a_c = g_c[bi] - f_c
        m_prev = m_scr[bi:bi + 1, :]
        big_m = jnp.maximum(_cummax_rows(a_c, rowg), m_prev)
        mt = f_c + big_m
        m_new = mt[chunk - 1:chunk, :]
        f_last = f_c[chunk - 1:chunk, :]
        decay.append(jnp.exp(f_last + m_prev - m_new))
        stacked = jnp.concatenate([_split_bf16(-big_m, 2), _split_bf16(jnp.exp(m_prev - big_m), 2),
                                   _split_bf16(-mt, 2), _split_bf16(jnp.exp(a_c + (f_last - m_new)), 2)],
                                  axis=0)
        rep.append(_dot(stacked, spread))
        a_r.append(_dot_nt_exact(sel8, a_c))
        m_scr[bi:bi + 1, :] = m_new
        m_ref[pl.ds(bi0 + bi, 1), :] = m_new[:, 0:MH]
    intra, rsum = {}, {}
    for u in units:
        bi, h = u
        w = jnp.exp(jnp.where(causal, rep[bi][0:chunk, hsl(h)] + a_r[bi][h:h + 1, :], NEG))
        sqk = s_qk[u] * w
        rsum[u] = jnp.sum(sqk, axis=-1, keepdims=True)
        intra[u] = _dot(sqk.astype(BF16), v_ref[bi, :, hsl(h)].astype(BF16))
    hm, ssq = {}, {}
    for u in units:
        bi, h = u
        s_inter = rep[bi][chunk:2 * chunk, hsl(h)]
        e_floor = jnp.exp(rep[bi][2 * chunk:3 * chunk, hsl(h)])
        num = intra[u] + s_inter * inter[u][:, :DH]
        den = rsum[u] + s_inter * inter[u][:, DH:]
        hm[u] = jax.nn.sigmoid(o_ref[bi, :, hsl(h)]) * (num / jnp.maximum(jnp.abs(den), e_floor))
        ssq[u] = jnp.sum(hm[u] * hm[u], axis=-1, keepdims=True)
    for u in units:
        bi, h = u
        oa_ref[bi, :, hsl(h)] = hm[u] * lax.rsqrt(ssq[u] * (1.0 / DH) + EPS) * mg_ref[:, hsl(h)]
        wk = rep[bi][3 * chunk:, hsl(h)]
        dec = decay[bi][:, h:h + 1]
        c_ref[bi, h] = dec * c_ref[bi, h] + _dot_tn((v_ref[bi, :, hsl(h)] * wk).astype(BF16), kb[u])
        n_ref[bi, h:h + 1, :] = (dec * n_ref[bi, h:h + 1, :]
                                 + jnp.sum(k_ref[bi, :, hsl(h)] * scale * wk, axis=0, keepdims=True))
    for bi in range(bb):
        z = cg_ref[bi] * hc_ref[bi]
        cb = cb_ref[bi]
        z1 = jnp.where(rowi == 0, cb[1:2, :], pltpu.roll(z, 1, 0))
        z2 = jnp.where(rowi == 0, cb[0:1, :], jnp.where(rowi == 1, cb[1:2, :], pltpu.roll(z, 2, 0)))
        yc = cw_ref[0:1, :] * z2 + cw_ref[1:2, :] * z1 + cw_ref[2:3, :] * z
        oc_ref[bi] = _rms(bg_ref[bi] * yc) * mg_ref[:, D_MLSTM + D_S5:]
        cb_ref[bi] = z[chunk - 2:chunk, :]


def _mlstm_conv(qkvo, gate, bch, w, states_in, states_prev, *, layer, depth, bb, chunk):
    b, s, _ = qkvo.shape
    assert s % chunk == 0 and b % bb == 0
    zero_state = states_in is None
    kern = functools.partial(_mlstm_conv_kernel, layer=layer, bb=bb, chunk=chunk, zero_state=zero_state)
    col = lambda wd, j: pl.BlockSpec((bb, chunk, wd), lambda i, c: (i, c, j))
    st = lambda shape: pl.BlockSpec((None, bb) + shape, lambda i, c: (layer, i) + (0,) * len(shape))
    m_spec = pl.BlockSpec((None, b, MH), lambda i, c: (layer, 0, 0))
    cst = lambda a: pl.BlockSpec(a.shape, lambda i, c: (0,) * a.ndim)
    state_specs = [st((MH, DH, DH)), st((MH, DH)), m_spec, st((CONV_W - 1, D_CONV))]
    out_state_specs = state_specs
    if layer == 0:
        st0 = lambda shape: pl.BlockSpec((depth, bb) + shape, lambda i, c: (0, i) + (0,) * len(shape))
        out_state_specs = [st0((MH, DH, DH)), st0((MH, DH)), pl.BlockSpec((depth, b, MH), lambda i, c: (0, 0, 0)),
                           st0((CONV_W - 1, D_CONV))]
    consts = (w["gb"], w["gbt"], w["cw"], w["mg"])
    args = [qkvo, qkvo, qkvo, qkvo, gate, bch, bch, bch, *consts]
    in_specs = [col(D_MLSTM, 0), col(D_MLSTM, 1), col(D_MLSTM, 2), col(D_MLSTM, 3), col(GATE_W, 0),
                col(D_CONV, 0), col(D_CONV, 1), col(D_CONV, 2)] + [cst(a) for a in consts]
    if not zero_state:
        args += list(states_in)
        in_specs += state_specs
    aliases = {}
    if layer > 0:
        aliases = {len(args) + j: 2 + j for j in range(4)}
        args += list(states_prev)
        in_specs += [_ANY] * 4
    return pl.pallas_call(
        kern,
        grid=(b // bb, s // chunk),
        in_specs=in_specs,
        out_specs=[col(D_MLSTM, 0), col(D_CONV, 0)] + out_state_specs,
        out_shape=[jax.ShapeDtypeStruct((b, s, D_MLSTM), F32), jax.ShapeDtypeStruct((b, s, D_CONV), F32)]
        + _state_shapes(depth, b),
        scratch_shapes=[pltpu.VMEM((-(-bb // SUBLANES) * SUBLANES, GATE_W), F32)],
        input_output_aliases=aliases,
        compiler_params=pltpu.CompilerParams(dimension_semantics=("arbitrary", "arbitrary"),
                                             vmem_limit_bytes=VMEM_LIMIT),
        name="mlstm_conv",
    )(*args)


def _gate_terms(graw, gb_ref, gbt_ref, tril, triu):
    sel = (lax.broadcasted_iota(jnp.int32, (SUBLANES, GATE_W), 0)
           == lax.broadcasted_iota(jnp.int32, (SUBLANES, GATE_W), 1)).astype(F32)
    g_c = graw + gb_ref[...]
    lf_c = _log_sigmoid(g_c)
    g_r = _dot_nt_exact(sel, graw) + gbt_ref[...]
    lf_r = _log_sigmoid(g_r)
    return g_c, lf_c, _dot_exact(tril, lf_c), g_r, _dot_exact(lf_r, triu)


def _head_scores(h, g_c, bt_c, g_r, bt_r, m_prev, mask):
    it_row = g_r[h:h + 1, :]
    bt_row = bt_r[MH + h:MH + h + 1, :]
    it_col = g_c[:, h:h + 1]
    bt_col = bt_c[:, MH + h:MH + h + 1]
    dmat = jnp.where(mask, bt_col - bt_row + it_row, NEG)
    inter = bt_col + m_prev
    mt = jnp.maximum(jnp.max(dmat, axis=-1, keepdims=True), inter)
    return jnp.exp(dmat - mt), jnp.exp(inter - mt), mt, it_col, bt_col


def _head_output(sqk, vb, s_inter, qc, qn, mt, o, mg):
    num = _dot(sqk.astype(BF16), vb) + s_inter * qc
    den = jnp.sum(sqk, axis=-1, keepdims=True) + s_inter * qn
    hm = jax.nn.sigmoid(o) * (num / jnp.maximum(jnp.abs(den), jnp.exp(-mt)))
    return _rms(hm) * mg


def _bcast_block_last(x, t, s):
    out = x
    for d in range(1, s):
        out = jnp.where(t == s - 1 - d, pltpu.roll(x, x.shape[0] - d, 0), out)
    return out


def _mlstm_conv_short_kernel(*refs, layer, nb, s):
    (q_ref, k_ref, v_ref, o_ref, gate_ref, bg_ref, cg_ref, hc_ref, gb_ref, gbt_ref, cw_ref, mg_ref,
     c0_ref, n0_ref, m0_ref, cb0_ref) = refs[:16]
    refs = refs[16:]
    if layer > 0:
        refs = refs[4:]
    oa_ref, oc_ref, c_ref, n_ref, m_ref, cb_ref, qc_scr, st_scr = refs
    if layer == 0:
        for ref in (c_ref, n_ref, m_ref, cb_ref):
            if ref.shape[0] > 1:
                ref[1:] = jnp.zeros((ref.shape[0] - 1,) + ref.shape[1:], F32)
        c_ref, n_ref, m_ref, cb_ref = c_ref.at[0], n_ref.at[0], m_ref.at[0], cb_ref.at[0]
    r = nb * s
    per_tile = SUBLANES // s
    shift = s.bit_length() - 1
    row = lax.broadcasted_iota(jnp.int32, (r, r), 0)
    col = lax.broadcasted_iota(jnp.int32, (r, r), 1)
    same = (row >> shift) == (col >> shift)
    mask = same & (col <= row)
    tril = mask.astype(F32)
    triu = (same & (row <= col)).astype(F32)
    expand = ((lax.broadcasted_iota(jnp.int32, (r, nb), 0) >> shift)
              == lax.broadcasted_iota(jnp.int32, (r, nb), 1)).astype(F32)
    gather = (lax.broadcasted_iota(jnp.int32, (nb, r), 0)
              == (lax.broadcasted_iota(jnp.int32, (nb, r), 1) >> shift)).astype(F32)
    t128 = lax.broadcasted_iota(jnp.int32, (r, DH), 0) & (s - 1)
    sub8 = lax.broadcasted_iota(jnp.int32, (SUBLANES, DH), 0) >> shift
    scale = DH ** -0.5

    g_c, lf_c, bt_c, g_r, bt_r = _gate_terms(gate_ref[...], gb_ref, gbt_ref, tril, triu)
    bt_last_c = _dot_exact(same.astype(F32), lf_c)
    m_rows = _dot_exact(expand, m0_ref[...])
    for h in range(MH):
        hs = slice(h * DH, (h + 1) * DH)
        m_prev = m_rows[:, h:h + 1]
        w, s_inter, mt, it_col, bt_col = _head_scores(h, g_c, bt_c, g_r, bt_r, m_prev, mask)
        qh = q_ref[:, hs]
        kh = k_ref[:, hs] * scale
        vh = v_ref[:, hs]
        qb = qh.astype(BF16)
        kb = kh.astype(BF16)
        sqk = _dot_nt(qb, kb) * w
        for j in range(r // SUBLANES):
            rows = slice(j * SUBLANES, (j + 1) * SUBLANES)
            acc = None
            q8 = q_ref[rows, hs].astype(BF16)
            for p in range(per_tile):
                part = _dot_nt(q8, c0_ref[j * per_tile + p, h].astype(BF16))
                acc = part if acc is None else jnp.where(sub8 == p, part, acc)
            qc_scr[rows, :] = acc
        n_rows = _dot_exact(expand, n0_ref[:, h, :])
        qn = jnp.sum(qh * n_rows, axis=-1, keepdims=True)
        oa_ref[:, hs] = _head_output(sqk, vh.astype(BF16), s_inter, qc_scr[...], qn, mt,
                                     o_ref[:, hs], mg_ref[:, hs])
        m_new = _bcast_block_last(jnp.broadcast_to(mt, (r, DH)), t128, s)
        bt_last = bt_last_c[:, MH + h:MH + h + 1]
        wk = jnp.exp(bt_last - bt_col + it_col - m_new)
        decay = jnp.exp(bt_last + m_prev - m_new)
        vw = vh * wk
        for j in range(r // SUBLANES):
            rows = slice(j * SUBLANES, (j + 1) * SUBLANES)
            k8 = (k_ref[rows, hs] * scale).astype(BF16)
            for p in range(per_tile):
                bidx = j * per_tile + p
                last = bidx * s + s - 1
                upd = _dot_tn(jnp.where(sub8 == p, vw[rows], 0.0).astype(BF16), k8)
                c_ref[bidx, h] = decay[last:last + 1, 0:1] * c0_ref[bidx, h] + upd
        st_scr[0] = decay
        st_scr[1] = m_new
        last_rows = pl.ds(s - 1, nb, stride=s)
        n_ref[:, h, :] = st_scr[0, last_rows, :] * n0_ref[:, h, :] + _dot_exact(gather, kh * wk)
        m_ref[:, h:h + 1] = st_scr[1, last_rows, :][:, 0:1]
    t256 = lax.broadcasted_iota(jnp.int32, (r, D_CONV), 0) & (s - 1)
    z = cg_ref[...] * hc_ref[...]
    cb_a = _dot_exact(expand, cb0_ref[:, 0, :])
    cb_b = _dot_exact(expand, cb0_ref[:, 1, :])
    z1 = jnp.where(t256 == 0, cb_b, pltpu.roll(z, 1, 0))
    z2 = jnp.where(t256 == 0, cb_a, jnp.where(t256 == 1, cb_b, pltpu.roll(z, 2, 0)))
    yc = cw_ref[0:1, :] * z2 + cw_ref[1:2, :] * z1 + cw_ref[2:3, :] * z
    oc_ref[...] = _rms(bg_ref[...] * yc) * mg_ref[:, D_MLSTM + D_S5:]
    for half in range(D_CONV // DH):
        lanes = slice(half * DH, (half + 1) * DH)
        st_scr[half] = z[:, lanes]
        cb_ref[:, 0, lanes] = st_scr[half, pl.ds(s - 2, nb, stride=s), :]
        cb_ref[:, 1, lanes] = st_scr[half, pl.ds(s - 1, nb, stride=s), :]


def _mlstm_conv_short(qkvo, gate, bch, w, states_in, states_prev, *, layer, depth, b, s, nb):
    assert SUBLANES % s == 0 and s >= CONV_W - 1 and b % nb == 0 and (nb * s) % SUBLANES == 0
    r = nb * s
    kern = functools.partial(_mlstm_conv_short_kernel, layer=layer, nb=nb, s=s)
    col = lambda wd, j: pl.BlockSpec((r, wd), lambda i: (i, j))
    st = lambda shape: pl.BlockSpec((None, nb) + shape, lambda i: (layer, i) + (0,) * len(shape))
    cst = lambda a: pl.BlockSpec(a.shape, lambda i: (0,) * a.ndim)
    state_specs = [st((MH, DH, DH)), st((MH, DH)), st((MH,)), st((CONV_W - 1, D_CONV))]
    out_state_specs = state_specs
    if layer == 0:
        st0 = lambda shape: pl.BlockSpec((depth, nb) + shape, lambda i: (0, i) + (0,) * len(shape))
        out_state_specs = [st0((MH, DH, DH)), st0((MH, DH)), st0((MH,)), st0((CONV_W - 1, D_CONV))]
    consts = (w["gb"], w["gbt"], w["cw"], w["mg"])
    args = [qkvo, qkvo, qkvo, qkvo, gate, bch, bch, bch, *consts, *states_in]
    in_specs = [col(D_MLSTM, 0), col(D_MLSTM, 1), col(D_MLSTM, 2), col(D_MLSTM, 3), col(GATE_W, 0),
                col(D_CONV, 0), col(D_CONV, 1), col(D_CONV, 2)] + [cst(a) for a in consts] + state_specs
    aliases = {}
    if layer > 0:
        aliases = {len(args) + j: 2 + j for j in range(4)}
        args += list(states_prev)
        in_specs += [_ANY] * 4
    return pl.pallas_call(
        kern,
        grid=(b // nb,),
        in_specs=in_specs,
        out_specs=[col(D_MLSTM, 0), col(D_CONV, 0)] + out_state_specs,
        out_shape=[jax.ShapeDtypeStruct((b * s, D_MLSTM), F32), jax.ShapeDtypeStruct((b * s, D_CONV), F32)]
        + _state_shapes(depth, b),
        scratch_shapes=[pltpu.VMEM((r, DH), F32), pltpu.VMEM((2, r, DH), F32)],
        input_output_aliases=aliases,
        compiler_params=pltpu.CompilerParams(dimension_semantics=("arbitrary",), vmem_limit_bytes=VMEM_LIMIT),
        name="mlstm_conv_short",
    )(*args)


def _s5_kernel(*refs, layer, nb, tt, zero_state, parts, batch_major):
    u_ref, lam_ref, bblk_ref, cblk_ref, d_ref, wglu_ref, mg_ref = refs[:7]
    refs = refs[7:]
    if not zero_state:
        hr0_ref, hi0_ref = refs[:2]
        refs = refs[2:]
    if layer > 0:
        refs = refs[2:]
    ob_ref, hr_ref, hi_ref, xs_ref, a_ref, bf_ref = refs[:6]
    if batch_major:
        ut_scr, ot_scr = refs[6:]
        ut_scr[...] = jnp.swapaxes(u_ref[...], 0, 1).reshape(tt * nb, D_S5)
        u_bm_ref, ob_bm_ref, u_ref, ob_ref = u_ref, ob_ref, ut_scr, ot_scr
    if layer == 0:
        hr_all, hi_all = hr_ref, hi_ref
        hr_ref, hi_ref = hr_all.at[0], hi_all.at[0]
    ti = pl.program_id(0)
    n = S5_N
    lc = 512

    @pl.when(ti == 0)
    def _():
        if layer == 0:
            hr_all[...] = jnp.zeros_like(hr_all)
            hi_all[...] = jnp.zeros_like(hi_all)
        hr_ref[...] = jnp.zeros_like(hr_ref) if zero_state else hr0_ref[...]
        hi_ref[...] = jnp.zeros_like(hi_ref) if zero_state else hi0_ref[...]
        lre = jnp.minimum(lam_ref[0:1, :], -1e-4)
        lim = lam_ref[1:2, :]
        dt = jnp.exp(lam_ref[2:3, :])
        mag = jnp.exp(lre * dt)
        ab_re = mag * jnp.cos(lim * dt)
        ab_im = mag * jnp.sin(lim * dt)
        den = lre * lre + lim * lim
        nr = ab_re - 1.0
        fre = (nr * lre + ab_im * lim) / den
        fim = (ab_im * lre - nr * lim) / den
        a_ref[:, :n] = jnp.broadcast_to(ab_re, (SUBLANES, n))
        a_ref[:, n:] = jnp.broadcast_to(ab_im, (SUBLANES, n))
        b_re = bblk_ref[:, :n]
        b_im = bblk_ref[:, n:]
        bf_ref[:, :n] = (fre * b_re - fim * b_im).astype(BF16)
        bf_ref[:, n:] = (fre * b_im + fim * b_re).astype(BF16)

    rows_p = nb * tt // parts
    part_rows = [slice(p * rows_p, (p + 1) * rows_p) for p in range(parts)]
    for rs in part_rows:
        xs_ref[rs, :] = _dot(u_ref[rs, :].astype(BF16), bf_ref[...])
    ar = [a_ref[:, c0:c0 + lc] for c0 in range(0, n, lc)]
    ai = [a_ref[:, n + c0:n + c0 + lc] for c0 in range(0, n, lc)]
    h = {}
    for p, rs in enumerate(part_rows):
        for rg in range(nb // SUBLANES):
            g8 = slice(rg * SUBLANES, (rg + 1) * SUBLANES)
            for ci, c0 in enumerate(range(0, n, lc)):
                if p == 0:
                    h[rg, ci] = (hr_ref[g8, c0:c0 + lc], hi_ref[g8, c0:c0 + lc])
            for t in range(p * tt // parts, (p + 1) * tt // parts):
                r8 = slice(t * nb + rg * SUBLANES, t * nb + (rg + 1) * SUBLANES)
                for ci, c0 in enumerate(range(0, n, lc)):
                    hr, hi = h[rg, ci]
                    nr_ = ar[ci] * hr - ai[ci] * hi + xs_ref[r8, c0:c0 + lc]
                    ni_ = ar[ci] * hi + ai[ci] * hr + xs_ref[r8, n + c0:n + c0 + lc]
                    xs_ref[r8, c0:c0 + lc] = nr_
                    xs_ref[r8, n + c0:n + c0 + lc] = ni_
                    h[rg, ci] = (nr_, ni_)
            for ci, c0 in enumerate(range(0, n, lc)):
                if p == parts - 1:
                    hr_ref[g8, c0:c0 + lc] = h[rg, ci][0]
                    hi_ref[g8, c0:c0 + lc] = h[rg, ci][1]
    y = [_dot(xs_ref[rs, :].astype(BF16), cblk_ref[...]) + d_ref[...] * u_ref[rs, :] for rs in part_rows]
    g = [jax.nn.gelu(v) for v in y]
    gate = [_dot(v.astype(BF16), wglu_ref[...]) for v in g]
    for rs, v, z in zip(part_rows, g, gate):
        ob_ref[rs, :] = _rms(v * jax.nn.sigmoid(z)) * mg_ref[:, D_MLSTM:D_MLSTM + D_S5]
    if batch_major:
        ob_bm_ref[...] = jnp.swapaxes(ot_scr[...].reshape(tt, nb, D_S5), 0, 1)


def _s5(u, w, states_in, states_prev, *, layer, depth, nb, tt):
    batch_major = u.ndim == 3
    rows = u.shape[0] * u.shape[1] if batch_major else u.shape[0]
    rt = nb * tt
    zero_state = states_in is None
    parts = tt // 16 if tt % 16 == 0 else 1
    kern = functools.partial(_s5_kernel, layer=layer, nb=nb, tt=tt, zero_state=zero_state, parts=parts,
                             batch_major=batch_major)
    if batch_major:
        assert u.shape[0] == nb and tt % SUBLANES == 0
        io_spec = pl.BlockSpec((nb, tt, D_S5), lambda i: (0, i, 0))
        io_shape = jax.ShapeDtypeStruct(u.shape, F32)
        io_scratch = [pltpu.VMEM((rt, D_S5), F32), pltpu.VMEM((rt, D_S5), F32)]
    else:
        io_spec = pl.BlockSpec((rt, D_S5), lambda i: (i, 0))
        io_shape = jax.ShapeDtypeStruct((rows, D_S5), F32)
        io_scratch = []
    cst = lambda a: pl.BlockSpec(a.shape, lambda i: (0,) * a.ndim)
    st = pl.BlockSpec((None, nb, S5_N), lambda i: (layer, 0, 0))
    st_out = pl.BlockSpec((depth, nb, S5_N), lambda i: (0, 0, 0)) if layer == 0 else st
    consts = (w["lam"], w["bblk"], w["cblk"], w["d"], w["wglu"], w["mg"])
    args = [u, *consts]
    in_specs = [io_spec] + [cst(a) for a in consts]
    if not zero_state:
        args += list(states_in)
        in_specs += [st, st]
    aliases = {}
    if layer > 0:
        aliases = {len(args) + j: 1 + j for j in range(2)}
        args += list(states_prev)
        in_specs += [_ANY] * 2
    return pl.pallas_call(
        kern,
        grid=(rows // rt,),
        in_specs=in_specs,
        out_specs=[io_spec, st_out, st_out],
        out_shape=[io_shape, jax.ShapeDtypeStruct((depth, nb, S5_N), F32),
                   jax.ShapeDtypeStruct((depth, nb, S5_N), F32)],
        scratch_shapes=[pltpu.VMEM((rt, 2 * S5_N), F32), pltpu.VMEM((SUBLANES, 2 * S5_N), F32),
                        pltpu.VMEM((D_S5, 2 * S5_N), BF16)] + io_scratch,
        input_output_aliases=aliases,
        compiler_params=pltpu.CompilerParams(dimension_semantics=("arbitrary",), vmem_limit_bytes=VMEM_LIMIT),
        name="s5",
    )(*args)


def _block_diag(w):
    g, r, c = w.shape
    eye = jnp.eye(g, dtype=w.dtype)
    return jnp.einsum("grc,gh->grhc", w, eye).reshape(g * r, g * c)


def _layer_weights(l, ig_bias, fg_bias, s5_a_re, s5_a_im, s5_log_dt, s5_b_re, s5_b_im, s5_c_re, s5_c_im, s5_d,
                   w_glu, conv_w, mix_g):
    gb = jnp.concatenate([ig_bias[l], fg_bias[l], jnp.zeros((GATE_W - 2 * MH,), F32)])
    lam = jnp.stack([s5_a_re[l].reshape(-1), s5_a_im[l].reshape(-1),
                     jnp.repeat(s5_log_dt[l], S5_P)])
    bblk = jnp.concatenate([_block_diag(jnp.swapaxes(s5_b_re[l], 1, 2)),
                            _block_diag(jnp.swapaxes(s5_b_im[l], 1, 2))], axis=1)
    cblk = jnp.concatenate([_block_diag(jnp.swapaxes(s5_c_re[l], 1, 2)),
                            -_block_diag(jnp.swapaxes(s5_c_im[l], 1, 2))], axis=0).astype(BF16)
    return dict(
        gb=gb.reshape(1, GATE_W), gbt=gb[:SUBLANES].reshape(SUBLANES, 1),
        lam=lam, bblk=bblk, cblk=cblk, d=s5_d[l].reshape(1, D_S5), wglu=w_glu[l].astype(BF16),
        cw=conv_w[l], mg=mix_g[l].reshape(1, D_MODEL))


def _stacked_weights(ffn1_w1, ffn1_w3, ffn1_w2, ffn2_w1, ffn2_w3, ffn2_w2, norm_g, w_in, w_out):
    o = 4 * D_MLSTM
    win = (w_in[:, :, :o].astype(BF16),
           jnp.pad(w_in[:, :, o:o + 2 * MH].astype(BF16), ((0, 0), (0, 0), (0, GATE_W - 2 * MH))),
           w_in[:, :, o + 2 * MH:].astype(BF16))
    return dict(f1=(ffn1_w1.astype(BF16), ffn1_w3.astype(BF16), ffn1_w2.astype(BF16)),
                f2=(ffn2_w1.astype(BF16), ffn2_w3.astype(BF16), ffn2_w2.astype(BF16)),
                g=norm_g, win=win, wo=w_out.astype(BF16))


def _run_group(x, states, weights, big, *, tm, tt, chunk=None, bb=None, nb=None):
    b, s, _ = x.shape
    depth = len(weights)
    xf = x.reshape(b * s, D_MODEL)
    mstates = sstates = None
    if states is not None:
        c_all, n_all, m_all, sr_all, si_all, cb_all = states
        ms_in = (c_all, n_all, m_all, cb_all)
        ss_in = (sr_all.reshape(depth, b, S5_N), si_all.reshape(depth, b, S5_N))
    else:
        ms_in = ss_in = None
    for l, w in enumerate(weights):
        x1, qkvo, gate, u, bch = _ffn_inproj(xf, big["g"], *big["f1"], big["win"], tm, l)
        if tt % SUBLANES == 0:
            ob, *sstates = _s5(u.reshape(b, s, D_S5), w, ss_in, sstates, layer=l, depth=depth, nb=b, tt=tt)
            ob = ob.reshape(b * s, D_S5)
        else:
            u_tm = jnp.swapaxes(u.reshape(b, s, D_S5), 0, 1).reshape(s * b, D_S5)
            ob_tm, *sstates = _s5(u_tm, w, ss_in, sstates, layer=l, depth=depth, nb=b, tt=tt)
            ob = jnp.swapaxes(ob_tm.reshape(s, b, D_S5), 0, 1).reshape(b * s, D_S5)
        if chunk is not None:
            r3 = lambda a: a.reshape(b, s, a.shape[-1])
            oa, oc, *mstates = _mlstm_conv(r3(qkvo), r3(gate), r3(bch), w, ms_in, mstates,
                                           layer=l, depth=depth, bb=bb, chunk=chunk)
            xf = _outproj_ffn(x1, oa.reshape(b * s, D_MLSTM), ob, oc.reshape(b * s, D_CONV),
                              big["g"], big["wo"], *big["f2"], tm, l)
        else:
            oa, oc, *mstates = _mlstm_conv_short(qkvo, gate, bch, w, ms_in, mstates,
                                                 layer=l, depth=depth, b=b, s=s, nb=nb)
            xf = _outproj_ffn(x1, oa, ob, oc, big["g"], big["wo"], *big["f2"], tm, l)
    c1, n1, m1, cb1 = mstates
    sr1, si1 = sstates
    return xf.reshape(b, s, D_MODEL), (c1, n1, m1, sr1.reshape(depth, b, S5_G, S5_P),
                                       si1.reshape(depth, b, S5_G, S5_P), cb1)


def kernel(x_prompt, x_sample, state_mlstm_C, state_mlstm_n, state_mlstm_m, state_s5_re, state_s5_im,
           state_conv, ffn1_w1, ffn1_w3, ffn1_w2, ffn2_w1, ffn2_w3, ffn2_w2, norm_g, w_in, ig_bias, fg_bias,
           s5_a_re, s5_a_im, s5_log_dt, s5_b_re, s5_b_im, s5_c_re, s5_c_im, s5_d, w_glu, conv_w, mix_g, w_out):
    depth = norm_g.shape[0]
    params = (ig_bias, fg_bias, s5_a_re, s5_a_im, s5_log_dt, s5_b_re, s5_b_im, s5_c_re, s5_c_im, s5_d, w_glu,
              conv_w, mix_g)
    weights = [_layer_weights(l, *params) for l in range(depth)]
    big = _stacked_weights(ffn1_w1, ffn1_w3, ffn1_w2, ffn2_w1, ffn2_w3, ffn2_w2, norm_g, w_in, w_out)
    y_p, st_p = _run_group(x_prompt, None, weights, big, tm=512, tt=128, chunk=128, bb=4)
    y_s, st_s = _run_group(
        x_sample, (state_mlstm_C, state_mlstm_n, state_mlstm_m, state_s5_re, state_s5_im, state_conv),
        weights, big, tm=512, tt=x_sample.shape[1], nb=16)
    return (y_p, y_s, *st_p, *st_s)
```

```python
import functools

import jax
import jax.numpy as jnp
from jax import lax
from jax.experimental import pallas as pl
from jax.experimental.pallas import tpu as pltpu

F32 = jnp.float32
BF16 = jnp.bfloat16

D_MODEL = 1024
MH = 4
DH = 128
D_MLSTM = MH * DH
S5_CH = 16
S5_G = 16
S5_P = 64
D_S5 = S5_G * S5_CH
S5_N = S5_G * S5_P
D_CONV = 256
CONV_W = 3
EPS = 1e-6
GATE_W = 128
SUBLANES = 8
NEG = -1e30
VMEM_LIMIT = 56 * 1024 * 1024
STATE_VMEM_BUDGET = 36 * 1024 * 1024
ROW_PARTS = 2


def _dot(a, b):
    return jnp.dot(a, b, preferred_element_type=F32)


def _dot_nt(a, b):
    return lax.dot_general(a, b, (((1,), (1,)), ((), ())), preferred_element_type=F32)


def _dot_tn(a, b):
    return lax.dot_general(a, b, (((0,), (0,)), ((), ())), preferred_element_type=F32)


def _dot_exact(a, b):
    return jnp.dot(a, b, preferred_element_type=F32, precision=lax.Precision.HIGHEST)


def _dot_nt_exact(a, b):
    return lax.dot_general(a, b, (((1,), (1,)), ((), ())), preferred_element_type=F32,
                           precision=lax.Precision.HIGHEST)


def _rms(x):
    return x * lax.rsqrt(jnp.mean(x * x, axis=-1, keepdims=True) + EPS)


def _log_sigmoid(x):
    return jnp.minimum(x, 0.0) - jnp.log(1.0 + jnp.exp(-jnp.abs(x)))


def _split_bf16(x, parts):
    out = []
    for _ in range(parts):
        p = x.astype(BF16)
        out.append(p)
        x = x - p.astype(F32)
    return jnp.concatenate(out, axis=1)


def _ffn_residual(xs, g_pre, g_post, w1, w3, w2):
    xn = [(_rms(x) * g_pre).astype(BF16) for x in xs]
    h1 = [_dot(v, w1[...]) for v in xn]
    h3 = [_dot(v, w3[...]) for v in xn]
    a = [(jax.nn.silu(p) * q).astype(BF16) for p, q in zip(h1, h3)]
    y = [_dot(v, w2[...]) for v in a]
    return [x + 0.5 * (_rms(t) * g_post) for x, t in zip(xs, y)]


def _row_parts(tm, parts):
    step = tm // parts
    return [slice(i * step, (i + 1) * step) for i in range(parts)]


def _layer_spec(a, layer):
    return pl.BlockSpec((None,) + a.shape[1:], lambda *_: (layer, 0, 0), pipeline_mode=pl.Buffered(1))


_ANY = pl.BlockSpec(memory_space=pl.ANY)


def _ffn_inproj_kernel(x_ref, g_ref, w1_ref, w3_ref, w2_ref, wq_ref, wg_ref, wr_ref,
                       x1_ref, qkvo_ref, gate_ref, u_ref, bch_ref):
    parts = _row_parts(x_ref.shape[0], ROW_PARTS)
    x1 = _ffn_residual([x_ref[r, :] for r in parts], g_ref[0:1, :], g_ref[1:2, :], w1_ref, w3_ref, w2_ref)
    hn = [(_rms(v) * g_ref[2:3, :]).astype(BF16) for v in x1]
    qkvo = [_dot(v, wq_ref[...]) for v in hn]
    gate = [_dot(v, wg_ref[...]) for v in hn]
    rest = [_dot(v, wr_ref[...]) for v in hn]
    for i, r in enumerate(parts):
        x1_ref[r, :] = x1[i]
        qkvo_ref[r, :] = qkvo[i]
        gate_ref[r, :] = gate[i]
        u_ref[r, :] = rest[i][:, :D_S5]
        bch_ref[r, :] = rest[i][:, D_S5:]


def _ffn_inproj(x, g, w1, w3, w2, win, tm, layer):
    t = x.shape[0]
    row = lambda w: pl.BlockSpec((tm, w), lambda i: (i, 0))
    return pl.pallas_call(
        _ffn_inproj_kernel,
        grid=(t // tm,),
        in_specs=[row(D_MODEL), _layer_spec(g, layer), _layer_spec(w1, layer), _layer_spec(w3, layer),
                  _layer_spec(w2, layer)] + [_layer_spec(a, layer) for a in win],
        out_specs=[row(D_MODEL), row(4 * D_MLSTM), row(GATE_W), row(D_S5), row(3 * D_CONV)],
        out_shape=[jax.ShapeDtypeStruct((t, D_MODEL), F32), jax.ShapeDtypeStruct((t, 4 * D_MLSTM), F32),
                   jax.ShapeDtypeStruct((t, GATE_W), F32), jax.ShapeDtypeStruct((t, D_S5), F32),
                   jax.ShapeDtypeStruct((t, 3 * D_CONV), F32)],
        compiler_params=pltpu.CompilerParams(dimension_semantics=("arbitrary",), vmem_limit_bytes=VMEM_LIMIT),
        name="ffn_inproj",
    )(x, g, w1, w3, w2, *win)


def _outproj_ffn_kernel(x_ref, a_ref, b_ref, c_ref, g_ref, wo_ref, w1_ref, w3_ref, w2_ref, y_ref):
    parts = _row_parts(x_ref.shape[0], ROW_PARTS)
    mo = [(_dot(a_ref[r, :].astype(BF16), wo_ref[0:D_MLSTM, :])
           + _dot(b_ref[r, :].astype(BF16), wo_ref[D_MLSTM:D_MLSTM + D_S5, :])
           + _dot(c_ref[r, :].astype(BF16), wo_ref[D_MLSTM + D_S5:, :])) for r in parts]
    x2 = [x_ref[r, :] + _rms(v) * g_ref[3:4, :] for r, v in zip(parts, mo)]
    y = _ffn_residual(x2, g_ref[4:5, :], g_ref[5:6, :], w1_ref, w3_ref, w2_ref)
    for r, v in zip(parts, y):
        y_ref[r, :] = v


def _outproj_ffn(x, oa, ob, oc, g, wo, w1, w3, w2, tm, layer):
    t = x.shape[0]
    row = lambda w: pl.BlockSpec((tm, w), lambda i: (i, 0))
    return pl.pallas_call(
        _outproj_ffn_kernel,
        grid=(t // tm,),
        in_specs=[row(D_MODEL), row(D_MLSTM), row(D_S5), row(D_CONV), _layer_spec(g, layer),
                  _layer_spec(wo, layer), _layer_spec(w1, layer), _layer_spec(w3, layer),
                  _layer_spec(w2, layer)],
        out_specs=row(D_MODEL),
        out_shape=jax.ShapeDtypeStruct((t, D_MODEL), F32),
        compiler_params=pltpu.CompilerParams(dimension_semantics=("arbitrary",), vmem_limit_bytes=VMEM_LIMIT),
        name="outproj_ffn",
    )(x, oa, ob, oc, g, wo, w1, w3, w2)


def _state_shapes(depth, b):
    return [jax.ShapeDtypeStruct((depth, b, MH, DH, DH), F32), jax.ShapeDtypeStruct((depth, b, MH, DH), F32),
            jax.ShapeDtypeStruct((depth, b, MH), F32), jax.ShapeDtypeStruct((depth, b, CONV_W - 1, D_CONV), F32)]


def _cummax_rows(x, rowi):
    s = 1
    while s < x.shape[0]:
        x = jnp.maximum(x, jnp.where(rowi >= s, pltpu.roll(x, s, 0), NEG))
        s *= 2
    return x


def _mlstm_conv_kernel(*refs, layer, bb, chunk, zero_state):
    (q_ref, k_ref, v_ref, o_ref, gate_ref, bg_ref, cg_ref, hc_ref, gb_ref, cw_ref, mg_ref) = refs[:11]
    refs = refs[11:]
    if not zero_state:
        c0_ref, n0_ref, m0_ref, cb0_ref = refs[:4]
        refs = refs[4:]
    if layer > 0:
        refs = refs[4:]
    oa_ref, oc_ref, c_ref, n_ref, m_ref, cb_ref, m_scr = refs
    other_layers = ()
    if layer == 0:
        other_layers = tuple(r.at[1:] for r in (c_ref, n_ref, m_ref, cb_ref) if r.shape[0] > 1)
        c_ref, n_ref, m_ref, cb_ref = c_ref.at[0], n_ref.at[0], m_ref.at[0], cb_ref.at[0]
    assert chunk == DH
    bi0 = pl.program_id(0) * bb
    ci = pl.program_id(1)

    @pl.when(ci == 0)
    def _():
        for r in other_layers:
            r[...] = jnp.zeros_like(r)
        m_scr[...] = jnp.zeros_like(m_scr)
        if zero_state:
            c_ref[...] = jnp.zeros_like(c_ref)
            n_ref[...] = jnp.zeros_like(n_ref)
            cb_ref[...] = jnp.zeros_like(cb_ref)
        else:
            c_ref[...] = c0_ref[...]
            n_ref[...] = n0_ref[...]
            cb_ref[...] = cb0_ref[...]
            for bi in range(bb):
                m_scr[bi:bi + 1, 0:MH] = m0_ref[pl.ds(bi0 + bi, 1), :]

    row = lax.broadcasted_iota(jnp.int32, (chunk, chunk), 0)
    col = lax.broadcasted_iota(jnp.int32, (chunk, chunk), 1)
    causal = col <= row
    tril = jnp.where(causal, 1.0, 0.0).astype(BF16)
    rowg = lax.broadcasted_iota(jnp.int32, (chunk, GATE_W), 0)
    rowi = lax.broadcasted_iota(jnp.int32, (chunk, D_CONV), 0)
    sel8 = (lax.broadcasted_iota(jnp.int32, (SUBLANES, GATE_W), 0)
            == lax.broadcasted_iota(jnp.int32, (SUBLANES, GATE_W), 1)).astype(F32)
    spread = jnp.where((lax.broadcasted_iota(jnp.int32, (2 * GATE_W, D_MLSTM), 0) & (GATE_W - 1))
                       == (lax.broadcasted_iota(jnp.int32, (2 * GATE_W, D_MLSTM), 1) >> 7), 1.0, 0.0).astype(BF16)
    scale = DH ** -0.5

    units = [(bi, h) for bi in range(bb) for h in range(MH)]
    hsl = lambda h: slice(h * DH, (h + 1) * DH)
    g_c = [gate_ref[bi] + gb_ref[...] for bi in range(bb)]
    csum = [_dot(tril, _split_bf16(pltpu.roll(_log_sigmoid(g), GATE_W - MH, 1), 3)) for g in g_c]
    qb = {u: q_ref[u[0], :, hsl(u[1])].astype(BF16) for u in units}
    kb = {u: (k_ref[u[0], :, hsl(u[1])] * scale).astype(BF16) for u in units}
    s_qk = {u: _dot_nt(qb[u], kb[u]) for u in units}
    inter = {u: _dot_nt(qb[u], jnp.concatenate(
        [c_ref[u[0], u[1]].astype(BF16),
         jnp.broadcast_to(n_ref[u[0], u[1]:u[1] + 1, :], (DH, DH)).astype(BF16)], axis=0)) for u in units}
    rep, a_r, decay = [], [], []
    for bi in range(bb):
        f_c = csum[bi][:, :GATE_W] + csum[bi][:, GATE_W:2 * GATE_W] + csum[bi][:, 2 * GATE_W:]
        a_c = g_c[bi] - f_c
        m_prev = m_scr[bi:bi + 1, :]
        big_m = jnp.maximum(_cummax_rows(a_c, rowg), m_prev)
        mt = f_c + big_m
        m_new = mt[chunk - 1:chunk, :]
        f_last = f_c[chunk - 1:chunk, :]
        decay.append(jnp.exp(f_last + m_prev - m_new))
        stacked = jnp.concatenate([_split_bf16(-big_m, 2), _split_bf16(jnp.exp(m_prev - big_m), 2),
                                   _split_bf16(-mt, 2), _split_bf16(jnp.exp(a_c + (f_last - m_new)), 2)],
                                  axis=0)
        rep.append(_dot(stacked, spread))
        a_r.append(_dot_nt_exact(sel8, a_c))
        m_scr[bi:bi + 1, :] = m_new
        m_ref[pl.ds(bi0 + bi, 1), :] = m_new[:, 0:MH]
    intra, rsum = {}, {}
    for u in units:
        bi, h = u
        w = jnp.exp(jnp.where(causal, rep[bi][0:chunk, hsl(h)] + a_r[bi][h:h + 1, :], NEG))
        sqk = s_qk[u] * w
        rsum[u] = jnp.sum(sqk, axis=-1, keepdims=True)
        intra[u] = _dot(sqk.astype(BF16), v_ref[bi, :, hsl(h)].astype(BF16))
    hm, ssq = {}, {}
    for u in units:
        bi, h = u
        s_inter = rep[bi][chunk:2 * chunk, hsl(h)]
        e_floor = jnp.exp(rep[bi][2 * chunk:3 * chunk, hsl(h)])
        num = intra[u] + s_inter * inter[u][:, :DH]
        den = rsum[u] + s_inter * inter[u][:, DH:]
        hm[u] = jax.nn.sigmoid(o_ref[bi, :, hsl(h)]) * (num / jnp.maximum(jnp.abs(den), e_floor))
        ssq[u] = jnp.sum(hm[u] * hm[u], axis=-1, keepdims=True)
    for u in units:
        bi, h = u
        oa_ref[bi, :, hsl(h)] = hm[u] * lax.rsqrt(ssq[u] * (1.0 / DH) + EPS) * mg_ref[:, hsl(h)]
        wk = rep[bi][3 * chunk:, hsl(h)]
        dec = decay[bi][:, h:h + 1]
        c_ref[bi, h] = dec * c_ref[bi, h] + _dot_tn((v_ref[bi, :, hsl(h)] * wk).astype(BF16), kb[u])
        n_ref[bi, h:h + 1, :] = (dec * n_ref[bi, h:h + 1, :]
                                 + jnp.sum(k_ref[bi, :, hsl(h)] * scale * wk, axis=0, keepdims=True))
    for bi in range(bb):
        z = cg_ref[bi] * hc_ref[bi]
        cb = cb_ref[bi]
        z1 = jnp.where(rowi == 0, cb[1:2, :], pltpu.roll(z, 1, 0))
        z2 = jnp.where(rowi == 0, cb[0:1, :], jnp.where(rowi == 1, cb[1:2, :], pltpu.roll(z, 2, 0)))
        yc = cw_ref[0:1, :] * z2 + cw_ref[1:2, :] * z1 + cw_ref[2:3, :] * z
        oc_ref[bi] = _rms(bg_ref[bi] * yc) * mg_ref[:, D_MLSTM + D_S5:]
        cb_ref[bi] = z[chunk - 2:chunk, :]


def _mlstm_conv(qkvo, gate, bch, w, states_in, states_prev, *, layer, depth, bb, chunk):
    b, s, _ = qkvo.shape
    assert s % chunk == 0 and b % bb == 0
    zero_state = states_in is None
    kern = functools.partial(_mlstm_conv_kernel, layer=layer, bb=bb, chunk=chunk, zero_state=zero_state)
    col = lambda wd, j: pl.BlockSpec((bb, chunk, wd), lambda i, c: (i, c, j))
    st = lambda shape: pl.BlockSpec((None, bb) + shape, lambda i, c: (layer, i) + (0,) * len(shape))
    m_spec = pl.BlockSpec((None, b, MH), lambda i, c: (layer, 0, 0))
    cst = lambda a: pl.BlockSpec(a.shape, lambda i, c: (0,) * a.ndim)
    state_specs = [st((MH, DH, DH)), st((MH, DH)), m_spec, st((CONV_W - 1, D_CONV))]
    out_state_specs = state_specs
    if layer == 0:
        st0 = lambda shape: pl.BlockSpec((depth, bb) + shape, lambda i, c: (0, i) + (0,) * len(shape))
        out_state_specs = [st0((MH, DH, DH)), st0((MH, DH)), pl.BlockSpec((depth, b, MH), lambda i, c: (0, 0, 0)),
                           st0((CONV_W - 1, D_CONV))]
    consts = (w["gb"], w["cw"], w["mg"])
    args = [qkvo, qkvo, qkvo, qkvo, gate, bch, bch, bch, *consts]
    in_specs = [col(D_MLSTM, 0), col(D_MLSTM, 1), col(D_MLSTM, 2), col(D_MLSTM, 3), col(GATE_W, 0),
                col(D_CONV, 0), col(D_CONV, 1), col(D_CONV, 2)] + [cst(a) for a in consts]
    if not zero_state:
        args += list(states_in)
        in_specs += state_specs
    aliases = {}
    if layer > 0:
        aliases = {len(args) + j: 2 + j for j in range(4)}
        args += list(states_prev)
        in_specs += [_ANY] * 4
    return pl.pallas_call(
        kern,
        grid=(b // bb, s // chunk),
        in_specs=in_specs,
        out_specs=[col(D_MLSTM, 0), col(D_CONV, 0)] + out_state_specs,
        out_shape=[jax.ShapeDtypeStruct((b, s, D_MLSTM), F32), jax.ShapeDtypeStruct((b, s, D_CONV), F32)]
        + _state_shapes(depth, b),
        scratch_shapes=[pltpu.VMEM((-(-bb // SUBLANES) * SUBLANES, GATE_W), F32)],
        input_output_aliases=aliases,
        compiler_params=pltpu.CompilerParams(dimension_semantics=("arbitrary", "arbitrary"),
                                             vmem_limit_bytes=VMEM_LIMIT),
        name="mlstm_conv",
    )(*args)


def _gate_terms(graw, gb_ref, gbt_ref, tril, triu):
    sel = (lax.broadcasted_iota(jnp.int32, (SUBLANES, GATE_W), 0)
           == lax.broadcasted_iota(jnp.int32, (SUBLANES, GATE_W), 1)).astype(F32)
    g_c = graw + gb_ref[...]
    lf_c = _log_sigmoid(g_c)
    g_r = _dot_nt_exact(sel, graw) + gbt_ref[...]
    lf_r = _log_sigmoid(g_r)
    return g_c, lf_c, _dot_exact(tril, lf_c), g_r, _dot_exact(lf_r, triu)


def _head_scores(h, g_c, bt_c, g_r, bt_r, m_prev, mask):
    it_row = g_r[h:h + 1, :]
    bt_row = bt_r[MH + h:MH + h + 1, :]
    it_col = g_c[:, h:h + 1]
    bt_col = bt_c[:, MH + h:MH + h + 1]
    dmat = jnp.where(mask, bt_col - bt_row + it_row, NEG)
    inter = bt_col + m_prev
    mt = jnp.maximum(jnp.max(dmat, axis=-1, keepdims=True), inter)
    return jnp.exp(dmat - mt), jnp.exp(inter - mt), mt, it_col, bt_col


def _head_output(sqk, vb, s_inter, qc, qn, mt, o, mg):
    num = _dot(sqk.astype(BF16), vb) + s_inter * qc
    den = jnp.sum(sqk, axis=-1, keepdims=True) + s_inter * qn
    hm = jax.nn.sigmoid(o) * (num / jnp.maximum(jnp.abs(den), jnp.exp(-mt)))
    return _rms(hm) * mg


def _bcast_block_last(x, t, s):
    out = x
    for d in range(1, s):
        out = jnp.where(t == s - 1 - d, pltpu.roll(x, x.shape[0] - d, 0), out)
    return out


def _mlstm_conv_short_kernel(*refs, layer, nb, s):
    (q_ref, k_ref, v_ref, o_ref, gate_ref, bg_ref, cg_ref, hc_ref, gb_ref, gbt_ref, cw_ref, mg_ref,
     c0_ref, n0_ref, m0_ref, cb0_ref) = refs[:16]
    refs = refs[16:]
    if layer > 0:
        refs = refs[4:]
    oa_ref, oc_ref, c_ref, n_ref, m_ref, cb_ref, qc_scr, st_scr = refs
    if layer == 0:
        for ref in (c_ref, n_ref, m_ref, cb_ref):
            if ref.shape[0] > 1:
                ref[1:] = jnp.zeros((ref.shape[0] - 1,) + ref.shape[1:], F32)
        c_ref, n_ref, m_ref, cb_ref = c_ref.at[0], n_ref.at[0], m_ref.at[0], cb_ref.at[0]
    r = nb * s
    per_tile = SUBLANES // s
    shift = s.bit_length() - 1
    row = lax.broadcasted_iota(jnp.int32, (r, r), 0)
    col = lax.broadcasted_iota(jnp.int32, (r, r), 1)
    same = (row >> shift) == (col >> shift)
    mask = same & (col <= row)
    tril = mask.astype(F32)
    triu = (same & (row <= col)).astype(F32)
    expand = ((lax.broadcasted_iota(jnp.int32, (r, nb), 0) >> shift)
              == lax.broadcasted_iota(jnp.int32, (r, nb), 1)).astype(F32)
    gather = (lax.broadcasted_iota(jnp.int32, (nb, r), 0)
              == (lax.broadcasted_iota(jnp.int32, (nb, r), 1) >> shift)).astype(F32)
    t128 = lax.broadcasted_iota(jnp.int32, (r, DH), 0) & (s - 1)
    sub8 = lax.broadcasted_iota(jnp.int32, (SUBLANES, DH), 0) >> shift
    scale = DH ** -0.5

    g_c, lf_c, bt_c, g_r, bt_r = _gate_terms(gate_ref[...], gb_ref, gbt_ref, tril, triu)
    bt_last_c = _dot_exact(same.astype(F32), lf_c)
    m_rows = _dot_exact(expand, m0_ref[...])
    for h in range(MH):
        hs = slice(h * DH, (h + 1) * DH)
        m_prev = m_rows[:, h:h + 1]
        w, s_inter, mt, it_col, bt_col = _head_scores(h, g_c, bt_c, g_r, bt_r, m_prev, mask)
        qh = q_ref[:, hs]
        kh = k_ref[:, hs] * scale
        vh = v_ref[:, hs]
        qb = qh.astype(BF16)
        kb = kh.astype(BF16)
        sqk = _dot_nt(qb, kb) * w
        for j in range(r // SUBLANES):
            rows = slice(j * SUBLANES, (j + 1) * SUBLANES)
            acc = None
            q8 = q_ref[rows, hs].astype(BF16)
            for p in range(per_tile):
                part = _dot_nt(q8, c0_ref[j * per_tile + p, h].astype(BF16))
                acc = part if acc is None else jnp.where(sub8 == p, part, acc)
            qc_scr[rows, :] = acc
        n_rows = _dot_exact(expand, n0_ref[:, h, :])
        qn = jnp.sum(qh * n_rows, axis=-1, keepdims=True)
        oa_ref[:, hs] = _head_output(sqk, vh.astype(BF16), s_inter, qc_scr[...], qn, mt,
                                     o_ref[:, hs], mg_ref[:, hs])
        m_new = _bcast_block_last(jnp.broadcast_to(mt, (r, DH)), t128, s)
        bt_last = bt_last_c[:, MH + h:MH + h + 1]
        wk = jnp.exp(bt_last - bt_col + it_col - m_new)
        decay = jnp.exp(bt_last + m_prev - m_new)
        vw = vh * wk
        for j in range(r // SUBLANES):
            rows = slice(j * SUBLANES, (j + 1) * SUBLANES)
            k8 = (k_ref[rows, hs] * scale).astype(BF16)
            for p in range(per_tile):
                bidx = j * per_tile + p
                last = bidx * s + s - 1
                upd = _dot_tn(jnp.where(sub8 == p, vw[rows], 0.0).astype(BF16), k8)
                c_ref[bidx, h] = decay[last:last + 1, 0:1] * c0_ref[bidx, h] + upd
        st_scr[0] = decay
        st_scr[1] = m_new
        last_rows = pl.ds(s - 1, nb, stride=s)
        n_ref[:, h, :] = st_scr[0, last_rows, :] * n0_ref[:, h, :] + _dot_exact(gather, kh * wk)
        m_ref[:, h:h + 1] = st_scr[1, last_rows, :][:, 0:1]
    t256 = lax.broadcasted_iota(jnp.int32, (r, D_CONV), 0) & (s - 1)
    z = cg_ref[...] * hc_ref[...]
    cb_a = _dot_exact(expand, cb0_ref[:, 0, :])
    cb_b = _dot_exact(expand, cb0_ref[:, 1, :])
    z1 = jnp.where(t256 == 0, cb_b, pltpu.roll(z, 1, 0))
    z2 = jnp.where(t256 == 0, cb_a, jnp.where(t256 == 1, cb_b, pltpu.roll(z, 2, 0)))
    yc = cw_ref[0:1, :] * z2 + cw_ref[1:2, :] * z1 + cw_ref[2:3, :] * z
    oc_ref[...] = _rms(bg_ref[...] * yc) * mg_ref[:, D_MLSTM + D_S5:]
    for half in range(D_CONV // DH):
        lanes = slice(half * DH, (half + 1) * DH)
        st_scr[half] = z[:, lanes]
        cb_ref[:, 0, lanes] = st_scr[half, pl.ds(s - 2, nb, stride=s), :]
        cb_ref[:, 1, lanes] = st_scr[half, pl.ds(s - 1, nb, stride=s), :]


def _short_batch_block(b, depth, layer):
    out_layers = depth if layer == 0 else 1
    nb = min(b, 32)
    while nb > SUBLANES and 2 * (1 + out_layers) * nb * MH * DH * DH * 4 > STATE_VMEM_BUDGET:
        nb //= 2
    return nb


def _mlstm_conv_short(qkvo, gate, bch, w, states_in, states_prev, *, layer, depth, b, s):
    nb = _short_batch_block(b, depth, layer)
    assert SUBLANES % s == 0 and s >= CONV_W - 1 and b % nb == 0 and (nb * s) % SUBLANES == 0
    r = nb * s
    kern = functools.partial(_mlstm_conv_short_kernel, layer=layer, nb=nb, s=s)
    col = lambda wd, j: pl.BlockSpec((r, wd), lambda i: (i, j))
    st = lambda shape: pl.BlockSpec((None, nb) + shape, lambda i: (layer, i) + (0,) * len(shape))
    cst = lambda a: pl.BlockSpec(a.shape, lambda i: (0,) * a.ndim)
    state_specs = [st((MH, DH, DH)), st((MH, DH)), st((MH,)), st((CONV_W - 1, D_CONV))]
    out_state_specs = state_specs
    if layer == 0:
        st0 = lambda shape: pl.BlockSpec((depth, nb) + shape, lambda i: (0, i) + (0,) * len(shape))
        out_state_specs = [st0((MH, DH, DH)), st0((MH, DH)), st0((MH,)), st0((CONV_W - 1, D_CONV))]
    consts = (w["gb"], w["gbt"], w["cw"], w["mg"])
    args = [qkvo, qkvo, qkvo, qkvo, gate, bch, bch, bch, *consts, *states_in]
    in_specs = [col(D_MLSTM, 0), col(D_MLSTM, 1), col(D_MLSTM, 2), col(D_MLSTM, 3), col(GATE_W, 0),
                col(D_CONV, 0), col(D_CONV, 1), col(D_CONV, 2)] + [cst(a) for a in consts] + state_specs
    aliases = {}
    if layer > 0:
        aliases = {len(args) + j: 2 + j for j in range(4)}
        args += list(states_prev)
        in_specs += [_ANY] * 4
    return pl.pallas_call(
        kern,
        grid=(b // nb,),
        in_specs=in_specs,
        out_specs=[col(D_MLSTM, 0), col(D_CONV, 0)] + out_state_specs,
        out_shape=[jax.ShapeDtypeStruct((b * s, D_MLSTM), F32), jax.ShapeDtypeStruct((b * s, D_CONV), F32)]
        + _state_shapes(depth, b),
        scratch_shapes=[pltpu.VMEM((r, DH), F32), pltpu.VMEM((2, r, DH), F32)],
        input_output_aliases=aliases,
        compiler_params=pltpu.CompilerParams(dimension_semantics=("arbitrary",), vmem_limit_bytes=VMEM_LIMIT),
        name="mlstm_conv_short",
    )(*args)


def _s5_kernel(*refs, layer, nb, tt, zero_state, parts, batch_major):
    u_ref, lam_ref, bblk_ref, cblk_ref, d_ref, wglu_ref, mg_ref = refs[:7]
    refs = refs[7:]
    if not zero_state:
        hr0_ref, hi0_ref = refs[:2]
        refs = refs[2:]
    if layer > 0:
        refs = refs[2:]
    ob_ref, hr_ref, hi_ref, xs_ref, a_ref, bf_ref = refs[:6]
    if batch_major:
        ut_scr, ot_scr = refs[6:]
        ut_scr[...] = jnp.swapaxes(u_ref[...], 0, 1).reshape(tt * nb, D_S5)
        u_bm_ref, ob_bm_ref, u_ref, ob_ref = u_ref, ob_ref, ut_scr, ot_scr
    if layer == 0:
        hr_all, hi_all = hr_ref, hi_ref
        hr_ref, hi_ref = hr_all.at[0], hi_all.at[0]
    ti = pl.program_id(0)
    n = S5_N
    lc = 512

    @pl.when(ti == 0)
    def _():
        if layer == 0:
            hr_all[...] = jnp.zeros_like(hr_all)
            hi_all[...] = jnp.zeros_like(hi_all)
        hr_ref[...] = jnp.zeros_like(hr_ref) if zero_state else hr0_ref[...]
        hi_ref[...] = jnp.zeros_like(hi_ref) if zero_state else hi0_ref[...]
        lre = jnp.minimum(lam_ref[0:1, :], -1e-4)
        lim = lam_ref[1:2, :]
        dt = jnp.exp(lam_ref[2:3, :])
        mag = jnp.exp(lre * dt)
        ab_re = mag * jnp.cos(lim * dt)
        ab_im = mag * jnp.sin(lim * dt)
        den = lre * lre + lim * lim
        nr = ab_re - 1.0
        fre = (nr * lre + ab_im * lim) / den
        fim = (ab_im * lre - nr * lim) / den
        a_ref[:, :n] = jnp.broadcast_to(ab_re, (SUBLANES, n))
        a_ref[:, n:] = jnp.broadcast_to(ab_im, (SUBLANES, n))
        b_re = bblk_ref[:, :n]
        b_im = bblk_ref[:, n:]
        bf_ref[:, :n] = (fre * b_re - fim * b_im).astype(BF16)
        bf_ref[:, n:] = (fre * b_im + fim * b_re).astype(BF16)

    rows_p = nb * tt // parts
    part_rows = [slice(p * rows_p, (p + 1) * rows_p) for p in range(parts)]
    for rs in part_rows:
        xs_ref[rs, :] = _dot(u_ref[rs, :].astype(BF16), bf_ref[...])
    ar = [a_ref[:, c0:c0 + lc] for c0 in range(0, n, lc)]
    ai = [a_ref[:, n + c0:n + c0 + lc] for c0 in range(0, n, lc)]
    h = {}
    for p, rs in enumerate(part_rows):
        for rg in range(nb // SUBLANES):
            g8 = slice(rg * SUBLANES, (rg + 1) * SUBLANES)
            for ci, c0 in enumerate(range(0, n, lc)):
                if p == 0:
                    h[rg, ci] = (hr_ref[g8, c0:c0 + lc], hi_ref[g8, c0:c0 + lc])
            for t in range(p * tt // parts, (p + 1) * tt // parts):
                r8 = slice(t * nb + rg * SUBLANES, t * nb + (rg + 1) * SUBLANES)
                for ci, c0 in enumerate(range(0, n, lc)):
                    hr, hi = h[rg, ci]
                    nr_ = ar[ci] * hr - ai[ci] * hi + xs_ref[r8, c0:c0 + lc]
                    ni_ = ar[ci] * hi + ai[ci] * hr + xs_ref[r8, n + c0:n + c0 + lc]
                    xs_ref[r8, c0:c0 + lc] = nr_
                    xs_ref[r8, n + c0:n + c0 + lc] = ni_
                    h[rg, ci] = (nr_, ni_)
            for ci, c0 in enumerate(range(0, n, lc)):
                if p == parts - 1:
                    hr_ref[g8, c0:c0 + lc] = h[rg, ci][0]
                    hi_ref[g8, c0:c0 + lc] = h[rg, ci][1]
    y = [_dot(xs_ref[rs, :].astype(BF16), cblk_ref[...]) + d_ref[...] * u_ref[rs, :] for rs in part_rows]
    g = [jax.nn.gelu(v) for v in y]
    gate = [_dot(v.astype(BF16), wglu_ref[...]) for v in g]
    for rs, v, z in zip(part_rows, g, gate):
        ob_ref[rs, :] = _rms(v * jax.nn.sigmoid(z)) * mg_ref[:, D_MLSTM:D_MLSTM + D_S5]
    if batch_major:
        ob_bm_ref[...] = jnp.swapaxes(ot_scr[...].reshape(tt, nb, D_S5), 0, 1)


def _s5(u, w, states_in, states_prev, *, layer, depth, nb, tt):
    batch_major = u.ndim == 3
    rows = u.shape[0] * u.shape[1] if batch_major else u.shape[0]
    rt = nb * tt
    zero_state = states_in is None
    parts = tt // 16 if tt % 16 == 0 else 1
    kern = functools.partial(_s5_kernel, layer=layer, nb=nb, tt=tt, zero_state=zero_state, parts=parts,
                             batch_major=batch_major)
    if batch_major:
        assert u.shape[0] == nb and tt % SUBLANES == 0
        io_spec = pl.BlockSpec((nb, tt, D_S5), lambda i: (0, i, 0))
        io_shape = jax.ShapeDtypeStruct(u.shape, F32)
        io_scratch = [pltpu.VMEM((rt, D_S5), F32), pltpu.VMEM((rt, D_S5), F32)]
    else:
        io_spec = pl.BlockSpec((rt, D_S5), lambda i: (i, 0))
        io_shape = jax.ShapeDtypeStruct((rows, D_S5), F32)
        io_scratch = []
    cst = lambda a: pl.BlockSpec(a.shape, lambda i: (0,) * a.ndim)
    st = pl.BlockSpec((None, nb, S5_N), lambda i: (layer, 0, 0))
    st_out = pl.BlockSpec((depth, nb, S5_N), lambda i: (0, 0, 0)) if layer == 0 else st
    consts = (w["lam"], w["bblk"], w["cblk"], w["d"], w["wglu"], w["mg"])
    args = [u, *consts]
    in_specs = [io_spec] + [cst(a) for a in consts]
    if not zero_state:
        args += list(states_in)
        in_specs += [st, st]
    aliases = {}
    if layer > 0:
        aliases = {len(args) + j: 1 + j for j in range(2)}
        args += list(states_prev)
        in_specs += [_ANY] * 2
    return pl.pallas_call(
        kern,
        grid=(rows // rt,),
        in_specs=in_specs,
        out_specs=[io_spec, st_out, st_out],
        out_shape=[io_shape, jax.ShapeDtypeStruct((depth, nb, S5_N), F32),
                   jax.ShapeDtypeStruct((depth, nb, S5_N), F32)],
        scratch_shapes=[pltpu.VMEM((rt, 2 * S5_N), F32), pltpu.VMEM((SUBLANES, 2 * S5_N), F32),
                        pltpu.VMEM((D_S5, 2 * S5_N), BF16)] + io_scratch,
        input_output_aliases=aliases,
        compiler_params=pltpu.CompilerParams(dimension_semantics=("arbitrary",), vmem_limit_bytes=VMEM_LIMIT),
        name="s5",
    )(*args)


def _block_diag(w):
    g, r, c = w.shape
    eye = jnp.eye(g, dtype=w.dtype)
    return jnp.einsum("grc,gh->grhc", w, eye).reshape(g * r, g * c)


def _layer_weights(l, ig_bias, fg_bias, s5_a_re, s5_a_im, s5_log_dt, s5_b_re, s5_b_im, s5_c_re, s5_c_im, s5_d,
                   w_glu, conv_w, mix_g):
    gb = jnp.concatenate([ig_bias[l], fg_bias[l], jnp.zeros((GATE_W - 2 * MH,), F32)])
    lam = jnp.stack([s5_a_re[l].reshape(-1), s5_a_im[l].reshape(-1),
                     jnp.repeat(s5_log_dt[l], S5_P)])
    bblk = jnp.concatenate([_block_diag(jnp.swapaxes(s5_b_re[l], 1, 2)),
                            _block_diag(jnp.swapaxes(s5_b_im[l], 1, 2))], axis=1)
    cblk = jnp.concatenate([_block_diag(jnp.swapaxes(s5_c_re[l], 1, 2)),
                            -_block_diag(jnp.swapaxes(s5_c_im[l], 1, 2))], axis=0).astype(BF16)
    return dict(
        gb=gb.reshape(1, GATE_W), gbt=gb[:SUBLANES].reshape(SUBLANES, 1),
        lam=lam, bblk=bblk, cblk=cblk, d=s5_d[l].reshape(1, D_S5), wglu=w_glu[l].astype(BF16),
        cw=conv_w[l], mg=mix_g[l].reshape(1, D_MODEL))


def _stacked_weights(ffn1_w1, ffn1_w3, ffn1_w2, ffn2_w1, ffn2_w3, ffn2_w2, norm_g, w_in, w_out):
    o = 4 * D_MLSTM
    win = (w_in[:, :, :o].astype(BF16),
           jnp.pad(w_in[:, :, o:o + 2 * MH].astype(BF16), ((0, 0), (0, 0), (0, GATE_W - 2 * MH))),
           w_in[:, :, o + 2 * MH:].astype(BF16))
    return dict(f1=(ffn1_w1.astype(BF16), ffn1_w3.astype(BF16), ffn1_w2.astype(BF16)),
                f2=(ffn2_w1.astype(BF16), ffn2_w3.astype(BF16), ffn2_w2.astype(BF16)),
                g=norm_g, win=win, wo=w_out.astype(BF16))


def _run_group(x, states, weights, big, *, tm, tt, chunk=None, bb=None):
    b, s, _ = x.shape
    depth = len(weights)
    xf = x.reshape(b * s, D_MODEL)
    mstates = sstates = None
    if states is not None:
        c_all, n_all, m_all, sr_all, si_all, cb_all = states
        ms_in = (c_all, n_all, m_all, cb_all)
        ss_in = (sr_all.reshape(depth, b, S5_N), si_all.reshape(depth, b, S5_N))
    else:
        ms_in = ss_in = None
    for l, w in enumerate(weights):
        x1, qkvo, gate, u, bch = _ffn_inproj(xf, big["g"], *big["f1"], big["win"], tm, l)
        if tt % SUBLANES == 0:
            ob, *sstates = _s5(u.reshape(b, s, D_S5), w, ss_in, sstates, layer=l, depth=depth, nb=b, tt=tt)
            ob = ob.reshape(b * s, D_S5)
        else:
            u_tm = jnp.swapaxes(u.reshape(b, s, D_S5), 0, 1).reshape(s * b, D_S5)
            ob_tm, *sstates = _s5(u_tm, w, ss_in, sstates, layer=l, depth=depth, nb=b, tt=tt)
            ob = jnp.swapaxes(ob_tm.reshape(s, b, D_S5), 0, 1).reshape(b * s, D_S5)
        if chunk is not None:
            r3 = lambda a: a.reshape(b, s, a.shape[-1])
            oa, oc, *mstates = _mlstm_conv(r3(qkvo), r3(gate), r3(bch), w, ms_in, mstates,
                                           layer=l, depth=depth, bb=bb, chunk=chunk)
            xf = _outproj_ffn(x1, oa.reshape(b * s, D_MLSTM), ob, oc.reshape(b * s, D_CONV),
                              big["g"], big["wo"], *big["f2"], tm, l)
        else:
            oa, oc, *mstates = _mlstm_conv_short(qkvo, gate, bch, w, ms_in, mstates,
                                                 layer=l, depth=depth, b=b, s=s)
            xf = _outproj_ffn(x1, oa, ob, oc, big["g"], big["wo"], *big["f2"], tm, l)
    c1, n1, m1, cb1 = mstates
    sr1, si1 = sstates
    return xf.reshape(b, s, D_MODEL), (c1, n1, m1, sr1.reshape(depth, b, S5_G, S5_P),
                                       si1.reshape(depth, b, S5_G, S5_P), cb1)


def kernel(x_prompt, x_sample, state_mlstm_C, state_mlstm_n, state_mlstm_m, state_s5_re, state_s5_im,
           state_conv, ffn1_w1, ffn1_w3, ffn1_w2, ffn2_w1, ffn2_w3, ffn2_w2, norm_g, w_in, ig_bias, fg_bias,
           s5_a_re, s5_a_im, s5_log_dt, s5_b_re, s5_b_im, s5_c_re, s5_c_im, s5_d, w_glu, conv_w, mix_g, w_out):
    depth = norm_g.shape[0]
    params = (ig_bias, fg_bias, s5_a_re, s5_a_im, s5_log_dt, s5_b_re, s5_b_im, s5_c_re, s5_c_im, s5_d, w_glu,
              conv_w, mix_g)
    weights = [_layer_weights(l, *params) for l in range(depth)]
    big = _stacked_weights(ffn1_w1, ffn1_w3, ffn1_w2, ffn2_w1, ffn2_w3, ffn2_w2, norm_g, w_in, w_out)
    y_p, st_p = _run_group(x_prompt, None, weights, big, tm=512, tt=256, chunk=128, bb=4)
    y_s, st_s = _run_group(
        x_sample, (state_mlstm_C, state_mlstm_n, state_mlstm_m, state_s5_re, state_s5_im, state_conv),
        weights, big, tm=512, tt=x_sample.shape[1])
    return (y_p, y_s, *st_p, *st_s)
```

```python
import functools

import jax
import jax.numpy as jnp
from jax import lax
from jax.experimental import pallas as pl
from jax.experimental.pallas import tpu as pltpu

F32 = jnp.float32
BF16 = jnp.bfloat16

D_MODEL = 1024
MH = 4
DH = 128
D_MLSTM = MH * DH
S5_CH = 16
S5_G = 16
S5_P = 64
D_S5 = S5_G * S5_CH
S5_N = S5_G * S5_P
D_CONV = 256
CONV_W = 3
EPS = 1e-6
GATE_W = 128
SUBLANES = 8
NEG = -1e30
VMEM_LIMIT = 56 * 1024 * 1024
STATE_VMEM_BUDGET = 36 * 1024 * 1024
ROW_PARTS = 2
S5_PARTS = 4


def _dot(a, b):
    return jnp.dot(a, b, preferred_element_type=F32)


def _dot_nt(a, b):
    return lax.dot_general(a, b, (((1,), (1,)), ((), ())), preferred_element_type=F32)


def _dot_tn(a, b):
    return lax.dot_general(a, b, (((0,), (0,)), ((), ())), preferred_element_type=F32)


def _dot_exact(a, b):
    return jnp.dot(a, b, preferred_element_type=F32, precision=lax.Precision.HIGHEST)


def _dot_nt_exact(a, b):
    return lax.dot_general(a, b, (((1,), (1,)), ((), ())), preferred_element_type=F32,
                           precision=lax.Precision.HIGHEST)


def _rms(x):
    return x * lax.rsqrt(jnp.mean(x * x, axis=-1, keepdims=True) + EPS)


def _log_sigmoid(x):
    return jnp.minimum(x, 0.0) - jnp.log(1.0 + jnp.exp(-jnp.abs(x)))


def _split_bf16(x, parts):
    out = []
    for _ in range(parts):
        p = x.astype(BF16)
        out.append(p)
        x = x - p.astype(F32)
    return jnp.concatenate(out, axis=1)


def _ffn_residual(xs, g_pre, g_post, w1, w3, w2):
    xn = [(_rms(x) * g_pre).astype(BF16) for x in xs]
    h1 = [_dot(v, w1[...]) for v in xn]
    h3 = [_dot(v, w3[...]) for v in xn]
    a = [(jax.nn.silu(p) * q).astype(BF16) for p, q in zip(h1, h3)]
    y = [_dot(v, w2[...]) for v in a]
    return [x + 0.5 * (_rms(t) * g_post) for x, t in zip(xs, y)]


def _row_parts(tm, parts):
    step = tm // parts
    return [slice(i * step, (i + 1) * step) for i in range(parts)]


def _layer_spec(a, layer):
    return pl.BlockSpec((None,) + a.shape[1:], lambda *_: (layer, 0, 0), pipeline_mode=pl.Buffered(1))


_ANY = pl.BlockSpec(memory_space=pl.ANY)


def _ffn_inproj_kernel(x_ref, g_ref, w1_ref, w3_ref, w2_ref, wq_ref, wg_ref, wr_ref,
                       x1_ref, qkvo_ref, gate_ref, u_ref, bch_ref):
    parts = _row_parts(x_ref.shape[0], ROW_PARTS)
    x1 = _ffn_residual([x_ref[r, :] for r in parts], g_ref[0:1, :], g_ref[1:2, :], w1_ref, w3_ref, w2_ref)
    hn = [(_rms(v) * g_ref[2:3, :]).astype(BF16) for v in x1]
    qkvo = [_dot(v, wq_ref[...]) for v in hn]
    gate = [_dot(v, wg_ref[...]) for v in hn]
    rest = [_dot(v, wr_ref[...]) for v in hn]
    for i, r in enumerate(parts):
        x1_ref[r, :] = x1[i]
        qkvo_ref[r, :] = qkvo[i]
        gate_ref[r, :] = gate[i]
        u_ref[r, :] = rest[i][:, :D_S5]
        bch_ref[r, :] = rest[i][:, D_S5:]


def _ffn_inproj(x, g, w1, w3, w2, win, tm, layer):
    t = x.shape[0]
    row = lambda w: pl.BlockSpec((tm, w), lambda i: (i, 0))
    return pl.pallas_call(
        _ffn_inproj_kernel,
        grid=(t // tm,),
        in_specs=[row(D_MODEL), _layer_spec(g, layer), _layer_spec(w1, layer), _layer_spec(w3, layer),
                  _layer_spec(w2, layer)] + [_layer_spec(a, layer) for a in win],
        out_specs=[row(D_MODEL), row(4 * D_MLSTM), row(GATE_W), row(D_S5), row(3 * D_CONV)],
        out_shape=[jax.ShapeDtypeStruct((t, D_MODEL), F32), jax.ShapeDtypeStruct((t, 4 * D_MLSTM), F32),
                   jax.ShapeDtypeStruct((t, GATE_W), F32), jax.ShapeDtypeStruct((t, D_S5), F32),
                   jax.ShapeDtypeStruct((t, 3 * D_CONV), F32)],
        compiler_params=pltpu.CompilerParams(dimension_semantics=("arbitrary",), vmem_limit_bytes=VMEM_LIMIT),
        name="ffn_inproj",
    )(x, g, w1, w3, w2, *win)


def _outproj_ffn_kernel(x_ref, a_ref, b_ref, c_ref, g_ref, wo_ref, w1_ref, w3_ref, w2_ref, y_ref):
    parts = _row_parts(x_ref.shape[0], ROW_PARTS)
    mo = [(_dot(a_ref[r, :].astype(BF16), wo_ref[0:D_MLSTM, :])
           + _dot(b_ref[r, :].astype(BF16), wo_ref[D_MLSTM:D_MLSTM + D_S5, :])
           + _dot(c_ref[r, :].astype(BF16), wo_ref[D_MLSTM + D_S5:, :])) for r in parts]
    x2 = [x_ref[r, :] + _rms(v) * g_ref[3:4, :] for r, v in zip(parts, mo)]
    y = _ffn_residual(x2, g_ref[4:5, :], g_ref[5:6, :], w1_ref, w3_ref, w2_ref)
    for r, v in zip(parts, y):
        y_ref[r, :] = v


def _outproj_ffn(x, oa, ob, oc, g, wo, w1, w3, w2, tm, layer):
    t = x.shape[0]
    row = lambda w: pl.BlockSpec((tm, w), lambda i: (i, 0))
    return pl.pallas_call(
        _outproj_ffn_kernel,
        grid=(t // tm,),
        in_specs=[row(D_MODEL), row(D_MLSTM), row(D_S5), row(D_CONV), _layer_spec(g, layer),
                  _layer_spec(wo, layer), _layer_spec(w1, layer), _layer_spec(w3, layer),
                  _layer_spec(w2, layer)],
        out_specs=row(D_MODEL),
        out_shape=jax.ShapeDtypeStruct((t, D_MODEL), F32),
        compiler_params=pltpu.CompilerParams(dimension_semantics=("arbitrary",), vmem_limit_bytes=VMEM_LIMIT),
        name="outproj_ffn",
    )(x, oa, ob, oc, g, wo, w1, w3, w2)


def _state_shapes(depth, b):
    return [jax.ShapeDtypeStruct((depth, b, MH, DH, DH), F32), jax.ShapeDtypeStruct((depth, b, MH, DH), F32),
            jax.ShapeDtypeStruct((depth, b, MH), F32), jax.ShapeDtypeStruct((depth, b, CONV_W - 1, D_CONV), F32)]


def _cummax_rows(x, rowi):
    s = 1
    while s < x.shape[0]:
        x = jnp.maximum(x, jnp.where(rowi >= s, pltpu.roll(x, s, 0), NEG))
        s *= 2
    return x


def _mlstm_conv_kernel(*refs, layer, bb, chunk, zero_state):
    (q_ref, k_ref, v_ref, o_ref, gate_ref, bg_ref, cg_ref, hc_ref, gb_ref, cw_ref, mg_ref) = refs[:11]
    refs = refs[11:]
    if not zero_state:
        c0_ref, n0_ref, m0_ref, cb0_ref = refs[:4]
        refs = refs[4:]
    if layer > 0:
        refs = refs[4:]
    oa_ref, oc_ref, c_ref, n_ref, m_ref, cb_ref, m_scr = refs
    other_layers = ()
    if layer == 0:
        other_layers = tuple(r.at[1:] for r in (c_ref, n_ref, m_ref, cb_ref) if r.shape[0] > 1)
        c_ref, n_ref, m_ref, cb_ref = c_ref.at[0], n_ref.at[0], m_ref.at[0], cb_ref.at[0]
    assert chunk == DH
    bi0 = pl.program_id(0) * bb
    ci = pl.program_id(1)

    @pl.when(ci == 0)
    def _():
        for r in other_layers:
            r[...] = jnp.zeros_like(r)
        m_scr[...] = jnp.zeros_like(m_scr)
        if zero_state:
            c_ref[...] = jnp.zeros_like(c_ref)
            n_ref[...] = jnp.zeros_like(n_ref)
            cb_ref[...] = jnp.zeros_like(cb_ref)
        else:
            c_ref[...] = c0_ref[...]
            n_ref[...] = n0_ref[...]
            cb_ref[...] = cb0_ref[...]
            for bi in range(bb):
                m_scr[bi:bi + 1, 0:MH] = m0_ref[pl.ds(bi0 + bi, 1), :]

    row = lax.broadcasted_iota(jnp.int32, (chunk, chunk), 0)
    col = lax.broadcasted_iota(jnp.int32, (chunk, chunk), 1)
    causal = col <= row
    tril = jnp.where(causal, 1.0, 0.0).astype(BF16)
    rowg = lax.broadcasted_iota(jnp.int32, (chunk, GATE_W), 0)
    rowi = lax.broadcasted_iota(jnp.int32, (chunk, D_CONV), 0)
    sel8 = (lax.broadcasted_iota(jnp.int32, (SUBLANES, GATE_W), 0)
            == lax.broadcasted_iota(jnp.int32, (SUBLANES, GATE_W), 1)).astype(F32)
    spread = jnp.where((lax.broadcasted_iota(jnp.int32, (2 * GATE_W, D_MLSTM), 0) & (GATE_W - 1))
                       == (lax.broadcasted_iota(jnp.int32, (2 * GATE_W, D_MLSTM), 1) >> 7), 1.0, 0.0).astype(BF16)
    scale = DH ** -0.5

    units = [(bi, h) for bi in range(bb) for h in range(MH)]
    hsl = lambda h: slice(h * DH, (h + 1) * DH)
    g_c = [gate_ref[bi] + gb_ref[...] for bi in range(bb)]
    csum = [_dot(tril, _split_bf16(pltpu.roll(_log_sigmoid(g), GATE_W - MH, 1), 3)) for g in g_c]
    qb = {u: q_ref[u[0], :, hsl(u[1])].astype(BF16) for u in units}
    kb = {u: (k_ref[u[0], :, hsl(u[1])] * scale).astype(BF16) for u in units}
    s_qk = {u: _dot_nt(qb[u], kb[u]) for u in units}
    inter = {u: _dot_nt(qb[u], jnp.concatenate(
        [c_ref[u[0], u[1]].astype(BF16),
         jnp.broadcast_to(n_ref[u[0], u[1]:u[1] + 1, :], (DH, DH)).astype(BF16)], axis=0)) for u in units}
    rep, a_r, decay = [], [], []
    for bi in range(bb):
        f_c = csum[bi][:, :GATE_W] + csum[bi][:, GATE_W:2 * GATE_W] + csum[bi][:, 2 * GATE_W:]
        a_c = g_c[bi] - f_c
        m_prev = m_scr[bi:bi + 1, :]
        big_m = jnp.maximum(_cummax_rows(a_c, rowg), m_prev)
        mt = f_c + big_m
        m_new = mt[chunk - 1:chunk, :]
        f_last = f_c[chunk - 1:chunk, :]
        decay.append(jnp.exp(f_last + m_prev - m_new))
        stacked = jnp.concatenate([_split_bf16(-big_m, 2), _split_bf16(jnp.exp(m_prev - big_m), 2),
                                   _split_bf16(-mt, 2), _split_bf16(jnp.exp(a_c + (f_last - m_new)), 2)],
                                  axis=0)
        rep.append(_dot(stacked, spread))
        a_r.append(_dot_nt_exact(sel8, a_c))
        m_scr[bi:bi + 1, :] = m_new
        m_ref[pl.ds(bi0 + bi, 1), :] = m_new[:, 0:MH]
    intra, rsum = {}, {}
    for u in units:
        bi, h = u
        w = jnp.exp(jnp.where(causal, rep[bi][0:chunk, hsl(h)] + a_r[bi][h:h + 1, :], NEG))
        sqk = s_qk[u] * w
        rsum[u] = jnp.sum(sqk, axis=-1, keepdims=True)
        intra[u] = _dot(sqk.astype(BF16), v_ref[bi, :, hsl(h)].astype(BF16))
    hm, ssq = {}, {}
    for u in units:
        bi, h = u
        s_inter = rep[bi][chunk:2 * chunk, hsl(h)]
        e_floor = jnp.exp(rep[bi][2 * chunk:3 * chunk, hsl(h)])
        num = intra[u] + s_inter * inter[u][:, :DH]
        den = rsum[u] + s_inter * inter[u][:, DH:]
        hm[u] = jax.nn.sigmoid(o_ref[bi, :, hsl(h)]) * (num / jnp.maximum(jnp.abs(den), e_floor))
        ssq[u] = jnp.sum(hm[u] * hm[u], axis=-1, keepdims=True)
    for u in units:
        bi, h = u
        oa_ref[bi, :, hsl(h)] = hm[u] * lax.rsqrt(ssq[u] * (1.0 / DH) + EPS) * mg_ref[:, hsl(h)]
        wk = rep[bi][3 * chunk:, hsl(h)]
        dec = decay[bi][:, h:h + 1]
        c_ref[bi, h] = dec * c_ref[bi, h] + _dot_tn((v_ref[bi, :, hsl(h)] * wk).astype(BF16), kb[u])
        n_ref[bi, h:h + 1, :] = (dec * n_ref[bi, h:h + 1, :]
                                 + jnp.sum(k_ref[bi, :, hsl(h)] * scale * wk, axis=0, keepdims=True))
    for bi in range(bb):
        z = cg_ref[bi] * hc_ref[bi]
        cb = cb_ref[bi]
        z1 = jnp.where(rowi == 0, cb[1:2, :], pltpu.roll(z, 1, 0))
        z2 = jnp.where(rowi == 0, cb[0:1, :], jnp.where(rowi == 1, cb[1:2, :], pltpu.roll(z, 2, 0)))
        yc = cw_ref[0:1, :] * z2 + cw_ref[1:2, :] * z1 + cw_ref[2:3, :] * z
        oc_ref[bi] = _rms(bg_ref[bi] * yc) * mg_ref[:, D_MLSTM + D_S5:]
        cb_ref[bi] = z[chunk - 2:chunk, :]


def _mlstm_conv(qkvo, gate, bch, w, states_in, states_prev, *, layer, depth, bb, chunk):
    b, s, _ = qkvo.shape
    assert s % chunk == 0 and b % bb == 0
    zero_state = states_in is None
    kern = functools.partial(_mlstm_conv_kernel, layer=layer, bb=bb, chunk=chunk, zero_state=zero_state)
    col = lambda wd, j: pl.BlockSpec((bb, chunk, wd), lambda i, c: (i, c, j))
    st = lambda shape: pl.BlockSpec((None, bb) + shape, lambda i, c: (layer, i) + (0,) * len(shape))
    m_spec = pl.BlockSpec((None, b, MH), lambda i, c: (layer, 0, 0))
    cst = lambda a: pl.BlockSpec(a.shape, lambda i, c: (0,) * a.ndim)
    state_specs = [st((MH, DH, DH)), st((MH, DH)), m_spec, st((CONV_W - 1, D_CONV))]
    out_state_specs = state_specs
    if layer == 0:
        st0 = lambda shape: pl.BlockSpec((depth, bb) + shape, lambda i, c: (0, i) + (0,) * len(shape))
        out_state_specs = [st0((MH, DH, DH)), st0((MH, DH)), pl.BlockSpec((depth, b, MH), lambda i, c: (0, 0, 0)),
                           st0((CONV_W - 1, D_CONV))]
    consts = (w["gb"], w["cw"], w["mg"])
    args = [qkvo, qkvo, qkvo, qkvo, gate, bch, bch, bch, *consts]
    in_specs = [col(D_MLSTM, 0), col(D_MLSTM, 1), col(D_MLSTM, 2), col(D_MLSTM, 3), col(GATE_W, 0),
                col(D_CONV, 0), col(D_CONV, 1), col(D_CONV, 2)] + [cst(a) for a in consts]
    if not zero_state:
        args += list(states_in)
        in_specs += state_specs
    aliases = {}
    if layer > 0:
        aliases = {len(args) + j: 2 + j for j in range(4)}
        args += list(states_prev)
        in_specs += [_ANY] * 4
    return pl.pallas_call(
        kern,
        grid=(b // bb, s // chunk),
        in_specs=in_specs,
        out_specs=[col(D_MLSTM, 0), col(D_CONV, 0)] + out_state_specs,
        out_shape=[jax.ShapeDtypeStruct((b, s, D_MLSTM), F32), jax.ShapeDtypeStruct((b, s, D_CONV), F32)]
        + _state_shapes(depth, b),
        scratch_shapes=[pltpu.VMEM((-(-bb // SUBLANES) * SUBLANES, GATE_W), F32)],
        input_output_aliases=aliases,
        compiler_params=pltpu.CompilerParams(dimension_semantics=("arbitrary", "arbitrary"),
                                             vmem_limit_bytes=VMEM_LIMIT),
        name="mlstm_conv",
    )(*args)


def _gate_terms(graw, gb_ref, gbt_ref, tril, triu):
    sel = (lax.broadcasted_iota(jnp.int32, (SUBLANES, GATE_W), 0)
           == lax.broadcasted_iota(jnp.int32, (SUBLANES, GATE_W), 1)).astype(F32)
    g_c = graw + gb_ref[...]
    lf_c = _log_sigmoid(g_c)
    g_r = _dot_nt_exact(sel, graw) + gbt_ref[...]
    lf_r = _log_sigmoid(g_r)
    return g_c, lf_c, _dot_exact(tril, lf_c), g_r, _dot_exact(lf_r, triu)


def _head_scores(h, g_c, bt_c, g_r, bt_r, m_prev, mask):
    it_row = g_r[h:h + 1, :]
    bt_row = bt_r[MH + h:MH + h + 1, :]
    it_col = g_c[:, h:h + 1]
    bt_col = bt_c[:, MH + h:MH + h + 1]
    dmat = jnp.where(mask, bt_col - bt_row + it_row, NEG)
    inter = bt_col + m_prev
    mt = jnp.maximum(jnp.max(dmat, axis=-1, keepdims=True), inter)
    return jnp.exp(dmat - mt), jnp.exp(inter - mt), mt, it_col, bt_col


def _head_output(sqk, vb, s_inter, qc, qn, mt, o, mg):
    num = _dot(sqk.astype(BF16), vb) + s_inter * qc
    den = jnp.sum(sqk, axis=-1, keepdims=True) + s_inter * qn
    hm = jax.nn.sigmoid(o) * (num / jnp.maximum(jnp.abs(den), jnp.exp(-mt)))
    return _rms(hm) * mg


def _bcast_block_last(x, t, s):
    out = x
    for d in range(1, s):
        out = jnp.where(t == s - 1 - d, pltpu.roll(x, x.shape[0] - d, 0), out)
    return out


def _mlstm_conv_short_kernel(*refs, layer, nb, s):
    (q_ref, k_ref, v_ref, o_ref, gate_ref, bg_ref, cg_ref, hc_ref, gb_ref, gbt_ref, cw_ref, mg_ref,
     c0_ref, n0_ref, m0_ref, cb0_ref) = refs[:16]
    refs = refs[16:]
    if layer > 0:
        refs = refs[4:]
    oa_ref, oc_ref, c_ref, n_ref, m_ref, cb_ref, qc_scr, st_scr = refs
    if layer == 0:
        for ref in (c_ref, n_ref, m_ref, cb_ref):
            if ref.shape[0] > 1:
                ref[1:] = jnp.zeros((ref.shape[0] - 1,) + ref.shape[1:], F32)
        c_ref, n_ref, m_ref, cb_ref = c_ref.at[0], n_ref.at[0], m_ref.at[0], cb_ref.at[0]
    r = nb * s
    per_tile = SUBLANES // s
    shift = s.bit_length() - 1
    row = lax.broadcasted_iota(jnp.int32, (r, r), 0)
    col = lax.broadcasted_iota(jnp.int32, (r, r), 1)
    same = (row >> shift) == (col >> shift)
    mask = same & (col <= row)
    tril = mask.astype(F32)
    triu = (same & (row <= col)).astype(F32)
    expand = ((lax.broadcasted_iota(jnp.int32, (r, nb), 0) >> shift)
              == lax.broadcasted_iota(jnp.int32, (r, nb), 1)).astype(F32)
    gather = (lax.broadcasted_iota(jnp.int32, (nb, r), 0)
              == (lax.broadcasted_iota(jnp.int32, (nb, r), 1) >> shift)).astype(F32)
    t128 = lax.broadcasted_iota(jnp.int32, (r, DH), 0) & (s - 1)
    sub8 = lax.broadcasted_iota(jnp.int32, (SUBLANES, DH), 0) >> shift
    scale = DH ** -0.5

    g_c, lf_c, bt_c, g_r, bt_r = _gate_terms(gate_ref[...], gb_ref, gbt_ref, tril, triu)
    bt_last_c = _dot_exact(same.astype(F32), lf_c)
    m_rows = _dot_exact(expand, m0_ref[...])
    for h in range(MH):
        hs = slice(h * DH, (h + 1) * DH)
        m_prev = m_rows[:, h:h + 1]
        w, s_inter, mt, it_col, bt_col = _head_scores(h, g_c, bt_c, g_r, bt_r, m_prev, mask)
        qh = q_ref[:, hs]
        kh = k_ref[:, hs] * scale
        vh = v_ref[:, hs]
        qb = qh.astype(BF16)
        kb = kh.astype(BF16)
        sqk = _dot_nt(qb, kb) * w
        for j in range(r // SUBLANES):
            rows = slice(j * SUBLANES, (j + 1) * SUBLANES)
            acc = None
            q8 = q_ref[rows, hs].astype(BF16)
            for p in range(per_tile):
                part = _dot_nt(q8, c0_ref[j * per_tile + p, h].astype(BF16))
                acc = part if acc is None else jnp.where(sub8 == p, part, acc)
            qc_scr[rows, :] = acc
        n_rows = _dot_exact(expand, n0_ref[:, h, :])
        qn = jnp.sum(qh * n_rows, axis=-1, keepdims=True)
        oa_ref[:, hs] = _head_output(sqk, vh.astype(BF16), s_inter, qc_scr[...], qn, mt,
                                     o_ref[:, hs], mg_ref[:, hs])
        m_new = _bcast_block_last(jnp.broadcast_to(mt, (r, DH)), t128, s)
        bt_last = bt_last_c[:, MH + h:MH + h + 1]
        wk = jnp.exp(bt_last - bt_col + it_col - m_new)
        decay = jnp.exp(bt_last + m_prev - m_new)
        vw = vh * wk
        for j in range(r // SUBLANES):
            rows = slice(j * SUBLANES, (j + 1) * SUBLANES)
            k8 = (k_ref[rows, hs] * scale).astype(BF16)
            for p in range(per_tile):
                bidx = j * per_tile + p
                last = bidx * s + s - 1
                upd = _dot_tn(jnp.where(sub8 == p, vw[rows], 0.0).astype(BF16), k8)
                c_ref[bidx, h] = decay[last:last + 1, 0:1] * c0_ref[bidx, h] + upd
        st_scr[0] = decay
        st_scr[1] = m_new
        last_rows = pl.ds(s - 1, nb, stride=s)
        n_ref[:, h, :] = st_scr[0, last_rows, :] * n0_ref[:, h, :] + _dot_exact(gather, kh * wk)
        m_ref[:, h:h + 1] = st_scr[1, last_rows, :][:, 0:1]
    t256 = lax.broadcasted_iota(jnp.int32, (r, D_CONV), 0) & (s - 1)
    z = cg_ref[...] * hc_ref[...]
    cb_a = _dot_exact(expand, cb0_ref[:, 0, :])
    cb_b = _dot_exact(expand, cb0_ref[:, 1, :])
    z1 = jnp.where(t256 == 0, cb_b, pltpu.roll(z, 1, 0))
    z2 = jnp.where(t256 == 0, cb_a, jnp.where(t256 == 1, cb_b, pltpu.roll(z, 2, 0)))
    yc = cw_ref[0:1, :] * z2 + cw_ref[1:2, :] * z1 + cw_ref[2:3, :] * z
    oc_ref[...] = _rms(bg_ref[...] * yc) * mg_ref[:, D_MLSTM + D_S5:]
    for half in range(D_CONV // DH):
        lanes = slice(half * DH, (half + 1) * DH)
        st_scr[half] = z[:, lanes]
        cb_ref[:, 0, lanes] = st_scr[half, pl.ds(s - 2, nb, stride=s), :]
        cb_ref[:, 1, lanes] = st_scr[half, pl.ds(s - 1, nb, stride=s), :]


def _short_batch_block(b, depth, layer):
    out_layers = depth if layer == 0 else 1
    nb = min(b, 32)
    while nb > SUBLANES and 2 * (1 + out_layers) * nb * MH * DH * DH * 4 > STATE_VMEM_BUDGET:
        nb //= 2
    return nb


def _mlstm_conv_short(qkvo, gate, bch, w, states_in, states_prev, *, layer, depth, b, s):
    nb = _short_batch_block(b, depth, layer)
    assert SUBLANES % s == 0 and s >= CONV_W - 1 and b % nb == 0 and (nb * s) % SUBLANES == 0
    r = nb * s
    kern = functools.partial(_mlstm_conv_short_kernel, layer=layer, nb=nb, s=s)
    col = lambda wd, j: pl.BlockSpec((r, wd), lambda i: (i, j))
    st = lambda shape: pl.BlockSpec((None, nb) + shape, lambda i: (layer, i) + (0,) * len(shape))
    cst = lambda a: pl.BlockSpec(a.shape, lambda i: (0,) * a.ndim)
    state_specs = [st((MH, DH, DH)), st((MH, DH)), st((MH,)), st((CONV_W - 1, D_CONV))]
    out_state_specs = state_specs
    if layer == 0:
        st0 = lambda shape: pl.BlockSpec((depth, nb) + shape, lambda i: (0, i) + (0,) * len(shape))
        out_state_specs = [st0((MH, DH, DH)), st0((MH, DH)), st0((MH,)), st0((CONV_W - 1, D_CONV))]
    consts = (w["gb"], w["gbt"], w["cw"], w["mg"])
    args = [qkvo, qkvo, qkvo, qkvo, gate, bch, bch, bch, *consts, *states_in]
    in_specs = [col(D_MLSTM, 0), col(D_MLSTM, 1), col(D_MLSTM, 2), col(D_MLSTM, 3), col(GATE_W, 0),
                col(D_CONV, 0), col(D_CONV, 1), col(D_CONV, 2)] + [cst(a) for a in consts] + state_specs
    aliases = {}
    if layer > 0:
        aliases = {len(args) + j: 2 + j for j in range(4)}
        args += list(states_prev)
        in_specs += [_ANY] * 4
    return pl.pallas_call(
        kern,
        grid=(b // nb,),
        in_specs=in_specs,
        out_specs=[col(D_MLSTM, 0), col(D_CONV, 0)] + out_state_specs,
        out_shape=[jax.ShapeDtypeStruct((b * s, D_MLSTM), F32), jax.ShapeDtypeStruct((b * s, D_CONV), F32)]
        + _state_shapes(depth, b),
        scratch_shapes=[pltpu.VMEM((r, DH), F32), pltpu.VMEM((2, r, DH), F32)],
        input_output_aliases=aliases,
        compiler_params=pltpu.CompilerParams(dimension_semantics=("arbitrary",), vmem_limit_bytes=VMEM_LIMIT),
        name="mlstm_conv_short",
    )(*args)


def _s5_kernel(*refs, layer, nb, tt, zero_state, parts, batch_major):
    u_ref, lam_ref, bblk_ref, cblk_ref, d_ref, wglu_ref, mg_ref = refs[:7]
    refs = refs[7:]
    if not zero_state:
        hr0_ref, hi0_ref = refs[:2]
        refs = refs[2:]
    if layer > 0:
        refs = refs[2:]
    ob_ref, hr_ref, hi_ref, xs_ref, a_ref, bf_ref = refs[:6]
    if batch_major:
        ut_scr, ot_scr = refs[6:]
        ut_scr[...] = jnp.swapaxes(u_ref[...], 0, 1).reshape(tt * nb, D_S5)
        u_bm_ref, ob_bm_ref, u_ref, ob_ref = u_ref, ob_ref, ut_scr, ot_scr
    if layer == 0:
        hr_all, hi_all = hr_ref, hi_ref
        hr_ref, hi_ref = hr_all.at[0], hi_all.at[0]
    ti = pl.program_id(0)
    n = S5_N
    lc = 512

    @pl.when(ti == 0)
    def _():
        if layer == 0:
            hr_all[...] = jnp.zeros_like(hr_all)
            hi_all[...] = jnp.zeros_like(hi_all)
        hr_ref[...] = jnp.zeros_like(hr_ref) if zero_state else hr0_ref[...]
        hi_ref[...] = jnp.zeros_like(hi_ref) if zero_state else hi0_ref[...]
        lre = jnp.minimum(lam_ref[0:1, :], -1e-4)
        lim = lam_ref[1:2, :]
        dt = jnp.exp(lam_ref[2:3, :])
        mag = jnp.exp(lre * dt)
        ab_re = mag * jnp.cos(lim * dt)
        ab_im = mag * jnp.sin(lim * dt)
        den = lre * lre + lim * lim
        nr = ab_re - 1.0
        fre = (nr * lre + ab_im * lim) / den
        fim = (ab_im * lre - nr * lim) / den
        a_ref[:, :n] = jnp.broadcast_to(ab_re, (SUBLANES, n))
        a_ref[:, n:] = jnp.broadcast_to(ab_im, (SUBLANES, n))
        b_re = bblk_ref[:, :n]
        b_im = bblk_ref[:, n:]
        bf_ref[:, :n] = (fre * b_re - fim * b_im).astype(BF16)
        bf_ref[:, n:] = (fre * b_im + fim * b_re).astype(BF16)

    rows_p = nb * tt // parts
    part_rows = [slice(p * rows_p, (p + 1) * rows_p) for p in range(parts)]
    xs_ref[...] = _dot(u_ref[...].astype(BF16), bf_ref[...])
    ar = [a_ref[:, c0:c0 + lc] for c0 in range(0, n, lc)]
    ai = [a_ref[:, n + c0:n + c0 + lc] for c0 in range(0, n, lc)]
    h, y = {}, []
    for p, rs in enumerate(part_rows):
        for rg in range(nb // SUBLANES):
            g8 = slice(rg * SUBLANES, (rg + 1) * SUBLANES)
            for ci, c0 in enumerate(range(0, n, lc)):
                if p == 0:
                    h[rg, ci] = (hr_ref[g8, c0:c0 + lc], hi_ref[g8, c0:c0 + lc])
            for t in range(p * tt // parts, (p + 1) * tt // parts):
                r8 = slice(t * nb + rg * SUBLANES, t * nb + (rg + 1) * SUBLANES)
                for ci, c0 in enumerate(range(0, n, lc)):
                    hr, hi = h[rg, ci]
                    nr_ = ar[ci] * hr - ai[ci] * hi + xs_ref[r8, c0:c0 + lc]
                    ni_ = ar[ci] * hi + ai[ci] * hr + xs_ref[r8, n + c0:n + c0 + lc]
                    xs_ref[r8, c0:c0 + lc] = nr_
                    xs_ref[r8, n + c0:n + c0 + lc] = ni_
                    h[rg, ci] = (nr_, ni_)
            for ci, c0 in enumerate(range(0, n, lc)):
                if p == parts - 1:
                    hr_ref[g8, c0:c0 + lc] = h[rg, ci][0]
                    hi_ref[g8, c0:c0 + lc] = h[rg, ci][1]
        y.append(_dot(xs_ref[rs, :].astype(BF16), cblk_ref[...]) + d_ref[...] * u_ref[rs, :])
    g = [jax.nn.gelu(v) for v in y]
    gate = [_dot(v.astype(BF16), wglu_ref[...]) for v in g]
    for rs, v, z in zip(part_rows, g, gate):
        ob_ref[rs, :] = _rms(v * jax.nn.sigmoid(z)) * mg_ref[:, D_MLSTM:D_MLSTM + D_S5]
    if batch_major:
        ob_bm_ref[...] = jnp.swapaxes(ot_scr[...].reshape(tt, nb, D_S5), 0, 1)


def _s5(u, w, states_in, states_prev, *, layer, depth, nb, tt):
    batch_major = u.ndim == 3
    rows = u.shape[0] * u.shape[1] if batch_major else u.shape[0]
    rt = nb * tt
    zero_state = states_in is None
    parts = S5_PARTS if tt % (SUBLANES * S5_PARTS) == 0 else 1
    kern = functools.partial(_s5_kernel, layer=layer, nb=nb, tt=tt, zero_state=zero_state, parts=parts,
                             batch_major=batch_major)
    if batch_major:
        assert u.shape[0] == nb and tt % SUBLANES == 0
        io_spec = pl.BlockSpec((nb, tt, D_S5), lambda i: (0, i, 0))
        io_shape = jax.ShapeDtypeStruct(u.shape, F32)
        io_scratch = [pltpu.VMEM((rt, D_S5), F32), pltpu.VMEM((rt, D_S5), F32)]
    else:
        io_spec = pl.BlockSpec((rt, D_S5), lambda i: (i, 0))
        io_shape = jax.ShapeDtypeStruct((rows, D_S5), F32)
        io_scratch = []
    cst = lambda a: pl.BlockSpec(a.shape, lambda i: (0,) * a.ndim)
    st = pl.BlockSpec((None, nb, S5_N), lambda i: (layer, 0, 0))
    st_out = pl.BlockSpec((depth, nb, S5_N), lambda i: (0, 0, 0)) if layer == 0 else st
    consts = (w["lam"], w["bblk"], w["cblk"], w["d"], w["wglu"], w["mg"])
    args = [u, *consts]
    in_specs = [io_spec] + [cst(a) for a in consts]
    if not zero_state:
        args += list(states_in)
        in_specs += [st, st]
    aliases = {}
    if layer > 0:
        aliases = {len(args) + j: 1 + j for j in range(2)}
        args += list(states_prev)
        in_specs += [_ANY] * 2
    return pl.pallas_call(
        kern,
        grid=(rows // rt,),
        in_specs=in_specs,
        out_specs=[io_spec, st_out, st_out],
        out_shape=[io_shape, jax.ShapeDtypeStruct((depth, nb, S5_N), F32),
                   jax.ShapeDtypeStruct((depth, nb, S5_N), F32)],
        scratch_shapes=[pltpu.VMEM((rt, 2 * S5_N), F32), pltpu.VMEM((SUBLANES, 2 * S5_N), F32),
                        pltpu.VMEM((D_S5, 2 * S5_N), BF16)] + io_scratch,
        input_output_aliases=aliases,
        compiler_params=pltpu.CompilerParams(dimension_semantics=("arbitrary",), vmem_limit_bytes=VMEM_LIMIT),
        name="s5",
    )(*args)


def _block_diag(w):
    g, r, c = w.shape
    eye = jnp.eye(g, dtype=w.dtype)
    return jnp.einsum("grc,gh->grhc", w, eye).reshape(g * r, g * c)


def _layer_weights(l, ig_bias, fg_bias, s5_a_re, s5_a_im, s5_log_dt, s5_b_re, s5_b_im, s5_c_re, s5_c_im, s5_d,
                   w_glu, conv_w, mix_g):
    gb = jnp.concatenate([ig_bias[l], fg_bias[l], jnp.zeros((GATE_W - 2 * MH,), F32)])
    lam = jnp.stack([s5_a_re[l].reshape(-1), s5_a_im[l].reshape(-1),
                     jnp.repeat(s5_log_dt[l], S5_P)])
    bblk = jnp.concatenate([_block_diag(jnp.swapaxes(s5_b_re[l], 1, 2)),
                            _block_diag(jnp.swapaxes(s5_b_im[l], 1, 2))], axis=1)
    cblk = jnp.concatenate([_block_diag(jnp.swapaxes(s5_c_re[l], 1, 2)),
                            -_block_diag(jnp.swapaxes(s5_c_im[l], 1, 2))], axis=0).astype(BF16)
    return dict(
        gb=gb.reshape(1, GATE_W), gbt=gb[:SUBLANES].reshape(SUBLANES, 1),
        lam=lam, bblk=bblk, cblk=cblk, d=s5_d[l].reshape(1, D_S5), wglu=w_glu[l].astype(BF16),
        cw=conv_w[l], mg=mix_g[l].reshape(1, D_MODEL))


def _stacked_weights(ffn1_w1, ffn1_w3, ffn1_w2, ffn2_w1, ffn2_w3, ffn2_w2, norm_g, w_in, w_out):
    o = 4 * D_MLSTM
    win = (w_in[:, :, :o].astype(BF16),
           jnp.pad(w_in[:, :, o:o + 2 * MH].astype(BF16), ((0, 0), (0, 0), (0, GATE_W - 2 * MH))),
           w_in[:, :, o + 2 * MH:].astype(BF16))
    return dict(f1=(ffn1_w1.astype(BF16), ffn1_w3.astype(BF16), ffn1_w2.astype(BF16)),
                f2=(ffn2_w1.astype(BF16), ffn2_w3.astype(BF16), ffn2_w2.astype(BF16)),
                g=norm_g, win=win, wo=w_out.astype(BF16))


def _run_group(x, states, weights, big, *, tm, tt, chunk=None, bb=None):
    b, s, _ = x.shape
    depth = len(weights)
    xf = x.reshape(b * s, D_MODEL)
    mstates = sstates = None
    if states is not None:
        c_all, n_all, m_all, sr_all, si_all, cb_all = states
        ms_in = (c_all, n_all, m_all, cb_all)
        ss_in = (sr_all.reshape(depth, b, S5_N), si_all.reshape(depth, b, S5_N))
    else:
        ms_in = ss_in = None
    for l, w in enumerate(weights):
        x1, qkvo, gate, u, bch = _ffn_inproj(xf, big["g"], *big["f1"], big["win"], tm, l)
        if tt % SUBLANES == 0:
            ob, *sstates = _s5(u.reshape(b, s, D_S5), w, ss_in, sstates, layer=l, depth=depth, nb=b, tt=tt)
            ob = ob.reshape(b * s, D_S5)
        else:
            u_tm = jnp.swapaxes(u.reshape(b, s, D_S5), 0, 1).reshape(s * b, D_S5)
            ob_tm, *sstates = _s5(u_tm, w, ss_in, sstates, layer=l, depth=depth, nb=b, tt=tt)
            ob = jnp.swapaxes(ob_tm.reshape(s, b, D_S5), 0, 1).reshape(b * s, D_S5)
        if chunk is not None:
            r3 = lambda a: a.reshape(b, s, a.shape[-1])
            oa, oc, *mstates = _mlstm_conv(r3(qkvo), r3(gate), r3(bch), w, ms_in, mstates,
                                           layer=l, depth=depth, bb=bb, chunk=chunk)
            xf = _outproj_ffn(x1, oa.reshape(b * s, D_MLSTM), ob, oc.reshape(b * s, D_CONV),
                              big["g"], big["wo"], *big["f2"], tm, l)
        else:
            oa, oc, *mstates = _mlstm_conv_short(qkvo, gate, bch, w, ms_in, mstates,
                                                 layer=l, depth=depth, b=b, s=s)
            xf = _outproj_ffn(x1, oa, ob, oc, big["g"], big["wo"], *big["f2"], tm, l)
    c1, n1, m1, cb1 = mstates
    sr1, si1 = sstates
    return xf.reshape(b, s, D_MODEL), (c1, n1, m1, sr1.reshape(depth, b, S5_G, S5_P),
                                       si1.reshape(depth, b, S5_G, S5_P), cb1)


def kernel(x_prompt, x_sample, state_mlstm_C, state_mlstm_n, state_mlstm_m, state_s5_re, state_s5_im,
           state_conv, ffn1_w1, ffn1_w3, ffn1_w2, ffn2_w1, ffn2_w3, ffn2_w2, norm_g, w_in, ig_bias, fg_bias,
           s5_a_re, s5_a_im, s5_log_dt, s5_b_re, s5_b_im, s5_c_re, s5_c_im, s5_d, w_glu, conv_w, mix_g, w_out):
    depth = norm_g.shape[0]
    params = (ig_bias, fg_bias, s5_a_re, s5_a_im, s5_log_dt, s5_b_re, s5_b_im, s5_c_re, s5_c_im, s5_d, w_glu,
              conv_w, mix_g)
    weights = [_layer_weights(l, *params) for l in range(depth)]
    big = _stacked_weights(ffn1_w1, ffn1_w3, ffn1_w2, ffn2_w1, ffn2_w3, ffn2_w2, norm_g, w_in, w_out)
    y_p, st_p = _run_group(x_prompt, None, weights, big, tm=512, tt=256, chunk=128, bb=4)
    y_s, st_s = _run_group(
        x_sample, (state_mlstm_C, state_mlstm_n, state_mlstm_m, state_s5_re, state_s5_im, state_conv),
        weights, big, tm=512, tt=x_sample.shape[1])
    return (y_p, y_s, *st_p, *st_s)
```

```python
import functools

import jax
import jax.numpy as jnp
from jax import lax
from jax.experimental import pallas as pl
from jax.experimental.pallas import tpu as pltpu

F32 = jnp.float32
BF16 = jnp.bfloat16

D_MODEL = 1024
MH = 4
DH = 128
D_MLSTM = MH * DH
S5_CH = 16
S5_G = 16
S5_P = 64
D_S5 = S5_G * S5_CH
S5_N = S5_G * S5_P
D_CONV = 256
CONV_W = 3
EPS = 1e-6
GATE_W = 128
SUBLANES = 8
NEG = -1e30
VMEM_LIMIT = 56 * 1024 * 1024
STATE_VMEM_BUDGET = 36 * 1024 * 1024
ROW_PARTS = 2
S5_PARTS = 2


def _dot(a, b):
    return jnp.dot(a, b, preferred_element_type=F32)


def _dot_nt(a, b):
    return lax.dot_general(a, b, (((1,), (1,)), ((), ())), preferred_element_type=F32)


def _dot_tn(a, b):
    return lax.dot_general(a, b, (((0,), (0,)), ((), ())), preferred_element_type=F32)


def _dot_exact(a, b):
    return jnp.dot(a, b, preferred_element_type=F32, precision=lax.Precision.HIGHEST)


def _dot_nt_exact(a, b):
    return lax.dot_general(a, b, (((1,), (1,)), ((), ())), preferred_element_type=F32,
                           precision=lax.Precision.HIGHEST)


def _rms(x):
    return x * lax.rsqrt(jnp.mean(x * x, axis=-1, keepdims=True) + EPS)


def _log_sigmoid(x):
    return jnp.minimum(x, 0.0) - jnp.log(1.0 + jnp.exp(-jnp.abs(x)))


def _split_bf16(x, parts):
    out = []
    for _ in range(parts):
        p = x.astype(BF16)
        out.append(p)
        x = x - p.astype(F32)
    return jnp.concatenate(out, axis=1)


def _ffn_residual(xs, g_pre, g_post, w1, w3, w2):
    xn = [(_rms(x) * g_pre).astype(BF16) for x in xs]
    h1 = [_dot(v, w1[...]) for v in xn]
    h3 = [_dot(v, w3[...]) for v in xn]
    a = [(jax.nn.silu(p) * q).astype(BF16) for p, q in zip(h1, h3)]
    y = [_dot(v, w2[...]) for v in a]
    return [x + 0.5 * (_rms(t) * g_post) for x, t in zip(xs, y)]


def _row_parts(tm, parts):
    step = tm // parts
    return [slice(i * step, (i + 1) * step) for i in range(parts)]


def _layer_spec(a, layer):
    return pl.BlockSpec((None,) + a.shape[1:], lambda *_: (layer, 0, 0), pipeline_mode=pl.Buffered(1))


_ANY = pl.BlockSpec(memory_space=pl.ANY)


def _ffn_inproj_kernel(x_ref, g_ref, w1_ref, w3_ref, w2_ref, wq_ref, wg_ref, wr_ref,
                       x1_ref, qkvo_ref, gate_ref, u_ref, bch_ref):
    parts = _row_parts(x_ref.shape[0], ROW_PARTS)
    x1 = _ffn_residual([x_ref[r, :] for r in parts], g_ref[0:1, :], g_ref[1:2, :], w1_ref, w3_ref, w2_ref)
    hn = [(_rms(v) * g_ref[2:3, :]).astype(BF16) for v in x1]
    qkvo = [_dot(v, wq_ref[...]) for v in hn]
    gate = [_dot(v, wg_ref[...]) for v in hn]
    rest = [_dot(v, wr_ref[...]) for v in hn]
    for i, r in enumerate(parts):
        x1_ref[r, :] = x1[i]
        qkvo_ref[r, :] = qkvo[i]
        gate_ref[r, :] = gate[i]
        u_ref[r, :] = rest[i][:, :D_S5]
        bch_ref[r, :] = rest[i][:, D_S5:]


def _ffn_inproj(x, g, w1, w3, w2, win, tm, layer):
    t = x.shape[0]
    row = lambda w: pl.BlockSpec((tm, w), lambda i: (i, 0))
    return pl.pallas_call(
        _ffn_inproj_kernel,
        grid=(t // tm,),
        in_specs=[row(D_MODEL), _layer_spec(g, layer), _layer_spec(w1, layer), _layer_spec(w3, layer),
                  _layer_spec(w2, layer)] + [_layer_spec(a, layer) for a in win],
        out_specs=[row(D_MODEL), row(4 * D_MLSTM), row(GATE_W), row(D_S5), row(3 * D_CONV)],
        out_shape=[jax.ShapeDtypeStruct((t, D_MODEL), F32), jax.ShapeDtypeStruct((t, 4 * D_MLSTM), F32),
                   jax.ShapeDtypeStruct((t, GATE_W), F32), jax.ShapeDtypeStruct((t, D_S5), F32),
                   jax.ShapeDtypeStruct((t, 3 * D_CONV), F32)],
        compiler_params=pltpu.CompilerParams(dimension_semantics=("arbitrary",), vmem_limit_bytes=VMEM_LIMIT),
        name="ffn_inproj",
    )(x, g, w1, w3, w2, *win)


def _outproj_ffn_kernel(x_ref, a_ref, b_ref, c_ref, g_ref, wo_ref, w1_ref, w3_ref, w2_ref, y_ref):
    parts = _row_parts(x_ref.shape[0], ROW_PARTS)
    mo = [(_dot(a_ref[r, :].astype(BF16), wo_ref[0:D_MLSTM, :])
           + _dot(b_ref[r, :].astype(BF16), wo_ref[D_MLSTM:D_MLSTM + D_S5, :])
           + _dot(c_ref[r, :].astype(BF16), wo_ref[D_MLSTM + D_S5:, :])) for r in parts]
    x2 = [x_ref[r, :] + _rms(v) * g_ref[3:4, :] for r, v in zip(parts, mo)]
    y = _ffn_residual(x2, g_ref[4:5, :], g_ref[5:6, :], w1_ref, w3_ref, w2_ref)
    for r, v in zip(parts, y):
        y_ref[r, :] = v


def _outproj_ffn(x, oa, ob, oc, g, wo, w1, w3, w2, tm, layer):
    t = x.shape[0]
    row = lambda w: pl.BlockSpec((tm, w), lambda i: (i, 0))
    return pl.pallas_call(
        _outproj_ffn_kernel,
        grid=(t // tm,),
        in_specs=[row(D_MODEL), row(D_MLSTM), row(D_S5), row(D_CONV), _layer_spec(g, layer),
                  _layer_spec(wo, layer), _layer_spec(w1, layer), _layer_spec(w3, layer),
                  _layer_spec(w2, layer)],
        out_specs=row(D_MODEL),
        out_shape=jax.ShapeDtypeStruct((t, D_MODEL), F32),
        compiler_params=pltpu.CompilerParams(dimension_semantics=("arbitrary",), vmem_limit_bytes=VMEM_LIMIT),
        name="outproj_ffn",
    )(x, oa, ob, oc, g, wo, w1, w3, w2)


def _state_shapes(depth, b):
    return [jax.ShapeDtypeStruct((depth, b, MH, DH, DH), F32), jax.ShapeDtypeStruct((depth, b, MH, DH), F32),
            jax.ShapeDtypeStruct((depth, b, MH), F32), jax.ShapeDtypeStruct((depth, b, CONV_W - 1, D_CONV), F32)]


def _cummax_rows(x, rowi):
    s = 1
    while s < x.shape[0]:
        x = jnp.maximum(x, jnp.where(rowi >= s, pltpu.roll(x, s, 0), NEG))
        s *= 2
    return x


def _mlstm_conv_kernel(*refs, layer, bb, chunk, zero_state):
    (q_ref, k_ref, v_ref, o_ref, gate_ref, bg_ref, cg_ref, hc_ref, gb_ref, cw_ref, mg_ref) = refs[:11]
    refs = refs[11:]
    if not zero_state:
        c0_ref, n0_ref, m0_ref, cb0_ref = refs[:4]
        refs = refs[4:]
    if layer > 0:
        refs = refs[4:]
    oa_ref, oc_ref, c_ref, n_ref, m_ref, cb_ref, m_scr = refs
    other_layers = ()
    if layer == 0:
        other_layers = tuple(r.at[1:] for r in (c_ref, n_ref, m_ref, cb_ref) if r.shape[0] > 1)
        c_ref, n_ref, m_ref, cb_ref = c_ref.at[0], n_ref.at[0], m_ref.at[0], cb_ref.at[0]
    assert chunk == DH
    bi0 = pl.program_id(0) * bb
    ci = pl.program_id(1)

    @pl.when(ci == 0)
    def _():
        for r in other_layers:
            r[...] = jnp.zeros_like(r)
        m_scr[...] = jnp.zeros_like(m_scr)
        if zero_state:
            c_ref[...] = jnp.zeros_like(c_ref)
            n_ref[...] = jnp.zeros_like(n_ref)
            cb_ref[...] = jnp.zeros_like(cb_ref)
        else:
            c_ref[...] = c0_ref[...]
            n_ref[...] = n0_ref[...]
            cb_ref[...] = cb0_ref[...]
            for bi in range(bb):
                m_scr[bi:bi + 1, 0:MH] = m0_ref[pl.ds(bi0 + bi, 1), :]

    row = lax.broadcasted_iota(jnp.int32, (chunk, chunk), 0)
    col = lax.broadcasted_iota(jnp.int32, (chunk, chunk), 1)
    causal = col <= row
    tril = jnp.where(causal, 1.0, 0.0).astype(BF16)
    rowg = lax.broadcasted_iota(jnp.int32, (chunk, GATE_W), 0)
    rowi = lax.broadcasted_iota(jnp.int32, (chunk, D_CONV), 0)
    spread = jnp.where((lax.broadcasted_iota(jnp.int32, (2 * GATE_W, D_MLSTM), 0) & (GATE_W - 1))
                       == (lax.broadcasted_iota(jnp.int32, (2 * GATE_W, D_MLSTM), 1) >> 7), 1.0, 0.0).astype(BF16)
    scale = DH ** -0.5

    units = [(bi, h) for bi in range(bb) for h in range(MH)]
    hsl = lambda h: slice(h * DH, (h + 1) * DH)
    g_c = [gate_ref[bi] + gb_ref[...] for bi in range(bb)]
    csum = [_dot(tril, _split_bf16(pltpu.roll(_log_sigmoid(g), GATE_W - MH, 1), 3)) for g in g_c]
    qb = {u: q_ref[u[0], :, hsl(u[1])].astype(BF16) for u in units}
    kb = {u: (k_ref[u[0], :, hsl(u[1])] * scale).astype(BF16) for u in units}
    s_qk = {u: _dot_nt(qb[u], kb[u]) for u in units}
    inter = {u: _dot_nt(qb[u], jnp.concatenate(
        [c_ref[u[0], u[1]].astype(BF16),
         jnp.broadcast_to(n_ref[u[0], u[1]:u[1] + 1, :], (DH, DH)).astype(BF16)], axis=0)) for u in units}
    rep, a_r, decay = [], [], []
    for bi in range(bb):
        f_c = csum[bi][:, :GATE_W] + csum[bi][:, GATE_W:2 * GATE_W] + csum[bi][:, 2 * GATE_W:]
        a_c = g_c[bi] - f_c
        m_prev = m_scr[bi:bi + 1, :]
        big_m = jnp.maximum(_cummax_rows(a_c, rowg), m_prev)
        mt = f_c + big_m
        m_new = mt[chunk - 1:chunk, :]
        f_last = f_c[chunk - 1:chunk, :]
        decay.append(jnp.exp(f_last + m_prev - m_new))
        stacked = jnp.concatenate([_split_bf16(-big_m, 2), _split_bf16(jnp.exp(m_prev - big_m), 2),
                                   _split_bf16(-mt, 2), _split_bf16(jnp.exp(a_c + (f_last - m_new)), 2)],
                                  axis=0)
        rep.append(_dot(stacked, spread))
        a_r.append(a_c.T[0:SUBLANES, :])
        m_scr[bi:bi + 1, :] = m_new
        m_ref[pl.ds(bi0 + bi, 1), :] = m_new[:, 0:MH]
    intra, rsum = {}, {}
    for u in units:
        bi, h = u
        w = jnp.exp(jnp.where(causal, rep[bi][0:chunk, hsl(h)] + a_r[bi][h:h + 1, :], NEG))
        sqk = s_qk[u] * w
        rsum[u] = jnp.sum(sqk, axis=-1, keepdims=True)
        intra[u] = _dot(sqk.astype(BF16), v_ref[bi, :, hsl(h)].astype(BF16))
    hm, ssq = {}, {}
    for u in units:
        bi, h = u
        s_inter = rep[bi][chunk:2 * chunk, hsl(h)]
        e_floor = jnp.exp(rep[bi][2 * chunk:3 * chunk, hsl(h)])
        num = intra[u] + s_inter * inter[u][:, :DH]
        den = rsum[u] + s_inter * inter[u][:, DH:]
        hm[u] = jax.nn.sigmoid(o_ref[bi, :, hsl(h)]) * (num / jnp.maximum(jnp.abs(den), e_floor))
        ssq[u] = jnp.sum(hm[u] * hm[u], axis=-1, keepdims=True)
    for u in units:
        bi, h = u
        oa_ref[bi, :, hsl(h)] = hm[u] * lax.rsqrt(ssq[u] * (1.0 / DH) + EPS) * mg_ref[:, hsl(h)]
        wk = rep[bi][3 * chunk:, hsl(h)]
        dec = decay[bi][:, h:h + 1]
        c_ref[bi, h] = dec * c_ref[bi, h] + _dot_tn((v_ref[bi, :, hsl(h)] * wk).astype(BF16), kb[u])
        n_ref[bi, h:h + 1, :] = (dec * n_ref[bi, h:h + 1, :]
                                 + jnp.sum(k_ref[bi, :, hsl(h)] * scale * wk, axis=0, keepdims=True))
    for bi in range(bb):
        z = cg_ref[bi] * hc_ref[bi]
        cb = cb_ref[bi]
        z1 = jnp.where(rowi == 0, cb[1:2, :], pltpu.roll(z, 1, 0))
        z2 = jnp.where(rowi == 0, cb[0:1, :], jnp.where(rowi == 1, cb[1:2, :], pltpu.roll(z, 2, 0)))
        yc = cw_ref[0:1, :] * z2 + cw_ref[1:2, :] * z1 + cw_ref[2:3, :] * z
        oc_ref[bi] = _rms(bg_ref[bi] * yc) * mg_ref[:, D_MLSTM + D_S5:]
        cb_ref[bi] = z[chunk - 2:chunk, :]


def _mlstm_conv(qkvo, gate, bch, w, states_in, states_prev, *, layer, depth, bb, chunk):
    b, s, _ = qkvo.shape
    assert s % chunk == 0 and b % bb == 0
    zero_state = states_in is None
    kern = functools.partial(_mlstm_conv_kernel, layer=layer, bb=bb, chunk=chunk, zero_state=zero_state)
    col = lambda wd, j: pl.BlockSpec((bb, chunk, wd), lambda i, c: (i, c, j))
    st = lambda shape: pl.BlockSpec((None, bb) + shape, lambda i, c: (layer, i) + (0,) * len(shape))
    m_spec = pl.BlockSpec((None, b, MH), lambda i, c: (layer, 0, 0))
    cst = lambda a: pl.BlockSpec(a.shape, lambda i, c: (0,) * a.ndim)
    state_specs = [st((MH, DH, DH)), st((MH, DH)), m_spec, st((CONV_W - 1, D_CONV))]
    out_state_specs = state_specs
    if layer == 0:
        st0 = lambda shape: pl.BlockSpec((depth, bb) + shape, lambda i, c: (0, i) + (0,) * len(shape))
        out_state_specs = [st0((MH, DH, DH)), st0((MH, DH)), pl.BlockSpec((depth, b, MH), lambda i, c: (0, 0, 0)),
                           st0((CONV_W - 1, D_CONV))]
    consts = (w["gb"], w["cw"], w["mg"])
    args = [qkvo, qkvo, qkvo, qkvo, gate, bch, bch, bch, *consts]
    in_specs = [col(D_MLSTM, 0), col(D_MLSTM, 1), col(D_MLSTM, 2), col(D_MLSTM, 3), col(GATE_W, 0),
                col(D_CONV, 0), col(D_CONV, 1), col(D_CONV, 2)] + [cst(a) for a in consts]
    if not zero_state:
        args += list(states_in)
        in_specs += state_specs
    aliases = {}
    if layer > 0:
        aliases = {len(args) + j: 2 + j for j in range(4)}
        args += list(states_prev)
        in_specs += [_ANY] * 4
    return pl.pallas_call(
        kern,
        grid=(b // bb, s // chunk),
        in_specs=in_specs,
        out_specs=[col(D_MLSTM, 0), col(D_CONV, 0)] + out_state_specs,
        out_shape=[jax.ShapeDtypeStruct((b, s, D_MLSTM), F32), jax.ShapeDtypeStruct((b, s, D_CONV), F32)]
        + _state_shapes(depth, b),
        scratch_shapes=[pltpu.VMEM((-(-bb // SUBLANES) * SUBLANES, GATE_W), F32)],
        input_output_aliases=aliases,
        compiler_params=pltpu.CompilerParams(dimension_semantics=("arbitrary", "arbitrary"),
                                             vmem_limit_bytes=VMEM_LIMIT),
        name="mlstm_conv",
    )(*args)


def _gate_terms(graw, gb_ref, gbt_ref, tril, triu):
    sel = (lax.broadcasted_iota(jnp.int32, (SUBLANES, GATE_W), 0)
           == lax.broadcasted_iota(jnp.int32, (SUBLANES, GATE_W), 1)).astype(F32)
    g_c = graw + gb_ref[...]
    lf_c = _log_sigmoid(g_c)
    g_r = _dot_nt_exact(sel, graw) + gbt_ref[...]
    lf_r = _log_sigmoid(g_r)
    return g_c, lf_c, _dot_exact(tril, lf_c), g_r, _dot_exact(lf_r, triu)


def _head_scores(h, g_c, bt_c, g_r, bt_r, m_prev, mask):
    it_row = g_r[h:h + 1, :]
    bt_row = bt_r[MH + h:MH + h + 1, :]
    it_col = g_c[:, h:h + 1]
    bt_col = bt_c[:, MH + h:MH + h + 1]
    dmat = jnp.where(mask, bt_col - bt_row + it_row, NEG)
    inter = bt_col + m_prev
    mt = jnp.maximum(jnp.max(dmat, axis=-1, keepdims=True), inter)
    return jnp.exp(dmat - mt), jnp.exp(inter - mt), mt, it_col, bt_col


def _head_output(sqk, vb, s_inter, qc, qn, mt, o, mg):
    num = _dot(sqk.astype(BF16), vb) + s_inter * qc
    den = jnp.sum(sqk, axis=-1, keepdims=True) + s_inter * qn
    hm = jax.nn.sigmoid(o) * (num / jnp.maximum(jnp.abs(den), jnp.exp(-mt)))
    return _rms(hm) * mg


def _bcast_block_last(x, t, s):
    out = x
    for d in range(1, s):
        out = jnp.where(t == s - 1 - d, pltpu.roll(x, x.shape[0] - d, 0), out)
    return out


def _mlstm_conv_short_kernel(*refs, layer, nb, s):
    (q_ref, k_ref, v_ref, o_ref, gate_ref, bg_ref, cg_ref, hc_ref, gb_ref, gbt_ref, cw_ref, mg_ref,
     c0_ref, n0_ref, m0_ref, cb0_ref) = refs[:16]
    refs = refs[16:]
    if layer > 0:
        refs = refs[4:]
    oa_ref, oc_ref, c_ref, n_ref, m_ref, cb_ref, qc_scr, st_scr = refs
    if layer == 0:
        for ref in (c_ref, n_ref, m_ref, cb_ref):
            if ref.shape[0] > 1:
                ref[1:] = jnp.zeros((ref.shape[0] - 1,) + ref.shape[1:], F32)
        c_ref, n_ref, m_ref, cb_ref = c_ref.at[0], n_ref.at[0], m_ref.at[0], cb_ref.at[0]
    r = nb * s
    per_tile = SUBLANES // s
    shift = s.bit_length() - 1
    row = lax.broadcasted_iota(jnp.int32, (r, r), 0)
    col = lax.broadcasted_iota(jnp.int32, (r, r), 1)
    same = (row >> shift) == (col >> shift)
    mask = same & (col <= row)
    tril = mask.astype(F32)
    triu = (same & (row <= col)).astype(F32)
    expand = ((lax.broadcasted_iota(jnp.int32, (r, nb), 0) >> shift)
              == lax.broadcasted_iota(jnp.int32, (r, nb), 1)).astype(F32)
    gather = (lax.broadcasted_iota(jnp.int32, (nb, r), 0)
              == (lax.broadcasted_iota(jnp.int32, (nb, r), 1) >> shift)).astype(F32)
    t128 = lax.broadcasted_iota(jnp.int32, (r, DH), 0) & (s - 1)
    sub8 = lax.broadcasted_iota(jnp.int32, (SUBLANES, DH), 0) >> shift
    scale = DH ** -0.5

    g_c, lf_c, bt_c, g_r, bt_r = _gate_terms(gate_ref[...], gb_ref, gbt_ref, tril, triu)
    bt_last_c = _dot_exact(same.astype(F32), lf_c)
    m_rows = _dot_exact(expand, m0_ref[...])
    for h in range(MH):
        hs = slice(h * DH, (h + 1) * DH)
        m_prev = m_rows[:, h:h + 1]
        w, s_inter, mt, it_col, bt_col = _head_scores(h, g_c, bt_c, g_r, bt_r, m_prev, mask)
        qh = q_ref[:, hs]
        kh = k_ref[:, hs] * scale
        vh = v_ref[:, hs]
        qb = qh.astype(BF16)
        kb = kh.astype(BF16)
        sqk = _dot_nt(qb, kb) * w
        for j in range(r // SUBLANES):
            rows = slice(j * SUBLANES, (j + 1) * SUBLANES)
            acc = None
            q8 = q_ref[rows, hs].astype(BF16)
            for p in range(per_tile):
                part = _dot_nt(q8, c0_ref[j * per_tile + p, h].astype(BF16))
                acc = part if acc is None else jnp.where(sub8 == p, part, acc)
            qc_scr[rows, :] = acc
        n_rows = _dot_exact(expand, n0_ref[:, h, :])
        qn = jnp.sum(qh * n_rows, axis=-1, keepdims=True)
        oa_ref[:, hs] = _head_output(sqk, vh.astype(BF16), s_inter, qc_scr[...], qn, mt,
                                     o_ref[:, hs], mg_ref[:, hs])
        m_new = _bcast_block_last(jnp.broadcast_to(mt, (r, DH)), t128, s)
        bt_last = bt_last_c[:, MH + h:MH + h + 1]
        wk = jnp.exp(bt_last - bt_col + it_col - m_new)
        decay = jnp.exp(bt_last + m_prev - m_new)
        vw = vh * wk
        for j in range(r // SUBLANES):
            rows = slice(j * SUBLANES, (j + 1) * SUBLANES)
            k8 = (k_ref[rows, hs] * scale).astype(BF16)
            for p in range(per_tile):
                bidx = j * per_tile + p
                last = bidx * s + s - 1
                upd = _dot_tn(jnp.where(sub8 == p, vw[rows], 0.0).astype(BF16), k8)
                c_ref[bidx, h] = decay[last:last + 1, 0:1] * c0_ref[bidx, h] + upd
        st_scr[0] = decay
        st_scr[1] = m_new
        last_rows = pl.ds(s - 1, nb, stride=s)
        n_ref[:, h, :] = st_scr[0, last_rows, :] * n0_ref[:, h, :] + _dot_exact(gather, kh * wk)
        m_ref[:, h:h + 1] = st_scr[1, last_rows, :][:, 0:1]
    t256 = lax.broadcasted_iota(jnp.int32, (r, D_CONV), 0) & (s - 1)
    z = cg_ref[...] * hc_ref[...]
    cb_a = _dot_exact(expand, cb0_ref[:, 0, :])
    cb_b = _dot_exact(expand, cb0_ref[:, 1, :])
    z1 = jnp.where(t256 == 0, cb_b, pltpu.roll(z, 1, 0))
    z2 = jnp.where(t256 == 0, cb_a, jnp.where(t256 == 1, cb_b, pltpu.roll(z, 2, 0)))
    yc = cw_ref[0:1, :] * z2 + cw_ref[1:2, :] * z1 + cw_ref[2:3, :] * z
    oc_ref[...] = _rms(bg_ref[...] * yc) * mg_ref[:, D_MLSTM + D_S5:]
    for half in range(D_CONV // DH):
        lanes = slice(half * DH, (half + 1) * DH)
        st_scr[half] = z[:, lanes]
        cb_ref[:, 0, lanes] = st_scr[half, pl.ds(s - 2, nb, stride=s), :]
        cb_ref[:, 1, lanes] = st_scr[half, pl.ds(s - 1, nb, stride=s), :]


def _short_batch_block(b, depth, layer):
    out_layers = depth if layer == 0 else 1
    nb = min(b, 32)
    while nb > SUBLANES and 2 * (1 + out_layers) * nb * MH * DH * DH * 4 > STATE_VMEM_BUDGET:
        nb //= 2
    return nb


def _mlstm_conv_short(qkvo, gate, bch, w, states_in, states_prev, *, layer, depth, b, s):
    nb = _short_batch_block(b, depth, layer)
    assert SUBLANES % s == 0 and s >= CONV_W - 1 and b % nb == 0 and (nb * s) % SUBLANES == 0
    r = nb * s
    kern = functools.partial(_mlstm_conv_short_kernel, layer=layer, nb=nb, s=s)
    col = lambda wd, j: pl.BlockSpec((r, wd), lambda i: (i, j))
    st = lambda shape: pl.BlockSpec((None, nb) + shape, lambda i: (layer, i) + (0,) * len(shape))
    cst = lambda a: pl.BlockSpec(a.shape, lambda i: (0,) * a.ndim)
    state_specs = [st((MH, DH, DH)), st((MH, DH)), st((MH,)), st((CONV_W - 1, D_CONV))]
    out_state_specs = state_specs
    if layer == 0:
        st0 = lambda shape: pl.BlockSpec((depth, nb) + shape, lambda i: (0, i) + (0,) * len(shape))
        out_state_specs = [st0((MH, DH, DH)), st0((MH, DH)), st0((MH,)), st0((CONV_W - 1, D_CONV))]
    consts = (w["gb"], w["gbt"], w["cw"], w["mg"])
    args = [qkvo, qkvo, qkvo, qkvo, gate, bch, bch, bch, *consts, *states_in]
    in_specs = [col(D_MLSTM, 0), col(D_MLSTM, 1), col(D_MLSTM, 2), col(D_MLSTM, 3), col(GATE_W, 0),
                col(D_CONV, 0), col(D_CONV, 1), col(D_CONV, 2)] + [cst(a) for a in consts] + state_specs
    aliases = {}
    if layer > 0:
        aliases = {len(args) + j: 2 + j for j in range(4)}
        args += list(states_prev)
        in_specs += [_ANY] * 4
    return pl.pallas_call(
        kern,
        grid=(b // nb,),
        in_specs=in_specs,
        out_specs=[col(D_MLSTM, 0), col(D_CONV, 0)] + out_state_specs,
        out_shape=[jax.ShapeDtypeStruct((b * s, D_MLSTM), F32), jax.ShapeDtypeStruct((b * s, D_CONV), F32)]
        + _state_shapes(depth, b),
        scratch_shapes=[pltpu.VMEM((r, DH), F32), pltpu.VMEM((2, r, DH), F32)],
        input_output_aliases=aliases,
        compiler_params=pltpu.CompilerParams(dimension_semantics=("arbitrary",), vmem_limit_bytes=VMEM_LIMIT),
        name="mlstm_conv_short",
    )(*args)


def _s5_kernel(*refs, layer, nb, tt, zero_state, parts, batch_major):
    u_ref, lam_ref, bblk_ref, cblk_ref, d_ref, wglu_ref, mg_ref = refs[:7]
    refs = refs[7:]
    if not zero_state:
        hr0_ref, hi0_ref = refs[:2]
        refs = refs[2:]
    if layer > 0:
        refs = refs[2:]
    ob_ref, hr_ref, hi_ref, xs_ref, a_ref, bf_ref = refs[:6]
    if batch_major:
        ut_scr, ot_scr = refs[6:]
        ut_scr[...] = jnp.swapaxes(u_ref[...], 0, 1).reshape(tt * nb, D_S5)
        u_bm_ref, ob_bm_ref, u_ref, ob_ref = u_ref, ob_ref, ut_scr, ot_scr
    if layer == 0:
        hr_all, hi_all = hr_ref, hi_ref
        hr_ref, hi_ref = hr_all.at[0], hi_all.at[0]
    ti = pl.program_id(0)
    n = S5_N
    lc = 512

    @pl.when(ti == 0)
    def _():
        if layer == 0:
            hr_all[...] = jnp.zeros_like(hr_all)
            hi_all[...] = jnp.zeros_like(hi_all)
        hr_ref[...] = jnp.zeros_like(hr_ref) if zero_state else hr0_ref[...]
        hi_ref[...] = jnp.zeros_like(hi_ref) if zero_state else hi0_ref[...]
        lre = jnp.minimum(lam_ref[0:1, :], -1e-4)
        lim = lam_ref[1:2, :]
        dt = jnp.exp(lam_ref[2:3, :])
        mag = jnp.exp(lre * dt)
        ab_re = mag * jnp.cos(lim * dt)
        ab_im = mag * jnp.sin(lim * dt)
        den = lre * lre + lim * lim
        nr = ab_re - 1.0
        fre = (nr * lre + ab_im * lim) / den
        fim = (ab_im * lre - nr * lim) / den
        a_ref[:, :n] = jnp.broadcast_to(ab_re, (SUBLANES, n))
        a_ref[:, n:] = jnp.broadcast_to(ab_im, (SUBLANES, n))
        b_re = bblk_ref[:, :n]
        b_im = bblk_ref[:, n:]
        bf_ref[:, :n] = (fre * b_re - fim * b_im).astype(BF16)
        bf_ref[:, n:] = (fre * b_im + fim * b_re).astype(BF16)

    rows_p = nb * tt // parts
    part_rows = [slice(p * rows_p, (p + 1) * rows_p) for p in range(parts)]
    xs_ref[...] = _dot(u_ref[...].astype(BF16), bf_ref[...])
    ar = [a_ref[:, c0:c0 + lc] for c0 in range(0, n, lc)]
    ai = [a_ref[:, n + c0:n + c0 + lc] for c0 in range(0, n, lc)]
    h, y = {}, []
    for p, rs in enumerate(part_rows):
        for rg in range(nb // SUBLANES):
            g8 = slice(rg * SUBLANES, (rg + 1) * SUBLANES)
            for ci, c0 in enumerate(range(0, n, lc)):
                if p == 0:
                    h[rg, ci] = (hr_ref[g8, c0:c0 + lc], hi_ref[g8, c0:c0 + lc])
            for t in range(p * tt // parts, (p + 1) * tt // parts):
                r8 = slice(t * nb + rg * SUBLANES, t * nb + (rg + 1) * SUBLANES)
                for ci, c0 in enumerate(range(0, n, lc)):
                    hr, hi = h[rg, ci]
                    nr_ = ar[ci] * hr - ai[ci] * hi + xs_ref[r8, c0:c0 + lc]
                    ni_ = ar[ci] * hi + ai[ci] * hr + xs_ref[r8, n + c0:n + c0 + lc]
                    xs_ref[r8, c0:c0 + lc] = nr_
                    xs_ref[r8, n + c0:n + c0 + lc] = ni_
                    h[rg, ci] = (nr_, ni_)
            for ci, c0 in enumerate(range(0, n, lc)):
                if p == parts - 1:
                    hr_ref[g8, c0:c0 + lc] = h[rg, ci][0]
                    hi_ref[g8, c0:c0 + lc] = h[rg, ci][1]
        y.append(_dot(xs_ref[rs, :].astype(BF16), cblk_ref[...]) + d_ref[...] * u_ref[rs, :])
    g = [jax.nn.gelu(v) for v in y]
    gate = [_dot(v.astype(BF16), wglu_ref[...]) for v in g]
    for rs, v, z in zip(part_rows, g, gate):
        ob_ref[rs, :] = _rms(v * jax.nn.sigmoid(z)) * mg_ref[:, D_MLSTM:D_MLSTM + D_S5]
    if batch_major:
        ob_bm_ref[...] = jnp.swapaxes(ot_scr[...].reshape(tt, nb, D_S5), 0, 1)


def _s5(u, w, states_in, states_prev, *, layer, depth, nb, tt):
    batch_major = u.ndim == 3
    rows = u.shape[0] * u.shape[1] if batch_major else u.shape[0]
    rt = nb * tt
    zero_state = states_in is None
    parts = S5_PARTS if tt % (SUBLANES * S5_PARTS) == 0 else 1
    kern = functools.partial(_s5_kernel, layer=layer, nb=nb, tt=tt, zero_state=zero_state, parts=parts,
                             batch_major=batch_major)
    if batch_major:
        assert u.shape[0] == nb and tt % SUBLANES == 0
        io_spec = pl.BlockSpec((nb, tt, D_S5), lambda i: (0, i, 0))
        io_shape = jax.ShapeDtypeStruct(u.shape, F32)
        io_scratch = [pltpu.VMEM((rt, D_S5), F32), pltpu.VMEM((rt, D_S5), F32)]
    else:
        io_spec = pl.BlockSpec((rt, D_S5), lambda i: (i, 0))
        io_shape = jax.ShapeDtypeStruct((rows, D_S5), F32)
        io_scratch = []
    cst = lambda a: pl.BlockSpec(a.shape, lambda i: (0,) * a.ndim)
    st = pl.BlockSpec((None, nb, S5_N), lambda i: (layer, 0, 0))
    st_out = pl.BlockSpec((depth, nb, S5_N), lambda i: (0, 0, 0)) if layer == 0 else st
    consts = (w["lam"], w["bblk"], w["cblk"], w["d"], w["wglu"], w["mg"])
    args = [u, *consts]
    in_specs = [io_spec] + [cst(a) for a in consts]
    if not zero_state:
        args += list(states_in)
        in_specs += [st, st]
    aliases = {}
    if layer > 0:
        aliases = {len(args) + j: 1 + j for j in range(2)}
        args += list(states_prev)
        in_specs += [_ANY] * 2
    return pl.pallas_call(
        kern,
        grid=(rows // rt,),
        in_specs=in_specs,
        out_specs=[io_spec, st_out, st_out],
        out_shape=[io_shape, jax.ShapeDtypeStruct((depth, nb, S5_N), F32),
                   jax.ShapeDtypeStruct((depth, nb, S5_N), F32)],
        scratch_shapes=[pltpu.VMEM((rt, 2 * S5_N), F32), pltpu.VMEM((SUBLANES, 2 * S5_N), F32),
                        pltpu.VMEM((D_S5, 2 * S5_N), BF16)] + io_scratch,
        input_output_aliases=aliases,
        compiler_params=pltpu.CompilerParams(dimension_semantics=("arbitrary",), vmem_limit_bytes=VMEM_LIMIT),
        name="s5",
    )(*args)


def _block_diag(w):
    g, r, c = w.shape
    eye = jnp.eye(g, dtype=w.dtype)
    return jnp.einsum("grc,gh->grhc", w, eye).reshape(g * r, g * c)


def _layer_weights(l, ig_bias, fg_bias, s5_a_re, s5_a_im, s5_log_dt, s5_b_re, s5_b_im, s5_c_re, s5_c_im, s5_d,
                   w_glu, conv_w, mix_g):
    gb = jnp.concatenate([ig_bias[l], fg_bias[l], jnp.zeros((GATE_W - 2 * MH,), F32)])
    lam = jnp.stack([s5_a_re[l].reshape(-1), s5_a_im[l].reshape(-1),
                     jnp.repeat(s5_log_dt[l], S5_P)])
    bblk = jnp.concatenate([_block_diag(jnp.swapaxes(s5_b_re[l], 1, 2)),
                            _block_diag(jnp.swapaxes(s5_b_im[l], 1, 2))], axis=1)
    cblk = jnp.concatenate([_block_diag(jnp.swapaxes(s5_c_re[l], 1, 2)),
                            -_block_diag(jnp.swapaxes(s5_c_im[l], 1, 2))], axis=0).astype(BF16)
    return dict(
        gb=gb.reshape(1, GATE_W), gbt=gb[:SUBLANES].reshape(SUBLANES, 1),
        lam=lam, bblk=bblk, cblk=cblk, d=s5_d[l].reshape(1, D_S5), wglu=w_glu[l].astype(BF16),
        cw=conv_w[l], mg=mix_g[l].reshape(1, D_MODEL))


def _stacked_weights(ffn1_w1, ffn1_w3, ffn1_w2, ffn2_w1, ffn2_w3, ffn2_w2, norm_g, w_in, w_out):
    o = 4 * D_MLSTM
    win = (w_in[:, :, :o].astype(BF16),
           jnp.pad(w_in[:, :, o:o + 2 * MH].astype(BF16), ((0, 0), (0, 0), (0, GATE_W - 2 * MH))),
           w_in[:, :, o + 2 * MH:].astype(BF16))
    return dict(f1=(ffn1_w1.astype(BF16), ffn1_w3.astype(BF16), ffn1_w2.astype(BF16)),
                f2=(ffn2_w1.astype(BF16), ffn2_w3.astype(BF16), ffn2_w2.astype(BF16)),
                g=norm_g, win=win, wo=w_out.astype(BF16))


def _run_group(x, states, weights, big, *, tm, tt, chunk=None, bb=None):
    b, s, _ = x.shape
    depth = len(weights)
    xf = x.reshape(b * s, D_MODEL)
    mstates = sstates = None
    if states is not None:
        c_all, n_all, m_all, sr_all, si_all, cb_all = states
        ms_in = (c_all, n_all, m_all, cb_all)
        ss_in = (sr_all.reshape(depth, b, S5_N), si_all.reshape(depth, b, S5_N))
    else:
        ms_in = ss_in = None
    for l, w in enumerate(weights):
        x1, qkvo, gate, u, bch = _ffn_inproj(xf, big["g"], *big["f1"], big["win"], tm, l)
        if tt % SUBLANES == 0:
            ob, *sstates = _s5(u.reshape(b, s, D_S5), w, ss_in, sstates, layer=l, depth=depth, nb=b, tt=tt)
            ob = ob.reshape(b * s, D_S5)
        else:
            u_tm = jnp.swapaxes(u.reshape(b, s, D_S5), 0, 1).reshape(s * b, D_S5)
            ob_tm, *sstates = _s5(u_tm, w, ss_in, sstates, layer=l, depth=depth, nb=b, tt=tt)
            ob = jnp.swapaxes(ob_tm.reshape(s, b, D_S5), 0, 1).reshape(b * s, D_S5)
        if chunk is not None:
            r3 = lambda a: a.reshape(b, s, a.shape[-1])
            oa, oc, *mstates = _mlstm_conv(r3(qkvo), r3(gate), r3(bch), w, ms_in, mstates,
                                           layer=l, depth=depth, bb=bb, chunk=chunk)
            xf = _outproj_ffn(x1, oa.reshape(b * s, D_MLSTM), ob, oc.reshape(b * s, D_CONV),
                              big["g"], big["wo"], *big["f2"], tm, l)
        else:
            oa, oc, *mstates = _mlstm_conv_short(qkvo, gate, bch, w, ms_in, mstates,
                                                 layer=l, depth=depth, b=b, s=s)
            xf = _outproj_ffn(x1, oa, ob, oc, big["g"], big["wo"], *big["f2"], tm, l)
    c1, n1, m1, cb1 = mstates
    sr1, si1 = sstates
    return xf.reshape(b, s, D_MODEL), (c1, n1, m1, sr1.reshape(depth, b, S5_G, S5_P),
                                       si1.reshape(depth, b, S5_G, S5_P), cb1)


def kernel(x_prompt, x_sample, state_mlstm_C, state_mlstm_n, state_mlstm_m, state_s5_re, state_s5_im,
           state_conv, ffn1_w1, ffn1_w3, ffn1_w2, ffn2_w1, ffn2_w3, ffn2_w2, norm_g, w_in, ig_bias, fg_bias,
           s5_a_re, s5_a_im, s5_log_dt, s5_b_re, s5_b_im, s5_c_re, s5_c_im, s5_d, w_glu, conv_w, mix_g, w_out):
    depth = norm_g.shape[0]
    params = (ig_bias, fg_bias, s5_a_re, s5_a_im, s5_log_dt, s5_b_re, s5_b_im, s5_c_re, s5_c_im, s5_d, w_glu,
              conv_w, mix_g)
    weights = [_layer_weights(l, *params) for l in range(depth)]
    big = _stacked_weights(ffn1_w1, ffn1_w3, ffn1_w2, ffn2_w1, ffn2_w3, ffn2_w2, norm_g, w_in, w_out)
    y_p, st_p = _run_group(x_prompt, None, weights, big, tm=512, tt=256, chunk=128, bb=4)
    y_s, st_s = _run_group(
        x_sample, (state_mlstm_C, state_mlstm_n, state_mlstm_m, state_s5_re, state_s5_im, state_conv),
        weights, big, tm=512, tt=x_sample.shape[1])
    return (y_p, y_s, *st_p, *st_s)
```

```python
import functools

import jax
import jax.numpy as jnp
from jax import lax
from jax.experimental import pallas as pl
from jax.experimental.pallas import tpu as pltpu

F32 = jnp.float32
BF16 = jnp.bfloat16

D_MODEL = 1024
MH = 4
DH = 128
D_MLSTM = MH * DH
S5_CH = 16
S5_G = 16
S5_P = 64
D_S5 = S5_G * S5_CH
S5_N = S5_G * S5_P
D_CONV = 256
CONV_W = 3
EPS = 1e-6
GATE_W = 128
SUBLANES = 8
NEG = -1e30
VMEM_LIMIT = 56 * 1024 * 1024
STATE_VMEM_BUDGET = 36 * 1024 * 1024
ROW_PARTS = 2
S5_PARTS = 2


def _dot(a, b):
    return jnp.dot(a, b, preferred_element_type=F32)


def _dot_nt(a, b):
    return lax.dot_general(a, b, (((1,), (1,)), ((), ())), preferred_element_type=F32)


def _dot_tn(a, b):
    return lax.dot_general(a, b, (((0,), (0,)), ((), ())), preferred_element_type=F32)


def _dot_exact(a, b):
    return jnp.dot(a, b, preferred_element_type=F32, precision=lax.Precision.HIGHEST)


def _dot_nt_exact(a, b):
    return lax.dot_general(a, b, (((1,), (1,)), ((), ())), preferred_element_type=F32,
                           precision=lax.Precision.HIGHEST)


def _rms(x):
    return x * lax.rsqrt(jnp.mean(x * x, axis=-1, keepdims=True) + EPS)


def _log_sigmoid(x):
    return jnp.minimum(x, 0.0) - jnp.log(1.0 + jnp.exp(-jnp.abs(x)))


def _split_bf16(x, parts):
    out = []
    for _ in range(parts):
        p = x.astype(BF16)
        out.append(p)
        x = x - p.astype(F32)
    return jnp.concatenate(out, axis=1)


def _ffn_residual(xs, g_pre, g_post, w1, w3, w2):
    rows = _row_parts(sum(x.shape[0] for x in xs), len(xs))
    xn = jnp.concatenate([(_rms(x) * g_pre).astype(BF16) for x in xs], axis=0)
    h1 = _dot(xn, w1[...])
    h3 = _dot(xn, w3[...])
    a = jnp.concatenate([(jax.nn.silu(h1[r]) * h3[r]).astype(BF16) for r in rows], axis=0)
    y = _dot(a, w2[...])
    return [x + 0.5 * (_rms(y[r]) * g_post) for x, r in zip(xs, rows)]


def _row_parts(tm, parts):
    step = tm // parts
    return [slice(i * step, (i + 1) * step) for i in range(parts)]


def _layer_spec(a, layer):
    return pl.BlockSpec((None,) + a.shape[1:], lambda *_: (layer, 0, 0), pipeline_mode=pl.Buffered(1))


_ANY = pl.BlockSpec(memory_space=pl.ANY)


def _ffn_inproj_kernel(x_ref, g_ref, w1_ref, w3_ref, w2_ref, wq_ref, wg_ref, wr_ref,
                       x1_ref, qkvo_ref, gate_ref, u_ref, bch_ref):
    parts = _row_parts(x_ref.shape[0], ROW_PARTS)
    x1 = _ffn_residual([x_ref[r, :] for r in parts], g_ref[0:1, :], g_ref[1:2, :], w1_ref, w3_ref, w2_ref)
    hn = [(_rms(v) * g_ref[2:3, :]).astype(BF16) for v in x1]
    qkvo = [_dot(v, wq_ref[...]) for v in hn]
    gate = [_dot(v, wg_ref[...]) for v in hn]
    rest = [_dot(v, wr_ref[...]) for v in hn]
    for i, r in enumerate(parts):
        x1_ref[r, :] = x1[i]
        qkvo_ref[r, :] = qkvo[i]
        gate_ref[r, :] = gate[i]
        u_ref[r, :] = rest[i][:, :D_S5]
        bch_ref[r, :] = rest[i][:, D_S5:]


def _ffn_inproj(x, g, w1, w3, w2, win, tm, layer):
    t = x.shape[0]
    row = lambda w: pl.BlockSpec((tm, w), lambda i: (i, 0))
    return pl.pallas_call(
        _ffn_inproj_kernel,
        grid=(t // tm,),
        in_specs=[row(D_MODEL), _layer_spec(g, layer), _layer_spec(w1, layer), _layer_spec(w3, layer),
                  _layer_spec(w2, layer)] + [_layer_spec(a, layer) for a in win],
        out_specs=[row(D_MODEL), row(4 * D_MLSTM), row(GATE_W), row(D_S5), row(3 * D_CONV)],
        out_shape=[jax.ShapeDtypeStruct((t, D_MODEL), F32), jax.ShapeDtypeStruct((t, 4 * D_MLSTM), F32),
                   jax.ShapeDtypeStruct((t, GATE_W), F32), jax.ShapeDtypeStruct((t, D_S5), F32),
                   jax.ShapeDtypeStruct((t, 3 * D_CONV), F32)],
        compiler_params=pltpu.CompilerParams(dimension_semantics=("arbitrary",), vmem_limit_bytes=VMEM_LIMIT),
        name="ffn_inproj",
    )(x, g, w1, w3, w2, *win)


def _outproj_ffn_kernel(x_ref, a_ref, b_ref, c_ref, g_ref, wo_ref, w1_ref, w3_ref, w2_ref, y_ref):
    parts = _row_parts(x_ref.shape[0], ROW_PARTS)
    mo = [(_dot(a_ref[r, :].astype(BF16), wo_ref[0:D_MLSTM, :])
           + _dot(b_ref[r, :].astype(BF16), wo_ref[D_MLSTM:D_MLSTM + D_S5, :])
           + _dot(c_ref[r, :].astype(BF16), wo_ref[D_MLSTM + D_S5:, :])) for r in parts]
    x2 = [x_ref[r, :] + _rms(v) * g_ref[3:4, :] for r, v in zip(parts, mo)]
    y = _ffn_residual(x2, g_ref[4:5, :], g_ref[5:6, :], w1_ref, w3_ref, w2_ref)
    for r, v in zip(parts, y):
        y_ref[r, :] = v


def _outproj_ffn(x, oa, ob, oc, g, wo, w1, w3, w2, tm, layer):
    t = x.shape[0]
    row = lambda w: pl.BlockSpec((tm, w), lambda i: (i, 0))
    return pl.pallas_call(
        _outproj_ffn_kernel,
        grid=(t // tm,),
        in_specs=[row(D_MODEL), row(D_MLSTM), row(D_S5), row(D_CONV), _layer_spec(g, layer),
                  _layer_spec(wo, layer), _layer_spec(w1, layer), _layer_spec(w3, layer),
                  _layer_spec(w2, layer)],
        out_specs=row(D_MODEL),
        out_shape=jax.ShapeDtypeStruct((t, D_MODEL), F32),
        compiler_params=pltpu.CompilerParams(dimension_semantics=("arbitrary",), vmem_limit_bytes=VMEM_LIMIT),
        name="outproj_ffn",
    )(x, oa, ob, oc, g, wo, w1, w3, w2)


def _state_shapes(depth, b):
    return [jax.ShapeDtypeStruct((depth, b, MH, DH, DH), F32), jax.ShapeDtypeStruct((depth, b, MH, DH), F32),
            jax.ShapeDtypeStruct((depth, b, MH), F32), jax.ShapeDtypeStruct((depth, b, CONV_W - 1, D_CONV), F32)]


def _cummax_rows(x, rowi):
    s = 1
    while s < x.shape[0]:
        x = jnp.maximum(x, jnp.where(rowi >= s, pltpu.roll(x, s, 0), NEG))
        s *= 2
    return x


def _mlstm_conv_kernel(*refs, layer, bb, chunk, zero_state):
    (q_ref, k_ref, v_ref, o_ref, gate_ref, bg_ref, cg_ref, hc_ref, gb_ref, cw_ref, mg_ref) = refs[:11]
    refs = refs[11:]
    if not zero_state:
        c0_ref, n0_ref, m0_ref, cb0_ref = refs[:4]
        refs = refs[4:]
    if layer > 0:
        refs = refs[4:]
    oa_ref, oc_ref, c_ref, n_ref, m_ref, cb_ref, m_scr = refs
    other_layers = ()
    if layer == 0:
        other_layers = tuple(r.at[1:] for r in (c_ref, n_ref, m_ref, cb_ref) if r.shape[0] > 1)
        c_ref, n_ref, m_ref, cb_ref = c_ref.at[0], n_ref.at[0], m_ref.at[0], cb_ref.at[0]
    assert chunk == DH
    bi0 = pl.program_id(0) * bb
    ci = pl.program_id(1)

    @pl.when(ci == 0)
    def _():
        for r in other_layers:
            r[...] = jnp.zeros_like(r)
        m_scr[...] = jnp.zeros_like(m_scr)
        if zero_state:
            c_ref[...] = jnp.zeros_like(c_ref)
            n_ref[...] = jnp.zeros_like(n_ref)
            cb_ref[...] = jnp.zeros_like(cb_ref)
        else:
            c_ref[...] = c0_ref[...]
            n_ref[...] = n0_ref[...]
            cb_ref[...] = cb0_ref[...]
            for bi in range(bb):
                m_scr[bi:bi + 1, 0:MH] = m0_ref[pl.ds(bi0 + bi, 1), :]

    row = lax.broadcasted_iota(jnp.int32, (chunk, chunk), 0)
    col = lax.broadcasted_iota(jnp.int32, (chunk, chunk), 1)
    causal = col <= row
    tril = jnp.where(causal, 1.0, 0.0).astype(BF16)
    rowg = lax.broadcasted_iota(jnp.int32, (chunk, GATE_W), 0)
    rowi = lax.broadcasted_iota(jnp.int32, (chunk, D_CONV), 0)
    spread = jnp.where((lax.broadcasted_iota(jnp.int32, (2 * GATE_W, D_MLSTM), 0) & (GATE_W - 1))
                       == (lax.broadcasted_iota(jnp.int32, (2 * GATE_W, D_MLSTM), 1) >> 7), 1.0, 0.0).astype(BF16)
    scale = DH ** -0.5

    units = [(bi, h) for bi in range(bb) for h in range(MH)]
    hsl = lambda h: slice(h * DH, (h + 1) * DH)
    g_c = [gate_ref[bi] + gb_ref[...] for bi in range(bb)]
    csum = [_dot(tril, _split_bf16(pltpu.roll(_log_sigmoid(g), GATE_W - MH, 1), 3)) for g in g_c]
    qb = {u: q_ref[u[0], :, hsl(u[1])].astype(BF16) for u in units}
    kb = {u: (k_ref[u[0], :, hsl(u[1])] * scale).astype(BF16) for u in units}
    s_qk = {u: _dot_nt(qb[u], kb[u]) for u in units}
    inter = {u: _dot_nt(qb[u], jnp.concatenate(
        [c_ref[u[0], u[1]].astype(BF16),
         jnp.broadcast_to(n_ref[u[0], u[1]:u[1] + 1, :], (DH, DH)).astype(BF16)], axis=0)) for u in units}
    rep, a_r, decay = [], [], []
    for bi in range(bb):
        f_c = csum[bi][:, :GATE_W] + csum[bi][:, GATE_W:2 * GATE_W] + csum[bi][:, 2 * GATE_W:]
        a_c = g_c[bi] - f_c
        m_prev = m_scr[bi:bi + 1, :]
        big_m = jnp.maximum(_cummax_rows(a_c, rowg), m_prev)
        mt = f_c + big_m
        m_new = mt[chunk - 1:chunk, :]
        f_last = f_c[chunk - 1:chunk, :]
        decay.append(jnp.exp(f_last + m_prev - m_new))
        stacked = jnp.concatenate([_split_bf16(-big_m, 2), _split_bf16(jnp.exp(m_prev - big_m), 2),
                                   _split_bf16(-mt, 2), _split_bf16(jnp.exp(a_c + (f_last - m_new)), 2)],
                                  axis=0)
        rep.append(_dot(stacked, spread))
        a_r.append(a_c.T[0:SUBLANES, :])
        m_scr[bi:bi + 1, :] = m_new
        m_ref[pl.ds(bi0 + bi, 1), :] = m_new[:, 0:MH]
    intra, rsum = {}, {}
    for u in units:
        bi, h = u
        w = jnp.exp(jnp.where(causal, rep[bi][0:chunk, hsl(h)] + a_r[bi][h:h + 1, :], NEG))
        sqk = s_qk[u] * w
        rsum[u] = jnp.sum(sqk, axis=-1, keepdims=True)
        intra[u] = _dot(sqk.astype(BF16), v_ref[bi, :, hsl(h)].astype(BF16))
    hm, ssq = {}, {}
    for u in units:
        bi, h = u
        s_inter = rep[bi][chunk:2 * chunk, hsl(h)]
        e_floor = jnp.exp(rep[bi][2 * chunk:3 * chunk, hsl(h)])
        num = intra[u] + s_inter * inter[u][:, :DH]
        den = rsum[u] + s_inter * inter[u][:, DH:]
        hm[u] = jax.nn.sigmoid(o_ref[bi, :, hsl(h)]) * (num / jnp.maximum(jnp.abs(den), e_floor))
        ssq[u] = jnp.sum(hm[u] * hm[u], axis=-1, keepdims=True)
    for u in units:
        bi, h = u
        oa_ref[bi, :, hsl(h)] = hm[u] * lax.rsqrt(ssq[u] * (1.0 / DH) + EPS) * mg_ref[:, hsl(h)]
        wk = rep[bi][3 * chunk:, hsl(h)]
        dec = decay[bi][:, h:h + 1]
        c_ref[bi, h] = dec * c_ref[bi, h] + _dot_tn((v_ref[bi, :, hsl(h)] * wk).astype(BF16), kb[u])
        n_ref[bi, h:h + 1, :] = (dec * n_ref[bi, h:h + 1, :]
                                 + jnp.sum(k_ref[bi, :, hsl(h)] * scale * wk, axis=0, keepdims=True))
    for bi in range(bb):
        z = cg_ref[bi] * hc_ref[bi]
        cb = cb_ref[bi]
        z1 = jnp.where(rowi == 0, cb[1:2, :], pltpu.roll(z, 1, 0))
        z2 = jnp.where(rowi == 0, cb[0:1, :], jnp.where(rowi == 1, cb[1:2, :], pltpu.roll(z, 2, 0)))
        yc = cw_ref[0:1, :] * z2 + cw_ref[1:2, :] * z1 + cw_ref[2:3, :] * z
        oc_ref[bi] = _rms(bg_ref[bi] * yc) * mg_ref[:, D_MLSTM + D_S5:]
        cb_ref[bi] = z[chunk - 2:chunk, :]


def _mlstm_conv(qkvo, gate, bch, w, states_in, states_prev, *, layer, depth, bb, chunk):
    b, s, _ = qkvo.shape
    assert s % chunk == 0 and b % bb == 0
    zero_state = states_in is None
    kern = functools.partial(_mlstm_conv_kernel, layer=layer, bb=bb, chunk=chunk, zero_state=zero_state)
    col = lambda wd, j: pl.BlockSpec((bb, chunk, wd), lambda i, c: (i, c, j))
    st = lambda shape: pl.BlockSpec((None, bb) + shape, lambda i, c: (layer, i) + (0,) * len(shape))
    m_spec = pl.BlockSpec((None, b, MH), lambda i, c: (layer, 0, 0))
    cst = lambda a: pl.BlockSpec(a.shape, lambda i, c: (0,) * a.ndim)
    state_specs = [st((MH, DH, DH)), st((MH, DH)), m_spec, st((CONV_W - 1, D_CONV))]
    out_state_specs = state_specs
    if layer == 0:
        st0 = lambda shape: pl.BlockSpec((depth, bb) + shape, lambda i, c: (0, i) + (0,) * len(shape))
        out_state_specs = [st0((MH, DH, DH)), st0((MH, DH)), pl.BlockSpec((depth, b, MH), lambda i, c: (0, 0, 0)),
                           st0((CONV_W - 1, D_CONV))]
    consts = (w["gb"], w["cw"], w["mg"])
    args = [qkvo, qkvo, qkvo, qkvo, gate, bch, bch, bch, *consts]
    in_specs = [col(D_MLSTM, 0), col(D_MLSTM, 1), col(D_MLSTM, 2), col(D_MLSTM, 3), col(GATE_W, 0),
                col(D_CONV, 0), col(D_CONV, 1), col(D_CONV, 2)] + [cst(a) for a in consts]
    if not zero_state:
        args += list(states_in)
        in_specs += state_specs
    aliases = {}
    if layer > 0:
        aliases = {len(args) + j: 2 + j for j in range(4)}
        args += list(states_prev)
        in_specs += [_ANY] * 4
    return pl.pallas_call(
        kern,
        grid=(b // bb, s // chunk),
        in_specs=in_specs,
        out_specs=[col(D_MLSTM, 0), col(D_CONV, 0)] + out_state_specs,
        out_shape=[jax.ShapeDtypeStruct((b, s, D_MLSTM), F32), jax.ShapeDtypeStruct((b, s, D_CONV), F32)]
        + _state_shapes(depth, b),
        scratch_shapes=[pltpu.VMEM((-(-bb // SUBLANES) * SUBLANES, GATE_W), F32)],
        input_output_aliases=aliases,
        compiler_params=pltpu.CompilerParams(dimension_semantics=("arbitrary", "arbitrary"),
                                             vmem_limit_bytes=VMEM_LIMIT),
        name="mlstm_conv",
    )(*args)


def _gate_terms(graw, gb_ref, gbt_ref, tril, triu):
    sel = (lax.broadcasted_iota(jnp.int32, (SUBLANES, GATE_W), 0)
           == lax.broadcasted_iota(jnp.int32, (SUBLANES, GATE_W), 1)).astype(F32)
    g_c = graw + gb_ref[...]
    lf_c = _log_sigmoid(g_c)
    g_r = _dot_nt_exact(sel, graw) + gbt_ref[...]
    lf_r = _log_sigmoid(g_r)
    return g_c, lf_c, _dot_exact(tril, lf_c), g_r, _dot_exact(lf_r, triu)


def _head_scores(h, g_c, bt_c, g_r, bt_r, m_prev, mask):
    it_row = g_r[h:h + 1, :]
    bt_row = bt_r[MH + h:MH + h + 1, :]
    it_col = g_c[:, h:h + 1]
    bt_col = bt_c[:, MH + h:MH + h + 1]
    dmat = jnp.where(mask, bt_col - bt_row + it_row, NEG)
    inter = bt_col + m_prev
    mt = jnp.maximum(jnp.max(dmat, axis=-1, keepdims=True), inter)
    return jnp.exp(dmat - mt), jnp.exp(inter - mt), mt, it_col, bt_col


def _head_output(sqk, vb, s_inter, qc, qn, mt, o, mg):
    num = _dot(sqk.astype(BF16), vb) + s_inter * qc
    den = jnp.sum(sqk, axis=-1, keepdims=True) + s_inter * qn
    hm = jax.nn.sigmoid(o) * (num / jnp.maximum(jnp.abs(den), jnp.exp(-mt)))
    return _rms(hm) * mg


def _bcast_block_last(x, t, s):
    out = x
    for d in range(1, s):
        out = jnp.where(t == s - 1 - d, pltpu.roll(x, x.shape[0] - d, 0), out)
    return out


def _mlstm_conv_short_kernel(*refs, layer, nb, s):
    (q_ref, k_ref, v_ref, o_ref, gate_ref, bg_ref, cg_ref, hc_ref, gb_ref, gbt_ref, cw_ref, mg_ref,
     c0_ref, n0_ref, m0_ref, cb0_ref) = refs[:16]
    refs = refs[16:]
    if layer > 0:
        refs = refs[4:]
    oa_ref, oc_ref, c_ref, n_ref, m_ref, cb_ref, qc_scr, st_scr = refs
    if layer == 0:
        for ref in (c_ref, n_ref, m_ref, cb_ref):
            if ref.shape[0] > 1:
                ref[1:] = jnp.zeros((ref.shape[0] - 1,) + ref.shape[1:], F32)
        c_ref, n_ref, m_ref, cb_ref = c_ref.at[0], n_ref.at[0], m_ref.at[0], cb_ref.at[0]
    r = nb * s
    per_tile = SUBLANES // s
    shift = s.bit_length() - 1
    row = lax.broadcasted_iota(jnp.int32, (r, r), 0)
    col = lax.broadcasted_iota(jnp.int32, (r, r), 1)
    same = (row >> shift) == (col >> shift)
    mask = same & (col <= row)
    tril = mask.astype(F32)
    triu = (same & (row <= col)).astype(F32)
    expand = ((lax.broadcasted_iota(jnp.int32, (r, nb), 0) >> shift)
              == lax.broadcasted_iota(jnp.int32, (r, nb), 1)).astype(F32)
    gather = (lax.broadcasted_iota(jnp.int32, (nb, r), 0)
              == (lax.broadcasted_iota(jnp.int32, (nb, r), 1) >> shift)).astype(F32)
    t128 = lax.broadcasted_iota(jnp.int32, (r, DH), 0) & (s - 1)
    sub8 = lax.broadcasted_iota(jnp.int32, (SUBLANES, DH), 0) >> shift
    scale = DH ** -0.5

    g_c, lf_c, bt_c, g_r, bt_r = _gate_terms(gate_ref[...], gb_ref, gbt_ref, tril, triu)
    bt_last_c = _dot_exact(same.astype(F32), lf_c)
    m_rows = _dot_exact(expand, m0_ref[...])
    for h in range(MH):
        hs = slice(h * DH, (h + 1) * DH)
        m_prev = m_rows[:, h:h + 1]
        w, s_inter, mt, it_col, bt_col = _head_scores(h, g_c, bt_c, g_r, bt_r, m_prev, mask)
        qh = q_ref[:, hs]
        kh = k_ref[:, hs] * scale
        vh = v_ref[:, hs]
        qb = qh.astype(BF16)
        kb = kh.astype(BF16)
        sqk = _dot_nt(qb, kb) * w
        for j in range(r // SUBLANES):
            rows = slice(j * SUBLANES, (j + 1) * SUBLANES)
            acc = None
            q8 = q_ref[rows, hs].astype(BF16)
            for p in range(per_tile):
                part = _dot_nt(q8, c0_ref[j * per_tile + p, h].astype(BF16))
                acc = part if acc is None else jnp.where(sub8 == p, part, acc)
            qc_scr[rows, :] = acc
        n_rows = _dot_exact(expand, n0_ref[:, h, :])
        qn = jnp.sum(qh * n_rows, axis=-1, keepdims=True)
        oa_ref[:, hs] = _head_output(sqk, vh.astype(BF16), s_inter, qc_scr[...], qn, mt,
                                     o_ref[:, hs], mg_ref[:, hs])
        m_new = _bcast_block_last(jnp.broadcast_to(mt, (r, DH)), t128, s)
        bt_last = bt_last_c[:, MH + h:MH + h + 1]
        wk = jnp.exp(bt_last - bt_col + it_col - m_new)
        decay = jnp.exp(bt_last + m_prev - m_new)
        vw = vh * wk
        for j in range(r // SUBLANES):
            rows = slice(j * SUBLANES, (j + 1) * SUBLANES)
            k8 = (k_ref[rows, hs] * scale).astype(BF16)
            for p in range(per_tile):
                bidx = j * per_tile + p
                last = bidx * s + s - 1
                upd = _dot_tn(jnp.where(sub8 == p, vw[rows], 0.0).astype(BF16), k8)
                c_ref[bidx, h] = decay[last:last + 1, 0:1] * c0_ref[bidx, h] + upd
        st_scr[0] = decay
        st_scr[1] = m_new
        last_rows = pl.ds(s - 1, nb, stride=s)
        n_ref[:, h, :] = st_scr[0, last_rows, :] * n0_ref[:, h, :] + _dot_exact(gather, kh * wk)
        m_ref[:, h:h + 1] = st_scr[1, last_rows, :][:, 0:1]
    t256 = lax.broadcasted_iota(jnp.int32, (r, D_CONV), 0) & (s - 1)
    z = cg_ref[...] * hc_ref[...]
    cb_a = _dot_exact(expand, cb0_ref[:, 0, :])
    cb_b = _dot_exact(expand, cb0_ref[:, 1, :])
    z1 = jnp.where(t256 == 0, cb_b, pltpu.roll(z, 1, 0))
    z2 = jnp.where(t256 == 0, cb_a, jnp.where(t256 == 1, cb_b, pltpu.roll(z, 2, 0)))
    yc = cw_ref[0:1, :] * z2 + cw_ref[1:2, :] * z1 + cw_ref[2:3, :] * z
    oc_ref[...] = _rms(bg_ref[...] * yc) * mg_ref[:, D_MLSTM + D_S5:]
    for half in range(D_CONV // DH):
        lanes = slice(half * DH, (half + 1) * DH)
        st_scr[half] = z[:, lanes]
        cb_ref[:, 0, lanes] = st_scr[half, pl.ds(s - 2, nb, stride=s), :]
        cb_ref[:, 1, lanes] = st_scr[half, pl.ds(s - 1, nb, stride=s), :]


def _short_batch_block(b, depth, layer):
    out_layers = depth if layer == 0 else 1
    nb = min(b, 32)
    while nb > SUBLANES and 2 * (1 + out_layers) * nb * MH * DH * DH * 4 > STATE_VMEM_BUDGET:
        nb //= 2
    return nb


def _mlstm_conv_short(qkvo, gate, bch, w, states_in, states_prev, *, layer, depth, b, s):
    nb = _short_batch_block(b, depth, layer)
    assert SUBLANES % s == 0 and s >= CONV_W - 1 and b % nb == 0 and (nb * s) % SUBLANES == 0
    r = nb * s
    kern = functools.partial(_mlstm_conv_short_kernel, layer=layer, nb=nb, s=s)
    col = lambda wd, j: pl.BlockSpec((r, wd), lambda i: (i, j))
    st = lambda shape: pl.BlockSpec((None, nb) + shape, lambda i: (layer, i) + (0,) * len(shape))
    cst = lambda a: pl.BlockSpec(a.shape, lambda i: (0,) * a.ndim)
    state_specs = [st((MH, DH, DH)), st((MH, DH)), st((MH,)), st((CONV_W - 1, D_CONV))]
    out_state_specs = state_specs
    if layer == 0:
        st0 = lambda shape: pl.BlockSpec((depth, nb) + shape, lambda i: (0, i) + (0,) * len(shape))
        out_state_specs = [st0((MH, DH, DH)), st0((MH, DH)), st0((MH,)), st0((CONV_W - 1, D_CONV))]
    consts = (w["gb"], w["gbt"], w["cw"], w["mg"])
    args = [qkvo, qkvo, qkvo, qkvo, gate, bch, bch, bch, *consts, *states_in]
    in_specs = [col(D_MLSTM, 0), col(D_MLSTM, 1), col(D_MLSTM, 2), col(D_MLSTM, 3), col(GATE_W, 0),
                col(D_CONV, 0), col(D_CONV, 1), col(D_CONV, 2)] + [cst(a) for a in consts] + state_specs
    aliases = {}
    if layer > 0:
        aliases = {len(args) + j: 2 + j for j in range(4)}
        args += list(states_prev)
        in_specs += [_ANY] * 4
    return pl.pallas_call(
        kern,
        grid=(b // nb,),
        in_specs=in_specs,
        out_specs=[col(D_MLSTM, 0), col(D_CONV, 0)] + out_state_specs,
        out_shape=[jax.ShapeDtypeStruct((b * s, D_MLSTM), F32), jax.ShapeDtypeStruct((b * s, D_CONV), F32)]
        + _state_shapes(depth, b),
        scratch_shapes=[pltpu.VMEM((r, DH), F32), pltpu.VMEM((2, r, DH), F32)],
        input_output_aliases=aliases,
        compiler_params=pltpu.CompilerParams(dimension_semantics=("arbitrary",), vmem_limit_bytes=VMEM_LIMIT),
        name="mlstm_conv_short",
    )(*args)


def _s5_kernel(*refs, layer, nb, tt, zero_state, parts, batch_major):
    u_ref, lam_ref, bblk_ref, cblk_ref, d_ref, wglu_ref, mg_ref = refs[:7]
    refs = refs[7:]
    if not zero_state:
        hr0_ref, hi0_ref = refs[:2]
        refs = refs[2:]
    if layer > 0:
        refs = refs[2:]
    ob_ref, hr_ref, hi_ref, xs_ref, a_ref, bf_ref = refs[:6]
    if batch_major:
        ut_scr, ot_scr = refs[6:]
        ut_scr[...] = jnp.swapaxes(u_ref[...], 0, 1).reshape(tt * nb, D_S5)
        u_bm_ref, ob_bm_ref, u_ref, ob_ref = u_ref, ob_ref, ut_scr, ot_scr
    if layer == 0:
        hr_all, hi_all = hr_ref, hi_ref
        hr_ref, hi_ref = hr_all.at[0], hi_all.at[0]
    ti = pl.program_id(0)
    n = S5_N
    lc = 512

    @pl.when(ti == 0)
    def _():
        if layer == 0:
            hr_all[...] = jnp.zeros_like(hr_all)
            hi_all[...] = jnp.zeros_like(hi_all)
        hr_ref[...] = jnp.zeros_like(hr_ref) if zero_state else hr0_ref[...]
        hi_ref[...] = jnp.zeros_like(hi_ref) if zero_state else hi0_ref[...]
        lre = jnp.minimum(lam_ref[0:1, :], -1e-4)
        lim = lam_ref[1:2, :]
        dt = jnp.exp(lam_ref[2:3, :])
        mag = jnp.exp(lre * dt)
        ab_re = mag * jnp.cos(lim * dt)
        ab_im = mag * jnp.sin(lim * dt)
        den = lre * lre + lim * lim
        nr = ab_re - 1.0
        fre = (nr * lre + ab_im * lim) / den
        fim = (ab_im * lre - nr * lim) / den
        a_ref[:, :n] = jnp.broadcast_to(ab_re, (SUBLANES, n))
        a_ref[:, n:] = jnp.broadcast_to(ab_im, (SUBLANES, n))
        b_re = bblk_ref[:, :n]
        b_im = bblk_ref[:, n:]
        bf_ref[:, :n] = (fre * b_re - fim * b_im).astype(BF16)
        bf_ref[:, n:] = (fre * b_im + fim * b_re).astype(BF16)

    rows_p = nb * tt // parts
    part_rows = [slice(p * rows_p, (p + 1) * rows_p) for p in range(parts)]
    xs_ref[...] = _dot(u_ref[...].astype(BF16), bf_ref[...])
    ar = [a_ref[:, c0:c0 + lc] for c0 in range(0, n, lc)]
    ai = [a_ref[:, n + c0:n + c0 + lc] for c0 in range(0, n, lc)]
    h, y = {}, []
    for p, rs in enumerate(part_rows):
        for rg in range(nb // SUBLANES):
            g8 = slice(rg * SUBLANES, (rg + 1) * SUBLANES)
            for ci, c0 in enumerate(range(0, n, lc)):
                if p == 0:
                    h[rg, ci] = (hr_ref[g8, c0:c0 + lc], hi_ref[g8, c0:c0 + lc])
            for t in range(p * tt // parts, (p + 1) * tt // parts):
                r8 = slice(t * nb + rg * SUBLANES, t * nb + (rg + 1) * SUBLANES)
                for ci, c0 in enumerate(range(0, n, lc)):
                    hr, hi = h[rg, ci]
                    nr_ = ar[ci] * hr - ai[ci] * hi + xs_ref[r8, c0:c0 + lc]
                    ni_ = ar[ci] * hi + ai[ci] * hr + xs_ref[r8, n + c0:n + c0 + lc]
                    xs_ref[r8, c0:c0 + lc] = nr_
                    xs_ref[r8, n + c0:n + c0 + lc] = ni_
                    h[rg, ci] = (nr_, ni_)
            for ci, c0 in enumerate(range(0, n, lc)):
                if p == parts - 1:
                    hr_ref[g8, c0:c0 + lc] = h[rg, ci][0]
                    hi_ref[g8, c0:c0 + lc] = h[rg, ci][1]
        y.append(_dot(xs_ref[rs, :].astype(BF16), cblk_ref[...]) + d_ref[...] * u_ref[rs, :])
    g = [jax.nn.gelu(v) for v in y]
    gate = [_dot(v.astype(BF16), wglu_ref[...]) for v in g]
    for rs, v, z in zip(part_rows, g, gate):
        ob_ref[rs, :] = _rms(v * jax.nn.sigmoid(z)) * mg_ref[:, D_MLSTM:D_MLSTM + D_S5]
    if batch_major:
        ob_bm_ref[...] = jnp.swapaxes(ot_scr[...].reshape(tt, nb, D_S5), 0, 1)


def _s5(u, w, states_in, states_prev, *, layer, depth, nb, tt):
    batch_major = u.ndim == 3
    rows = u.shape[0] * u.shape[1] if batch_major else u.shape[0]
    rt = nb * tt
    zero_state = states_in is None
    parts = S5_PARTS if tt % (SUBLANES * S5_PARTS) == 0 else 1
    kern = functools.partial(_s5_kernel, layer=layer, nb=nb, tt=tt, zero_state=zero_state, parts=parts,
                             batch_major=batch_major)
    if batch_major:
        assert u.shape[0] == nb and tt % SUBLANES == 0
        io_spec = pl.BlockSpec((nb, tt, D_S5), lambda i: (0, i, 0))
        io_shape = jax.ShapeDtypeStruct(u.shape, F32)
        io_scratch = [pltpu.VMEM((rt, D_S5), F32), pltpu.VMEM((rt, D_S5), F32)]
    else:
        io_spec = pl.BlockSpec((rt, D_S5), lambda i: (i, 0))
        io_shape = jax.ShapeDtypeStruct((rows, D_S5), F32)
        io_scratch = []
    cst = lambda a: pl.BlockSpec(a.shape, lambda i: (0,) * a.ndim)
    st = pl.BlockSpec((None, nb, S5_N), lambda i: (layer, 0, 0))
    st_out = pl.BlockSpec((depth, nb, S5_N), lambda i: (0, 0, 0)) if layer == 0 else st
    consts = (w["lam"], w["bblk"], w["cblk"], w["d"], w["wglu"], w["mg"])
    args = [u, *consts]
    in_specs = [io_spec] + [cst(a) for a in consts]
    if not zero_state:
        args += list(states_in)
        in_specs += [st, st]
    aliases = {}
    if layer > 0:
        aliases = {len(args) + j: 1 + j for j in range(2)}
        args += list(states_prev)
        in_specs += [_ANY] * 2
    return pl.pallas_call(
        kern,
        grid=(rows // rt,),
        in_specs=in_specs,
        out_specs=[io_spec, st_out, st_out],
        out_shape=[io_shape, jax.ShapeDtypeStruct((depth, nb, S5_N), F32),
                   jax.ShapeDtypeStruct((depth, nb, S5_N), F32)],
        scratch_shapes=[pltpu.VMEM((rt, 2 * S5_N), F32), pltpu.VMEM((SUBLANES, 2 * S5_N), F32),
                        pltpu.VMEM((D_S5, 2 * S5_N), BF16)] + io_scratch,
        input_output_aliases=aliases,
        compiler_params=pltpu.CompilerParams(dimension_semantics=("arbitrary",), vmem_limit_bytes=VMEM_LIMIT),
        name="s5",
    )(*args)


def _block_diag(w):
    g, r, c = w.shape
    eye = jnp.eye(g, dtype=w.dtype)
    return jnp.einsum("grc,gh->grhc", w, eye).reshape(g * r, g * c)


def _layer_weights(l, ig_bias, fg_bias, s5_a_re, s5_a_im, s5_log_dt, s5_b_re, s5_b_im, s5_c_re, s5_c_im, s5_d,
                   w_glu, conv_w, mix_g):
    gb = jnp.concatenate([ig_bias[l], fg_bias[l], jnp.zeros((GATE_W - 2 * MH,), F32)])
    lam = jnp.stack([s5_a_re[l].reshape(-1), s5_a_im[l].reshape(-1),
                     jnp.repeat(s5_log_dt[l], S5_P)])
    bblk = jnp.concatenate([_block_diag(jnp.swapaxes(s5_b_re[l], 1, 2)),
                            _block_diag(jnp.swapaxes(s5_b_im[l], 1, 2))], axis=1)
    cblk = jnp.concatenate([_block_diag(jnp.swapaxes(s5_c_re[l], 1, 2)),
                            -_block_diag(jnp.swapaxes(s5_c_im[l], 1, 2))], axis=0).astype(BF16)
    return dict(
        gb=gb.reshape(1, GATE_W), gbt=gb[:SUBLANES].reshape(SUBLANES, 1),
        lam=lam, bblk=bblk, cblk=cblk, d=s5_d[l].reshape(1, D_S5), wglu=w_glu[l].astype(BF16),
        cw=conv_w[l], mg=mix_g[l].reshape(1, D_MODEL))


def _stacked_weights(ffn1_w1, ffn1_w3, ffn1_w2, ffn2_w1, ffn2_w3, ffn2_w2, norm_g, w_in, w_out):
    o = 4 * D_MLSTM
    win = (w_in[:, :, :o].astype(BF16),
           jnp.pad(w_in[:, :, o:o + 2 * MH].astype(BF16), ((0, 0), (0, 0), (0, GATE_W - 2 * MH))),
           w_in[:, :, o + 2 * MH:].astype(BF16))
    return dict(f1=(ffn1_w1.astype(BF16), ffn1_w3.astype(BF16), ffn1_w2.astype(BF16)),
                f2=(ffn2_w1.astype(BF16), ffn2_w3.astype(BF16), ffn2_w2.astype(BF16)),
                g=norm_g, win=win, wo=w_out.astype(BF16))


def _run_group(x, states, weights, big, *, tm, tt, chunk=None, bb=None):
    b, s, _ = x.shape
    depth = len(weights)
    xf = x.reshape(b * s, D_MODEL)
    mstates = sstates = None
    if states is not None:
        c_all, n_all, m_all, sr_all, si_all, cb_all = states
        ms_in = (c_all, n_all, m_all, cb_all)
        ss_in = (sr_all.reshape(depth, b, S5_N), si_all.reshape(depth, b, S5_N))
    else:
        ms_in = ss_in = None
    for l, w in enumerate(weights):
        x1, qkvo, gate, u, bch = _ffn_inproj(xf, big["g"], *big["f1"], big["win"], tm, l)
        if tt % SUBLANES == 0:
            ob, *sstates = _s5(u.reshape(b, s, D_S5), w, ss_in, sstates, layer=l, depth=depth, nb=b, tt=tt)
            ob = ob.reshape(b * s, D_S5)
        else:
            u_tm = jnp.swapaxes(u.reshape(b, s, D_S5), 0, 1).reshape(s * b, D_S5)
            ob_tm, *sstates = _s5(u_tm, w, ss_in, sstates, layer=l, depth=depth, nb=b, tt=tt)
            ob = jnp.swapaxes(ob_tm.reshape(s, b, D_S5), 0, 1).reshape(b * s, D_S5)
        if chunk is not None:
            r3 = lambda a: a.reshape(b, s, a.shape[-1])
            oa, oc, *mstates = _mlstm_conv(r3(qkvo), r3(gate), r3(bch), w, ms_in, mstates,
                                           layer=l, depth=depth, bb=bb, chunk=chunk)
            xf = _outproj_ffn(x1, oa.reshape(b * s, D_MLSTM), ob, oc.reshape(b * s, D_CONV),
                              big["g"], big["wo"], *big["f2"], tm, l)
        else:
            oa, oc, *mstates = _mlstm_conv_short(qkvo, gate, bch, w, ms_in, mstates,
                                                 layer=l, depth=depth, b=b, s=s)
            xf = _outproj_ffn(x1, oa, ob, oc, big["g"], big["wo"], *big["f2"], tm, l)
    c1, n1, m1, cb1 = mstates
    sr1, si1 = sstates
    return xf.reshape(b, s, D_MODEL), (c1, n1, m1, sr1.reshape(depth, b, S5_G, S5_P),
                                       si1.reshape(depth, b, S5_G, S5_P), cb1)


def kernel(x_prompt, x_sample, state_mlstm_C, state_mlstm_n, state_mlstm_m, state_s5_re, state_s5_im,
           state_conv, ffn1_w1, ffn1_w3, ffn1_w2, ffn2_w1, ffn2_w3, ffn2_w2, norm_g, w_in, ig_bias, fg_bias,
           s5_a_re, s5_a_im, s5_log_dt, s5_b_re, s5_b_im, s5_c_re, s5_c_im, s5_d, w_glu, conv_w, mix_g, w_out):
    depth = norm_g.shape[0]
    params = (ig_bias, fg_bias, s5_a_re, s5_a_im, s5_log_dt, s5_b_re, s5_b_im, s5_c_re, s5_c_im, s5_d, w_glu,
              conv_w, mix_g)
    weights = [_layer_weights(l, *params) for l in range(depth)]
    big = _stacked_weights(ffn1_w1, ffn1_w3, ffn1_w2, ffn2_w1, ffn2_w3, ffn2_w2, norm_g, w_in, w_out)
    y_p, st_p = _run_group(x_prompt, None, weights, big, tm=512, tt=256, chunk=128, bb=4)
    y_s, st_s = _run_group(
        x_sample, (state_mlstm_C, state_mlstm_n, state_mlstm_m, state_s5_re, state_s5_im, state_conv),
        weights, big, tm=512, tt=x_sample.shape[1])
    return (y_p, y_s, *st_p, *st_s)
```

```python
import functools

import jax
import jax.numpy as jnp
from jax import lax
from jax.experimental import pallas as pl
from jax.experimental.pallas import tpu as pltpu

F32 = jnp.float32
BF16 = jnp.bfloat16

D_MODEL = 1024
MH = 4
DH = 128
D_MLSTM = MH * DH
S5_CH = 16
S5_G = 16
S5_P = 64
D_S5 = S5_G * S5_CH
S5_N = S5_G * S5_P
D_CONV = 256
CONV_W = 3
EPS = 1e-6
GATE_W = 128
SUBLANES = 8
NEG = -1e30
VMEM_LIMIT = 56 * 1024 * 1024
STATE_VMEM_BUDGET = 36 * 1024 * 1024
FFN_ROWS = 512
ROW_PARTS = 2
S5_STEPS = 256
S5_PARTS = 2
MLSTM_CHUNK = DH
MLSTM_ROWS = 4


def _dot(a, b):
    return jnp.dot(a, b, preferred_element_type=F32)


def _dot_nt(a, b):
    return lax.dot_general(a, b, (((1,), (1,)), ((), ())), preferred_element_type=F32)


def _dot_tn(a, b):
    return lax.dot_general(a, b, (((0,), (0,)), ((), ())), preferred_element_type=F32)


def _dot_exact(a, b):
    return jnp.dot(a, b, preferred_element_type=F32, precision=lax.Precision.HIGHEST)


def _dot_nt_exact(a, b):
    return lax.dot_general(a, b, (((1,), (1,)), ((), ())), preferred_element_type=F32,
                           precision=lax.Precision.HIGHEST)


def _rms(x):
    return x * lax.rsqrt(jnp.mean(x * x, axis=-1, keepdims=True) + EPS)


def _log_sigmoid(x):
    return jnp.minimum(x, 0.0) - jnp.log(1.0 + jnp.exp(-jnp.abs(x)))


def _split_bf16(x, parts):
    out = []
    for _ in range(parts):
        p = x.astype(BF16)
        out.append(p)
        x = x - p.astype(F32)
    return jnp.concatenate(out, axis=1)


def _ffn_residual(xs, g_pre, g_post, w1, w3, w2):
    xn = [(_rms(x) * g_pre).astype(BF16) for x in xs]
    h1 = [_dot(v, w1[...]) for v in xn]
    h3 = [_dot(v, w3[...]) for v in xn]
    a = [(jax.nn.silu(p) * q).astype(BF16) for p, q in zip(h1, h3)]
    y = [_dot(v, w2[...]) for v in a]
    return [x + 0.5 * (_rms(t) * g_post) for x, t in zip(xs, y)]


def _row_parts(tm, parts):
    step = tm // parts
    return [slice(i * step, (i + 1) * step) for i in range(parts)]


def _layer_spec(a, layer):
    return pl.BlockSpec((None,) + a.shape[1:], lambda *_: (layer, 0, 0), pipeline_mode=pl.Buffered(1))


_ANY = pl.BlockSpec(memory_space=pl.ANY)


def _ffn_inproj_kernel(x_ref, g_ref, w1_ref, w3_ref, w2_ref, wq_ref, wg_ref, wr_ref,
                       x1_ref, qkvo_ref, gate_ref, u_ref, bch_ref):
    parts = _row_parts(x_ref.shape[0], ROW_PARTS)
    x1 = _ffn_residual([x_ref[r, :] for r in parts], g_ref[0:1, :], g_ref[1:2, :], w1_ref, w3_ref, w2_ref)
    hn = [(_rms(v) * g_ref[2:3, :]).astype(BF16) for v in x1]
    qkvo = [_dot(v, wq_ref[...]) for v in hn]
    gate = [_dot(v, wg_ref[...]) for v in hn]
    rest = [_dot(v, wr_ref[...]) for v in hn]
    for i, r in enumerate(parts):
        x1_ref[r, :] = x1[i]
        qkvo_ref[r, :] = qkvo[i]
        gate_ref[r, :] = gate[i]
        u_ref[r, :] = rest[i][:, :D_S5]
        bch_ref[r, :] = rest[i][:, D_S5:]


def _ffn_inproj(x, g, w1, w3, w2, win, tm, layer):
    t = x.shape[0]
    row = lambda w: pl.BlockSpec((tm, w), lambda i: (i, 0))
    return pl.pallas_call(
        _ffn_inproj_kernel,
        grid=(t // tm,),
        in_specs=[row(D_MODEL), _layer_spec(g, layer), _layer_spec(w1, layer), _layer_spec(w3, layer),
                  _layer_spec(w2, layer)] + [_layer_spec(a, layer) for a in win],
        out_specs=[row(D_MODEL), row(4 * D_MLSTM), row(GATE_W), row(D_S5), row(3 * D_CONV)],
        out_shape=[jax.ShapeDtypeStruct((t, D_MODEL), F32), jax.ShapeDtypeStruct((t, 4 * D_MLSTM), F32),
                   jax.ShapeDtypeStruct((t, GATE_W), F32), jax.ShapeDtypeStruct((t, D_S5), F32),
                   jax.ShapeDtypeStruct((t, 3 * D_CONV), F32)],
        compiler_params=pltpu.CompilerParams(dimension_semantics=("arbitrary",), vmem_limit_bytes=VMEM_LIMIT),
        name="ffn_inproj",
    )(x, g, w1, w3, w2, *win)


def _outproj_ffn_kernel(x_ref, a_ref, b_ref, c_ref, g_ref, wo_ref, w1_ref, w3_ref, w2_ref, y_ref):
    parts = _row_parts(x_ref.shape[0], ROW_PARTS)
    mo = [(_dot(a_ref[r, :].astype(BF16), wo_ref[0:D_MLSTM, :])
           + _dot(b_ref[r, :].astype(BF16), wo_ref[D_MLSTM:D_MLSTM + D_S5, :])
           + _dot(c_ref[r, :].astype(BF16), wo_ref[D_MLSTM + D_S5:, :])) for r in parts]
    x2 = [x_ref[r, :] + _rms(v) * g_ref[3:4, :] for r, v in zip(parts, mo)]
    y = _ffn_residual(x2, g_ref[4:5, :], g_ref[5:6, :], w1_ref, w3_ref, w2_ref)
    for r, v in zip(parts, y):
        y_ref[r, :] = v


def _outproj_ffn(x, oa, ob, oc, g, wo, w1, w3, w2, tm, layer):
    t = x.shape[0]
    row = lambda w: pl.BlockSpec((tm, w), lambda i: (i, 0))
    return pl.pallas_call(
        _outproj_ffn_kernel,
        grid=(t // tm,),
        in_specs=[row(D_MODEL), row(D_MLSTM), row(D_S5), row(D_CONV), _layer_spec(g, layer),
                  _layer_spec(wo, layer), _layer_spec(w1, layer), _layer_spec(w3, layer),
                  _layer_spec(w2, layer)],
        out_specs=row(D_MODEL),
        out_shape=jax.ShapeDtypeStruct((t, D_MODEL), F32),
        compiler_params=pltpu.CompilerParams(dimension_semantics=("arbitrary",), vmem_limit_bytes=VMEM_LIMIT),
        name="outproj_ffn",
    )(x, oa, ob, oc, g, wo, w1, w3, w2)


def _state_shapes(depth, b):
    return [jax.ShapeDtypeStruct((depth, b, MH, DH, DH), F32), jax.ShapeDtypeStruct((depth, b, MH, DH), F32),
            jax.ShapeDtypeStruct((depth, b, MH), F32), jax.ShapeDtypeStruct((depth, b, CONV_W - 1, D_CONV), F32)]


def _cummax_rows(x, rowi):
    s = 1
    while s < x.shape[0]:
        x = jnp.maximum(x, jnp.where(rowi >= s, pltpu.roll(x, s, 0), NEG))
        s *= 2
    return x


def _mlstm_conv_kernel(*refs, layer, bb, chunk, zero_state):
    (q_ref, k_ref, v_ref, o_ref, gate_ref, bg_ref, cg_ref, hc_ref, gb_ref, cw_ref, mg_ref) = refs[:11]
    refs = refs[11:]
    if not zero_state:
        c0_ref, n0_ref, m0_ref, cb0_ref = refs[:4]
        refs = refs[4:]
    if layer > 0:
        refs = refs[4:]
    oa_ref, oc_ref, c_ref, n_ref, m_ref, cb_ref, m_scr = refs
    other_layers = ()
    if layer == 0:
        other_layers = tuple(r.at[1:] for r in (c_ref, n_ref, m_ref, cb_ref) if r.shape[0] > 1)
        c_ref, n_ref, m_ref, cb_ref = c_ref.at[0], n_ref.at[0], m_ref.at[0], cb_ref.at[0]
    assert chunk == DH
    bi0 = pl.program_id(0) * bb
    ci = pl.program_id(1)

    @pl.when(ci == 0)
    def _():
        for r in other_layers:
            r[...] = jnp.zeros_like(r)
        m_scr[...] = jnp.zeros_like(m_scr)
        if zero_state:
            c_ref[...] = jnp.zeros_like(c_ref)
            n_ref[...] = jnp.zeros_like(n_ref)
            cb_ref[...] = jnp.zeros_like(cb_ref)
        else:
            c_ref[...] = c0_ref[...]
            n_ref[...] = n0_ref[...]
            cb_ref[...] = cb0_ref[...]
            for bi in range(bb):
                m_scr[bi:bi + 1, 0:MH] = m0_ref[pl.ds(bi0 + bi, 1), :]

    row = lax.broadcasted_iota(jnp.int32, (chunk, chunk), 0)
    col = lax.broadcasted_iota(jnp.int32, (chunk, chunk), 1)
    causal = col <= row
    tril = jnp.where(causal, 1.0, 0.0).astype(BF16)
    rowg = lax.broadcasted_iota(jnp.int32, (chunk, GATE_W), 0)
    rowi = lax.broadcasted_iota(jnp.int32, (chunk, D_CONV), 0)
    spread = jnp.where((lax.broadcasted_iota(jnp.int32, (2 * GATE_W, D_MLSTM), 0) & (GATE_W - 1))
                       == (lax.broadcasted_iota(jnp.int32, (2 * GATE_W, D_MLSTM), 1) >> 7), 1.0, 0.0).astype(BF16)
    scale = DH ** -0.5

    units = [(bi, h) for bi in range(bb) for h in range(MH)]
    hsl = lambda h: slice(h * DH, (h + 1) * DH)
    g_c = [gate_ref[bi] + gb_ref[...] for bi in range(bb)]
    csum_all = _dot(tril, jnp.concatenate(
        [_split_bf16(pltpu.roll(_log_sigmoid(g), GATE_W - MH, 1), 3) for g in g_c], axis=1))
    csum = [csum_all[:, bi * 3 * GATE_W:(bi + 1) * 3 * GATE_W] for bi in range(bb)]
    qb = {u: q_ref[u[0], :, hsl(u[1])].astype(BF16) for u in units}
    kb = {u: (k_ref[u[0], :, hsl(u[1])] * scale).astype(BF16) for u in units}
    s_qk = {u: _dot_nt(qb[u], kb[u]) for u in units}
    inter = {u: _dot_nt(qb[u], jnp.concatenate(
        [c_ref[u[0], u[1]].astype(BF16),
         jnp.broadcast_to(n_ref[u[0], u[1]:u[1] + 1, :], (DH, DH)).astype(BF16)], axis=0)) for u in units}
    stacked, a_r, decay = [], [], []
    for bi in range(bb):
        f_c = csum[bi][:, :GATE_W] + csum[bi][:, GATE_W:2 * GATE_W] + csum[bi][:, 2 * GATE_W:]
        a_c = g_c[bi] - f_c
        m_prev = m_scr[bi:bi + 1, :]
        big_m = jnp.maximum(_cummax_rows(a_c, rowg), m_prev)
        mt = f_c + big_m
        m_new = mt[chunk - 1:chunk, :]
        f_last = f_c[chunk - 1:chunk, :]
        decay.append(jnp.exp(f_last + m_prev - m_new))
        stacked += [_split_bf16(-big_m, 2), _split_bf16(jnp.exp(m_prev - big_m), 2),
                    _split_bf16(-mt, 2), _split_bf16(jnp.exp(a_c + (f_last - m_new)), 2)]
        a_r.append(a_c.T[0:SUBLANES, :])
        m_scr[bi:bi + 1, :] = m_new
        m_ref[pl.ds(bi0 + bi, 1), :] = m_new[:, 0:MH]
    rep_all = _dot(jnp.concatenate(stacked, axis=0), spread)
    rep = [rep_all[bi * 4 * chunk:(bi + 1) * 4 * chunk] for bi in range(bb)]
    intra, rsum = {}, {}
    for u in units:
        bi, h = u
        w = jnp.exp(jnp.where(causal, rep[bi][0:chunk, hsl(h)] + a_r[bi][h:h + 1, :], NEG))
        sqk = s_qk[u] * w
        rsum[u] = jnp.sum(sqk, axis=-1, keepdims=True)
        intra[u] = _dot(sqk.astype(BF16), v_ref[bi, :, hsl(h)].astype(BF16))
    hm, ssq = {}, {}
    for u in units:
        bi, h = u
        s_inter = rep[bi][chunk:2 * chunk, hsl(h)]
        e_floor = jnp.exp(rep[bi][2 * chunk:3 * chunk, hsl(h)])
        num = intra[u] + s_inter * inter[u][:, :DH]
        den = rsum[u] + s_inter * inter[u][:, DH:]
        hm[u] = jax.nn.sigmoid(o_ref[bi, :, hsl(h)]) * (num / jnp.maximum(jnp.abs(den), e_floor))
        ssq[u] = jnp.sum(hm[u] * hm[u], axis=-1, keepdims=True)
    for u in units:
        bi, h = u
        oa_ref[bi, :, hsl(h)] = hm[u] * lax.rsqrt(ssq[u] * (1.0 / DH) + EPS) * mg_ref[:, hsl(h)]
        wk = rep[bi][3 * chunk:, hsl(h)]
        dec = decay[bi][:, h:h + 1]
        c_ref[bi, h] = dec * c_ref[bi, h] + _dot_tn((v_ref[bi, :, hsl(h)] * wk).astype(BF16), kb[u])
        n_ref[bi, h:h + 1, :] = (dec * n_ref[bi, h:h + 1, :]
                                 + jnp.sum(k_ref[bi, :, hsl(h)] * scale * wk, axis=0, keepdims=True))
    for bi in range(bb):
        z = cg_ref[bi] * hc_ref[bi]
        cb = cb_ref[bi]
        z1 = jnp.where(rowi == 0, cb[1:2, :], pltpu.roll(z, 1, 0))
        z2 = jnp.where(rowi == 0, cb[0:1, :], jnp.where(rowi == 1, cb[1:2, :], pltpu.roll(z, 2, 0)))
        yc = cw_ref[0:1, :] * z2 + cw_ref[1:2, :] * z1 + cw_ref[2:3, :] * z
        oc_ref[bi] = _rms(bg_ref[bi] * yc) * mg_ref[:, D_MLSTM + D_S5:]
        cb_ref[bi] = z[chunk - 2:chunk, :]


def _mlstm_conv(qkvo, gate, bch, w, states_in, states_prev, *, layer, depth, bb, chunk):
    b, s, _ = qkvo.shape
    assert s % chunk == 0 and b % bb == 0
    zero_state = states_in is None
    kern = functools.partial(_mlstm_conv_kernel, layer=layer, bb=bb, chunk=chunk, zero_state=zero_state)
    col = lambda wd, j: pl.BlockSpec((bb, chunk, wd), lambda i, c: (i, c, j))
    st = lambda shape: pl.BlockSpec((None, bb) + shape, lambda i, c: (layer, i) + (0,) * len(shape))
    m_spec = pl.BlockSpec((None, b, MH), lambda i, c: (layer, 0, 0))
    cst = lambda a: pl.BlockSpec(a.shape, lambda i, c: (0,) * a.ndim)
    state_specs = [st((MH, DH, DH)), st((MH, DH)), m_spec, st((CONV_W - 1, D_CONV))]
    out_state_specs = state_specs
    if layer == 0:
        st0 = lambda shape: pl.BlockSpec((depth, bb) + shape, lambda i, c: (0, i) + (0,) * len(shape))
        out_state_specs = [st0((MH, DH, DH)), st0((MH, DH)), pl.BlockSpec((depth, b, MH), lambda i, c: (0, 0, 0)),
                           st0((CONV_W - 1, D_CONV))]
    consts = (w["gb"], w["cw"], w["mg"])
    args = [qkvo, qkvo, qkvo, qkvo, gate, bch, bch, bch, *consts]
    in_specs = [col(D_MLSTM, 0), col(D_MLSTM, 1), col(D_MLSTM, 2), col(D_MLSTM, 3), col(GATE_W, 0),
                col(D_CONV, 0), col(D_CONV, 1), col(D_CONV, 2)] + [cst(a) for a in consts]
    if not zero_state:
        args += list(states_in)
        in_specs += state_specs
    aliases = {}
    if layer > 0:
        aliases = {len(args) + j: 2 + j for j in range(4)}
        args += list(states_prev)
        in_specs += [_ANY] * 4
    return pl.pallas_call(
        kern,
        grid=(b // bb, s // chunk),
        in_specs=in_specs,
        out_specs=[col(D_MLSTM, 0), col(D_CONV, 0)] + out_state_specs,
        out_shape=[jax.ShapeDtypeStruct((b, s, D_MLSTM), F32), jax.ShapeDtypeStruct((b, s, D_CONV), F32)]
        + _state_shapes(depth, b),
        scratch_shapes=[pltpu.VMEM((-(-bb // SUBLANES) * SUBLANES, GATE_W), F32)],
        input_output_aliases=aliases,
        compiler_params=pltpu.CompilerParams(dimension_semantics=("arbitrary", "arbitrary"),
                                             vmem_limit_bytes=VMEM_LIMIT),
        name="mlstm_conv",
    )(*args)


def _gate_terms(graw, gb_ref, gbt_ref, tril, triu):
    sel = (lax.broadcasted_iota(jnp.int32, (SUBLANES, GATE_W), 0)
           == lax.broadcasted_iota(jnp.int32, (SUBLANES, GATE_W), 1)).astype(F32)
    g_c = graw + gb_ref[...]
    lf_c = _log_sigmoid(g_c)
    g_r = _dot_nt_exact(sel, graw) + gbt_ref[...]
    lf_r = _log_sigmoid(g_r)
    return g_c, lf_c, _dot_exact(tril, lf_c), g_r, _dot_exact(lf_r, triu)


def _head_scores(h, g_c, bt_c, g_r, bt_r, m_prev, mask):
    it_row = g_r[h:h + 1, :]
    bt_row = bt_r[MH + h:MH + h + 1, :]
    it_col = g_c[:, h:h + 1]
    bt_col = bt_c[:, MH + h:MH + h + 1]
    dmat = jnp.where(mask, bt_col - bt_row + it_row, NEG)
    inter = bt_col + m_prev
    mt = jnp.maximum(jnp.max(dmat, axis=-1, keepdims=True), inter)
    return jnp.exp(dmat - mt), jnp.exp(inter - mt), mt, it_col, bt_col


def _head_output(sqk, vb, s_inter, qc, qn, mt, o, mg):
    num = _dot(sqk.astype(BF16), vb) + s_inter * qc
    den = jnp.sum(sqk, axis=-1, keepdims=True) + s_inter * qn
    hm = jax.nn.sigmoid(o) * (num / jnp.maximum(jnp.abs(den), jnp.exp(-mt)))
    return _rms(hm) * mg


def _bcast_block_last(x, t, s):
    out = x
    for d in range(1, s):
        out = jnp.where(t == s - 1 - d, pltpu.roll(x, x.shape[0] - d, 0), out)
    return out


def _mlstm_conv_short_kernel(*refs, layer, nb, s):
    (q_ref, k_ref, v_ref, o_ref, gate_ref, bg_ref, cg_ref, hc_ref, gb_ref, gbt_ref, cw_ref, mg_ref,
     c0_ref, n0_ref, m0_ref, cb0_ref) = refs[:16]
    refs = refs[16:]
    if layer > 0:
        refs = refs[4:]
    oa_ref, oc_ref, c_ref, n_ref, m_ref, cb_ref, qc_scr, st_scr = refs
    if layer == 0:
        for ref in (c_ref, n_ref, m_ref, cb_ref):
            if ref.shape[0] > 1:
                ref[1:] = jnp.zeros((ref.shape[0] - 1,) + ref.shape[1:], F32)
        c_ref, n_ref, m_ref, cb_ref = c_ref.at[0], n_ref.at[0], m_ref.at[0], cb_ref.at[0]
    r = nb * s
    per_tile = SUBLANES // s
    shift = s.bit_length() - 1
    row = lax.broadcasted_iota(jnp.int32, (r, r), 0)
    col = lax.broadcasted_iota(jnp.int32, (r, r), 1)
    same = (row >> shift) == (col >> shift)
    mask = same & (col <= row)
    tril = mask.astype(F32)
    triu = (same & (row <= col)).astype(F32)
    expand = ((lax.broadcasted_iota(jnp.int32, (r, nb), 0) >> shift)
              == lax.broadcasted_iota(jnp.int32, (r, nb), 1)).astype(F32)
    gather = (lax.broadcasted_iota(jnp.int32, (nb, r), 0)
              == (lax.broadcasted_iota(jnp.int32, (nb, r), 1) >> shift)).astype(F32)
    t128 = lax.broadcasted_iota(jnp.int32, (r, DH), 0) & (s - 1)
    sub8 = lax.broadcasted_iota(jnp.int32, (SUBLANES, DH), 0) >> shift
    scale = DH ** -0.5

    g_c, lf_c, bt_c, g_r, bt_r = _gate_terms(gate_ref[...], gb_ref, gbt_ref, tril, triu)
    bt_last_c = _dot_exact(same.astype(F32), lf_c)
    m_rows = _dot_exact(expand, m0_ref[...])
    for h in range(MH):
        hs = slice(h * DH, (h + 1) * DH)
        m_prev = m_rows[:, h:h + 1]
        w, s_inter, mt, it_col, bt_col = _head_scores(h, g_c, bt_c, g_r, bt_r, m_prev, mask)
        qh = q_ref[:, hs]
        kh = k_ref[:, hs] * scale
        vh = v_ref[:, hs]
        qb = qh.astype(BF16)
        kb = kh.astype(BF16)
        sqk = _dot_nt(qb, kb) * w
        for j in range(r // SUBLANES):
            rows = slice(j * SUBLANES, (j + 1) * SUBLANES)
            acc = None
            q8 = q_ref[rows, hs].astype(BF16)
            for p in range(per_tile):
                part = _dot_nt(q8, c0_ref[j * per_tile + p, h].astype(BF16))
                acc = part if acc is None else jnp.where(sub8 == p, part, acc)
            qc_scr[rows, :] = acc
        n_rows = _dot_exact(expand, n0_ref[:, h, :])
        qn = jnp.sum(qh * n_rows, axis=-1, keepdims=True)
        oa_ref[:, hs] = _head_output(sqk, vh.astype(BF16), s_inter, qc_scr[...], qn, mt,
                                     o_ref[:, hs], mg_ref[:, hs])
        m_new = _bcast_block_last(jnp.broadcast_to(mt, (r, DH)), t128, s)
        bt_last = bt_last_c[:, MH + h:MH + h + 1]
        wk = jnp.exp(bt_last - bt_col + it_col - m_new)
        decay = jnp.exp(bt_last + m_prev - m_new)
        vw = vh * wk
        for j in range(r // SUBLANES):
            rows = slice(j * SUBLANES, (j + 1) * SUBLANES)
            k8 = (k_ref[rows, hs] * scale).astype(BF16)
            for p in range(per_tile):
                bidx = j * per_tile + p
                last = bidx * s + s - 1
                upd = _dot_tn(jnp.where(sub8 == p, vw[rows], 0.0).astype(BF16), k8)
                c_ref[bidx, h] = decay[last:last + 1, 0:1] * c0_ref[bidx, h] + upd
        st_scr[0] = decay
        st_scr[1] = m_new
        last_rows = pl.ds(s - 1, nb, stride=s)
        n_ref[:, h, :] = st_scr[0, last_rows, :] * n0_ref[:, h, :] + _dot_exact(gather, kh * wk)
        m_ref[:, h:h + 1] = st_scr[1, last_rows, :][:, 0:1]
    t256 = lax.broadcasted_iota(jnp.int32, (r, D_CONV), 0) & (s - 1)
    z = cg_ref[...] * hc_ref[...]
    cb_a = _dot_exact(expand, cb0_ref[:, 0, :])
    cb_b = _dot_exact(expand, cb0_ref[:, 1, :])
    z1 = jnp.where(t256 == 0, cb_b, pltpu.roll(z, 1, 0))
    z2 = jnp.where(t256 == 0, cb_a, jnp.where(t256 == 1, cb_b, pltpu.roll(z, 2, 0)))
    yc = cw_ref[0:1, :] * z2 + cw_ref[1:2, :] * z1 + cw_ref[2:3, :] * z
    oc_ref[...] = _rms(bg_ref[...] * yc) * mg_ref[:, D_MLSTM + D_S5:]
    for half in range(D_CONV // DH):
        lanes = slice(half * DH, (half + 1) * DH)
        st_scr[half] = z[:, lanes]
        cb_ref[:, 0, lanes] = st_scr[half, pl.ds(s - 2, nb, stride=s), :]
        cb_ref[:, 1, lanes] = st_scr[half, pl.ds(s - 1, nb, stride=s), :]


def _short_batch_block(b, depth, layer):
    out_layers = depth if layer == 0 else 1
    nb = min(b, 32)
    while nb > SUBLANES and 2 * (1 + out_layers) * nb * MH * DH * DH * 4 > STATE_VMEM_BUDGET:
        nb //= 2
    return nb


def _mlstm_conv_short(qkvo, gate, bch, w, states_in, states_prev, *, layer, depth, b, s):
    nb = _short_batch_block(b, depth, layer)
    assert SUBLANES % s == 0 and s >= CONV_W - 1 and b % nb == 0 and (nb * s) % SUBLANES == 0
    r = nb * s
    kern = functools.partial(_mlstm_conv_short_kernel, layer=layer, nb=nb, s=s)
    col = lambda wd, j: pl.BlockSpec((r, wd), lambda i: (i, j))
    st = lambda shape: pl.BlockSpec((None, nb) + shape, lambda i: (layer, i) + (0,) * len(shape))
    cst = lambda a: pl.BlockSpec(a.shape, lambda i: (0,) * a.ndim)
    state_specs = [st((MH, DH, DH)), st((MH, DH)), st((MH,)), st((CONV_W - 1, D_CONV))]
    out_state_specs = state_specs
    if layer == 0:
        st0 = lambda shape: pl.BlockSpec((depth, nb) + shape, lambda i: (0, i) + (0,) * len(shape))
        out_state_specs = [st0((MH, DH, DH)), st0((MH, DH)), st0((MH,)), st0((CONV_W - 1, D_CONV))]
    consts = (w["gb"], w["gbt"], w["cw"], w["mg"])
    args = [qkvo, qkvo, qkvo, qkvo, gate, bch, bch, bch, *consts, *states_in]
    in_specs = [col(D_MLSTM, 0), col(D_MLSTM, 1), col(D_MLSTM, 2), col(D_MLSTM, 3), col(GATE_W, 0),
                col(D_CONV, 0), col(D_CONV, 1), col(D_CONV, 2)] + [cst(a) for a in consts] + state_specs
    aliases = {}
    if layer > 0:
        aliases = {len(args) + j: 2 + j for j in range(4)}
        args += list(states_prev)
        in_specs += [_ANY] * 4
    return pl.pallas_call(
        kern,
        grid=(b // nb,),
        in_specs=in_specs,
        out_specs=[col(D_MLSTM, 0), col(D_CONV, 0)] + out_state_specs,
        out_shape=[jax.ShapeDtypeStruct((b * s, D_MLSTM), F32), jax.ShapeDtypeStruct((b * s, D_CONV), F32)]
        + _state_shapes(depth, b),
        scratch_shapes=[pltpu.VMEM((r, DH), F32), pltpu.VMEM((2, r, DH), F32)],
        input_output_aliases=aliases,
        compiler_params=pltpu.CompilerParams(dimension_semantics=("arbitrary",), vmem_limit_bytes=VMEM_LIMIT),
        name="mlstm_conv_short",
    )(*args)


def _s5_kernel(*refs, layer, nb, tt, zero_state, parts, batch_major):
    u_ref, lam_ref, bblk_ref, cblk_ref, d_ref, wglu_ref, mg_ref = refs[:7]
    refs = refs[7:]
    if not zero_state:
        hr0_ref, hi0_ref = refs[:2]
        refs = refs[2:]
    if layer > 0:
        refs = refs[2:]
    ob_ref, hr_ref, hi_ref, xs_ref, a_ref, bf_ref = refs[:6]
    if batch_major:
        ut_scr, ot_scr = refs[6:]
        ut_scr[...] = jnp.swapaxes(u_ref[...], 0, 1).reshape(tt * nb, D_S5)
        u_bm_ref, ob_bm_ref, u_ref, ob_ref = u_ref, ob_ref, ut_scr, ot_scr
    if layer == 0:
        hr_all, hi_all = hr_ref, hi_ref
        hr_ref, hi_ref = hr_all.at[0], hi_all.at[0]
    ti = pl.program_id(0)
    n = S5_N
    lc = 512

    @pl.when(ti == 0)
    def _():
        if layer == 0:
            hr_all[...] = jnp.zeros_like(hr_all)
            hi_all[...] = jnp.zeros_like(hi_all)
        hr_ref[...] = jnp.zeros_like(hr_ref) if zero_state else hr0_ref[...]
        hi_ref[...] = jnp.zeros_like(hi_ref) if zero_state else hi0_ref[...]
        lre = jnp.minimum(lam_ref[0:1, :], -1e-4)
        lim = lam_ref[1:2, :]
        dt = jnp.exp(lam_ref[2:3, :])
        mag = jnp.exp(lre * dt)
        ab_re = mag * jnp.cos(lim * dt)
        ab_im = mag * jnp.sin(lim * dt)
        den = lre * lre + lim * lim
        nr = ab_re - 1.0
        fre = (nr * lre + ab_im * lim) / den
        fim = (ab_im * lre - nr * lim) / den
        a_ref[:, :n] = jnp.broadcast_to(ab_re, (SUBLANES, n))
        a_ref[:, n:] = jnp.broadcast_to(ab_im, (SUBLANES, n))
        b_re = bblk_ref[:, :n]
        b_im = bblk_ref[:, n:]
        bf_ref[:, :n] = (fre * b_re - fim * b_im).astype(BF16)
        bf_ref[:, n:] = (fre * b_im + fim * b_re).astype(BF16)

    rows_p = nb * tt // parts
    part_rows = [slice(p * rows_p, (p + 1) * rows_p) for p in range(parts)]
    xs_ref[...] = _dot(u_ref[...].astype(BF16), bf_ref[...])
    ar = [a_ref[:, c0:c0 + lc] for c0 in range(0, n, lc)]
    ai = [a_ref[:, n + c0:n + c0 + lc] for c0 in range(0, n, lc)]
    h, y = {}, []
    for p, rs in enumerate(part_rows):
        for rg in range(nb // SUBLANES):
            g8 = slice(rg * SUBLANES, (rg + 1) * SUBLANES)
            for ci, c0 in enumerate(range(0, n, lc)):
                if p == 0:
                    h[rg, ci] = (hr_ref[g8, c0:c0 + lc], hi_ref[g8, c0:c0 + lc])
            for t in range(p * tt // parts, (p + 1) * tt // parts):
                r8 = slice(t * nb + rg * SUBLANES, t * nb + (rg + 1) * SUBLANES)
                for ci, c0 in enumerate(range(0, n, lc)):
                    hr, hi = h[rg, ci]
                    nr_ = ar[ci] * hr - ai[ci] * hi + xs_ref[r8, c0:c0 + lc]
                    ni_ = ar[ci] * hi + ai[ci] * hr + xs_ref[r8, n + c0:n + c0 + lc]
                    xs_ref[r8, c0:c0 + lc] = nr_
                    xs_ref[r8, n + c0:n + c0 + lc] = ni_
                    h[rg, ci] = (nr_, ni_)
            for ci, c0 in enumerate(range(0, n, lc)):
                if p == parts - 1:
                    hr_ref[g8, c0:c0 + lc] = h[rg, ci][0]
                    hi_ref[g8, c0:c0 + lc] = h[rg, ci][1]
        y.append(_dot(xs_ref[rs, :].astype(BF16), cblk_ref[...]) + d_ref[...] * u_ref[rs, :])
    g = [jax.nn.gelu(v) for v in y]
    gate = [_dot(v.astype(BF16), wglu_ref[...]) for v in g]
    for rs, v, z in zip(part_rows, g, gate):
        ob_ref[rs, :] = _rms(v * jax.nn.sigmoid(z)) * mg_ref[:, D_MLSTM:D_MLSTM + D_S5]
    if batch_major:
        ob_bm_ref[...] = jnp.swapaxes(ot_scr[...].reshape(tt, nb, D_S5), 0, 1)


def _s5(u, w, states_in, states_prev, *, layer, depth, nb, tt):
    batch_major = u.ndim == 3
    rows = u.shape[0] * u.shape[1] if batch_major else u.shape[0]
    rt = nb * tt
    zero_state = states_in is None
    parts = S5_PARTS if tt % (SUBLANES * S5_PARTS) == 0 else 1
    kern = functools.partial(_s5_kernel, layer=layer, nb=nb, tt=tt, zero_state=zero_state, parts=parts,
                             batch_major=batch_major)
    if batch_major:
        assert u.shape[0] == nb and tt % SUBLANES == 0
        io_spec = pl.BlockSpec((nb, tt, D_S5), lambda i: (0, i, 0))
        io_shape = jax.ShapeDtypeStruct(u.shape, F32)
        io_scratch = [pltpu.VMEM((rt, D_S5), F32), pltpu.VMEM((rt, D_S5), F32)]
    else:
        io_spec = pl.BlockSpec((rt, D_S5), lambda i: (i, 0))
        io_shape = jax.ShapeDtypeStruct((rows, D_S5), F32)
        io_scratch = []
    cst = lambda a: pl.BlockSpec(a.shape, lambda i: (0,) * a.ndim)
    st = pl.BlockSpec((None, nb, S5_N), lambda i: (layer, 0, 0))
    st_out = pl.BlockSpec((depth, nb, S5_N), lambda i: (0, 0, 0)) if layer == 0 else st
    consts = (w["lam"], w["bblk"], w["cblk"], w["d"], w["wglu"], w["mg"])
    args = [u, *consts]
    in_specs = [io_spec] + [cst(a) for a in consts]
    if not zero_state:
        args += list(states_in)
        in_specs += [st, st]
    aliases = {}
    if layer > 0:
        aliases = {len(args) + j: 1 + j for j in range(2)}
        args += list(states_prev)
        in_specs += [_ANY] * 2
    return pl.pallas_call(
        kern,
        grid=(rows // rt,),
        in_specs=in_specs,
        out_specs=[io_spec, st_out, st_out],
        out_shape=[io_shape, jax.ShapeDtypeStruct((depth, nb, S5_N), F32),
                   jax.ShapeDtypeStruct((depth, nb, S5_N), F32)],
        scratch_shapes=[pltpu.VMEM((rt, 2 * S5_N), F32), pltpu.VMEM((SUBLANES, 2 * S5_N), F32),
                        pltpu.VMEM((D_S5, 2 * S5_N), BF16)] + io_scratch,
        input_output_aliases=aliases,
        compiler_params=pltpu.CompilerParams(dimension_semantics=("arbitrary",), vmem_limit_bytes=VMEM_LIMIT),
        name="s5",
    )(*args)


def _block_diag(w):
    g, r, c = w.shape
    eye = jnp.eye(g, dtype=w.dtype)
    return jnp.einsum("grc,gh->grhc", w, eye).reshape(g * r, g * c)


def _layer_weights(l, ig_bias, fg_bias, s5_a_re, s5_a_im, s5_log_dt, s5_b_re, s5_b_im, s5_c_re, s5_c_im, s5_d,
                   w_glu, conv_w, mix_g):
    gb = jnp.concatenate([ig_bias[l], fg_bias[l], jnp.zeros((GATE_W - 2 * MH,), F32)])
    lam = jnp.stack([s5_a_re[l].reshape(-1), s5_a_im[l].reshape(-1),
                     jnp.repeat(s5_log_dt[l], S5_P)])
    bblk = jnp.concatenate([_block_diag(jnp.swapaxes(s5_b_re[l], 1, 2)),
                            _block_diag(jnp.swapaxes(s5_b_im[l], 1, 2))], axis=1)
    cblk = jnp.concatenate([_block_diag(jnp.swapaxes(s5_c_re[l], 1, 2)),
                            -_block_diag(jnp.swapaxes(s5_c_im[l], 1, 2))], axis=0).astype(BF16)
    return dict(
        gb=gb.reshape(1, GATE_W), gbt=gb[:SUBLANES].reshape(SUBLANES, 1),
        lam=lam, bblk=bblk, cblk=cblk, d=s5_d[l].reshape(1, D_S5), wglu=w_glu[l].astype(BF16),
        cw=conv_w[l], mg=mix_g[l].reshape(1, D_MODEL))


def _stacked_weights(ffn1_w1, ffn1_w3, ffn1_w2, ffn2_w1, ffn2_w3, ffn2_w2, norm_g, w_in, w_out):
    o = 4 * D_MLSTM
    win = (w_in[:, :, :o].astype(BF16),
           jnp.pad(w_in[:, :, o:o + 2 * MH].astype(BF16), ((0, 0), (0, 0), (0, GATE_W - 2 * MH))),
           w_in[:, :, o + 2 * MH:].astype(BF16))
    return dict(f1=(ffn1_w1.astype(BF16), ffn1_w3.astype(BF16), ffn1_w2.astype(BF16)),
                f2=(ffn2_w1.astype(BF16), ffn2_w3.astype(BF16), ffn2_w2.astype(BF16)),
                g=norm_g, win=win, wo=w_out.astype(BF16))


def _run_group(x, states, weights, big):
    b, s, _ = x.shape
    depth = len(weights)
    tm = min(FFN_ROWS, b * s)
    assert (b * s) % tm == 0
    long_seq = s % MLSTM_CHUNK == 0 and b % MLSTM_ROWS == 0
    chunk, bb = (MLSTM_CHUNK, MLSTM_ROWS) if long_seq else (None, None)
    tt = S5_STEPS if s % S5_STEPS == 0 else s
    xf = x.reshape(b * s, D_MODEL)
    mstates = sstates = None
    if states is not None:
        c_all, n_all, m_all, sr_all, si_all, cb_all = states
        ms_in = (c_all, n_all, m_all, cb_all)
        ss_in = (sr_all.reshape(depth, b, S5_N), si_all.reshape(depth, b, S5_N))
    else:
        ms_in = ss_in = None
    for l, w in enumerate(weights):
        x1, qkvo, gate, u, bch = _ffn_inproj(xf, big["g"], *big["f1"], big["win"], tm, l)
        if tt % SUBLANES == 0:
            ob, *sstates = _s5(u.reshape(b, s, D_S5), w, ss_in, sstates, layer=l, depth=depth, nb=b, tt=tt)
            ob = ob.reshape(b * s, D_S5)
        else:
            u_tm = jnp.swapaxes(u.reshape(b, s, D_S5), 0, 1).reshape(s * b, D_S5)
            ob_tm, *sstates = _s5(u_tm, w, ss_in, sstates, layer=l, depth=depth, nb=b, tt=tt)
            ob = jnp.swapaxes(ob_tm.reshape(s, b, D_S5), 0, 1).reshape(b * s, D_S5)
        if chunk is not None:
            r3 = lambda a: a.reshape(b, s, a.shape[-1])
            oa, oc, *mstates = _mlstm_conv(r3(qkvo), r3(gate), r3(bch), w, ms_in, mstates,
                                           layer=l, depth=depth, bb=bb, chunk=chunk)
            xf = _outproj_ffn(x1, oa.reshape(b * s, D_MLSTM), ob, oc.reshape(b * s, D_CONV),
                              big["g"], big["wo"], *big["f2"], tm, l)
        else:
            oa, oc, *mstates = _mlstm_conv_short(qkvo, gate, bch, w, ms_in, mstates,
                                                 layer=l, depth=depth, b=b, s=s)
            xf = _outproj_ffn(x1, oa, ob, oc, big["g"], big["wo"], *big["f2"], tm, l)
    c1, n1, m1, cb1 = mstates
    sr1, si1 = sstates
    return xf.reshape(b, s, D_MODEL), (c1, n1, m1, sr1.reshape(depth, b, S5_G, S5_P),
                                       si1.reshape(depth, b, S5_G, S5_P), cb1)


def kernel(x_prompt, x_sample, state_mlstm_C, state_mlstm_n, state_mlstm_m, state_s5_re, state_s5_im,
           state_conv, ffn1_w1, ffn1_w3, ffn1_w2, ffn2_w1, ffn2_w3, ffn2_w2, norm_g, w_in, ig_bias, fg_bias,
           s5_a_re, s5_a_im, s5_log_dt, s5_b_re, s5_b_im, s5_c_re, s5_c_im, s5_d, w_glu, conv_w, mix_g, w_out):
    depth = norm_g.shape[0]
    params = (ig_bias, fg_bias, s5_a_re, s5_a_im, s5_log_dt, s5_b_re, s5_b_im, s5_c_re, s5_c_im, s5_d, w_glu,
              conv_w, mix_g)
    weights = [_layer_weights(l, *params) for l in range(depth)]
    big = _stacked_weights(ffn1_w1, ffn1_w3, ffn1_w2, ffn2_w1, ffn2_w3, ffn2_w2, norm_g, w_in, w_out)
    y_p, st_p = _run_group(x_prompt, None, weights, big)
    y_s, st_s = _run_group(
        x_sample, (state_mlstm_C, state_mlstm_n, state_mlstm_m, state_s5_re, state_s5_im, state_conv),
        weights, big)
    return (y_p, y_s, *st_p, *st_s)
```

```python
import functools

import jax
import jax.numpy as jnp
from jax import lax
from jax.experimental import pallas as pl
from jax.experimental.pallas import tpu as pltpu

F32 = jnp.float32
BF16 = jnp.bfloat16

D_MODEL = 1024
MH = 4
DH = 128
D_MLSTM = MH * DH
S5_CH = 16
S5_G = 16
S5_P = 64
D_S5 = S5_G * S5_CH
S5_N = S5_G * S5_P
D_CONV = 256
CONV_W = 3
EPS = 1e-6
GATE_W = 128
SUBLANES = 8
NEG = -1e30
VMEM_LIMIT = 56 * 1024 * 1024
STATE_VMEM_BUDGET = 36 * 1024 * 1024
FFN_ROWS = 512
ROW_PARTS = 2
S5_STEPS = 256
S5_PARTS = 2
MLSTM_CHUNK = DH
MLSTM_ROWS = 8


def _dot(a, b):
    return jnp.dot(a, b, preferred_element_type=F32)


def _dot_nt(a, b):
    return lax.dot_general(a, b, (((1,), (1,)), ((), ())), preferred_element_type=F32)


def _dot_tn(a, b):
    return lax.dot_general(a, b, (((0,), (0,)), ((), ())), preferred_element_type=F32)


def _dot_exact(a, b):
    return jnp.dot(a, b, preferred_element_type=F32, precision=lax.Precision.HIGHEST)


def _dot_nt_exact(a, b):
    return lax.dot_general(a, b, (((1,), (1,)), ((), ())), preferred_element_type=F32,
                           precision=lax.Precision.HIGHEST)


def _rms(x):
    return x * lax.rsqrt(jnp.mean(x * x, axis=-1, keepdims=True) + EPS)


def _log_sigmoid(x):
    return jnp.minimum(x, 0.0) - jnp.log(1.0 + jnp.exp(-jnp.abs(x)))


def _split_bf16(x, parts):
    out = []
    for _ in range(parts):
        p = x.astype(BF16)
        out.append(p)
        x = x - p.astype(F32)
    return jnp.concatenate(out, axis=1)


def _ffn_residual(xs, g_pre, g_post, w1, w3, w2):
    xn = [(_rms(x) * g_pre).astype(BF16) for x in xs]
    h1 = [_dot(v, w1[...]) for v in xn]
    h3 = [_dot(v, w3[...]) for v in xn]
    a = [(jax.nn.silu(p) * q).astype(BF16) for p, q in zip(h1, h3)]
    y = [_dot(v, w2[...]) for v in a]
    return [x + 0.5 * (_rms(t) * g_post) for x, t in zip(xs, y)]


def _row_parts(tm, parts):
    step = tm // parts
    return [slice(i * step, (i + 1) * step) for i in range(parts)]


def _layer_spec(a, layer):
    return pl.BlockSpec((None,) + a.shape[1:], lambda *_: (layer, 0, 0), pipeline_mode=pl.Buffered(1))


_ANY = pl.BlockSpec(memory_space=pl.ANY)


def _ffn_inproj_kernel(x_ref, g_ref, w1_ref, w3_ref, w2_ref, wq_ref, wg_ref, wr_ref,
                       x1_ref, qkvo_ref, gate_ref, u_ref, bch_ref):
    parts = _row_parts(x_ref.shape[0], ROW_PARTS)
    x1 = _ffn_residual([x_ref[r, :] for r in parts], g_ref[0:1, :], g_ref[1:2, :], w1_ref, w3_ref, w2_ref)
    hn = [(_rms(v) * g_ref[2:3, :]).astype(BF16) for v in x1]
    qkvo = [_dot(v, wq_ref[...]) for v in hn]
    gate = [_dot(v, wg_ref[...]) for v in hn]
    rest = [_dot(v, wr_ref[...]) for v in hn]
    for i, r in enumerate(parts):
        x1_ref[r, :] = x1[i]
        qkvo_ref[r, :] = qkvo[i]
        gate_ref[r, :] = gate[i]
        u_ref[r, :] = rest[i][:, :D_S5]
        bch_ref[r, :] = rest[i][:, D_S5:]


def _ffn_inproj(x, g, w1, w3, w2, win, tm, layer):
    t = x.shape[0]
    row = lambda w: pl.BlockSpec((tm, w), lambda i: (i, 0))
    return pl.pallas_call(
        _ffn_inproj_kernel,
        grid=(t // tm,),
        in_specs=[row(D_MODEL), _layer_spec(g, layer), _layer_spec(w1, layer), _layer_spec(w3, layer),
                  _layer_spec(w2, layer)] + [_layer_spec(a, layer) for a in win],
        out_specs=[row(D_MODEL), row(4 * D_MLSTM), row(GATE_W), row(D_S5), row(3 * D_CONV)],
        out_shape=[jax.ShapeDtypeStruct((t, D_MODEL), F32), jax.ShapeDtypeStruct((t, 4 * D_MLSTM), F32),
                   jax.ShapeDtypeStruct((t, GATE_W), F32), jax.ShapeDtypeStruct((t, D_S5), F32),
                   jax.ShapeDtypeStruct((t, 3 * D_CONV), F32)],
        compiler_params=pltpu.CompilerParams(dimension_semantics=("arbitrary",), vmem_limit_bytes=VMEM_LIMIT),
        name="ffn_inproj",
    )(x, g, w1, w3, w2, *win)


def _outproj_ffn_kernel(x_ref, a_ref, b_ref, c_ref, g_ref, wo_ref, w1_ref, w3_ref, w2_ref, y_ref):
    parts = _row_parts(x_ref.shape[0], ROW_PARTS)
    mo = [(_dot(a_ref[r, :].astype(BF16), wo_ref[0:D_MLSTM, :])
           + _dot(b_ref[r, :].astype(BF16), wo_ref[D_MLSTM:D_MLSTM + D_S5, :])
           + _dot(c_ref[r, :].astype(BF16), wo_ref[D_MLSTM + D_S5:, :])) for r in parts]
    x2 = [x_ref[r, :] + _rms(v) * g_ref[3:4, :] for r, v in zip(parts, mo)]
    y = _ffn_residual(x2, g_ref[4:5, :], g_ref[5:6, :], w1_ref, w3_ref, w2_ref)
    for r, v in zip(parts, y):
        y_ref[r, :] = v


def _outproj_ffn(x, oa, ob, oc, g, wo, w1, w3, w2, tm, layer):
    t = x.shape[0]
    row = lambda w: pl.BlockSpec((tm, w), lambda i: (i, 0))
    return pl.pallas_call(
        _outproj_ffn_kernel,
        grid=(t // tm,),
        in_specs=[row(D_MODEL), row(D_MLSTM), row(D_S5), row(D_CONV), _layer_spec(g, layer),
                  _layer_spec(wo, layer), _layer_spec(w1, layer), _layer_spec(w3, layer),
                  _layer_spec(w2, layer)],
        out_specs=row(D_MODEL),
        out_shape=jax.ShapeDtypeStruct((t, D_MODEL), F32),
        compiler_params=pltpu.CompilerParams(dimension_semantics=("arbitrary",), vmem_limit_bytes=VMEM_LIMIT),
        name="outproj_ffn",
    )(x, oa, ob, oc, g, wo, w1, w3, w2)


def _state_shapes(depth, b):
    return [jax.ShapeDtypeStruct((depth, b, MH, DH, DH), F32), jax.ShapeDtypeStruct((depth, b, MH, DH), F32),
            jax.ShapeDtypeStruct((depth, b, MH), F32), jax.ShapeDtypeStruct((depth, b, CONV_W - 1, D_CONV), F32)]


def _cummax_rows(x, rowi):
    s = 1
    while s < x.shape[0]:
        x = jnp.maximum(x, jnp.where(rowi >= s, pltpu.roll(x, s, 0), NEG))
        s *= 2
    return x


def _mlstm_conv_kernel(*refs, layer, bb, chunk, zero_state):
    (q_ref, k_ref, v_ref, o_ref, gate_ref, bg_ref, cg_ref, hc_ref, gb_ref, cw_ref, mg_ref) = refs[:11]
    refs = refs[11:]
    if not zero_state:
        c0_ref, n0_ref, m0_ref, cb0_ref = refs[:4]
        refs = refs[4:]
    if layer > 0:
        refs = refs[4:]
    oa_ref, oc_ref, c_ref, n_ref, m_ref, cb_ref, m_scr = refs
    other_layers = ()
    if layer == 0:
        other_layers = tuple(r.at[1:] for r in (c_ref, n_ref, m_ref, cb_ref) if r.shape[0] > 1)
        c_ref, n_ref, m_ref, cb_ref = c_ref.at[0], n_ref.at[0], m_ref.at[0], cb_ref.at[0]
    assert chunk == DH
    bi0 = pl.program_id(0) * bb
    ci = pl.program_id(1)

    @pl.when(ci == 0)
    def _():
        for r in other_layers:
            r[...] = jnp.zeros_like(r)
        m_scr[...] = jnp.zeros_like(m_scr)
        if zero_state:
            c_ref[...] = jnp.zeros_like(c_ref)
            n_ref[...] = jnp.zeros_like(n_ref)
            cb_ref[...] = jnp.zeros_like(cb_ref)
        else:
            c_ref[...] = c0_ref[...]
            n_ref[...] = n0_ref[...]
            cb_ref[...] = cb0_ref[...]
            for bi in range(bb):
                m_scr[bi:bi + 1, 0:MH] = m0_ref[pl.ds(bi0 + bi, 1), :]

    row = lax.broadcasted_iota(jnp.int32, (chunk, chunk), 0)
    col = lax.broadcasted_iota(jnp.int32, (chunk, chunk), 1)
    causal = col <= row
    tril = jnp.where(causal, 1.0, 0.0).astype(BF16)
    rowg = lax.broadcasted_iota(jnp.int32, (chunk, GATE_W), 0)
    rowi = lax.broadcasted_iota(jnp.int32, (chunk, D_CONV), 0)
    spread = jnp.where((lax.broadcasted_iota(jnp.int32, (2 * GATE_W, D_MLSTM), 0) & (GATE_W - 1))
                       == (lax.broadcasted_iota(jnp.int32, (2 * GATE_W, D_MLSTM), 1) >> 7), 1.0, 0.0).astype(BF16)
    scale = DH ** -0.5

    units = [(bi, h) for bi in range(bb) for h in range(MH)]
    hsl = lambda h: slice(h * DH, (h + 1) * DH)
    g_c = [gate_ref[bi] + gb_ref[...] for bi in range(bb)]
    csum_all = _dot(tril, jnp.concatenate(
        [_split_bf16(pltpu.roll(_log_sigmoid(g), GATE_W - MH, 1), 3) for g in g_c], axis=1))
    csum = [csum_all[:, bi * 3 * GATE_W:(bi + 1) * 3 * GATE_W] for bi in range(bb)]
    qb = {u: q_ref[u[0], :, hsl(u[1])].astype(BF16) for u in units}
    kb = {u: (k_ref[u[0], :, hsl(u[1])] * scale).astype(BF16) for u in units}
    s_qk = {u: _dot_nt(qb[u], kb[u]) for u in units}
    inter = {u: _dot_nt(qb[u], jnp.concatenate(
        [c_ref[u[0], u[1]].astype(BF16),
         jnp.broadcast_to(n_ref[u[0], u[1]:u[1] + 1, :], (DH, DH)).astype(BF16)], axis=0)) for u in units}
    stacked, a_r, decay = [], [], []
    for bi in range(bb):
        f_c = csum[bi][:, :GATE_W] + csum[bi][:, GATE_W:2 * GATE_W] + csum[bi][:, 2 * GATE_W:]
        a_c = g_c[bi] - f_c
        m_prev = m_scr[bi:bi + 1, :]
        big_m = jnp.maximum(_cummax_rows(a_c, rowg), m_prev)
        mt = f_c + big_m
        m_new = mt[chunk - 1:chunk, :]
        f_last = f_c[chunk - 1:chunk, :]
        decay.append(jnp.exp(f_last + m_prev - m_new))
        stacked += [_split_bf16(-big_m, 2), _split_bf16(jnp.exp(m_prev - big_m), 2),
                    _split_bf16(-mt, 2), _split_bf16(jnp.exp(a_c + (f_last - m_new)), 2)]
        a_r.append(a_c.T[0:SUBLANES, :])
        m_scr[bi:bi + 1, :] = m_new
        m_ref[pl.ds(bi0 + bi, 1), :] = m_new[:, 0:MH]
    rep_all = _dot(jnp.concatenate(stacked, axis=0), spread)
    rep = [rep_all[bi * 4 * chunk:(bi + 1) * 4 * chunk] for bi in range(bb)]
    intra, rsum = {}, {}
    for u in units:
        bi, h = u
        w = jnp.exp(jnp.where(causal, rep[bi][0:chunk, hsl(h)] + a_r[bi][h:h + 1, :], NEG))
        sqk = s_qk[u] * w
        rsum[u] = jnp.sum(sqk, axis=-1, keepdims=True)
        intra[u] = _dot(sqk.astype(BF16), v_ref[bi, :, hsl(h)].astype(BF16))
    hm, ssq = {}, {}
    for u in units:
        bi, h = u
        s_inter = rep[bi][chunk:2 * chunk, hsl(h)]
        e_floor = jnp.exp(rep[bi][2 * chunk:3 * chunk, hsl(h)])
        num = intra[u] + s_inter * inter[u][:, :DH]
        den = rsum[u] + s_inter * inter[u][:, DH:]
        hm[u] = jax.nn.sigmoid(o_ref[bi, :, hsl(h)]) * (num / jnp.maximum(jnp.abs(den), e_floor))
        ssq[u] = jnp.sum(hm[u] * hm[u], axis=-1, keepdims=True)
    for u in units:
        bi, h = u
        oa_ref[bi, :, hsl(h)] = hm[u] * lax.rsqrt(ssq[u] * (1.0 / DH) + EPS) * mg_ref[:, hsl(h)]
        wk = rep[bi][3 * chunk:, hsl(h)]
        dec = decay[bi][:, h:h + 1]
        c_ref[bi, h] = dec * c_ref[bi, h] + _dot_tn((v_ref[bi, :, hsl(h)] * wk).astype(BF16), kb[u])
        n_ref[bi, h:h + 1, :] = (dec * n_ref[bi, h:h + 1, :]
                                 + jnp.sum(k_ref[bi, :, hsl(h)] * scale * wk, axis=0, keepdims=True))
    for bi in range(bb):
        z = cg_ref[bi] * hc_ref[bi]
        cb = cb_ref[bi]
        z1 = jnp.where(rowi == 0, cb[1:2, :], pltpu.roll(z, 1, 0))
        z2 = jnp.where(rowi == 0, cb[0:1, :], jnp.where(rowi == 1, cb[1:2, :], pltpu.roll(z, 2, 0)))
        yc = cw_ref[0:1, :] * z2 + cw_ref[1:2, :] * z1 + cw_ref[2:3, :] * z
        oc_ref[bi] = _rms(bg_ref[bi] * yc) * mg_ref[:, D_MLSTM + D_S5:]
        cb_ref[bi] = z[chunk - 2:chunk, :]


def _mlstm_conv(qkvo, gate, bch, w, states_in, states_prev, *, layer, depth, bb, chunk):
    b, s, _ = qkvo.shape
    assert s % chunk == 0 and b % bb == 0
    zero_state = states_in is None
    kern = functools.partial(_mlstm_conv_kernel, layer=layer, bb=bb, chunk=chunk, zero_state=zero_state)
    col = lambda wd, j: pl.BlockSpec((bb, chunk, wd), lambda i, c: (i, c, j))
    st = lambda shape: pl.BlockSpec((None, bb) + shape, lambda i, c: (layer, i) + (0,) * len(shape))
    m_spec = pl.BlockSpec((None, b, MH), lambda i, c: (layer, 0, 0))
    cst = lambda a: pl.BlockSpec(a.shape, lambda i, c: (0,) * a.ndim)
    state_specs = [st((MH, DH, DH)), st((MH, DH)), m_spec, st((CONV_W - 1, D_CONV))]
    out_state_specs = state_specs
    if layer == 0:
        st0 = lambda shape: pl.BlockSpec((depth, bb) + shape, lambda i, c: (0, i) + (0,) * len(shape))
        out_state_specs = [st0((MH, DH, DH)), st0((MH, DH)), pl.BlockSpec((depth, b, MH), lambda i, c: (0, 0, 0)),
                           st0((CONV_W - 1, D_CONV))]
    consts = (w["gb"], w["cw"], w["mg"])
    args = [qkvo, qkvo, qkvo, qkvo, gate, bch, bch, bch, *consts]
    in_specs = [col(D_MLSTM, 0), col(D_MLSTM, 1), col(D_MLSTM, 2), col(D_MLSTM, 3), col(GATE_W, 0),
                col(D_CONV, 0), col(D_CONV, 1), col(D_CONV, 2)] + [cst(a) for a in consts]
    if not zero_state:
        args += list(states_in)
        in_specs += state_specs
    aliases = {}
    if layer > 0:
        aliases = {len(args) + j: 2 + j for j in range(4)}
        args += list(states_prev)
        in_specs += [_ANY] * 4
    return pl.pallas_call(
        kern,
        grid=(b // bb, s // chunk),
        in_specs=in_specs,
        out_specs=[col(D_MLSTM, 0), col(D_CONV, 0)] + out_state_specs,
        out_shape=[jax.ShapeDtypeStruct((b, s, D_MLSTM), F32), jax.ShapeDtypeStruct((b, s, D_CONV), F32)]
        + _state_shapes(depth, b),
        scratch_shapes=[pltpu.VMEM((-(-bb // SUBLANES) * SUBLANES, GATE_W), F32)],
        input_output_aliases=aliases,
        compiler_params=pltpu.CompilerParams(dimension_semantics=("arbitrary", "arbitrary"),
                                             vmem_limit_bytes=VMEM_LIMIT),
        name="mlstm_conv",
    )(*args)


def _gate_terms(graw, gb_ref, gbt_ref, tril, triu):
    sel = (lax.broadcasted_iota(jnp.int32, (SUBLANES, GATE_W), 0)
           == lax.broadcasted_iota(jnp.int32, (SUBLANES, GATE_W), 1)).astype(F32)
    g_c = graw + gb_ref[...]
    lf_c = _log_sigmoid(g_c)
    g_r = _dot_nt_exact(sel, graw) + gbt_ref[...]
    lf_r = _log_sigmoid(g_r)
    return g_c, lf_c, _dot_exact(tril, lf_c), g_r, _dot_exact(lf_r, triu)


def _head_scores(h, g_c, bt_c, g_r, bt_r, m_prev, mask):
    it_row = g_r[h:h + 1, :]
    bt_row = bt_r[MH + h:MH + h + 1, :]
    it_col = g_c[:, h:h + 1]
    bt_col = bt_c[:, MH + h:MH + h + 1]
    dmat = jnp.where(mask, bt_col - bt_row + it_row, NEG)
    inter = bt_col + m_prev
    mt = jnp.maximum(jnp.max(dmat, axis=-1, keepdims=True), inter)
    return jnp.exp(dmat - mt), jnp.exp(inter - mt), mt, it_col, bt_col


def _head_output(sqk, vb, s_inter, qc, qn, mt, o, mg):
    num = _dot(sqk.astype(BF16), vb) + s_inter * qc
    den = jnp.sum(sqk, axis=-1, keepdims=True) + s_inter * qn
    hm = jax.nn.sigmoid(o) * (num / jnp.maximum(jnp.abs(den), jnp.exp(-mt)))
    return _rms(hm) * mg


def _bcast_block_last(x, t, s):
    out = x
    for d in range(1, s):
        out = jnp.where(t == s - 1 - d, pltpu.roll(x, x.shape[0] - d, 0), out)
    return out


def _mlstm_conv_short_kernel(*refs, layer, nb, s):
    (q_ref, k_ref, v_ref, o_ref, gate_ref, bg_ref, cg_ref, hc_ref, gb_ref, gbt_ref, cw_ref, mg_ref,
     c0_ref, n0_ref, m0_ref, cb0_ref) = refs[:16]
    refs = refs[16:]
    if layer > 0:
        refs = refs[4:]
    oa_ref, oc_ref, c_ref, n_ref, m_ref, cb_ref, qc_scr, st_scr = refs
    if layer == 0:
        for ref in (c_ref, n_ref, m_ref, cb_ref):
            if ref.shape[0] > 1:
                ref[1:] = jnp.zeros((ref.shape[0] - 1,) + ref.shape[1:], F32)
        c_ref, n_ref, m_ref, cb_ref = c_ref.at[0], n_ref.at[0], m_ref.at[0], cb_ref.at[0]
    r = nb * s
    per_tile = SUBLANES // s
    shift = s.bit_length() - 1
    row = lax.broadcasted_iota(jnp.int32, (r, r), 0)
    col = lax.broadcasted_iota(jnp.int32, (r, r), 1)
    same = (row >> shift) == (col >> shift)
    mask = same & (col <= row)
    tril = mask.astype(F32)
    triu = (same & (row <= col)).astype(F32)
    expand = ((lax.broadcasted_iota(jnp.int32, (r, nb), 0) >> shift)
              == lax.broadcasted_iota(jnp.int32, (r, nb), 1)).astype(F32)
    gather = (lax.broadcasted_iota(jnp.int32, (nb, r), 0)
              == (lax.broadcasted_iota(jnp.int32, (nb, r), 1) >> shift)).astype(F32)
    t128 = lax.broadcasted_iota(jnp.int32, (r, DH), 0) & (s - 1)
    sub8 = lax.broadcasted_iota(jnp.int32, (SUBLANES, DH), 0) >> shift
    scale = DH ** -0.5

    g_c, lf_c, bt_c, g_r, bt_r = _gate_terms(gate_ref[...], gb_ref, gbt_ref, tril, triu)
    bt_last_c = _dot_exact(same.astype(F32), lf_c)
    m_rows = _dot_exact(expand, m0_ref[...])
    for h in range(MH):
        hs = slice(h * DH, (h + 1) * DH)
        m_prev = m_rows[:, h:h + 1]
        w, s_inter, mt, it_col, bt_col = _head_scores(h, g_c, bt_c, g_r, bt_r, m_prev, mask)
        qh = q_ref[:, hs]
        kh = k_ref[:, hs] * scale
        vh = v_ref[:, hs]
        qb = qh.astype(BF16)
        kb = kh.astype(BF16)
        sqk = _dot_nt(qb, kb) * w
        for j in range(r // SUBLANES):
            rows = slice(j * SUBLANES, (j + 1) * SUBLANES)
            acc = None
            q8 = q_ref[rows, hs].astype(BF16)
            for p in range(per_tile):
                part = _dot_nt(q8, c0_ref[j * per_tile + p, h].astype(BF16))
                acc = part if acc is None else jnp.where(sub8 == p, part, acc)
            qc_scr[rows, :] = acc
        n_rows = _dot_exact(expand, n0_ref[:, h, :])
        qn = jnp.sum(qh * n_rows, axis=-1, keepdims=True)
        oa_ref[:, hs] = _head_output(sqk, vh.astype(BF16), s_inter, qc_scr[...], qn, mt,
                                     o_ref[:, hs], mg_ref[:, hs])
        m_new = _bcast_block_last(jnp.broadcast_to(mt, (r, DH)), t128, s)
        bt_last = bt_last_c[:, MH + h:MH + h + 1]
        wk = jnp.exp(bt_last - bt_col + it_col - m_new)
        decay = jnp.exp(bt_last + m_prev - m_new)
        vw = vh * wk
        for j in range(r // SUBLANES):
            rows = slice(j * SUBLANES, (j + 1) * SUBLANES)
            k8 = (k_ref[rows, hs] * scale).astype(BF16)
            for p in range(per_tile):
                bidx = j * per_tile + p
                last = bidx * s + s - 1
                upd = _dot_tn(jnp.where(sub8 == p, vw[rows], 0.0).astype(BF16), k8)
                c_ref[bidx, h] = decay[last:last + 1, 0:1] * c0_ref[bidx, h] + upd
        st_scr[0] = decay
        st_scr[1] = m_new
        last_rows = pl.ds(s - 1, nb, stride=s)
        n_ref[:, h, :] = st_scr[0, last_rows, :] * n0_ref[:, h, :] + _dot_exact(gather, kh * wk)
        m_ref[:, h:h + 1] = st_scr[1, last_rows, :][:, 0:1]
    t256 = lax.broadcasted_iota(jnp.int32, (r, D_CONV), 0) & (s - 1)
    z = cg_ref[...] * hc_ref[...]
    cb_a = _dot_exact(expand, cb0_ref[:, 0, :])
    cb_b = _dot_exact(expand, cb0_ref[:, 1, :])
    z1 = jnp.where(t256 == 0, cb_b, pltpu.roll(z, 1, 0))
    z2 = jnp.where(t256 == 0, cb_a, jnp.where(t256 == 1, cb_b, pltpu.roll(z, 2, 0)))
    yc = cw_ref[0:1, :] * z2 + cw_ref[1:2, :] * z1 + cw_ref[2:3, :] * z
    oc_ref[...] = _rms(bg_ref[...] * yc) * mg_ref[:, D_MLSTM + D_S5:]
    for half in range(D_CONV // DH):
        lanes = slice(half * DH, (half + 1) * DH)
        st_scr[half] = z[:, lanes]
        cb_ref[:, 0, lanes] = st_scr[half, pl.ds(s - 2, nb, stride=s), :]
        cb_ref[:, 1, lanes] = st_scr[half, pl.ds(s - 1, nb, stride=s), :]


def _short_batch_block(b, depth, layer):
    out_layers = depth if layer == 0 else 1
    nb = min(b, 32)
    while nb > SUBLANES and 2 * (1 + out_layers) * nb * MH * DH * DH * 4 > STATE_VMEM_BUDGET:
        nb //= 2
    return nb


def _mlstm_conv_short(qkvo, gate, bch, w, states_in, states_prev, *, layer, depth, b, s):
    nb = _short_batch_block(b, depth, layer)
    assert SUBLANES % s == 0 and s >= CONV_W - 1 and b % nb == 0 and (nb * s) % SUBLANES == 0
    r = nb * s
    kern = functools.partial(_mlstm_conv_short_kernel, layer=layer, nb=nb, s=s)
    col = lambda wd, j: pl.BlockSpec((r, wd), lambda i: (i, j))
    st = lambda shape: pl.BlockSpec((None, nb) + shape, lambda i: (layer, i) + (0,) * len(shape))
    cst = lambda a: pl.BlockSpec(a.shape, lambda i: (0,) * a.ndim)
    state_specs = [st((MH, DH, DH)), st((MH, DH)), st((MH,)), st((CONV_W - 1, D_CONV))]
    out_state_specs = state_specs
    if layer == 0:
        st0 = lambda shape: pl.BlockSpec((depth, nb) + shape, lambda i: (0, i) + (0,) * len(shape))
        out_state_specs = [st0((MH, DH, DH)), st0((MH, DH)), st0((MH,)), st0((CONV_W - 1, D_CONV))]
    consts = (w["gb"], w["gbt"], w["cw"], w["mg"])
    args = [qkvo, qkvo, qkvo, qkvo, gate, bch, bch, bch, *consts, *states_in]
    in_specs = [col(D_MLSTM, 0), col(D_MLSTM, 1), col(D_MLSTM, 2), col(D_MLSTM, 3), col(GATE_W, 0),
                col(D_CONV, 0), col(D_CONV, 1), col(D_CONV, 2)] + [cst(a) for a in consts] + state_specs
    aliases = {}
    if layer > 0:
        aliases = {len(args) + j: 2 + j for j in range(4)}
        args += list(states_prev)
        in_specs += [_ANY] * 4
    return pl.pallas_call(
        kern,
        grid=(b // nb,),
        in_specs=in_specs,
        out_specs=[col(D_MLSTM, 0), col(D_CONV, 0)] + out_state_specs,
        out_shape=[jax.ShapeDtypeStruct((b * s, D_MLSTM), F32), jax.ShapeDtypeStruct((b * s, D_CONV), F32)]
        + _state_shapes(depth, b),
        scratch_shapes=[pltpu.VMEM((r, DH), F32), pltpu.VMEM((2, r, DH), F32)],
        input_output_aliases=aliases,
        compiler_params=pltpu.CompilerParams(dimension_semantics=("arbitrary",), vmem_limit_bytes=VMEM_LIMIT),
        name="mlstm_conv_short",
    )(*args)


def _s5_kernel(*refs, layer, nb, tt, zero_state, parts, batch_major):
    u_ref, lam_ref, bblk_ref, cblk_ref, d_ref, wglu_ref, mg_ref = refs[:7]
    refs = refs[7:]
    if not zero_state:
        hr0_ref, hi0_ref = refs[:2]
        refs = refs[2:]
    if layer > 0:
        refs = refs[2:]
    ob_ref, hr_ref, hi_ref, xs_ref, a_ref, bf_ref = refs[:6]
    if batch_major:
        ut_scr, ot_scr = refs[6:]
        ut_scr[...] = jnp.swapaxes(u_ref[...], 0, 1).reshape(tt * nb, D_S5)
        u_bm_ref, ob_bm_ref, u_ref, ob_ref = u_ref, ob_ref, ut_scr, ot_scr
    if layer == 0:
        hr_all, hi_all = hr_ref, hi_ref
        hr_ref, hi_ref = hr_all.at[0], hi_all.at[0]
    ti = pl.program_id(0)
    n = S5_N
    lc = 512

    @pl.when(ti == 0)
    def _():
        if layer == 0:
            hr_all[...] = jnp.zeros_like(hr_all)
            hi_all[...] = jnp.zeros_like(hi_all)
        hr_ref[...] = jnp.zeros_like(hr_ref) if zero_state else hr0_ref[...]
        hi_ref[...] = jnp.zeros_like(hi_ref) if zero_state else hi0_ref[...]
        lre = jnp.minimum(lam_ref[0:1, :], -1e-4)
        lim = lam_ref[1:2, :]
        dt = jnp.exp(lam_ref[2:3, :])
        mag = jnp.exp(lre * dt)
        ab_re = mag * jnp.cos(lim * dt)
        ab_im = mag * jnp.sin(lim * dt)
        den = lre * lre + lim * lim
        nr = ab_re - 1.0
        fre = (nr * lre + ab_im * lim) / den
        fim = (ab_im * lre - nr * lim) / den
        a_ref[:, :n] = jnp.broadcast_to(ab_re, (SUBLANES, n))
        a_ref[:, n:] = jnp.broadcast_to(ab_im, (SUBLANES, n))
        b_re = bblk_ref[:, :n]
        b_im = bblk_ref[:, n:]
        bf_ref[:, :n] = (fre * b_re - fim * b_im).astype(BF16)
        bf_ref[:, n:] = (fre * b_im + fim * b_re).astype(BF16)

    rows_p = nb * tt // parts
    part_rows = [slice(p * rows_p, (p + 1) * rows_p) for p in range(parts)]
    xs_ref[...] = _dot(u_ref[...].astype(BF16), bf_ref[...])
    ar = [a_ref[:, c0:c0 + lc] for c0 in range(0, n, lc)]
    ai = [a_ref[:, n + c0:n + c0 + lc] for c0 in range(0, n, lc)]
    h, y = {}, []
    for p, rs in enumerate(part_rows):
        for rg in range(nb // SUBLANES):
            g8 = slice(rg * SUBLANES, (rg + 1) * SUBLANES)
            for ci, c0 in enumerate(range(0, n, lc)):
                if p == 0:
                    h[rg, ci] = (hr_ref[g8, c0:c0 + lc], hi_ref[g8, c0:c0 + lc])
            for t in range(p * tt // parts, (p + 1) * tt // parts):
                r8 = slice(t * nb + rg * SUBLANES, t * nb + (rg + 1) * SUBLANES)
                for ci, c0 in enumerate(range(0, n, lc)):
                    hr, hi = h[rg, ci]
                    nr_ = ar[ci] * hr - ai[ci] * hi + xs_ref[r8, c0:c0 + lc]
                    ni_ = ar[ci] * hi + ai[ci] * hr + xs_ref[r8, n + c0:n + c0 + lc]
                    xs_ref[r8, c0:c0 + lc] = nr_
                    xs_ref[r8, n + c0:n + c0 + lc] = ni_
                    h[rg, ci] = (nr_, ni_)
            for ci, c0 in enumerate(range(0, n, lc)):
                if p == parts - 1:
                    hr_ref[g8, c0:c0 + lc] = h[rg, ci][0]
                    hi_ref[g8, c0:c0 + lc] = h[rg, ci][1]
        y.append(_dot(xs_ref[rs, :].astype(BF16), cblk_ref[...]) + d_ref[...] * u_ref[rs, :])
    g = [jax.nn.gelu(v) for v in y]
    gate = [_dot(v.astype(BF16), wglu_ref[...]) for v in g]
    for rs, v, z in zip(part_rows, g, gate):
        ob_ref[rs, :] = _rms(v * jax.nn.sigmoid(z)) * mg_ref[:, D_MLSTM:D_MLSTM + D_S5]
    if batch_major:
        ob_bm_ref[...] = jnp.swapaxes(ot_scr[...].reshape(tt, nb, D_S5), 0, 1)


def _s5(u, w, states_in, states_prev, *, layer, depth, nb, tt):
    batch_major = u.ndim == 3
    rows = u.shape[0] * u.shape[1] if batch_major else u.shape[0]
    rt = nb * tt
    zero_state = states_in is None
    parts = S5_PARTS if tt % (SUBLANES * S5_PARTS) == 0 else 1
    kern = functools.partial(_s5_kernel, layer=layer, nb=nb, tt=tt, zero_state=zero_state, parts=parts,
                             batch_major=batch_major)
    if batch_major:
        assert u.shape[0] == nb and tt % SUBLANES == 0
        io_spec = pl.BlockSpec((nb, tt, D_S5), lambda i: (0, i, 0))
        io_shape = jax.ShapeDtypeStruct(u.shape, F32)
        io_scratch = [pltpu.VMEM((rt, D_S5), F32), pltpu.VMEM((rt, D_S5), F32)]
    else:
        io_spec = pl.BlockSpec((rt, D_S5), lambda i: (i, 0))
        io_shape = jax.ShapeDtypeStruct((rows, D_S5), F32)
        io_scratch = []
    cst = lambda a: pl.BlockSpec(a.shape, lambda i: (0,) * a.ndim)
    st = pl.BlockSpec((None, nb, S5_N), lambda i: (layer, 0, 0))
    st_out = pl.BlockSpec((depth, nb, S5_N), lambda i: (0, 0, 0)) if layer == 0 else st
    consts = (w["lam"], w["bblk"], w["cblk"], w["d"], w["wglu"], w["mg"])
    args = [u, *consts]
    in_specs = [io_spec] + [cst(a) for a in consts]
    if not zero_state:
        args += list(states_in)
        in_specs += [st, st]
    aliases = {}
    if layer > 0:
        aliases = {len(args) + j: 1 + j for j in range(2)}
        args += list(states_prev)
        in_specs += [_ANY] * 2
    return pl.pallas_call(
        kern,
        grid=(rows // rt,),
        in_specs=in_specs,
        out_specs=[io_spec, st_out, st_out],
        out_shape=[io_shape, jax.ShapeDtypeStruct((depth, nb, S5_N), F32),
                   jax.ShapeDtypeStruct((depth, nb, S5_N), F32)],
        scratch_shapes=[pltpu.VMEM((rt, 2 * S5_N), F32), pltpu.VMEM((SUBLANES, 2 * S5_N), F32),
                        pltpu.VMEM((D_S5, 2 * S5_N), BF16)] + io_scratch,
        input_output_aliases=aliases,
        compiler_params=pltpu.CompilerParams(dimension_semantics=("arbitrary",), vmem_limit_bytes=VMEM_LIMIT),
        name="s5",
    )(*args)


def _block_diag(w):
    g, r, c = w.shape
    eye = jnp.eye(g, dtype=w.dtype)
    return jnp.einsum("grc,gh->grhc", w, eye).reshape(g * r, g * c)


def _layer_weights(l, ig_bias, fg_bias, s5_a_re, s5_a_im, s5_log_dt, s5_b_re, s5_b_im, s5_c_re, s5_c_im, s5_d,
                   w_glu, conv_w, mix_g):
    gb = jnp.concatenate([ig_bias[l], fg_bias[l], jnp.zeros((GATE_W - 2 * MH,), F32)])
    lam = jnp.stack([s5_a_re[l].reshape(-1), s5_a_im[l].reshape(-1),
                     jnp.repeat(s5_log_dt[l], S5_P)])
    bblk = jnp.concatenate([_block_diag(jnp.swapaxes(s5_b_re[l], 1, 2)),
                            _block_diag(jnp.swapaxes(s5_b_im[l], 1, 2))], axis=1)
    cblk = jnp.concatenate([_block_diag(jnp.swapaxes(s5_c_re[l], 1, 2)),
                            -_block_diag(jnp.swapaxes(s5_c_im[l], 1, 2))], axis=0).astype(BF16)
    return dict(
        gb=gb.reshape(1, GATE_W), gbt=gb[:SUBLANES].reshape(SUBLANES, 1),
        lam=lam, bblk=bblk, cblk=cblk, d=s5_d[l].reshape(1, D_S5), wglu=w_glu[l].astype(BF16),
        cw=conv_w[l], mg=mix_g[l].reshape(1, D_MODEL))


def _stacked_weights(ffn1_w1, ffn1_w3, ffn1_w2, ffn2_w1, ffn2_w3, ffn2_w2, norm_g, w_in, w_out):
    o = 4 * D_MLSTM
    win = (w_in[:, :, :o].astype(BF16),
           jnp.pad(w_in[:, :, o:o + 2 * MH].astype(BF16), ((0, 0), (0, 0), (0, GATE_W - 2 * MH))),
           w_in[:, :, o + 2 * MH:].astype(BF16))
    return dict(f1=(ffn1_w1.astype(BF16), ffn1_w3.astype(BF16), ffn1_w2.astype(BF16)),
                f2=(ffn2_w1.astype(BF16), ffn2_w3.astype(BF16), ffn2_w2.astype(BF16)),
                g=norm_g, win=win, wo=w_out.astype(BF16))


def _run_group(x, states, weights, big):
    b, s, _ = x.shape
    depth = len(weights)
    tm = min(FFN_ROWS, b * s)
    assert (b * s) % tm == 0
    long_seq = s % MLSTM_CHUNK == 0 and b % MLSTM_ROWS == 0
    chunk, bb = (MLSTM_CHUNK, MLSTM_ROWS) if long_seq else (None, None)
    tt = S5_STEPS if s % S5_STEPS == 0 else s
    xf = x.reshape(b * s, D_MODEL)
    mstates = sstates = None
    if states is not None:
        c_all, n_all, m_all, sr_all, si_all, cb_all = states
        ms_in = (c_all, n_all, m_all, cb_all)
        ss_in = (sr_all.reshape(depth, b, S5_N), si_all.reshape(depth, b, S5_N))
    else:
        ms_in = ss_in = None
    for l, w in enumerate(weights):
        x1, qkvo, gate, u, bch = _ffn_inproj(xf, big["g"], *big["f1"], big["win"], tm, l)
        if tt % SUBLANES == 0:
            ob, *sstates = _s5(u.reshape(b, s, D_S5), w, ss_in, sstates, layer=l, depth=depth, nb=b, tt=tt)
            ob = ob.reshape(b * s, D_S5)
        else:
            u_tm = jnp.swapaxes(u.reshape(b, s, D_S5), 0, 1).reshape(s * b, D_S5)
            ob_tm, *sstates = _s5(u_tm, w, ss_in, sstates, layer=l, depth=depth, nb=b, tt=tt)
            ob = jnp.swapaxes(ob_tm.reshape(s, b, D_S5), 0, 1).reshape(b * s, D_S5)
        if chunk is not None:
            r3 = lambda a: a.reshape(b, s, a.shape[-1])
            oa, oc, *mstates = _mlstm_conv(r3(qkvo), r3(gate), r3(bch), w, ms_in, mstates,
                                           layer=l, depth=depth, bb=bb, chunk=chunk)
            xf = _outproj_ffn(x1, oa.reshape(b * s, D_MLSTM), ob, oc.reshape(b * s, D_CONV),
                              big["g"], big["wo"], *big["f2"], tm, l)
        else:
            oa, oc, *mstates = _mlstm_conv_short(qkvo, gate, bch, w, ms_in, mstates,
                                                 layer=l, depth=depth, b=b, s=s)
            xf = _outproj_ffn(x1, oa, ob, oc, big["g"], big["wo"], *big["f2"], tm, l)
    c1, n1, m1, cb1 = mstates
    sr1, si1 = sstates
    return xf.reshape(b, s, D_MODEL), (c1, n1, m1, sr1.reshape(depth, b, S5_G, S5_P),
                                       si1.reshape(depth, b, S5_G, S5_P), cb1)


def kernel(x_prompt, x_sample, state_mlstm_C, state_mlstm_n, state_mlstm_m, state_s5_re, state_s5_im,
           state_conv, ffn1_w1, ffn1_w3, ffn1_w2, ffn2_w1, ffn2_w3, ffn2_w2, norm_g, w_in, ig_bias, fg_bias,
           s5_a_re, s5_a_im, s5_log_dt, s5_b_re, s5_b_im, s5_c_re, s5_c_im, s5_d, w_glu, conv_w, mix_g, w_out):
    depth = norm_g.shape[0]
    params = (ig_bias, fg_bias, s5_a_re, s5_a_im, s5_log_dt, s5_b_re, s5_b_im, s5_c_re, s5_c_im, s5_d, w_glu,
              conv_w, mix_g)
    weights = [_layer_weights(l, *params) for l in range(depth)]
    big = _stacked_weights(ffn1_w1, ffn1_w3, ffn1_w2, ffn2_w1, ffn2_w3, ffn2_w2, norm_g, w_in, w_out)
    y_p, st_p = _run_group(x_prompt, None, weights, big)
    y_s, st_s = _run_group(
        x_sample, (state_mlstm_C, state_mlstm_n, state_mlstm_m, state_s5_re, state_s5_im, state_conv),
        weights, big)
    return (y_p, y_s, *st_p, *st_s)
```

```python
import functools

import jax
import jax.numpy as jnp
from jax import lax
from jax.experimental import pallas as pl
from jax.experimental.pallas import tpu as pltpu

F32 = jnp.float32
BF16 = jnp.bfloat16

D_MODEL = 1024
MH = 4
DH = 128
D_MLSTM = MH * DH
S5_CH = 16
S5_G = 16
S5_P = 64
D_S5 = S5_G * S5_CH
S5_N = S5_G * S5_P
D_CONV = 256
CONV_W = 3
EPS = 1e-6
GATE_W = 128
SUBLANES = 8
NEG = -1e30
VMEM_LIMIT = 56 * 1024 * 1024
STATE_VMEM_BUDGET = 36 * 1024 * 1024
FFN_ROWS = 512
ROW_PARTS = 2
S5_STEPS = 256
S5_PARTS = 2
MLSTM_CHUNK = DH
MLSTM_ROWS = 8


def _dot(a, b):
    return jnp.dot(a, b, preferred_element_type=F32)


def _dot_nt(a, b):
    return lax.dot_general(a, b, (((1,), (1,)), ((), ())), preferred_element_type=F32)


def _dot_tn(a, b):
    return lax.dot_general(a, b, (((0,), (0,)), ((), ())), preferred_element_type=F32)


def _dot_exact(a, b):
    return jnp.dot(a, b, preferred_element_type=F32, precision=lax.Precision.HIGHEST)


def _dot_nt_exact(a, b):
    return lax.dot_general(a, b, (((1,), (1,)), ((), ())), preferred_element_type=F32,
                           precision=lax.Precision.HIGHEST)


def _rms(x):
    return x * lax.rsqrt(jnp.mean(x * x, axis=-1, keepdims=True) + EPS)


def _log_sigmoid(x):
    return jnp.minimum(x, 0.0) - jnp.log(1.0 + jnp.exp(-jnp.abs(x)))


def _split_bf16(x, parts):
    out = []
    for _ in range(parts):
        p = x.astype(BF16)
        out.append(p)
        x = x - p.astype(F32)
    return jnp.concatenate(out, axis=1)


def _ffn_residual(xs, g_pre, g_post, w1, w3, w2):
    xn = [(_rms(x) * g_pre).astype(BF16) for x in xs]
    h1 = [_dot(v, w1[...]) for v in xn]
    h3 = [_dot(v, w3[...]) for v in xn]
    a = [(jax.nn.silu(p) * q).astype(BF16) for p, q in zip(h1, h3)]
    y = [_dot(v, w2[...]) for v in a]
    return [x + 0.5 * (_rms(t) * g_post) for x, t in zip(xs, y)]


def _row_parts(tm, parts):
    step = tm // parts
    return [slice(i * step, (i + 1) * step) for i in range(parts)]


def _layer_spec(a, layer):
    return pl.BlockSpec((None,) + a.shape[1:], lambda *_: (layer, 0, 0), pipeline_mode=pl.Buffered(1))


_ANY = pl.BlockSpec(memory_space=pl.ANY)


def _ffn_inproj_kernel(x_ref, g_ref, w1_ref, w3_ref, w2_ref, wq_ref, wg_ref, wr_ref,
                       x1_ref, qkvo_ref, gate_ref, u_ref, bch_ref):
    parts = _row_parts(x_ref.shape[0], ROW_PARTS)
    x1 = _ffn_residual([x_ref[r, :] for r in parts], g_ref[0:1, :], g_ref[1:2, :], w1_ref, w3_ref, w2_ref)
    hn = [(_rms(v) * g_ref[2:3, :]).astype(BF16) for v in x1]
    qkvo = [_dot(v, wq_ref[...]) for v in hn]
    gate = [_dot(v, wg_ref[...]) for v in hn]
    rest = [_dot(v, wr_ref[...]) for v in hn]
    for i, r in enumerate(parts):
        x1_ref[r, :] = x1[i]
        qkvo_ref[r, :] = qkvo[i]
        gate_ref[r, :] = gate[i]
        u_ref[r, :] = rest[i][:, :D_S5]
        bch_ref[r, :] = rest[i][:, D_S5:]


def _ffn_inproj(x, g, w1, w3, w2, win, tm, layer):
    t = x.shape[0]
    row = lambda w: pl.BlockSpec((tm, w), lambda i: (i, 0))
    return pl.pallas_call(
        _ffn_inproj_kernel,
        grid=(t // tm,),
        in_specs=[row(D_MODEL), _layer_spec(g, layer), _layer_spec(w1, layer), _layer_spec(w3, layer),
                  _layer_spec(w2, layer)] + [_layer_spec(a, layer) for a in win],
        out_specs=[row(D_MODEL), row(4 * D_MLSTM), row(GATE_W), row(D_S5), row(3 * D_CONV)],
        out_shape=[jax.ShapeDtypeStruct((t, D_MODEL), F32), jax.ShapeDtypeStruct((t, 4 * D_MLSTM), F32),
                   jax.ShapeDtypeStruct((t, GATE_W), F32), jax.ShapeDtypeStruct((t, D_S5), F32),
                   jax.ShapeDtypeStruct((t, 3 * D_CONV), F32)],
        compiler_params=pltpu.CompilerParams(dimension_semantics=("arbitrary",), vmem_limit_bytes=VMEM_LIMIT),
        name="ffn_inproj",
    )(x, g, w1, w3, w2, *win)


def _outproj_ffn_kernel(x_ref, a_ref, b_ref, c_ref, g_ref, wo_ref, w1_ref, w3_ref, w2_ref, y_ref):
    parts = _row_parts(x_ref.shape[0], ROW_PARTS)
    mo = [(_dot(a_ref[r, :].astype(BF16), wo_ref[0:D_MLSTM, :])
           + _dot(b_ref[r, :].astype(BF16), wo_ref[D_MLSTM:D_MLSTM + D_S5, :])
           + _dot(c_ref[r, :].astype(BF16), wo_ref[D_MLSTM + D_S5:, :])) for r in parts]
    x2 = [x_ref[r, :] + _rms(v) * g_ref[3:4, :] for r, v in zip(parts, mo)]
    y = _ffn_residual(x2, g_ref[4:5, :], g_ref[5:6, :], w1_ref, w3_ref, w2_ref)
    for r, v in zip(parts, y):
        y_ref[r, :] = v


def _outproj_ffn(x, oa, ob, oc, g, wo, w1, w3, w2, tm, layer):
    t = x.shape[0]
    row = lambda w: pl.BlockSpec((tm, w), lambda i: (i, 0))
    return pl.pallas_call(
        _outproj_ffn_kernel,
        grid=(t // tm,),
        in_specs=[row(D_MODEL), row(D_MLSTM), row(D_S5), row(D_CONV), _layer_spec(g, layer),
                  _layer_spec(wo, layer), _layer_spec(w1, layer), _layer_spec(w3, layer),
                  _layer_spec(w2, layer)],
        out_specs=row(D_MODEL),
        out_shape=jax.ShapeDtypeStruct((t, D_MODEL), F32),
        compiler_params=pltpu.CompilerParams(dimension_semantics=("arbitrary",), vmem_limit_bytes=VMEM_LIMIT),
        name="outproj_ffn",
    )(x, oa, ob, oc, g, wo, w1, w3, w2)


def _state_shapes(depth, b):
    return [jax.ShapeDtypeStruct((depth, b, MH, DH, DH), F32), jax.ShapeDtypeStruct((depth, b, MH, DH), F32),
            jax.ShapeDtypeStruct((depth, b, MH), F32), jax.ShapeDtypeStruct((depth, b, CONV_W - 1, D_CONV), F32)]


def _cummax_lanes(x, lane):
    s = 1
    while s < x.shape[1]:
        x = jnp.maximum(x, jnp.where(lane >= s, pltpu.roll(x, s, 1), NEG))
        s *= 2
    return x


def _mlstm_conv_kernel(*refs, layer, bb, chunk, zero_state):
    (q_ref, k_ref, v_ref, o_ref, gate_ref, bg_ref, cg_ref, hc_ref, gb_ref, cw_ref, mg_ref) = refs[:11]
    refs = refs[11:]
    if not zero_state:
        c0_ref, n0_ref, m0_ref, cb0_ref = refs[:4]
        refs = refs[4:]
    if layer > 0:
        refs = refs[4:]
    oa_ref, oc_ref, c_ref, n_ref, m_ref, cb_ref, m_scr = refs
    other_layers = ()
    if layer == 0:
        other_layers = tuple(r.at[1:] for r in (c_ref, n_ref, m_ref, cb_ref) if r.shape[0] > 1)
        c_ref, n_ref, m_ref, cb_ref = c_ref.at[0], n_ref.at[0], m_ref.at[0], cb_ref.at[0]
    assert chunk == DH
    bi0 = pl.program_id(0) * bb
    ci = pl.program_id(1)

    @pl.when(ci == 0)
    def _():
        for r in other_layers:
            r[...] = jnp.zeros_like(r)
        m_scr[...] = jnp.zeros_like(m_scr)
        if zero_state:
            c_ref[...] = jnp.zeros_like(c_ref)
            n_ref[...] = jnp.zeros_like(n_ref)
            cb_ref[...] = jnp.zeros_like(cb_ref)
        else:
            c_ref[...] = c0_ref[...]
            n_ref[...] = n0_ref[...]
            cb_ref[...] = cb0_ref[...]
            for bi in range(bb):
                m_scr[bi:bi + 1, 0:MH] = m0_ref[pl.ds(bi0 + bi, 1), :]

    row = lax.broadcasted_iota(jnp.int32, (chunk, chunk), 0)
    col = lax.broadcasted_iota(jnp.int32, (chunk, chunk), 1)
    causal = col <= row
    tril = jnp.where(causal, 1.0, 0.0).astype(BF16)
    laneg = lax.broadcasted_iota(jnp.int32, (chunk, GATE_W), 1)
    lane8 = lax.broadcasted_iota(jnp.int32, (SUBLANES, chunk), 1)
    rowi = lax.broadcasted_iota(jnp.int32, (chunk, D_CONV), 0)
    spread = jnp.where((lax.broadcasted_iota(jnp.int32, (2 * GATE_W, D_MLSTM), 0) & (GATE_W - 1))
                       == (lax.broadcasted_iota(jnp.int32, (2 * GATE_W, D_MLSTM), 1) >> 7), 1.0, 0.0).astype(BF16)
    scale = DH ** -0.5

    units = [(bi, h) for bi in range(bb) for h in range(MH)]
    hsl = lambda h: slice(h * DH, (h + 1) * DH)
    g_c = [gate_ref[bi] + gb_ref[...] for bi in range(bb)]
    csum_all = _dot(tril, jnp.concatenate(
        [_split_bf16(pltpu.roll(_log_sigmoid(g), GATE_W - MH, 1), 3) for g in g_c], axis=1))
    csum = [csum_all[:, bi * 3 * GATE_W:(bi + 1) * 3 * GATE_W] for bi in range(bb)]
    qb = {u: q_ref[u[0], :, hsl(u[1])].astype(BF16) for u in units}
    kb = {u: (k_ref[u[0], :, hsl(u[1])] * scale).astype(BF16) for u in units}
    s_qk = {u: _dot_nt(qb[u], kb[u]) for u in units}
    inter = {u: _dot_nt(qb[u], jnp.concatenate(
        [c_ref[u[0], u[1]].astype(BF16),
         jnp.broadcast_to(n_ref[u[0], u[1]:u[1] + 1, :], (DH, DH)).astype(BF16)], axis=0)) for u in units}
    stacked, a_r, decay = [], [], []
    for bi in range(bb):
        f_c = csum[bi][:, :GATE_W] + csum[bi][:, GATE_W:2 * GATE_W] + csum[bi][:, 2 * GATE_W:]
        a_c = jnp.where(laneg < SUBLANES, g_c[bi] - f_c, 0.0)
        m_prev = m_scr[bi:bi + 1, :]
        a_t = a_c.T[0:SUBLANES, :]
        run = jnp.concatenate([_cummax_lanes(a_t, lane8), jnp.zeros((chunk - SUBLANES, chunk), F32)], axis=0).T
        big_m = jnp.maximum(run, m_prev)
        mt = f_c + big_m
        m_new = mt[chunk - 1:chunk, :]
        f_last = f_c[chunk - 1:chunk, :]
        decay.append(jnp.exp(f_last + m_prev - m_new))
        stacked += [_split_bf16(-big_m, 2), _split_bf16(jnp.exp(m_prev - big_m), 2),
                    _split_bf16(-mt, 2), _split_bf16(jnp.exp(a_c + (f_last - m_new)), 2)]
        a_r.append(a_t)
        m_scr[bi:bi + 1, :] = m_new
        m_ref[pl.ds(bi0 + bi, 1), :] = m_new[:, 0:MH]
    rep_all = _dot(jnp.concatenate(stacked, axis=0), spread)
    rep = [rep_all[bi * 4 * chunk:(bi + 1) * 4 * chunk] for bi in range(bb)]
    intra, rsum = {}, {}
    for u in units:
        bi, h = u
        w = jnp.exp(jnp.where(causal, rep[bi][0:chunk, hsl(h)] + a_r[bi][h:h + 1, :], NEG))
        sqk = s_qk[u] * w
        rsum[u] = jnp.sum(sqk, axis=-1, keepdims=True)
        intra[u] = _dot(sqk.astype(BF16), v_ref[bi, :, hsl(h)].astype(BF16))
    hm, ssq = {}, {}
    for u in units:
        bi, h = u
        s_inter = rep[bi][chunk:2 * chunk, hsl(h)]
        e_floor = jnp.exp(rep[bi][2 * chunk:3 * chunk, hsl(h)])
        num = intra[u] + s_inter * inter[u][:, :DH]
        den = rsum[u] + s_inter * inter[u][:, DH:]
        hm[u] = jax.nn.sigmoid(o_ref[bi, :, hsl(h)]) * (num / jnp.maximum(jnp.abs(den), e_floor))
        ssq[u] = jnp.sum(hm[u] * hm[u], axis=-1, keepdims=True)
    for u in units:
        bi, h = u
        oa_ref[bi, :, hsl(h)] = hm[u] * lax.rsqrt(ssq[u] * (1.0 / DH) + EPS) * mg_ref[:, hsl(h)]
        wk = rep[bi][3 * chunk:, hsl(h)]
        dec = decay[bi][:, h:h + 1]
        c_ref[bi, h] = dec * c_ref[bi, h] + _dot_tn((v_ref[bi, :, hsl(h)] * wk).astype(BF16), kb[u])
        n_ref[bi, h:h + 1, :] = (dec * n_ref[bi, h:h + 1, :]
                                 + jnp.sum(k_ref[bi, :, hsl(h)] * scale * wk, axis=0, keepdims=True))
    for bi in range(bb):
        z = cg_ref[bi] * hc_ref[bi]
        cb = cb_ref[bi]
        z1 = jnp.where(rowi == 0, cb[1:2, :], pltpu.roll(z, 1, 0))
        z2 = jnp.where(rowi == 0, cb[0:1, :], jnp.where(rowi == 1, cb[1:2, :], pltpu.roll(z, 2, 0)))
        yc = cw_ref[0:1, :] * z2 + cw_ref[1:2, :] * z1 + cw_ref[2:3, :] * z
        oc_ref[bi] = _rms(bg_ref[bi] * yc) * mg_ref[:, D_MLSTM + D_S5:]
        cb_ref[bi] = z[chunk - 2:chunk, :]


def _mlstm_conv(qkvo, gate, bch, w, states_in, states_prev, *, layer, depth, bb, chunk):
    b, s, _ = qkvo.shape
    assert s % chunk == 0 and b % bb == 0
    zero_state = states_in is None
    kern = functools.partial(_mlstm_conv_kernel, layer=layer, bb=bb, chunk=chunk, zero_state=zero_state)
    col = lambda wd, j: pl.BlockSpec((bb, chunk, wd), lambda i, c: (i, c, j))
    st = lambda shape: pl.BlockSpec((None, bb) + shape, lambda i, c: (layer, i) + (0,) * len(shape))
    m_spec = pl.BlockSpec((None, b, MH), lambda i, c: (layer, 0, 0))
    cst = lambda a: pl.BlockSpec(a.shape, lambda i, c: (0,) * a.ndim)
    state_specs = [st((MH, DH, DH)), st((MH, DH)), m_spec, st((CONV_W - 1, D_CONV))]
    out_state_specs = state_specs
    if layer == 0:
        st0 = lambda shape: pl.BlockSpec((depth, bb) + shape, lambda i, c: (0, i) + (0,) * len(shape))
        out_state_specs = [st0((MH, DH, DH)), st0((MH, DH)), pl.BlockSpec((depth, b, MH), lambda i, c: (0, 0, 0)),
                           st0((CONV_W - 1, D_CONV))]
    consts = (w["gb"], w["cw"], w["mg"])
    args = [qkvo, qkvo, qkvo, qkvo, gate, bch, bch, bch, *consts]
    in_specs = [col(D_MLSTM, 0), col(D_MLSTM, 1), col(D_MLSTM, 2), col(D_MLSTM, 3), col(GATE_W, 0),
                col(D_CONV, 0), col(D_CONV, 1), col(D_CONV, 2)] + [cst(a) for a in consts]
    if not zero_state:
        args += list(states_in)
        in_specs += state_specs
    aliases = {}
    if layer > 0:
        aliases = {len(args) + j: 2 + j for j in range(4)}
        args += list(states_prev)
        in_specs += [_ANY] * 4
    return pl.pallas_call(
        kern,
        grid=(b // bb, s // chunk),
        in_specs=in_specs,
        out_specs=[col(D_MLSTM, 0), col(D_CONV, 0)] + out_state_specs,
        out_shape=[jax.ShapeDtypeStruct((b, s, D_MLSTM), F32), jax.ShapeDtypeStruct((b, s, D_CONV), F32)]
        + _state_shapes(depth, b),
        scratch_shapes=[pltpu.VMEM((-(-bb // SUBLANES) * SUBLANES, GATE_W), F32)],
        input_output_aliases=aliases,
        compiler_params=pltpu.CompilerParams(dimension_semantics=("arbitrary", "arbitrary"),
                                             vmem_limit_bytes=VMEM_LIMIT),
        name="mlstm_conv",
    )(*args)


def _gate_terms(graw, gb_ref, gbt_ref, tril, triu):
    sel = (lax.broadcasted_iota(jnp.int32, (SUBLANES, GATE_W), 0)
           == lax.broadcasted_iota(jnp.int32, (SUBLANES, GATE_W), 1)).astype(F32)
    g_c = graw + gb_ref[...]
    lf_c = _log_sigmoid(g_c)
    g_r = _dot_nt_exact(sel, graw) + gbt_ref[...]
    lf_r = _log_sigmoid(g_r)
    return g_c, lf_c, _dot_exact(tril, lf_c), g_r, _dot_exact(lf_r, triu)


def _head_scores(h, g_c, bt_c, g_r, bt_r, m_prev, mask):
    it_row = g_r[h:h + 1, :]
    bt_row = bt_r[MH + h:MH + h + 1, :]
    it_col = g_c[:, h:h + 1]
    bt_col = bt_c[:, MH + h:MH + h + 1]
    dmat = jnp.where(mask, bt_col - bt_row + it_row, NEG)
    inter = bt_col + m_prev
    mt = jnp.maximum(jnp.max(dmat, axis=-1, keepdims=True), inter)
    return jnp.exp(dmat - mt), jnp.exp(inter - mt), mt, it_col, bt_col


def _head_output(sqk, vb, s_inter, qc, qn, mt, o, mg):
    num = _dot(sqk.astype(BF16), vb) + s_inter * qc
    den = jnp.sum(sqk, axis=-1, keepdims=True) + s_inter * qn
    hm = jax.nn.sigmoid(o) * (num / jnp.maximum(jnp.abs(den), jnp.exp(-mt)))
    return _rms(hm) * mg


def _bcast_block_last(x, t, s):
    out = x
    for d in range(1, s):
        out = jnp.where(t == s - 1 - d, pltpu.roll(x, x.shape[0] - d, 0), out)
    return out


def _mlstm_conv_short_kernel(*refs, layer, nb, s):
    (q_ref, k_ref, v_ref, o_ref, gate_ref, bg_ref, cg_ref, hc_ref, gb_ref, gbt_ref, cw_ref, mg_ref,
     c0_ref, n0_ref, m0_ref, cb0_ref) = refs[:16]
    refs = refs[16:]
    if layer > 0:
        refs = refs[4:]
    oa_ref, oc_ref, c_ref, n_ref, m_ref, cb_ref, qc_scr, st_scr = refs
    if layer == 0:
        for ref in (c_ref, n_ref, m_ref, cb_ref):
            if ref.shape[0] > 1:
                ref[1:] = jnp.zeros((ref.shape[0] - 1,) + ref.shape[1:], F32)
        c_ref, n_ref, m_ref, cb_ref = c_ref.at[0], n_ref.at[0], m_ref.at[0], cb_ref.at[0]
    r = nb * s
    per_tile = SUBLANES // s
    shift = s.bit_length() - 1
    row = lax.broadcasted_iota(jnp.int32, (r, r), 0)
    col = lax.broadcasted_iota(jnp.int32, (r, r), 1)
    same = (row >> shift) == (col >> shift)
    mask = same & (col <= row)
    tril = mask.astype(F32)
    triu = (same & (row <= col)).astype(F32)
    expand = ((lax.broadcasted_iota(jnp.int32, (r, nb), 0) >> shift)
              == lax.broadcasted_iota(jnp.int32, (r, nb), 1)).astype(F32)
    gather = (lax.broadcasted_iota(jnp.int32, (nb, r), 0)
              == (lax.broadcasted_iota(jnp.int32, (nb, r), 1) >> shift)).astype(F32)
    t128 = lax.broadcasted_iota(jnp.int32, (r, DH), 0) & (s - 1)
    sub8 = lax.broadcasted_iota(jnp.int32, (SUBLANES, DH), 0) >> shift
    scale = DH ** -0.5

    g_c, lf_c, bt_c, g_r, bt_r = _gate_terms(gate_ref[...], gb_ref, gbt_ref, tril, triu)
    bt_last_c = _dot_exact(same.astype(F32), lf_c)
    m_rows = _dot_exact(expand, m0_ref[...])
    for h in range(MH):
        hs = slice(h * DH, (h + 1) * DH)
        m_prev = m_rows[:, h:h + 1]
        w, s_inter, mt, it_col, bt_col = _head_scores(h, g_c, bt_c, g_r, bt_r, m_prev, mask)
        qh = q_ref[:, hs]
        kh = k_ref[:, hs] * scale
        vh = v_ref[:, hs]
        qb = qh.astype(BF16)
        kb = kh.astype(BF16)
        sqk = _dot_nt(qb, kb) * w
        for j in range(r // SUBLANES):
            rows = slice(j * SUBLANES, (j + 1) * SUBLANES)
            acc = None
            q8 = q_ref[rows, hs].astype(BF16)
            for p in range(per_tile):
                part = _dot_nt(q8, c0_ref[j * per_tile + p, h].astype(BF16))
                acc = part if acc is None else jnp.where(sub8 == p, part, acc)
            qc_scr[rows, :] = acc
        n_rows = _dot_exact(expand, n0_ref[:, h, :])
        qn = jnp.sum(qh * n_rows, axis=-1, keepdims=True)
        oa_ref[:, hs] = _head_output(sqk, vh.astype(BF16), s_inter, qc_scr[...], qn, mt,
                                     o_ref[:, hs], mg_ref[:, hs])
        m_new = _bcast_block_last(jnp.broadcast_to(mt, (r, DH)), t128, s)
        bt_last = bt_last_c[:, MH + h:MH + h + 1]
        wk = jnp.exp(bt_last - bt_col + it_col - m_new)
        decay = jnp.exp(bt_last + m_prev - m_new)
        vw = vh * wk
        for j in range(r // SUBLANES):
            rows = slice(j * SUBLANES, (j + 1) * SUBLANES)
            k8 = (k_ref[rows, hs] * scale).astype(BF16)
            for p in range(per_tile):
                bidx = j * per_tile + p
                last = bidx * s + s - 1
                upd = _dot_tn(jnp.where(sub8 == p, vw[rows], 0.0).astype(BF16), k8)
                c_ref[bidx, h] = decay[last:last + 1, 0:1] * c0_ref[bidx, h] + upd
        st_scr[0] = decay
        st_scr[1] = m_new
        last_rows = pl.ds(s - 1, nb, stride=s)
        n_ref[:, h, :] = st_scr[0, last_rows, :] * n0_ref[:, h, :] + _dot_exact(gather, kh * wk)
        m_ref[:, h:h + 1] = st_scr[1, last_rows, :][:, 0:1]
    t256 = lax.broadcasted_iota(jnp.int32, (r, D_CONV), 0) & (s - 1)
    z = cg_ref[...] * hc_ref[...]
    cb_a = _dot_exact(expand, cb0_ref[:, 0, :])
    cb_b = _dot_exact(expand, cb0_ref[:, 1, :])
    z1 = jnp.where(t256 == 0, cb_b, pltpu.roll(z, 1, 0))
    z2 = jnp.where(t256 == 0, cb_a, jnp.where(t256 == 1, cb_b, pltpu.roll(z, 2, 0)))
    yc = cw_ref[0:1, :] * z2 + cw_ref[1:2, :] * z1 + cw_ref[2:3, :] * z
    oc_ref[...] = _rms(bg_ref[...] * yc) * mg_ref[:, D_MLSTM + D_S5:]
    for half in range(D_CONV // DH):
        lanes = slice(half * DH, (half + 1) * DH)
        st_scr[half] = z[:, lanes]
        cb_ref[:, 0, lanes] = st_scr[half, pl.ds(s - 2, nb, stride=s), :]
        cb_ref[:, 1, lanes] = st_scr[half, pl.ds(s - 1, nb, stride=s), :]


def _short_batch_block(b, depth, layer):
    out_layers = depth if layer == 0 else 1
    nb = min(b, 32)
    while nb > SUBLANES and 2 * (1 + out_layers) * nb * MH * DH * DH * 4 > STATE_VMEM_BUDGET:
        nb //= 2
    return nb


def _mlstm_conv_short(qkvo, gate, bch, w, states_in, states_prev, *, layer, depth, b, s):
    nb = _short_batch_block(b, depth, layer)
    assert SUBLANES % s == 0 and s >= CONV_W - 1 and b % nb == 0 and (nb * s) % SUBLANES == 0
    r = nb * s
    kern = functools.partial(_mlstm_conv_short_kernel, layer=layer, nb=nb, s=s)
    col = lambda wd, j: pl.BlockSpec((r, wd), lambda i: (i, j))
    st = lambda shape: pl.BlockSpec((None, nb) + shape, lambda i: (layer, i) + (0,) * len(shape))
    cst = lambda a: pl.BlockSpec(a.shape, lambda i: (0,) * a.ndim)
    state_specs = [st((MH, DH, DH)), st((MH, DH)), st((MH,)), st((CONV_W - 1, D_CONV))]
    out_state_specs = state_specs
    if layer == 0:
        st0 = lambda shape: pl.BlockSpec((depth, nb) + shape, lambda i: (0, i) + (0,) * len(shape))
        out_state_specs = [st0((MH, DH, DH)), st0((MH, DH)), st0((MH,)), st0((CONV_W - 1, D_CONV))]
    consts = (w["gb"], w["gbt"], w["cw"], w["mg"])
    args = [qkvo, qkvo, qkvo, qkvo, gate, bch, bch, bch, *consts, *states_in]
    in_specs = [col(D_MLSTM, 0), col(D_MLSTM, 1), col(D_MLSTM, 2), col(D_MLSTM, 3), col(GATE_W, 0),
                col(D_CONV, 0), col(D_CONV, 1), col(D_CONV, 2)] + [cst(a) for a in consts] + state_specs
    aliases = {}
    if layer > 0:
        aliases = {len(args) + j: 2 + j for j in range(4)}
        args += list(states_prev)
        in_specs += [_ANY] * 4
    return pl.pallas_call(
        kern,
        grid=(b // nb,),
        in_specs=in_specs,
        out_specs=[col(D_MLSTM, 0), col(D_CONV, 0)] + out_state_specs,
        out_shape=[jax.ShapeDtypeStruct((b * s, D_MLSTM), F32), jax.ShapeDtypeStruct((b * s, D_CONV), F32)]
        + _state_shapes(depth, b),
        scratch_shapes=[pltpu.VMEM((r, DH), F32), pltpu.VMEM((2, r, DH), F32)],
        input_output_aliases=aliases,
        compiler_params=pltpu.CompilerParams(dimension_semantics=("arbitrary",), vmem_limit_bytes=VMEM_LIMIT),
        name="mlstm_conv_short",
    )(*args)


def _s5_kernel(*refs, layer, nb, tt, zero_state, parts, batch_major):
    u_ref, lam_ref, bblk_ref, cblk_ref, d_ref, wglu_ref, mg_ref = refs[:7]
    refs = refs[7:]
    if not zero_state:
        hr0_ref, hi0_ref = refs[:2]
        refs = refs[2:]
    if layer > 0:
        refs = refs[2:]
    ob_ref, hr_ref, hi_ref, xs_ref, a_ref, bf_ref = refs[:6]
    if batch_major:
        ut_scr, ot_scr = refs[6:]
        ut_scr[...] = jnp.swapaxes(u_ref[...], 0, 1).reshape(tt * nb, D_S5)
        u_bm_ref, ob_bm_ref, u_ref, ob_ref = u_ref, ob_ref, ut_scr, ot_scr
    if layer == 0:
        hr_all, hi_all = hr_ref, hi_ref
        hr_ref, hi_ref = hr_all.at[0], hi_all.at[0]
    ti = pl.program_id(0)
    n = S5_N
    lc = 512

    @pl.when(ti == 0)
    def _():
        if layer == 0:
            hr_all[...] = jnp.zeros_like(hr_all)
            hi_all[...] = jnp.zeros_like(hi_all)
        hr_ref[...] = jnp.zeros_like(hr_ref) if zero_state else hr0_ref[...]
        hi_ref[...] = jnp.zeros_like(hi_ref) if zero_state else hi0_ref[...]
        lre = jnp.minimum(lam_ref[0:1, :], -1e-4)
        lim = lam_ref[1:2, :]
        dt = jnp.exp(lam_ref[2:3, :])
        mag = jnp.exp(lre * dt)
        ab_re = mag * jnp.cos(lim * dt)
        ab_im = mag * jnp.sin(lim * dt)
        den = lre * lre + lim * lim
        nr = ab_re - 1.0
        fre = (nr * lre + ab_im * lim) / den
        fim = (ab_im * lre - nr * lim) / den
        a_ref[:, :n] = jnp.broadcast_to(ab_re, (SUBLANES, n))
        a_ref[:, n:] = jnp.broadcast_to(ab_im, (SUBLANES, n))
        b_re = bblk_ref[:, :n]
        b_im = bblk_ref[:, n:]
        bf_ref[:, :n] = (fre * b_re - fim * b_im).astype(BF16)
        bf_ref[:, n:] = (fre * b_im + fim * b_re).astype(BF16)

    rows_p = nb * tt // parts
    part_rows = [slice(p * rows_p, (p + 1) * rows_p) for p in range(parts)]
    xs_ref[...] = _dot(u_ref[...].astype(BF16), bf_ref[...])
    ar = [a_ref[:, c0:c0 + lc] for c0 in range(0, n, lc)]
    ai = [a_ref[:, n + c0:n + c0 + lc] for c0 in range(0, n, lc)]
    h, y = {}, []
    for p, rs in enumerate(part_rows):
        for rg in range(nb // SUBLANES):
            g8 = slice(rg * SUBLANES, (rg + 1) * SUBLANES)
            for ci, c0 in enumerate(range(0, n, lc)):
                if p == 0:
                    h[rg, ci] = (hr_ref[g8, c0:c0 + lc], hi_ref[g8, c0:c0 + lc])
            for t in range(p * tt // parts, (p + 1) * tt // parts):
                r8 = slice(t * nb + rg * SUBLANES, t * nb + (rg + 1) * SUBLANES)
                for ci, c0 in enumerate(range(0, n, lc)):
                    hr, hi = h[rg, ci]
                    nr_ = ar[ci] * hr - ai[ci] * hi + xs_ref[r8, c0:c0 + lc]
                    ni_ = ar[ci] * hi + ai[ci] * hr + xs_ref[r8, n + c0:n + c0 + lc]
                    xs_ref[r8, c0:c0 + lc] = nr_
                    xs_ref[r8, n + c0:n + c0 + lc] = ni_
                    h[rg, ci] = (nr_, ni_)
            for ci, c0 in enumerate(range(0, n, lc)):
                if p == parts - 1:
                    hr_ref[g8, c0:c0 + lc] = h[rg, ci][0]
                    hi_ref[g8, c0:c0 + lc] = h[rg, ci][1]
        y.append(_dot(xs_ref[rs, :].astype(BF16), cblk_ref[...]) + d_ref[...] * u_ref[rs, :])
    g = [jax.nn.gelu(v) for v in y]
    gate = [_dot(v.astype(BF16), wglu_ref[...]) for v in g]
    for rs, v, z in zip(part_rows, g, gate):
        ob_ref[rs, :] = _rms(v * jax.nn.sigmoid(z)) * mg_ref[:, D_MLSTM:D_MLSTM + D_S5]
    if batch_major:
        ob_bm_ref[...] = jnp.swapaxes(ot_scr[...].reshape(tt, nb, D_S5), 0, 1)


def _s5(u, w, states_in, states_prev, *, layer, depth, nb, tt):
    batch_major = u.ndim == 3
    rows = u.shape[0] * u.shape[1] if batch_major else u.shape[0]
    rt = nb * tt
    zero_state = states_in is None
    parts = S5_PARTS if tt % (SUBLANES * S5_PARTS) == 0 else 1
    kern = functools.partial(_s5_kernel, layer=layer, nb=nb, tt=tt, zero_state=zero_state, parts=parts,
                             batch_major=batch_major)
    if batch_major:
        assert u.shape[0] == nb and tt % SUBLANES == 0
        io_spec = pl.BlockSpec((nb, tt, D_S5), lambda i: (0, i, 0))
        io_shape = jax.ShapeDtypeStruct(u.shape, F32)
        io_scratch = [pltpu.VMEM((rt, D_S5), F32), pltpu.VMEM((rt, D_S5), F32)]
    else:
        io_spec = pl.BlockSpec((rt, D_S5), lambda i: (i, 0))
        io_shape = jax.ShapeDtypeStruct((rows, D_S5), F32)
        io_scratch = []
    cst = lambda a: pl.BlockSpec(a.shape, lambda i: (0,) * a.ndim)
    st = pl.BlockSpec((None, nb, S5_N), lambda i: (layer, 0, 0))
    st_out = pl.BlockSpec((depth, nb, S5_N), lambda i: (0, 0, 0)) if layer == 0 else st
    consts = (w["lam"], w["bblk"], w["cblk"], w["d"], w["wglu"], w["mg"])
    args = [u, *consts]
    in_specs = [io_spec] + [cst(a) for a in consts]
    if not zero_state:
        args += list(states_in)
        in_specs += [st, st]
    aliases = {}
    if layer > 0:
        aliases = {len(args) + j: 1 + j for j in range(2)}
        args += list(states_prev)
        in_specs += [_ANY] * 2
    return pl.pallas_call(
        kern,
        grid=(rows // rt,),
        in_specs=in_specs,
        out_specs=[io_spec, st_out, st_out],
        out_shape=[io_shape, jax.ShapeDtypeStruct((depth, nb, S5_N), F32),
                   jax.ShapeDtypeStruct((depth, nb, S5_N), F32)],
        scratch_shapes=[pltpu.VMEM((rt, 2 * S5_N), F32), pltpu.VMEM((SUBLANES, 2 * S5_N), F32),
                        pltpu.VMEM((D_S5, 2 * S5_N), BF16)] + io_scratch,
        input_output_aliases=aliases,
        compiler_params=pltpu.CompilerParams(dimension_semantics=("arbitrary",), vmem_limit_bytes=VMEM_LIMIT),
        name="s5",
    )(*args)


def _block_diag(w):
    g, r, c = w.shape
    eye = jnp.eye(g, dtype=w.dtype)
    return jnp.einsum("grc,gh->grhc", w, eye).reshape(g * r, g * c)


def _layer_weights(l, ig_bias, fg_bias, s5_a_re, s5_a_im, s5_log_dt, s5_b_re, s5_b_im, s5_c_re, s5_c_im, s5_d,
                   w_glu, conv_w, mix_g):
    gb = jnp.concatenate([ig_bias[l], fg_bias[l], jnp.zeros((GATE_W - 2 * MH,), F32)])
    lam = jnp.stack([s5_a_re[l].reshape(-1), s5_a_im[l].reshape(-1),
                     jnp.repeat(s5_log_dt[l], S5_P)])
    bblk = jnp.concatenate([_block_diag(jnp.swapaxes(s5_b_re[l], 1, 2)),
                            _block_diag(jnp.swapaxes(s5_b_im[l], 1, 2))], axis=1)
    cblk = jnp.concatenate([_block_diag(jnp.swapaxes(s5_c_re[l], 1, 2)),
                            -_block_diag(jnp.swapaxes(s5_c_im[l], 1, 2))], axis=0).astype(BF16)
    return dict(
        gb=gb.reshape(1, GATE_W), gbt=gb[:SUBLANES].reshape(SUBLANES, 1),
        lam=lam, bblk=bblk, cblk=cblk, d=s5_d[l].reshape(1, D_S5), wglu=w_glu[l].astype(BF16),
        cw=conv_w[l], mg=mix_g[l].reshape(1, D_MODEL))


def _stacked_weights(ffn1_w1, ffn1_w3, ffn1_w2, ffn2_w1, ffn2_w3, ffn2_w2, norm_g, w_in, w_out):
    o = 4 * D_MLSTM
    win = (w_in[:, :, :o].astype(BF16),
           jnp.pad(w_in[:, :, o:o + 2 * MH].astype(BF16), ((0, 0), (0, 0), (0, GATE_W - 2 * MH))),
           w_in[:, :, o + 2 * MH:].astype(BF16))
    return dict(f1=(ffn1_w1.astype(BF16), ffn1_w3.astype(BF16), ffn1_w2.astype(BF16)),
                f2=(ffn2_w1.astype(BF16), ffn2_w3.astype(BF16), ffn2_w2.astype(BF16)),
                g=norm_g, win=win, wo=w_out.astype(BF16))


def _run_group(x, states, weights, big):
    b, s, _ = x.shape
    depth = len(weights)
    tm = min(FFN_ROWS, b * s)
    assert (b * s) % tm == 0
    long_seq = s % MLSTM_CHUNK == 0 and b % MLSTM_ROWS == 0
    chunk, bb = (MLSTM_CHUNK, MLSTM_ROWS) if long_seq else (None, None)
    tt = S5_STEPS if s % S5_STEPS == 0 else s
    xf = x.reshape(b * s, D_MODEL)
    mstates = sstates = None
    if states is not None:
        c_all, n_all, m_all, sr_all, si_all, cb_all = states
        ms_in = (c_all, n_all, m_all, cb_all)
        ss_in = (sr_all.reshape(depth, b, S5_N), si_all.reshape(depth, b, S5_N))
    else:
        ms_in = ss_in = None
    for l, w in enumerate(weights):
        x1, qkvo, gate, u, bch = _ffn_inproj(xf, big["g"], *big["f1"], big["win"], tm, l)
        if tt % SUBLANES == 0:
            ob, *sstates = _s5(u.reshape(b, s, D_S5), w, ss_in, sstates, layer=l, depth=depth, nb=b, tt=tt)
            ob = ob.reshape(b * s, D_S5)
        else:
            u_tm = jnp.swapaxes(u.reshape(b, s, D_S5), 0, 1).reshape(s * b, D_S5)
            ob_tm, *sstates = _s5(u_tm, w, ss_in, sstates, layer=l, depth=depth, nb=b, tt=tt)
            ob = jnp.swapaxes(ob_tm.reshape(s, b, D_S5), 0, 1).reshape(b * s, D_S5)
        if chunk is not None:
            r3 = lambda a: a.reshape(b, s, a.shape[-1])
            oa, oc, *mstates = _mlstm_conv(r3(qkvo), r3(gate), r3(bch), w, ms_in, mstates,
                                           layer=l, depth=depth, bb=bb, chunk=chunk)
            xf = _outproj_ffn(x1, oa.reshape(b * s, D_MLSTM), ob, oc.reshape(b * s, D_CONV),
                              big["g"], big["wo"], *big["f2"], tm, l)
        else:
            oa, oc, *mstates = _mlstm_conv_short(qkvo, gate, bch, w, ms_in, mstates,
                                                 layer=l, depth=depth, b=b, s=s)
            xf = _outproj_ffn(x1, oa, ob, oc, big["g"], big["wo"], *big["f2"], tm, l)
    c1, n1, m1, cb1 = mstates
    sr1, si1 = sstates
    return xf.reshape(b, s, D_MODEL), (c1, n1, m1, sr1.reshape(depth, b, S5_G, S5_P),
                                       si1.reshape(depth, b, S5_G, S5_P), cb1)


def kernel(x_prompt, x_sample, state_mlstm_C, state_mlstm_n, state_mlstm_m, state_s5_re, state_s5_im,
           state_conv, ffn1_w1, ffn1_w3, ffn1_w2, ffn2_w1, ffn2_w3, ffn2_w2, norm_g, w_in, ig_bias, fg_bias,
           s5_a_re, s5_a_im, s5_log_dt, s5_b_re, s5_b_im, s5_c_re, s5_c_im, s5_d, w_glu, conv_w, mix_g, w_out):
    depth = norm_g.shape[0]
    params = (ig_bias, fg_bias, s5_a_re, s5_a_im, s5_log_dt, s5_b_re, s5_b_im, s5_c_re, s5_c_im, s5_d, w_glu,
              conv_w, mix_g)
    weights = [_layer_weights(l, *params) for l in range(depth)]
    big = _stacked_weights(ffn1_w1, ffn1_w3, ffn1_w2, ffn2_w1, ffn2_w3, ffn2_w2, norm_g, w_in, w_out)
    y_p, st_p = _run_group(x_prompt, None, weights, big)
    y_s, st_s = _run_group(
        x_sample, (state_mlstm_C, state_mlstm_n, state_mlstm_m, state_s5_re, state_s5_im, state_conv),
        weights, big)
    return (y_p, y_s, *st_p, *st_s)
```

```python
import functools

import jax
import jax.numpy as jnp
from jax import lax
from jax.experimental import pallas as pl
from jax.experimental.pallas import tpu as pltpu

F32 = jnp.float32
BF16 = jnp.bfloat16

D_MODEL = 1024
MH = 4
DH = 128
D_MLSTM = MH * DH
S5_CH = 16
S5_G = 16
S5_P = 64
D_S5 = S5_G * S5_CH
S5_N = S5_G * S5_P
D_CONV = 256
CONV_W = 3
EPS = 1e-6
GATE_W = 128
SUBLANES = 8
NEG = -1e30
VMEM_LIMIT = 56 * 1024 * 1024
STATE_VMEM_BUDGET = 36 * 1024 * 1024
FFN_ROWS = 512
ROW_PARTS = 2
S5_STEPS = 256
S5_PARTS = 2
MLSTM_CHUNK = DH
MLSTM_ROWS = 8


def _dot(a, b):
    return jnp.dot(a, b, preferred_element_type=F32)


def _dot_nt(a, b):
    return lax.dot_general(a, b, (((1,), (1,)), ((), ())), preferred_element_type=F32)


def _dot_tn(a, b):
    return lax.dot_general(a, b, (((0,), (0,)), ((), ())), preferred_element_type=F32)


def _dot_exact(a, b):
    return jnp.dot(a, b, preferred_element_type=F32, precision=lax.Precision.HIGHEST)


def _dot_nt_exact(a, b):
    return lax.dot_general(a, b, (((1,), (1,)), ((), ())), preferred_element_type=F32,
                           precision=lax.Precision.HIGHEST)


def _rms(x):
    return x * lax.rsqrt(jnp.mean(x * x, axis=-1, keepdims=True) + EPS)


def _log_sigmoid(x):
    return jnp.minimum(x, 0.0) - jnp.log(1.0 + jnp.exp(-jnp.abs(x)))


def _split_bf16(x, parts):
    out = []
    for _ in range(parts):
        p = x.astype(BF16)
        out.append(p)
        x = x - p.astype(F32)
    return jnp.concatenate(out, axis=1)


def _ffn_residual(xs, g_pre, g_post, w1, w3, w2):
    xn = [(_rms(x) * g_pre).astype(BF16) for x in xs]
    h1 = [_dot(v, w1[...]) for v in xn]
    h3 = [_dot(v, w3[...]) for v in xn]
    a = [(jax.nn.silu(p) * q).astype(BF16) for p, q in zip(h1, h3)]
    y = [_dot(v, w2[...]) for v in a]
    return [x + 0.5 * (_rms(t) * g_post) for x, t in zip(xs, y)]


def _row_parts(tm, parts):
    step = tm // parts
    return [slice(i * step, (i + 1) * step) for i in range(parts)]


def _layer_spec(a, layer):
    return pl.BlockSpec((None,) + a.shape[1:], lambda *_: (layer, 0, 0), pipeline_mode=pl.Buffered(1))


_ANY = pl.BlockSpec(memory_space=pl.ANY)


def _ffn_inproj_kernel(x_ref, g_ref, w1_ref, w3_ref, w2_ref, wq_ref, wg_ref, wr_ref,
                       x1_ref, qkvo_ref, gate_ref, u_ref, bch_ref):
    parts = _row_parts(x_ref.shape[0], ROW_PARTS)
    x1 = _ffn_residual([x_ref[r, :] for r in parts], g_ref[0:1, :], g_ref[1:2, :], w1_ref, w3_ref, w2_ref)
    hn = [(_rms(v) * g_ref[2:3, :]).astype(BF16) for v in x1]
    qkvo = [_dot(v, wq_ref[...]) for v in hn]
    gate = [_dot(v, wg_ref[...]) for v in hn]
    rest = [_dot(v, wr_ref[...]) for v in hn]
    for i, r in enumerate(parts):
        x1_ref[r, :] = x1[i]
        qkvo_ref[r, :] = qkvo[i]
        gate_ref[r, :] = gate[i]
        u_ref[r, :] = rest[i][:, :D_S5]
        bch_ref[r, :] = rest[i][:, D_S5:]


def _ffn_inproj(x, g, w1, w3, w2, win, tm, layer):
    t = x.shape[0]
    row = lambda w: pl.BlockSpec((tm, w), lambda i: (i, 0))
    return pl.pallas_call(
        _ffn_inproj_kernel,
        grid=(t // tm,),
        in_specs=[row(D_MODEL), _layer_spec(g, layer), _layer_spec(w1, layer), _layer_spec(w3, layer),
                  _layer_spec(w2, layer)] + [_layer_spec(a, layer) for a in win],
        out_specs=[row(D_MODEL), row(4 * D_MLSTM), row(GATE_W), row(D_S5), row(3 * D_CONV)],
        out_shape=[jax.ShapeDtypeStruct((t, D_MODEL), F32), jax.ShapeDtypeStruct((t, 4 * D_MLSTM), F32),
                   jax.ShapeDtypeStruct((t, GATE_W), F32), jax.ShapeDtypeStruct((t, D_S5), F32),
                   jax.ShapeDtypeStruct((t, 3 * D_CONV), F32)],
        compiler_params=pltpu.CompilerParams(dimension_semantics=("arbitrary",), vmem_limit_bytes=VMEM_LIMIT),
        name="ffn_inproj",
    )(x, g, w1, w3, w2, *win)


def _outproj_ffn_kernel(x_ref, a_ref, b_ref, c_ref, g_ref, wo_ref, w1_ref, w3_ref, w2_ref, y_ref):
    parts = _row_parts(x_ref.shape[0], ROW_PARTS)
    mo = [(_dot(a_ref[r, :].astype(BF16), wo_ref[0:D_MLSTM, :])
           + _dot(b_ref[r, :].astype(BF16), wo_ref[D_MLSTM:D_MLSTM + D_S5, :])
           + _dot(c_ref[r, :].astype(BF16), wo_ref[D_MLSTM + D_S5:, :])) for r in parts]
    x2 = [x_ref[r, :] + _rms(v) * g_ref[3:4, :] for r, v in zip(parts, mo)]
    y = _ffn_residual(x2, g_ref[4:5, :], g_ref[5:6, :], w1_ref, w3_ref, w2_ref)
    for r, v in zip(parts, y):
        y_ref[r, :] = v


def _outproj_ffn(x, oa, ob, oc, g, wo, w1, w3, w2, tm, layer):
    t = x.shape[0]
    row = lambda w: pl.BlockSpec((tm, w), lambda i: (i, 0))
    return pl.pallas_call(
        _outproj_ffn_kernel,
        grid=(t // tm,),
        in_specs=[row(D_MODEL), row(D_MLSTM), row(D_S5), row(D_CONV), _layer_spec(g, layer),
                  _layer_spec(wo, layer), _layer_spec(w1, layer), _layer_spec(w3, layer),
                  _layer_spec(w2, layer)],
        out_specs=row(D_MODEL),
        out_shape=jax.ShapeDtypeStruct((t, D_MODEL), F32),
        compiler_params=pltpu.CompilerParams(dimension_semantics=("arbitrary",), vmem_limit_bytes=VMEM_LIMIT),
        name="outproj_ffn",
    )(x, oa, ob, oc, g, wo, w1, w3, w2)


def _state_shapes(depth, b):
    return [jax.ShapeDtypeStruct((depth, b, MH, DH, DH), F32), jax.ShapeDtypeStruct((depth, b, MH, DH), F32),
            jax.ShapeDtypeStruct((depth, b, MH), F32), jax.ShapeDtypeStruct((depth, b, CONV_W - 1, D_CONV), F32)]


def _cummax_rows(x, rowi):
    s = 1
    while s < x.shape[0]:
        x = jnp.maximum(x, jnp.where(rowi >= s, pltpu.roll(x, s, 0), NEG))
        s *= 2
    return x


def _mlstm_conv_kernel(*refs, layer, bb, chunk, zero_state):
    (q_ref, k_ref, v_ref, o_ref, gate_ref, bg_ref, cg_ref, hc_ref, gb_ref, cw_ref, mg_ref) = refs[:11]
    refs = refs[11:]
    if not zero_state:
        c0_ref, n0_ref, m0_ref, cb0_ref = refs[:4]
        refs = refs[4:]
    if layer > 0:
        refs = refs[4:]
    oa_ref, oc_ref, c_ref, n_ref, m_ref, cb_ref, m_scr = refs
    other_layers = ()
    if layer == 0:
        other_layers = tuple(r.at[1:] for r in (c_ref, n_ref, m_ref, cb_ref) if r.shape[0] > 1)
        c_ref, n_ref, m_ref, cb_ref = c_ref.at[0], n_ref.at[0], m_ref.at[0], cb_ref.at[0]
    assert chunk == DH
    bi0 = pl.program_id(0) * bb
    ci = pl.program_id(1)

    @pl.when(ci == 0)
    def _():
        for r in other_layers:
            r[...] = jnp.zeros_like(r)
        m_scr[...] = jnp.zeros_like(m_scr)
        if zero_state:
            c_ref[...] = jnp.zeros_like(c_ref)
            n_ref[...] = jnp.zeros_like(n_ref)
            cb_ref[...] = jnp.zeros_like(cb_ref)
        else:
            c_ref[...] = c0_ref[...]
            n_ref[...] = n0_ref[...]
            cb_ref[...] = cb0_ref[...]
            for bi in range(bb):
                m_scr[bi:bi + 1, 0:MH] = m0_ref[pl.ds(bi0 + bi, 1), :]

    row = lax.broadcasted_iota(jnp.int32, (chunk, chunk), 0)
    col = lax.broadcasted_iota(jnp.int32, (chunk, chunk), 1)
    causal = col <= row
    tril = jnp.where(causal, 1.0, 0.0).astype(BF16)
    rowg = lax.broadcasted_iota(jnp.int32, (chunk, GATE_W), 0)
    rowi = lax.broadcasted_iota(jnp.int32, (chunk, D_CONV), 0)
    spread = jnp.where((lax.broadcasted_iota(jnp.int32, (2 * GATE_W, D_MLSTM), 0) & (GATE_W - 1))
                       == (lax.broadcasted_iota(jnp.int32, (2 * GATE_W, D_MLSTM), 1) >> 7), 1.0, 0.0).astype(BF16)
    scale = DH ** -0.5

    units = [(bi, h) for bi in range(bb) for h in range(MH)]
    hsl = lambda h: slice(h * DH, (h + 1) * DH)
    g_c = [gate_ref[bi] + gb_ref[...] for bi in range(bb)]
    csum_all = _dot(tril, jnp.concatenate(
        [_split_bf16(pltpu.roll(_log_sigmoid(g), GATE_W - MH, 1), 3) for g in g_c], axis=1))
    csum = [csum_all[:, bi * 3 * GATE_W:(bi + 1) * 3 * GATE_W] for bi in range(bb)]
    qb = {u: q_ref[u[0], :, hsl(u[1])].astype(BF16) for u in units}
    kb = {u: (k_ref[u[0], :, hsl(u[1])] * scale).astype(BF16) for u in units}
    s_qk = {u: _dot_nt(qb[u], kb[u]) for u in units}
    inter = {u: _dot_nt(qb[u], jnp.concatenate(
        [c_ref[u[0], u[1]].astype(BF16),
         jnp.broadcast_to(n_ref[u[0], u[1]:u[1] + 1, :], (DH, DH)).astype(BF16)], axis=0)) for u in units}
    stacked, a_r, decay = [], [], []
    for bi in range(bb):
        f_c = csum[bi][:, :GATE_W] + csum[bi][:, GATE_W:2 * GATE_W] + csum[bi][:, 2 * GATE_W:]
        a_c = g_c[bi] - f_c
        m_prev = m_scr[bi:bi + 1, :]
        big_m = jnp.maximum(_cummax_rows(a_c, rowg), m_prev)
        mt = f_c + big_m
        m_new = mt[chunk - 1:chunk, :]
        f_last = f_c[chunk - 1:chunk, :]
        decay.append(jnp.exp(f_last + m_prev - m_new))
        stacked += [_split_bf16(-big_m, 2), _split_bf16(jnp.exp(m_prev - big_m), 2),
                    _split_bf16(-mt, 2), _split_bf16(jnp.exp(a_c + (f_last - m_new)), 2)]
        a_r.append(a_c.T[0:SUBLANES, :])
        m_scr[bi:bi + 1, :] = m_new
        m_ref[pl.ds(bi0 + bi, 1), :] = m_new[:, 0:MH]
    rep_all = _dot(jnp.concatenate(stacked, axis=0), spread)
    rep = [rep_all[bi * 4 * chunk:(bi + 1) * 4 * chunk] for bi in range(bb)]
    intra, rsum = {}, {}
    for u in units:
        bi, h = u
        w = jnp.exp(jnp.where(causal, rep[bi][0:chunk, hsl(h)] + a_r[bi][h:h + 1, :], NEG))
        sqk = s_qk[u] * w
        rsum[u] = jnp.sum(sqk, axis=-1, keepdims=True)
        intra[u] = _dot(sqk.astype(BF16), v_ref[bi, :, hsl(h)].astype(BF16))
    hm, ssq = {}, {}
    for u in units:
        bi, h = u
        s_inter = rep[bi][chunk:2 * chunk, hsl(h)]
        e_floor = jnp.exp(rep[bi][2 * chunk:3 * chunk, hsl(h)])
        num = intra[u] + s_inter * inter[u][:, :DH]
        den = rsum[u] + s_inter * inter[u][:, DH:]
        hm[u] = jax.nn.sigmoid(o_ref[bi, :, hsl(h)]) * (num / jnp.maximum(jnp.abs(den), e_floor))
        ssq[u] = jnp.sum(hm[u] * hm[u], axis=-1, keepdims=True)
    for u in units:
        bi, h = u
        oa_ref[bi, :, hsl(h)] = hm[u] * lax.rsqrt(ssq[u] * (1.0 / DH) + EPS) * mg_ref[:, hsl(h)]
        wk = rep[bi][3 * chunk:, hsl(h)]
        dec = decay[bi][:, h:h + 1]
        c_ref[bi, h] = dec * c_ref[bi, h] + _dot_tn((v_ref[bi, :, hsl(h)] * wk).astype(BF16), kb[u])
        n_ref[bi, h:h + 1, :] = (dec * n_ref[bi, h:h + 1, :]
                                 + jnp.sum(k_ref[bi, :, hsl(h)] * scale * wk, axis=0, keepdims=True))
    for bi in range(bb):
        z = cg_ref[bi] * hc_ref[bi]
        cb = cb_ref[bi]
        z1 = jnp.where(rowi == 0, cb[1:2, :], pltpu.roll(z, 1, 0))
        z2 = jnp.where(rowi == 0, cb[0:1, :], jnp.where(rowi == 1, cb[1:2, :], pltpu.roll(z, 2, 0)))
        yc = cw_ref[0:1, :] * z2 + cw_ref[1:2, :] * z1 + cw_ref[2:3, :] * z
        oc_ref[bi] = _rms(bg_ref[bi] * yc) * mg_ref[:, D_MLSTM + D_S5:]
        cb_ref[bi] = z[chunk - 2:chunk, :]


def _mlstm_conv(qkvo, gate, bch, w, states_in, states_prev, *, layer, depth, bb, chunk):
    b, s, _ = qkvo.shape
    assert s % chunk == 0 and b % bb == 0
    zero_state = states_in is None
    kern = functools.partial(_mlstm_conv_kernel, layer=layer, bb=bb, chunk=chunk, zero_state=zero_state)
    col = lambda wd, j: pl.BlockSpec((bb, chunk, wd), lambda i, c: (i, c, j))
    st = lambda shape: pl.BlockSpec((None, bb) + shape, lambda i, c: (layer, i) + (0,) * len(shape))
    m_spec = pl.BlockSpec((None, b, MH), lambda i, c: (layer, 0, 0))
    cst = lambda a: pl.BlockSpec(a.shape, lambda i, c: (0,) * a.ndim)
    state_specs = [st((MH, DH, DH)), st((MH, DH)), m_spec, st((CONV_W - 1, D_CONV))]
    out_state_specs = state_specs
    if layer == 0:
        st0 = lambda shape: pl.BlockSpec((depth, bb) + shape, lambda i, c: (0, i) + (0,) * len(shape))
        out_state_specs = [st0((MH, DH, DH)), st0((MH, DH)), pl.BlockSpec((depth, b, MH), lambda i, c: (0, 0, 0)),
                           st0((CONV_W - 1, D_CONV))]
    consts = (w["gb"], w["cw"], w["mg"])
    args = [qkvo, qkvo, qkvo, qkvo, gate, bch, bch, bch, *consts]
    in_specs = [col(D_MLSTM, 0), col(D_MLSTM, 1), col(D_MLSTM, 2), col(D_MLSTM, 3), col(GATE_W, 0),
                col(D_CONV, 0), col(D_CONV, 1), col(D_CONV, 2)] + [cst(a) for a in consts]
    if not zero_state:
        args += list(states_in)
        in_specs += state_specs
    aliases = {}
    if layer > 0:
        aliases = {len(args) + j: 2 + j for j in range(4)}
        args += list(states_prev)
        in_specs += [_ANY] * 4
    return pl.pallas_call(
        kern,
        grid=(b // bb, s // chunk),
        in_specs=in_specs,
        out_specs=[col(D_MLSTM, 0), col(D_CONV, 0)] + out_state_specs,
        out_shape=[jax.ShapeDtypeStruct((b, s, D_MLSTM), F32), jax.ShapeDtypeStruct((b, s, D_CONV), F32)]
        + _state_shapes(depth, b),
        scratch_shapes=[pltpu.VMEM((-(-bb // SUBLANES) * SUBLANES, GATE_W), F32)],
        input_output_aliases=aliases,
        compiler_params=pltpu.CompilerParams(dimension_semantics=("arbitrary", "arbitrary"),
                                             vmem_limit_bytes=VMEM_LIMIT),
        name="mlstm_conv",
    )(*args)


def _gate_terms(graw, gb_ref, gbt_ref, tril, triu):
    sel = (lax.broadcasted_iota(jnp.int32, (SUBLANES, GATE_W), 0)
           == lax.broadcasted_iota(jnp.int32, (SUBLANES, GATE_W), 1)).astype(F32)
    g_c = graw + gb_ref[...]
    lf_c = _log_sigmoid(g_c)
    g_r = _dot_nt_exact(sel, graw) + gbt_ref[...]
    lf_r = _log_sigmoid(g_r)
    return g_c, lf_c, _dot_exact(tril, lf_c), g_r, _dot_exact(lf_r, triu)


def _head_scores(h, g_c, bt_c, g_r, bt_r, m_prev, mask):
    it_row = g_r[h:h + 1, :]
    bt_row = bt_r[MH + h:MH + h + 1, :]
    it_col = g_c[:, h:h + 1]
    bt_col = bt_c[:, MH + h:MH + h + 1]
    dmat = jnp.where(mask, bt_col - bt_row + it_row, NEG)
    inter = bt_col + m_prev
    mt = jnp.maximum(jnp.max(dmat, axis=-1, keepdims=True), inter)
    return jnp.exp(dmat - mt), jnp.exp(inter - mt), mt, it_col, bt_col


def _head_output(sqk, vb, s_inter, qc, qn, mt, o, mg):
    num = _dot(sqk.astype(BF16), vb) + s_inter * qc
    den = jnp.sum(sqk, axis=-1, keepdims=True) + s_inter * qn
    hm = jax.nn.sigmoid(o) * (num / jnp.maximum(jnp.abs(den), jnp.exp(-mt)))
    return _rms(hm) * mg


def _bcast_block_last(x, t, s):
    out = x
    for d in range(1, s):
        out = jnp.where(t == s - 1 - d, pltpu.roll(x, x.shape[0] - d, 0), out)
    return out


def _mlstm_conv_short_kernel(*refs, layer, nb, s):
    (q_ref, k_ref, v_ref, o_ref, gate_ref, bg_ref, cg_ref, hc_ref, gb_ref, gbt_ref, cw_ref, mg_ref,
     c0_ref, n0_ref, m0_ref, cb0_ref) = refs[:16]
    refs = refs[16:]
    if layer > 0:
        refs = refs[4:]
    oa_ref, oc_ref, c_ref, n_ref, m_ref, cb_ref, qc_scr, st_scr = refs
    if layer == 0:
        for ref in (c_ref, n_ref, m_ref, cb_ref):
            if ref.shape[0] > 1:
                ref[1:] = jnp.zeros((ref.shape[0] - 1,) + ref.shape[1:], F32)
        c_ref, n_ref, m_ref, cb_ref = c_ref.at[0], n_ref.at[0], m_ref.at[0], cb_ref.at[0]
    r = nb * s
    per_tile = SUBLANES // s
    shift = s.bit_length() - 1
    row = lax.broadcasted_iota(jnp.int32, (r, r), 0)
    col = lax.broadcasted_iota(jnp.int32, (r, r), 1)
    same = (row >> shift) == (col >> shift)
    mask = same & (col <= row)
    tril = mask.astype(F32)
    triu = (same & (row <= col)).astype(F32)
    expand = ((lax.broadcasted_iota(jnp.int32, (r, nb), 0) >> shift)
              == lax.broadcasted_iota(jnp.int32, (r, nb), 1)).astype(F32)
    gather = (lax.broadcasted_iota(jnp.int32, (nb, r), 0)
              == (lax.broadcasted_iota(jnp.int32, (nb, r), 1) >> shift)).astype(F32)
    t128 = lax.broadcasted_iota(jnp.int32, (r, DH), 0) & (s - 1)
    sub8 = lax.broadcasted_iota(jnp.int32, (SUBLANES, DH), 0) >> shift
    scale = DH ** -0.5

    g_c, lf_c, bt_c, g_r, bt_r = _gate_terms(gate_ref[...], gb_ref, gbt_ref, tril, triu)
    bt_last_c = _dot_exact(same.astype(F32), lf_c)
    m_rows = _dot_exact(expand, m0_ref[...])
    for h in range(MH):
        hs = slice(h * DH, (h + 1) * DH)
        m_prev = m_rows[:, h:h + 1]
        w, s_inter, mt, it_col, bt_col = _head_scores(h, g_c, bt_c, g_r, bt_r, m_prev, mask)
        qh = q_ref[:, hs]
        kh = k_ref[:, hs] * scale
        vh = v_ref[:, hs]
        qb = qh.astype(BF16)
        kb = kh.astype(BF16)
        sqk = _dot_nt(qb, kb) * w
        for j in range(r // SUBLANES):
            rows = slice(j * SUBLANES, (j + 1) * SUBLANES)
            acc = None
            q8 = q_ref[rows, hs].astype(BF16)
            for p in range(per_tile):
                part = _dot_nt(q8, c0_ref[j * per_tile + p, h].astype(BF16))
                acc = part if acc is None else jnp.where(sub8 == p, part, acc)
            qc_scr[rows, :] = acc
        n_rows = _dot_exact(expand, n0_ref[:, h, :])
        qn = jnp.sum(qh * n_rows, axis=-1, keepdims=True)
        oa_ref[:, hs] = _head_output(sqk, vh.astype(BF16), s_inter, qc_scr[...], qn, mt,
                                     o_ref[:, hs], mg_ref[:, hs])
        m_new = _bcast_block_last(jnp.broadcast_to(mt, (r, DH)), t128, s)
        bt_last = bt_last_c[:, MH + h:MH + h + 1]
        wk = jnp.exp(bt_last - bt_col + it_col - m_new)
        decay = jnp.exp(bt_last + m_prev - m_new)
        vw = vh * wk
        for j in range(r // SUBLANES):
            rows = slice(j * SUBLANES, (j + 1) * SUBLANES)
            k8 = (k_ref[rows, hs] * scale).astype(BF16)
            for p in range(per_tile):
                bidx = j * per_tile + p
                last = bidx * s + s - 1
                upd = _dot_tn(jnp.where(sub8 == p, vw[rows], 0.0).astype(BF16), k8)
                c_ref[bidx, h] = decay[last:last + 1, 0:1] * c0_ref[bidx, h] + upd
        st_scr[0] = decay
        st_scr[1] = m_new
        last_rows = pl.ds(s - 1, nb, stride=s)
        n_ref[:, h, :] = st_scr[0, last_rows, :] * n0_ref[:, h, :] + _dot_exact(gather, kh * wk)
        m_ref[:, h:h + 1] = st_scr[1, last_rows, :][:, 0:1]
    t256 = lax.broadcasted_iota(jnp.int32, (r, D_CONV), 0) & (s - 1)
    z = cg_ref[...] * hc_ref[...]
    cb_a = _dot_exact(expand, cb0_ref[:, 0, :])
    cb_b = _dot_exact(expand, cb0_ref[:, 1, :])
    z1 = jnp.where(t256 == 0, cb_b, pltpu.roll(z, 1, 0))
    z2 = jnp.where(t256 == 0, cb_a, jnp.where(t256 == 1, cb_b, pltpu.roll(z, 2, 0)))
    yc = cw_ref[0:1, :] * z2 + cw_ref[1:2, :] * z1 + cw_ref[2:3, :] * z
    oc_ref[...] = _rms(bg_ref[...] * yc) * mg_ref[:, D_MLSTM + D_S5:]
    for half in range(D_CONV // DH):
        lanes = slice(half * DH, (half + 1) * DH)
        st_scr[half] = z[:, lanes]
        cb_ref[:, 0, lanes] = st_scr[half, pl.ds(s - 2, nb, stride=s), :]
        cb_ref[:, 1, lanes] = st_scr[half, pl.ds(s - 1, nb, stride=s), :]


def _short_batch_block(b, depth, layer):
    out_layers = depth if layer == 0 else 1
    nb = min(b, 32)
    while nb > SUBLANES and 2 * (1 + out_layers) * nb * MH * DH * DH * 4 > STATE_VMEM_BUDGET:
        nb //= 2
    return nb


def _mlstm_conv_short(qkvo, gate, bch, w, states_in, states_prev, *, layer, depth, b, s):
    nb = _short_batch_block(b, depth, layer)
    assert SUBLANES % s == 0 and s >= CONV_W - 1 and b % nb == 0 and (nb * s) % SUBLANES == 0
    r = nb * s
    kern = functools.partial(_mlstm_conv_short_kernel, layer=layer, nb=nb, s=s)
    col = lambda wd, j: pl.BlockSpec((r, wd), lambda i: (i, j))
    st = lambda shape: pl.BlockSpec((None, nb) + shape, lambda i: (layer, i) + (0,) * len(shape))
    cst = lambda a: pl.BlockSpec(a.shape, lambda i: (0,) * a.ndim)
    state_specs = [st((MH, DH, DH)), st((MH, DH)), st((MH,)), st((CONV_W - 1, D_CONV))]
    out_state_specs = state_specs
    if layer == 0:
        st0 = lambda shape: pl.BlockSpec((depth, nb) + shape, lambda i: (0, i) + (0,) * len(shape))
        out_state_specs = [st0((MH, DH, DH)), st0((MH, DH)), st0((MH,)), st0((CONV_W - 1, D_CONV))]
    consts = (w["gb"], w["gbt"], w["cw"], w["mg"])
    args = [qkvo, qkvo, qkvo, qkvo, gate, bch, bch, bch, *consts, *states_in]
    in_specs = [col(D_MLSTM, 0), col(D_MLSTM, 1), col(D_MLSTM, 2), col(D_MLSTM, 3), col(GATE_W, 0),
                col(D_CONV, 0), col(D_CONV, 1), col(D_CONV, 2)] + [cst(a) for a in consts] + state_specs
    aliases = {}
    if layer > 0:
        aliases = {len(args) + j: 2 + j for j in range(4)}
        args += list(states_prev)
        in_specs += [_ANY] * 4
    return pl.pallas_call(
        kern,
        grid=(b // nb,),
        in_specs=in_specs,
        out_specs=[col(D_MLSTM, 0), col(D_CONV, 0)] + out_state_specs,
        out_shape=[jax.ShapeDtypeStruct((b * s, D_MLSTM), F32), jax.ShapeDtypeStruct((b * s, D_CONV), F32)]
        + _state_shapes(depth, b),
        scratch_shapes=[pltpu.VMEM((r, DH), F32), pltpu.VMEM((2, r, DH), F32)],
        input_output_aliases=aliases,
        compiler_params=pltpu.CompilerParams(dimension_semantics=("arbitrary",), vmem_limit_bytes=VMEM_LIMIT),
        name="mlstm_conv_short",
    )(*args)


def _s5_kernel(*refs, layer, nb, tt, zero_state, parts, batch_major):
    u_ref, lam_ref, bblk_ref, cblk_ref, d_ref, wglu_ref, mg_ref = refs[:7]
    refs = refs[7:]
    if not zero_state:
        hr0_ref, hi0_ref = refs[:2]
        refs = refs[2:]
    if layer > 0:
        refs = refs[2:]
    ob_ref, hr_ref, hi_ref, xs_ref, a_ref, bf_ref = refs[:6]
    if batch_major:
        ut_scr, ot_scr = refs[6:]
        ut_scr[...] = jnp.swapaxes(u_ref[...], 0, 1).reshape(tt * nb, D_S5)
        u_bm_ref, ob_bm_ref, u_ref, ob_ref = u_ref, ob_ref, ut_scr, ot_scr
    if layer == 0:
        hr_all, hi_all = hr_ref, hi_ref
        hr_ref, hi_ref = hr_all.at[0], hi_all.at[0]
    ti = pl.program_id(0)
    n = S5_N
    lc = 512

    @pl.when(ti == 0)
    def _():
        if layer == 0:
            hr_all[...] = jnp.zeros_like(hr_all)
            hi_all[...] = jnp.zeros_like(hi_all)
        hr_ref[...] = jnp.zeros_like(hr_ref) if zero_state else hr0_ref[...]
        hi_ref[...] = jnp.zeros_like(hi_ref) if zero_state else hi0_ref[...]
        lre = jnp.minimum(lam_ref[0:1, :], -1e-4)
        lim = lam_ref[1:2, :]
        dt = jnp.exp(lam_ref[2:3, :])
        mag = jnp.exp(lre * dt)
        ab_re = mag * jnp.cos(lim * dt)
        ab_im = mag * jnp.sin(lim * dt)
        den = lre * lre + lim * lim
        nr = ab_re - 1.0
        fre = (nr * lre + ab_im * lim) / den
        fim = (ab_im * lre - nr * lim) / den
        a_ref[:, :n] = jnp.broadcast_to(ab_re, (SUBLANES, n))
        a_ref[:, n:] = jnp.broadcast_to(ab_im, (SUBLANES, n))
        b_re = bblk_ref[:, :n]
        b_im = bblk_ref[:, n:]
        bf_ref[:, :n] = (fre * b_re - fim * b_im).astype(BF16)
        bf_ref[:, n:] = (fre * b_im + fim * b_re).astype(BF16)

    rows_p = nb * tt // parts
    part_rows = [slice(p * rows_p, (p + 1) * rows_p) for p in range(parts)]
    for rs in part_rows:
        xs_ref[rs, :] = _dot(u_ref[rs, :].astype(BF16), bf_ref[...])
    ar = [a_ref[:, c0:c0 + lc] for c0 in range(0, n, lc)]
    ai = [a_ref[:, n + c0:n + c0 + lc] for c0 in range(0, n, lc)]
    h, y = {}, []
    for p, rs in enumerate(part_rows):
        for rg in range(nb // SUBLANES):
            g8 = slice(rg * SUBLANES, (rg + 1) * SUBLANES)
            for ci, c0 in enumerate(range(0, n, lc)):
                if p == 0:
                    h[rg, ci] = (hr_ref[g8, c0:c0 + lc], hi_ref[g8, c0:c0 + lc])
            for t in range(p * tt // parts, (p + 1) * tt // parts):
                r8 = slice(t * nb + rg * SUBLANES, t * nb + (rg + 1) * SUBLANES)
                for ci, c0 in enumerate(range(0, n, lc)):
                    hr, hi = h[rg, ci]
                    nr_ = ar[ci] * hr - ai[ci] * hi + xs_ref[r8, c0:c0 + lc]
                    ni_ = ar[ci] * hi + ai[ci] * hr + xs_ref[r8, n + c0:n + c0 + lc]
                    xs_ref[r8, c0:c0 + lc] = nr_
                    xs_ref[r8, n + c0:n + c0 + lc] = ni_
                    h[rg, ci] = (nr_, ni_)
            for ci, c0 in enumerate(range(0, n, lc)):
                if p == parts - 1:
                    hr_ref[g8, c0:c0 + lc] = h[rg, ci][0]
                    hi_ref[g8, c0:c0 + lc] = h[rg, ci][1]
        y.append(_dot(xs_ref[rs, :].astype(BF16), cblk_ref[...]) + d_ref[...] * u_ref[rs, :])
    g = [jax.nn.gelu(v) for v in y]
    gate = [_dot(v.astype(BF16), wglu_ref[...]) for v in g]
    for rs, v, z in zip(part_rows, g, gate):
        ob_ref[rs, :] = _rms(v * jax.nn.sigmoid(z)) * mg_ref[:, D_MLSTM:D_MLSTM + D_S5]
    if batch_major:
        ob_bm_ref[...] = jnp.swapaxes(ot_scr[...].reshape(tt, nb, D_S5), 0, 1)


def _s5(u, w, states_in, states_prev, *, layer, depth, nb, tt):
    batch_major = u.ndim == 3
    rows = u.shape[0] * u.shape[1] if batch_major else u.shape[0]
    rt = nb * tt
    zero_state = states_in is None
    parts = S5_PARTS if tt % (SUBLANES * S5_PARTS) == 0 else 1
    kern = functools.partial(_s5_kernel, layer=layer, nb=nb, tt=tt, zero_state=zero_state, parts=parts,
                             batch_major=batch_major)
    if batch_major:
        assert u.shape[0] == nb and tt % SUBLANES == 0
        io_spec = pl.BlockSpec((nb, tt, D_S5), lambda i: (0, i, 0))
        io_shape = jax.ShapeDtypeStruct(u.shape, F32)
        io_scratch = [pltpu.VMEM((rt, D_S5), F32), pltpu.VMEM((rt, D_S5), F32)]
    else:
        io_spec = pl.BlockSpec((rt, D_S5), lambda i: (i, 0))
        io_shape = jax.ShapeDtypeStruct((rows, D_S5), F32)
        io_scratch = []
    cst = lambda a: pl.BlockSpec(a.shape, lambda i: (0,) * a.ndim)
    st = pl.BlockSpec((None, nb, S5_N), lambda i: (layer, 0, 0))
    st_out = pl.BlockSpec((depth, nb, S5_N), lambda i: (0, 0, 0)) if layer == 0 else st
    consts = (w["lam"], w["bblk"], w["cblk"], w["d"], w["wglu"], w["mg"])
    args = [u, *consts]
    in_specs = [io_spec] + [cst(a) for a in consts]
    if not zero_state:
        args += list(states_in)
        in_specs += [st, st]
    aliases = {}
    if layer > 0:
        aliases = {len(args) + j: 1 + j for j in range(2)}
        args += list(states_prev)
        in_specs += [_ANY] * 2
    return pl.pallas_call(
        kern,
        grid=(rows // rt,),
        in_specs=in_specs,
        out_specs=[io_spec, st_out, st_out],
        out_shape=[io_shape, jax.ShapeDtypeStruct((depth, nb, S5_N), F32),
                   jax.ShapeDtypeStruct((depth, nb, S5_N), F32)],
        scratch_shapes=[pltpu.VMEM((rt, 2 * S5_N), F32), pltpu.VMEM((SUBLANES, 2 * S5_N), F32),
                        pltpu.VMEM((D_S5, 2 * S5_N), BF16)] + io_scratch,
        input_output_aliases=aliases,
        compiler_params=pltpu.CompilerParams(dimension_semantics=("arbitrary",), vmem_limit_bytes=VMEM_LIMIT),
        name="s5",
    )(*args)


def _block_diag(w):
    g, r, c = w.shape
    eye = jnp.eye(g, dtype=w.dtype)
    return jnp.einsum("grc,gh->grhc", w, eye).reshape(g * r, g * c)


def _layer_weights(l, ig_bias, fg_bias, s5_a_re, s5_a_im, s5_log_dt, s5_b_re, s5_b_im, s5_c_re, s5_c_im, s5_d,
                   w_glu, conv_w, mix_g):
    gb = jnp.concatenate([ig_bias[l], fg_bias[l], jnp.zeros((GATE_W - 2 * MH,), F32)])
    lam = jnp.stack([s5_a_re[l].reshape(-1), s5_a_im[l].reshape(-1),
                     jnp.repeat(s5_log_dt[l], S5_P)])
    bblk = jnp.concatenate([_block_diag(jnp.swapaxes(s5_b_re[l], 1, 2)),
                            _block_diag(jnp.swapaxes(s5_b_im[l], 1, 2))], axis=1)
    cblk = jnp.concatenate([_block_diag(jnp.swapaxes(s5_c_re[l], 1, 2)),
                            -_block_diag(jnp.swapaxes(s5_c_im[l], 1, 2))], axis=0).astype(BF16)
    return dict(
        gb=gb.reshape(1, GATE_W), gbt=gb[:SUBLANES].reshape(SUBLANES, 1),
        lam=lam, bblk=bblk, cblk=cblk, d=s5_d[l].reshape(1, D_S5), wglu=w_glu[l].astype(BF16),
        cw=conv_w[l], mg=mix_g[l].reshape(1, D_MODEL))


def _stacked_weights(ffn1_w1, ffn1_w3, ffn1_w2, ffn2_w1, ffn2_w3, ffn2_w2, norm_g, w_in, w_out):
    o = 4 * D_MLSTM
    win = (w_in[:, :, :o].astype(BF16),
           jnp.pad(w_in[:, :, o:o + 2 * MH].astype(BF16), ((0, 0), (0, 0), (0, GATE_W - 2 * MH))),
           w_in[:, :, o + 2 * MH:].astype(BF16))
    return dict(f1=(ffn1_w1.astype(BF16), ffn1_w3.astype(BF16), ffn1_w2.astype(BF16)),
                f2=(ffn2_w1.astype(BF16), ffn2_w3.astype(BF16), ffn2_w2.astype(BF16)),
                g=norm_g, win=win, wo=w_out.astype(BF16))


def _run_group(x, states, weights, big):
    b, s, _ = x.shape
    depth = len(weights)
    tm = min(FFN_ROWS, b * s)
    assert (b * s) % tm == 0
    long_seq = s % MLSTM_CHUNK == 0 and b % MLSTM_ROWS == 0
    chunk, bb = (MLSTM_CHUNK, MLSTM_ROWS) if long_seq else (None, None)
    tt = S5_STEPS if s % S5_STEPS == 0 else s
    xf = x.reshape(b * s, D_MODEL)
    mstates = sstates = None
    if states is not None:
        c_all, n_all, m_all, sr_all, si_all, cb_all = states
        ms_in = (c_all, n_all, m_all, cb_all)
        ss_in = (sr_all.reshape(depth, b, S5_N), si_all.reshape(depth, b, S5_N))
    else:
        ms_in = ss_in = None
    for l, w in enumerate(weights):
        x1, qkvo, gate, u, bch = _ffn_inproj(xf, big["g"], *big["f1"], big["win"], tm, l)
        if tt % SUBLANES == 0:
            ob, *sstates = _s5(u.reshape(b, s, D_S5), w, ss_in, sstates, layer=l, depth=depth, nb=b, tt=tt)
            ob = ob.reshape(b * s, D_S5)
        else:
            u_tm = jnp.swapaxes(u.reshape(b, s, D_S5), 0, 1).reshape(s * b, D_S5)
            ob_tm, *sstates = _s5(u_tm, w, ss_in, sstates, layer=l, depth=depth, nb=b, tt=tt)
            ob = jnp.swapaxes(ob_tm.reshape(s, b, D_S5), 0, 1).reshape(b * s, D_S5)
        if chunk is not None:
            r3 = lambda a: a.reshape(b, s, a.shape[-1])
            oa, oc, *mstates = _mlstm_conv(r3(qkvo), r3(gate), r3(bch), w, ms_in, mstates,
                                           layer=l, depth=depth, bb=bb, chunk=chunk)
            xf = _outproj_ffn(x1, oa.reshape(b * s, D_MLSTM), ob, oc.reshape(b * s, D_CONV),
                              big["g"], big["wo"], *big["f2"], tm, l)
        else:
            oa, oc, *mstates = _mlstm_conv_short(qkvo, gate, bch, w, ms_in, mstates,
                                                 layer=l, depth=depth, b=b, s=s)
            xf = _outproj_ffn(x1, oa, ob, oc, big["g"], big["wo"], *big["f2"], tm, l)
    c1, n1, m1, cb1 = mstates
    sr1, si1 = sstates
    return xf.reshape(b, s, D_MODEL), (c1, n1, m1, sr1.reshape(depth, b, S5_G, S5_P),
                                       si1.reshape(depth, b, S5_G, S5_P), cb1)


def kernel(x_prompt, x_sample, state_mlstm_C, state_mlstm_n, state_mlstm_m, state_s5_re, state_s5_im,
           state_conv, ffn1_w1, ffn1_w3, ffn1_w2, ffn2_w1, ffn2_w3, ffn2_w2, norm_g, w_in, ig_bias, fg_bias,
           s5_a_re, s5_a_im, s5_log_dt, s5_b_re, s5_b_im, s5_c_re, s5_c_im, s5_d, w_glu, conv_w, mix_g, w_out):
    depth = norm_g.shape[0]
    params = (ig_bias, fg_bias, s5_a_re, s5_a_im, s5_log_dt, s5_b_re, s5_b_im, s5_c_re, s5_c_im, s5_d, w_glu,
              conv_w, mix_g)
    weights = [_layer_weights(l, *params) for l in range(depth)]
    big = _stacked_weights(ffn1_w1, ffn1_w3, ffn1_w2, ffn2_w1, ffn2_w3, ffn2_w2, norm_g, w_in, w_out)
    y_p, st_p = _run_group(x_prompt, None, weights, big)
    y_s, st_s = _run_group(
        x_sample, (state_mlstm_C, state_mlstm_n, state_mlstm_m, state_s5_re, state_s5_im, state_conv),
        weights, big)
    return (y_p, y_s, *st_p, *st_s)
```
